```python
import jax, jax.numpy as jnp
from jax import lax
import numpy as np

D_MODEL = 1024
BATCH = 2
SEQ = 8192
DEPTH = 1

CONV_CH = D_MODEL // 2
CONV_K = 3
N_HEADS = 8
HEAD_DIM = (D_MODEL // 2) // N_HEADS
N_KV_HEADS = 2
GQA = N_HEADS // N_KV_HEADS
N_BRANCH = 3
CMP_BLOCK = 32
CMP_STRIDE = 16
CMP_HIDDEN = 256
SEL_BLOCK = 64
SEL_TOP = 16
WINDOW = 512
Q_BLOCK = 128
D_FF = ((8 * D_MODEL // 3 + 255) // 256) * 256
NORM_EPS = 1e-6

kernel_name = "hymba_conv_nsa_alibi_block"


def _proj_sizes():
    kv = N_KV_HEADS * HEAD_DIM
    return [CONV_CH, CONV_CH, CONV_CH, N_HEADS * HEAD_DIM, kv, kv, kv, kv, kv, kv, N_HEADS * N_BRANCH]


def _split_points():
    return [int(v) for v in np.cumsum(_proj_sizes())[:-1]]


def _rmsnorm(x, g):
    xf = x.astype(jnp.float32)
    y = xf * lax.rsqrt(jnp.mean(xf * xf, axis=-1, keepdims=True) + NORM_EPS)
    return (y * g.astype(jnp.float32)).astype(x.dtype)


def _masked_softmax(s, mask):
    s = jnp.where(mask, s, -jnp.inf)
    m = jnp.max(s, axis=-1, keepdims=True)
    m = jnp.where(jnp.isfinite(m), m, 0.0)
    e = jnp.where(mask, jnp.exp(s - m), 0.0)
    return e / jnp.maximum(jnp.sum(e, axis=-1, keepdims=True), 1e-30)


def _alibi_slopes():
    h = jnp.arange(1, N_HEADS + 1, dtype=jnp.float32)
    return jnp.exp2(-8.0 * h / N_HEADS)


def _short_conv(b_gate, c_gate, h, conv_w):
    u = c_gate * h
    up = jnp.pad(u, ((0, 0), (CONV_K - 1, 0), (0, 0)))
    S = u.shape[1]
    conv = sum(conv_w[i] * up[:, i:i + S] for i in range(CONV_K))
    return b_gate * conv


def _compress(k, pos, w1, w2):
    B, S = k.shape[:2]
    nc = (S - CMP_BLOCK) // CMP_STRIDE + 1
    idx = CMP_STRIDE * jnp.arange(nc)[:, None] + jnp.arange(CMP_BLOCK)[None, :]
    blk = k[:, idx] + pos[None, None, :, None, :]
    blk = blk.transpose(0, 1, 3, 2, 4).reshape(B, nc, N_KV_HEADS, CMP_BLOCK * HEAD_DIM)
    return jax.nn.gelu(blk @ w1) @ w2


def _nsa(q, k_cmp, v_cmp, k_sel, v_sel, k_win, v_win, gates,
         k_cmp_pos, k_cmp_w1, k_cmp_w2, v_cmp_pos, v_cmp_w1, v_cmp_w2):
    B, S = q.shape[:2]
    f32 = jnp.float32
    scale = HEAD_DIM ** -0.5
    slopes = _alibi_slopes().reshape(N_KV_HEADS, GQA)[None, :, :, None, None]

    kc = _compress(k_cmp, k_cmp_pos, k_cmp_w1, k_cmp_w2)
    vc = _compress(v_cmp, v_cmp_pos, v_cmp_w1, v_cmp_w2)
    nc = kc.shape[1]
    c_start = CMP_STRIDE * jnp.arange(nc)
    c_end = c_start + CMP_BLOCK - 1

    n_sel = S // SEL_BLOCK
    n_top = min(SEL_TOP, n_sel)
    s_start = SEL_BLOCK * jnp.arange(n_sel)
    ov = jnp.minimum(c_end[:, None] + 1, s_start[None, :] + SEL_BLOCK) - jnp.maximum(c_start[:, None], s_start[None, :])
    overlap = jnp.clip(ov, 0).astype(f32) / CMP_BLOCK
    ks_blk = k_sel.reshape(B, n_sel, SEL_BLOCK, N_KV_HEADS, HEAD_DIM).transpose(0, 3, 1, 2, 4)
    vs_blk = v_sel.reshape(B, n_sel, SEL_BLOCK, N_KV_HEADS, HEAD_DIM).transpose(0, 3, 1, 2, 4)
    bi = jnp.arange(B)[:, None, None, None]
    ki = jnp.arange(N_KV_HEADS)[None, :, None, None]

    kw_pad = jnp.pad(k_win, ((0, 0), (WINDOW, 0), (0, 0), (0, 0)))
    vw_pad = jnp.pad(v_win, ((0, 0), (WINDOW, 0), (0, 0), (0, 0)))

    nq = S // Q_BLOCK
    qb = q.reshape(B, nq, Q_BLOCK, N_KV_HEADS, GQA, HEAD_DIM).transpose(1, 0, 2, 3, 4, 5)
    gb = gates.reshape(B, nq, Q_BLOCK, N_KV_HEADS, GQA, N_BRANCH).transpose(1, 0, 2, 3, 4, 5)
    starts = jnp.arange(nq, dtype=jnp.int32) * Q_BLOCK

    def block(args):
        qi, gi, q0 = args
        t = q0 + jnp.arange(Q_BLOCK, dtype=jnp.int32)

        dist_c = (t[:, None] - c_end[None, :]).astype(f32)
        s_c = jnp.einsum('bqkgd,bnkd->bkgqn', qi, kc).astype(f32) * scale - slopes * dist_c
        p_c = _masked_softmax(s_c, dist_c >= 0)
        o_c = jnp.einsum('bkgqn,bnkd->bqkgd', p_c.astype(vc.dtype), vc)

        imp = jnp.einsum('bkgqn,nj->bkqj', p_c, overlap)
        jblk = jnp.arange(n_sel)[None, :]
        imp = jnp.where(s_start[None, :] > t[:, None], -jnp.inf, imp)
        forced = (jblk == 0) | (jblk == (t[:, None] // SEL_BLOCK))
        imp = jnp.where(forced, jnp.inf, imp)
        _, idx = lax.top_k(imp, n_top)
        k_g = ks_blk[bi, ki, idx].reshape(B, N_KV_HEADS, Q_BLOCK, n_top * SEL_BLOCK, HEAD_DIM)
        v_g = vs_blk[bi, ki, idx].reshape(B, N_KV_HEADS, Q_BLOCK, n_top * SEL_BLOCK, HEAD_DIM)
        pos = (idx[..., None] * SEL_BLOCK + jnp.arange(SEL_BLOCK)).reshape(B, N_KV_HEADS, Q_BLOCK, n_top * SEL_BLOCK)
        dist_s = (t[None, None, :, None] - pos).astype(f32)[:, :, None]
        s_s = jnp.einsum('bqkgd,bkqmd->bkgqm', qi, k_g).astype(f32) * scale - slopes * dist_s
        p_s = _masked_softmax(s_s, dist_s >= 0)
        o_s = jnp.einsum('bkgqm,bkqmd->bqkgd', p_s.astype(v_g.dtype), v_g)

        kwi = lax.dynamic_slice_in_dim(kw_pad, q0, Q_BLOCK + WINDOW, axis=1)
        vwi = lax.dynamic_slice_in_dim(vw_pad, q0, Q_BLOCK + WINDOW, axis=1)
        kpos = q0 - WINDOW + jnp.arange(Q_BLOCK + WINDOW, dtype=jnp.int32)
        dist_w = t[:, None] - kpos[None, :]
        mask_w = (dist_w >= 0) & (dist_w < WINDOW) & (kpos[None, :] >= 0)
        s_w = jnp.einsum('bqkgd,bskd->bkgqs', qi, kwi).astype(f32) * scale - slopes * dist_w.astype(f32)
        p_w = _masked_softmax(s_w, mask_w)
        o_w = jnp.einsum('bkgqs,bskd->bqkgd', p_w.astype(vwi.dtype), vwi)

        return gi[..., 0:1] * o_c + gi[..., 1:2] * o_s + gi[..., 2:3] * o_w

    out = lax.map(block, (qb, gb, starts))
    return out.transpose(1, 0, 2, 3, 4, 5).reshape(B, S, N_HEADS * HEAD_DIM)


def setup_inputs(seed: int = 0) -> dict:
    key = jax.random.key(seed)
    ks = jax.random.split(key, 20)
    nrm = jax.random.normal
    d_in = sum(_proj_sizes())
    f = jnp.float32
    return {
        "x": nrm(ks[0], (BATCH, SEQ, D_MODEL), f),
        "norm1_g": 1.0 + 0.1 * nrm(ks[1], (DEPTH, D_MODEL), f),
        "w_in": nrm(ks[2], (DEPTH, D_MODEL, d_in), f) * D_MODEL ** -0.5,
        "conv_w": nrm(ks[3], (DEPTH, CONV_K, CONV_CH), f) * CONV_K ** -0.5,
        "k_cmp_pos": 0.1 * nrm(ks[4], (DEPTH, CMP_BLOCK, HEAD_DIM), f),
        "k_cmp_w1": nrm(ks[5], (DEPTH, CMP_BLOCK * HEAD_DIM, CMP_HIDDEN), f) * (CMP_BLOCK * HEAD_DIM) ** -0.5,
        "k_cmp_w2": nrm(ks[6], (DEPTH, CMP_HIDDEN, HEAD_DIM), f) * CMP_HIDDEN ** -0.5,
        "v_cmp_pos": 0.1 * nrm(ks[7], (DEPTH, CMP_BLOCK, HEAD_DIM), f),
        "v_cmp_w1": nrm(ks[8], (DEPTH, CMP_BLOCK * HEAD_DIM, CMP_HIDDEN), f) * (CMP_BLOCK * HEAD_DIM) ** -0.5,
        "v_cmp_w2": nrm(ks[9], (DEPTH, CMP_HIDDEN, HEAD_DIM), f) * CMP_HIDDEN ** -0.5,
        "gn_conv_g": 1.0 + 0.1 * nrm(ks[10], (DEPTH, CONV_CH), f),
        "gn_nsa_g": 1.0 + 0.1 * nrm(ks[11], (DEPTH, N_HEADS * HEAD_DIM), f),
        "w_out": nrm(ks[12], (DEPTH, CONV_CH + N_HEADS * HEAD_DIM, D_MODEL), f) * (CONV_CH + N_HEADS * HEAD_DIM) ** -0.5,
        "norm2_g": 1.0 + 0.1 * nrm(ks[13], (DEPTH, D_MODEL), f),
        "w_gate": nrm(ks[14], (DEPTH, D_MODEL, D_FF), f) * D_MODEL ** -0.5,
        "w_up": nrm(ks[15], (DEPTH, D_MODEL, D_FF), f) * D_MODEL ** -0.5,
        "w_down": nrm(ks[16], (DEPTH, D_FF, D_MODEL), f) * D_FF ** -0.5,
        "norm_f_g": 1.0 + 0.1 * nrm(ks[17], (D_MODEL,), f),
    }


def reference(x, norm1_g, w_in, conv_w, k_cmp_pos, k_cmp_w1, k_cmp_w2, v_cmp_pos, v_cmp_w1, v_cmp_w2,
              gn_conv_g, gn_nsa_g, w_out, norm2_g, w_gate, w_up, w_down, norm_f_g):
    h = x
    B, S, _ = x.shape
    for l in range(DEPTH):
        u = _rmsnorm(h, norm1_g[l])
        proj = u @ w_in[l]
        cb, cc, ch, q, kc, vc, ksl, vsl, kw, vw, g = jnp.split(proj, _split_points(), axis=-1)

        conv_out = _short_conv(cb, cc, ch, conv_w[l])

        kvs = lambda t_: t_.reshape(B, S, N_KV_HEADS, HEAD_DIM)
        qh = q.reshape(B, S, N_KV_HEADS, GQA, HEAD_DIM)
        gates = jax.nn.sigmoid(g.reshape(B, S, N_KV_HEADS, GQA, N_BRANCH))
        nsa_out = _nsa(qh, kvs(kc), kvs(vc), kvs(ksl), kvs(vsl), kvs(kw), kvs(vw), gates,
                       k_cmp_pos[l], k_cmp_w1[l], k_cmp_w2[l], v_cmp_pos[l], v_cmp_w1[l], v_cmp_w2[l])

        mixed = jnp.concatenate([_rmsnorm(conv_out, gn_conv_g[l]), _rmsnorm(nsa_out, gn_nsa_g[l])], axis=-1)
        h = h + mixed @ w_out[l]

        u2 = _rmsnorm(h, norm2_g[l])
        h = h + (jax.nn.silu(u2 @ w_gate[l]) * (u2 @ w_up[l])) @ w_down[l]
    return _rmsnorm(h, norm_f_g)
```

```python
import functools

import numpy as np
import jax
import jax.numpy as jnp
from jax import lax
from jax.experimental import pallas as pl
from jax.experimental.pallas import tpu as pltpu

D_MODEL = 1024
CONV_CH = 512
CONV_K = 3
N_HEADS = 8
HEAD_DIM = 64
N_KV_HEADS = 2
GQA = 4
N_BRANCH = 3
CMP_BLOCK = 32
CMP_STRIDE = 16
CMP_HIDDEN = 256
SEL_BLOCK = 64
SEL_TOP = 16
WINDOW = 512
D_FF = 2816
NORM_EPS = 1e-6

F32 = jnp.float32
BF16 = jnp.bfloat16

LANES = 128
VMEM_LIMIT = 56 * 1024 * 1024

TQ = 128
TK = 512
NEG = -1e30
SEL_NEG = -32768.0

_CONV_W = 3 * CONV_CH
_Q_OFF = _CONV_W
_KV_OFF = _Q_OFF + N_HEADS * HEAD_DIM
_KV_W = 6 * N_KV_HEADS * HEAD_DIM
_G_OFF = _KV_OFF + _KV_W
_G_W = N_KV_HEADS * LANES
_PROJ_W = _G_OFF + _G_W


def _rms(x, g):
    return x * lax.rsqrt(jnp.mean(x * x, axis=-1, keepdims=True) + NORM_EPS) * g


def _dot(a, b):
    return jnp.dot(a, b, preferred_element_type=F32)


def _dot_nt(a, b):
    return lax.dot_general(a, b, (((1,), (1,)), ((), ())), preferred_element_type=F32)


def _inproj_kernel(x_ref, g_ref, w_ref, conv_ref, q_ref, kv_ref, gate_ref):
    u = _rms(x_ref[...], g_ref[...]).astype(BF16)
    conv_ref[...] = _dot(u, w_ref[:, 0:_CONV_W])
    q_ref[...] = _dot(u, w_ref[:, _Q_OFF:_KV_OFF]).astype(BF16)
    kv_ref[...] = _dot(u, w_ref[:, _KV_OFF:_G_OFF]).astype(BF16)
    gate_ref[...] = jax.nn.sigmoid(_dot(u, w_ref[:, _G_OFF:_PROJ_W]))


def _inproj(x2, g, w, tm=512):
    rows = x2.shape[0]
    return pl.pallas_call(
        _inproj_kernel,
        grid=(rows // tm,),
        in_specs=[
            pl.BlockSpec((tm, D_MODEL), lambda i: (i, 0)),
            pl.BlockSpec((1, D_MODEL), lambda i: (0, 0)),
            pl.BlockSpec((D_MODEL, _PROJ_W), lambda i: (0, 0), pipeline_mode=pl.Buffered(1)),
        ],
        out_specs=[
            pl.BlockSpec((tm, _CONV_W), lambda i: (i, 0)),
            pl.BlockSpec((tm, N_HEADS * HEAD_DIM), lambda i: (i, 0)),
            pl.BlockSpec((tm, _KV_W), lambda i: (i, 0)),
            pl.BlockSpec((tm, _G_W), lambda i: (i, 0)),
        ],
        out_shape=[
            jax.ShapeDtypeStruct((rows, _CONV_W), F32),
            jax.ShapeDtypeStruct((rows, N_HEADS * HEAD_DIM), BF16),
            jax.ShapeDtypeStruct((rows, _KV_W), BF16),
            jax.ShapeDtypeStruct((rows, _G_W), F32),
        ],
        compiler_params=pltpu.CompilerParams(
            dimension_semantics=("arbitrary",), vmem_limit_bytes=VMEM_LIMIT),
        name="inproj",
    )(x2, g, w)


def _compress_kernel(c_ref, pos_ref, w1_ref, w2_ref, o_ref):
    half = CMP_STRIDE * HEAD_DIM
    c = c_ref[...]
    a = _dot(c, w1_ref[0:half, :])
    b = _dot(c, w1_ref[half:2 * half, :])
    pb = _dot(pos_ref[...], w1_ref[...])
    nch = c.shape[0]
    h = a + pltpu.roll(b, nch - 1, axis=0) + pb[0:1, :]
    h = jax.nn.gelu(h, approximate=True)
    o_ref[...] = _dot(h.astype(BF16), w2_ref[...])


def _compress(chunks, pos, w1, w2):
    _, B, KVH, nch, half = chunks.shape
    return pl.pallas_call(
        _compress_kernel,
        grid=(2, B, KVH),
        in_specs=[
            pl.BlockSpec((None, None, None, nch, half), lambda s, b, k: (s, b, k, 0, 0)),
            pl.BlockSpec((None, 8, 2 * half), lambda s, b, k: (s, 0, 0)),
            pl.BlockSpec((None, 2 * half, CMP_HIDDEN), lambda s, b, k: (s, 0, 0)),
            pl.BlockSpec((None, CMP_HIDDEN, HEAD_DIM), lambda s, b, k: (s, 0, 0)),
        ],
        out_specs=pl.BlockSpec((None, None, None, nch, HEAD_DIM), lambda s, b, k: (s, b, k, 0, 0)),
        out_shape=jax.ShapeDtypeStruct((2, B, KVH, nch, HEAD_DIM), F32),
        compiler_params=pltpu.CompilerParams(
            dimension_semantics=("arbitrary",) * 3, vmem_limit_bytes=VMEM_LIMIT),
        name="compress",
    )(chunks, pos, w1, w2)


def _softmax_rows(s, mask):
    s = jnp.where(mask, s, NEG)
    m = jnp.max(s, axis=1, keepdims=True)
    e = jnp.where(mask, jnp.exp(s - m), 0.0)
    l = jnp.sum(e, axis=1, keepdims=True)
    return e / jnp.maximum(l, 1e-30)


def _nsa_kernel(q_ref, qf_ref, kc_ref, vc_ref, ks_ref, vs_ref, kw_ref, vw_ref, ov_ref, g_ref,
                o_ref, qa_ref, imp_ref):
    i = pl.program_id(2)
    t0 = i * TQ
    rows = GQA * TQ
    n_sel = ov_ref.shape[1]
    ncmp = kc_ref.shape[0]

    q = q_ref[...]
    for g in range(GQA):
        qg = q[:, g * HEAD_DIM:(g + 1) * HEAD_DIM].astype(F32) * (HEAD_DIM ** -0.5)
        qa_ref[g * TQ:(g + 1) * TQ, 0:HEAD_DIM] = qg.astype(BF16)
    qa_ref[:, HEAD_DIM:LANES] = qf_ref[...]
    qa1 = qa_ref[:, 0:LANES]

    def row_t(shape):
        r = lax.broadcasted_iota(jnp.int32, shape, 0)
        return t0 + (r & (TQ - 1))

    s_c = _dot_nt(qa1, kc_ref[...])
    n_idx = lax.broadcasted_iota(jnp.int32, (rows, ncmp), 1)
    mask_c = (CMP_STRIDE * n_idx + (CMP_BLOCK - 1)) <= row_t((rows, ncmp))
    p_c = _softmax_rows(s_c, mask_c)
    o_c = _dot(p_c.astype(BF16), vc_ref[...])

    p_sum = p_c[0:TQ] + p_c[TQ:2 * TQ] + p_c[2 * TQ:3 * TQ] + p_c[3 * TQ:4 * TQ]
    p_hi = p_sum.astype(BF16)
    p_lo = (p_sum - p_hi.astype(F32)).astype(BF16)
    imp = _dot(p_hi, ov_ref[...]) + _dot(p_lo, ov_ref[...])
    imp_t = imp.T
    jb = lax.broadcasted_iota(jnp.int32, (n_sel, TQ), 0)
    tq = t0 + lax.broadcasted_iota(jnp.int32, (n_sel, TQ), 1)
    imp_t = jnp.where(SEL_BLOCK * jb > tq, -jnp.inf, imp_t)
    forced = (jb == 0) | (jb == (tq >> 6))
    imp_t = jnp.where(forced, jnp.inf, imp_t)
    imp_ref[...] = imp_t

    def rank_body(ii, cnt):
        r = imp_ref[pl.ds(ii, 1), :]
        x = imp_ref[...]
        tie = jnp.where(jb > ii, 1.0, 0.0)
        return cnt + jnp.where(r > x, 1.0, jnp.where(r == x, tie, 0.0))

    cnt = lax.fori_loop(0, n_sel, rank_body, jnp.zeros((n_sel, TQ), F32))
    bias_t = jnp.where(cnt < float(SEL_TOP), 0.0, SEL_NEG)
    bias = bias_t.T.astype(BF16)
    for g in range(GQA):
        qa_ref[g * TQ:(g + 1) * TQ, LANES:2 * LANES] = bias

    qa2 = qa_ref[...]

    def sel_tile(kt, carry, causal):
        m, l, acc = carry
        k0 = pl.multiple_of(kt * TK, TK)
        s = _dot_nt(qa2, ks_ref[pl.ds(k0, TK), :])
        if causal:
            kpos = k0 + lax.broadcasted_iota(jnp.int32, (rows, TK), 1)
            s = jnp.where(kpos <= row_t((rows, TK)), s, NEG)
        m_new = jnp.maximum(m, jnp.max(s, axis=1, keepdims=True))
        alpha = jnp.exp(m - m_new)
        p = jnp.exp(s - m_new)
        l = alpha * l + jnp.sum(p, axis=1, keepdims=True)
        acc = alpha * acc + _dot(p.astype(BF16), vs_ref[pl.ds(k0, TK), :])
        return m_new, l, acc

    n_full = t0 // TK
    init = (jnp.full((rows, 1), NEG, F32), jnp.zeros((rows, 1), F32), jnp.zeros((rows, HEAD_DIM), F32))
    carry = lax.fori_loop(0, n_full, lambda kt, c: sel_tile(kt, c, False), init)
    _, l_s, acc_s = sel_tile(n_full, carry, True)
    o_s = acc_s / l_s

    wk = TQ + WINDOW
    w0 = pl.multiple_of(t0, TQ)
    s_w = _dot_nt(qa1, kw_ref[pl.ds(w0, wk), :])
    col = lax.broadcasted_iota(jnp.int32, (rows, wk), 1)
    r_in = lax.broadcasted_iota(jnp.int32, (rows, wk), 0) & (TQ - 1)
    dist = r_in + WINDOW - col
    mask_w = (dist >= 0) & (dist < WINDOW) & (t0 - WINDOW + col >= 0)
    p_w = _softmax_rows(s_w, mask_w)
    o_w = _dot(p_w.astype(BF16), vw_ref[pl.ds(w0, wk), :])

    gt = g_ref[...]
    for g in range(GQA):
        sl = slice(g * TQ, (g + 1) * TQ)
        c0 = g * N_BRANCH
        o = (gt[:, c0:c0 + 1] * o_c[sl] + gt[:, c0 + 1:c0 + 2] * o_s[sl]
             + gt[:, c0 + 2:c0 + 3] * o_w[sl])
        o_ref[:, g * HEAD_DIM:(g + 1) * HEAD_DIM] = o


def _nsa(q, qfeat, kc_aug, vc, ks_aug, vs, kw_aug, vw, ov, gates, B, S):
    nq = S // TQ
    gw = GQA * HEAD_DIM
    ncmp = kc_aug.shape[2]
    n_sel = ov.shape[1]
    sp = kw_aug.shape[2]
    kvmap = lambda b, k, i: (b, k, 0, 0)
    return pl.pallas_call(
        _nsa_kernel,
        grid=(B, N_KV_HEADS, nq),
        in_specs=[
            pl.BlockSpec((TQ, gw), lambda b, k, i: (b * nq + i, k)),
            pl.BlockSpec((None, GQA * TQ, HEAD_DIM), lambda b, k, i: (k, 0, 0)),
            pl.BlockSpec((None, None, ncmp, LANES), kvmap),
            pl.BlockSpec((None, None, ncmp, HEAD_DIM), kvmap),
            pl.BlockSpec((None, None, S, 2 * LANES), kvmap),
            pl.BlockSpec((None, None, S, HEAD_DIM), kvmap),
            pl.BlockSpec((None, None, sp, LANES), kvmap),
            pl.BlockSpec((None, None, sp, HEAD_DIM), kvmap),
            pl.BlockSpec((ncmp, n_sel), lambda b, k, i: (0, 0)),
            pl.BlockSpec((TQ, LANES), lambda b, k, i: (b * nq + i, k)),
        ],
        out_specs=pl.BlockSpec((TQ, gw), lambda b, k, i: (b * nq + i, k)),
        out_shape=jax.ShapeDtypeStruct((B * S, N_KV_HEADS * gw), F32),
        scratch_shapes=[
            pltpu.VMEM((GQA * TQ, 2 * LANES), BF16),
            pltpu.VMEM((n_sel, TQ), F32),
        ],
        compiler_params=pltpu.CompilerParams(
            dimension_semantics=("arbitrary",) * 3, vmem_limit_bytes=VMEM_LIMIT),
        name="nsa",
    )(q, qfeat, kc_aug, vc, ks_aug, vs, kw_aug, vw, ov, gates)


def _mix_kernel(S, tm, cb_ref, cc_ref, ch_ref, ccp_ref, chp_ref, nsa_ref, x_ref, cw_ref,
                gc_ref, gn_ref, wo_ref, h_ref):
    i = pl.program_id(0)
    u = cc_ref[...] * ch_ref[...]
    prev = ccp_ref[...] * chp_ref[...]
    first = (i * tm) % S == 0
    prev = jnp.where(first, 0.0, prev)
    r = lax.broadcasted_iota(jnp.int32, u.shape, 0)
    u1 = jnp.where(r == 0, prev[7:8, :], pltpu.roll(u, 1, axis=0))
    u2 = pltpu.roll(u, 2, axis=0)
    u2 = jnp.where(r == 0, prev[6:7, :], jnp.where(r == 1, prev[7:8, :], u2))
    cw = cw_ref[...]
    conv = cw[0:1, :] * u2 + cw[1:2, :] * u1 + cw[2:3, :] * u
    y = cb_ref[...] * conv
    mixed_c = _rms(y, gc_ref[...]).astype(BF16)
    mixed_n = _rms(nsa_ref[...], gn_ref[...]).astype(BF16)
    h_ref[...] = (x_ref[...] + _dot(mixed_c, wo_ref[0:CONV_CH, :])
                  + _dot(mixed_n, wo_ref[CONV_CH:2 * CONV_CH, :]))


def _mix(conv_in, nsa_out, x2, conv_w, gc, gn, wo, S, tm=512):
    rows = x2.shape[0]
    prev_map = lambda c: (lambda i: (jnp.maximum(i * (tm // 8) - 1, 0), c))
    return pl.pallas_call(
        functools.partial(_mix_kernel, S, tm),
        grid=(rows // tm,),
        in_specs=[
            pl.BlockSpec((tm, CONV_CH), lambda i: (i, 0)),
            pl.BlockSpec((tm, CONV_CH), lambda i: (i, 1)),
            pl.BlockSpec((tm, CONV_CH), lambda i: (i, 2)),
            pl.BlockSpec((8, CONV_CH), prev_map(1)),
            pl.BlockSpec((8, CONV_CH), prev_map(2)),
            pl.BlockSpec((tm, CONV_CH), lambda i: (i, 0)),
            pl.BlockSpec((tm, D_MODEL), lambda i: (i, 0)),
            pl.BlockSpec((CONV_K, CONV_CH), lambda i: (0, 0)),
            pl.BlockSpec((1, CONV_CH), lambda i: (0, 0)),
            pl.BlockSpec((1, CONV_CH), lambda i: (0, 0)),
            pl.BlockSpec((2 * CONV_CH, D_MODEL), lambda i: (0, 0), pipeline_mode=pl.Buffered(1)),
        ],
        out_specs=pl.BlockSpec((tm, D_MODEL), lambda i: (i, 0)),
        out_shape=jax.ShapeDtypeStruct((rows, D_MODEL), F32),
        compiler_params=pltpu.CompilerParams(
            dimension_semantics=("arbitrary",), vmem_limit_bytes=VMEM_LIMIT),
        name="mix",
    )(conv_in, conv_in, conv_in, conv_in, conv_in, nsa_out, x2, conv_w, gc, gn, wo)


_FF_CHUNK = 1408


def _ffn_kernel(h_ref, g2_ref, wg_ref, wu_ref, wd_ref, gf_ref, o_ref):
    h = h_ref[...]
    u = _rms(h, g2_ref[...]).astype(BF16)
    acc = h
    for c in range(D_FF // _FF_CHUNK):
        sl = slice(c * _FF_CHUNK, (c + 1) * _FF_CHUNK)
        a = jax.nn.silu(_dot(u, wg_ref[:, sl])) * _dot(u, wu_ref[:, sl])
        acc = acc + _dot(a.astype(BF16), wd_ref[sl, :])
    o_ref[...] = _rms(acc, gf_ref[...])


def _ffn(h, g2, wg, wu, wd, gf, tm=512):
    rows = h.shape[0]
    const = lambda i: (0, 0)
    return pl.pallas_call(
        _ffn_kernel,
        grid=(rows // tm,),
        in_specs=[
            pl.BlockSpec((tm, D_MODEL), lambda i: (i, 0)),
            pl.BlockSpec((1, D_MODEL), const),
            pl.BlockSpec((D_MODEL, D_FF), const, pipeline_mode=pl.Buffered(1)),
            pl.BlockSpec((D_MODEL, D_FF), const, pipeline_mode=pl.Buffered(1)),
            pl.BlockSpec((D_FF, D_MODEL), const, pipeline_mode=pl.Buffered(1)),
            pl.BlockSpec((1, D_MODEL), const),
        ],
        out_specs=pl.BlockSpec((tm, D_MODEL), lambda i: (i, 0)),
        out_shape=jax.ShapeDtypeStruct((rows, D_MODEL), F32),
        compiler_params=pltpu.CompilerParams(
            dimension_semantics=("arbitrary",), vmem_limit_bytes=VMEM_LIMIT),
        name="ffn",
    )(h, g2, wg, wu, wd, gf)


def _pos_features(pos, width):
    f = np.zeros((pos.shape[0], width), np.float32)
    f[:, 0] = pos // SEL_BLOCK
    f[:, 1] = pos % SEL_BLOCK
    return f


def _query_features():
    f = np.zeros((N_KV_HEADS, GQA * TQ, HEAD_DIM), np.float32)
    for k in range(N_KV_HEADS):
        for g in range(GQA):
            slope = 2.0 ** (-8.0 * (k * GQA + g + 1) / N_HEADS)
            f[k, g * TQ:(g + 1) * TQ, 0] = slope * SEL_BLOCK
            f[k, g * TQ:(g + 1) * TQ, 1] = slope
    return f


def _overlap(ncmp_pad, n_sel):
    nc = ncmp_pad - 1
    c_start = CMP_STRIDE * np.arange(nc)
    c_end = c_start + CMP_BLOCK - 1
    s_start = SEL_BLOCK * np.arange(n_sel)
    ov = np.minimum(c_end[:, None] + 1, s_start[None, :] + SEL_BLOCK) - np.maximum(c_start[:, None], s_start[None, :])
    out = np.zeros((ncmp_pad, n_sel), np.float32)
    out[:nc] = np.clip(ov, 0, None).astype(np.float32) / CMP_BLOCK
    return out


def _layer(h, norm1_g, w_in, conv_w, k_cmp_pos, k_cmp_w1, k_cmp_w2, v_cmp_pos, v_cmp_w1, v_cmp_w2,
           gn_conv_g, gn_nsa_g, w_out, norm2_g, w_gate, w_up, w_down):
    B, S, _ = h.shape
    rows = B * S
    x2 = h.reshape(rows, D_MODEL)
    n_sel = S // SEL_BLOCK
    nch = S // CMP_STRIDE

    ng = GQA * N_BRANCH
    w_main, w_g = w_in[:, :_G_OFF], w_in[:, _G_OFF:]
    gpad = jnp.zeros((D_MODEL, LANES - ng), w_in.dtype)
    w_pad = jnp.concatenate([w_main, w_g[:, :ng], gpad, w_g[:, ng:], gpad], axis=1).astype(BF16)

    conv_in, q, kv, gates = _inproj(x2, norm1_g.reshape(1, D_MODEL), w_pad)

    kvt = kv.reshape(B, S, 6, N_KV_HEADS, HEAD_DIM).transpose(2, 0, 3, 1, 4)

    chunks = kvt[0:2].reshape(2, B, N_KV_HEADS, nch, CMP_STRIDE * HEAD_DIM)
    pos = jnp.stack([k_cmp_pos, v_cmp_pos]).reshape(2, 1, CMP_BLOCK * HEAD_DIM)
    pos = jnp.broadcast_to(pos, (2, 8, CMP_BLOCK * HEAD_DIM)).astype(BF16)
    w1 = jnp.stack([k_cmp_w1, v_cmp_w1]).astype(BF16)
    w2 = jnp.stack([k_cmp_w2, v_cmp_w2]).astype(BF16)
    cmp_out = _compress(chunks, pos, w1, w2)

    def bcast(f):
        return jnp.broadcast_to(jnp.asarray(f, BF16), (B, N_KV_HEADS) + f.shape)

    c_end = CMP_STRIDE * np.arange(nch) + CMP_BLOCK - 1
    kc_aug = jnp.concatenate([cmp_out[0].astype(BF16), bcast(_pos_features(c_end, HEAD_DIM))], axis=-1)
    vc = cmp_out[1].astype(BF16)

    tok = np.arange(S)
    onehot = (tok[:, None] // SEL_BLOCK == np.arange(n_sel)[None, :]).astype(np.float32)
    ks_aug = jnp.concatenate([kvt[2], bcast(_pos_features(tok, HEAD_DIM)), bcast(onehot)], axis=-1)
    kw_aug = jnp.concatenate([kvt[4], bcast(_pos_features(tok, HEAD_DIM))], axis=-1)
    lpad = ((0, 0), (0, 0), (WINDOW, 0), (0, 0))
    kw_aug = jnp.pad(kw_aug, lpad)
    vw = jnp.pad(kvt[5], lpad)

    nsa_out = _nsa(q, jnp.asarray(_query_features(), BF16), kc_aug, vc, ks_aug, kvt[3], kw_aug, vw,
                   jnp.asarray(_overlap(nch, n_sel), BF16), gates, B, S)

    h2 = _mix(conv_in, nsa_out, x2, conv_w, gn_conv_g.reshape(1, CONV_CH),
              gn_nsa_g.reshape(1, N_HEADS * HEAD_DIM), w_out.astype(BF16), S)
    return h2


def kernel(x, norm1_g, w_in, conv_w, k_cmp_pos, k_cmp_w1, k_cmp_w2, v_cmp_pos, v_cmp_w1, v_cmp_w2,
           gn_conv_g, gn_nsa_g, w_out, norm2_g, w_gate, w_up, w_down, norm_f_g):
    B, S, _ = x.shape
    depth = norm1_g.shape[0]
    assert depth == 1, "the final norm is fused into the (single) layer's FFN kernel"
    h2 = _layer(x, norm1_g[0], w_in[0], conv_w[0], k_cmp_pos[0], k_cmp_w1[0], k_cmp_w2[0],
                v_cmp_pos[0], v_cmp_w1[0], v_cmp_w2[0], gn_conv_g[0], gn_nsa_g[0], w_out[0],
                norm2_g[0], w_gate[0], w_up[0], w_down[0])
    out = _ffn(h2, norm2_g[0].reshape(1, D_MODEL), w_gate[0].astype(BF16), w_up[0].astype(BF16),
               w_down[0].astype(BF16), norm_f_g.reshape(1, D_MODEL))
    return out.reshape(B, S, D_MODEL)
```

```python
import functools

import numpy as np
import jax
import jax.numpy as jnp
from jax import lax
from jax.experimental import pallas as pl
from jax.experimental.pallas import tpu as pltpu

D_MODEL = 1024
CONV_CH = 512
CONV_K = 3
N_HEADS = 8
HEAD_DIM = 64
N_KV_HEADS = 2
GQA = 4
N_BRANCH = 3
CMP_BLOCK = 32
CMP_STRIDE = 16
CMP_HIDDEN = 256
SEL_BLOCK = 64
SEL_TOP = 16
WINDOW = 512
D_FF = 2816
NORM_EPS = 1e-6

F32 = jnp.float32
BF16 = jnp.bfloat16

LANES = 128
VMEM_LIMIT = 56 * 1024 * 1024

TQ = 128
TK = 512
NEG = -1e30
MASK_BIAS = -float(2 ** 30)
_INF_BITS = 0x7F800000

_CONV_W = 3 * CONV_CH
_Q_OFF = _CONV_W
_KV_OFF = _Q_OFF + N_HEADS * HEAD_DIM
_KV_W = 6 * N_KV_HEADS * HEAD_DIM
_G_OFF = _KV_OFF + _KV_W
_G_W = N_KV_HEADS * LANES
_PROJ_W = _G_OFF + _G_W


def _rms(x, g):
    return x * lax.rsqrt(jnp.mean(x * x, axis=-1, keepdims=True) + NORM_EPS) * g


def _dot(a, b):
    return jnp.dot(a, b, preferred_element_type=F32)


def _dot_nt(a, b):
    return lax.dot_general(a, b, (((1,), (1,)), ((), ())), preferred_element_type=F32)


def _inproj_kernel(x_ref, g_ref, w_ref, conv_ref, q_ref, kv_ref, gate_ref):
    u = _rms(x_ref[...], g_ref[...]).astype(BF16)
    conv_ref[...] = _dot(u, w_ref[:, 0:_CONV_W])
    q_ref[...] = _dot(u, w_ref[:, _Q_OFF:_KV_OFF]).astype(BF16)
    kv_ref[...] = _dot(u, w_ref[:, _KV_OFF:_G_OFF]).astype(BF16)
    gate_ref[...] = jax.nn.sigmoid(_dot(u, w_ref[:, _G_OFF:_PROJ_W]))


def _inproj(x2, g, w, tm=512):
    rows = x2.shape[0]
    return pl.pallas_call(
        _inproj_kernel,
        grid=(rows // tm,),
        in_specs=[
            pl.BlockSpec((tm, D_MODEL), lambda i: (i, 0)),
            pl.BlockSpec((1, D_MODEL), lambda i: (0, 0)),
            pl.BlockSpec((D_MODEL, _PROJ_W), lambda i: (0, 0), pipeline_mode=pl.Buffered(1)),
        ],
        out_specs=[
            pl.BlockSpec((tm, _CONV_W), lambda i: (i, 0)),
            pl.BlockSpec((tm, N_HEADS * HEAD_DIM), lambda i: (i, 0)),
            pl.BlockSpec((tm, _KV_W), lambda i: (i, 0)),
            pl.BlockSpec((tm, _G_W), lambda i: (i, 0)),
        ],
        out_shape=[
            jax.ShapeDtypeStruct((rows, _CONV_W), F32),
            jax.ShapeDtypeStruct((rows, N_HEADS * HEAD_DIM), BF16),
            jax.ShapeDtypeStruct((rows, _KV_W), BF16),
            jax.ShapeDtypeStruct((rows, _G_W), F32),
        ],
        compiler_params=pltpu.CompilerParams(
            dimension_semantics=("arbitrary",), vmem_limit_bytes=VMEM_LIMIT),
        name="inproj",
    )(x2, g, w)


def _compress_kernel(c_ref, pos_ref, w1_ref, w2_ref, o_ref):
    half = CMP_STRIDE * HEAD_DIM
    c = c_ref[...]
    a = _dot(c, w1_ref[0:half, :])
    b = _dot(c, w1_ref[half:2 * half, :])
    pb = _dot(pos_ref[...], w1_ref[...])
    nch = c.shape[0]
    h = a + pltpu.roll(b, nch - 1, axis=0) + pb[0:1, :]
    h = jax.nn.gelu(h, approximate=True)
    o_ref[...] = _dot(h.astype(BF16), w2_ref[...])


def _compress(chunks, pos, w1, w2):
    _, B, KVH, nch, half = chunks.shape
    return pl.pallas_call(
        _compress_kernel,
        grid=(2, B, KVH),
        in_specs=[
            pl.BlockSpec((None, None, None, nch, half), lambda s, b, k: (s, b, k, 0, 0)),
            pl.BlockSpec((None, 8, 2 * half), lambda s, b, k: (s, 0, 0)),
            pl.BlockSpec((None, 2 * half, CMP_HIDDEN), lambda s, b, k: (s, 0, 0)),
            pl.BlockSpec((None, CMP_HIDDEN, HEAD_DIM), lambda s, b, k: (s, 0, 0)),
        ],
        out_specs=pl.BlockSpec((None, None, None, nch, HEAD_DIM), lambda s, b, k: (s, b, k, 0, 0)),
        out_shape=jax.ShapeDtypeStruct((2, B, KVH, nch, HEAD_DIM), F32),
        compiler_params=pltpu.CompilerParams(
            dimension_semantics=("arbitrary",) * 3, vmem_limit_bytes=VMEM_LIMIT),
        name="compress",
    )(chunks, pos, w1, w2)


def _softmax_rows(s, mask):
    s = jnp.where(mask, s, NEG)
    m = jnp.max(s, axis=1, keepdims=True)
    e = jnp.where(mask, jnp.exp(s - m), 0.0)
    l = jnp.sum(e, axis=1, keepdims=True)
    return e / jnp.maximum(l, 1e-30)


def _nsa_kernel(q_ref, qf_ref, kc_ref, vc_ref, ks_ref, vs_ref, kw_ref, vw_ref, ov_ref, g_ref,
                o_ref, qa_ref, tiles_ref):
    i = pl.program_id(2)
    t0 = i * TQ
    rows = GQA * TQ
    n_sel = ov_ref.shape[1]
    ncmp = kc_ref.shape[0]

    q = q_ref[...]
    for g in range(GQA):
        qg = q[:, g * HEAD_DIM:(g + 1) * HEAD_DIM].astype(F32) * (HEAD_DIM ** -0.5)
        qa_ref[g * TQ:(g + 1) * TQ, 0:HEAD_DIM] = qg.astype(BF16)
    qa_ref[:, HEAD_DIM:LANES] = qf_ref[...]
    qa1 = qa_ref[:, 0:LANES]

    def row_t(shape):
        r = lax.broadcasted_iota(jnp.int32, shape, 0)
        return t0 + (r & (TQ - 1))

    diag_ok = (lax.broadcasted_iota(jnp.int32, (rows, TQ), 1)
               <= (lax.broadcasted_iota(jnp.int32, (rows, TQ), 0) & (TQ - 1)))

    wk = TQ + WINDOW
    w0 = pl.multiple_of(t0, TQ)
    s_w = _dot_nt(qa1, kw_ref[pl.ds(w0, wk), :])
    s_old = jnp.where(diag_ok, NEG, s_w[:, 0:TQ])
    s_mid = s_w[:, TQ:WINDOW]
    s_new = jnp.where(diag_ok, s_w[:, WINDOW:wk], NEG)
    m_w = jnp.maximum(jnp.maximum(jnp.max(s_old, axis=1, keepdims=True),
                                  jnp.max(s_mid, axis=1, keepdims=True)),
                      jnp.max(s_new, axis=1, keepdims=True))
    e_w = jnp.concatenate([jnp.exp(s_old - m_w), jnp.exp(s_mid - m_w), jnp.exp(s_new - m_w)], axis=1)
    l_w = jnp.sum(e_w, axis=1, keepdims=True)
    o_w = _dot(e_w.astype(BF16), vw_ref[pl.ds(w0, wk), :]) / l_w

    s_c = _dot_nt(qa1, kc_ref[...])
    n_idx = lax.broadcasted_iota(jnp.int32, (rows, ncmp), 1)
    mask_c = (CMP_STRIDE * n_idx + (CMP_BLOCK - 1)) <= row_t((rows, ncmp))
    p_c = _softmax_rows(s_c, mask_c)
    o_c = _dot(p_c.astype(BF16), vc_ref[...])

    p_sum = p_c[0:TQ] + p_c[TQ:2 * TQ] + p_c[2 * TQ:3 * TQ] + p_c[3 * TQ:4 * TQ]
    p_hi = p_sum.astype(BF16)
    p_lo = (p_sum - p_hi.astype(F32)).astype(BF16)
    imp = _dot(p_hi, ov_ref[...]) + _dot(p_lo, ov_ref[...])
    imp_t = imp.T
    jb = lax.broadcasted_iota(jnp.int32, (n_sel, TQ), 0)
    tq = t0 + lax.broadcasted_iota(jnp.int32, (n_sel, TQ), 1)
    keys = pltpu.bitcast(imp_t, jnp.int32)
    keys = jnp.where(SEL_BLOCK * jb > tq, -1, keys)
    forced = (jb == 0) | (jb == (tq >> 6))
    keys = jnp.where(forced, _INF_BITS, keys)

    thr = jnp.zeros((1, TQ), jnp.int32)
    for bit in range(30, -1, -1):
        cand = thr | (1 << bit)
        cnt = jnp.sum(jnp.where(keys >= cand, 1.0, 0.0), axis=0, keepdims=True)
        thr = jnp.where(cnt >= float(SEL_TOP), cand, thr)
    above = keys > thr
    tied = keys == thr
    n_above = jnp.sum(jnp.where(above, 1.0, 0.0), axis=0, keepdims=True)
    lower = (lax.broadcasted_iota(jnp.int32, (n_sel, n_sel), 1)
             < lax.broadcasted_iota(jnp.int32, (n_sel, n_sel), 0))
    tied_before = _dot(jnp.where(lower, 1.0, 0.0).astype(BF16), jnp.where(tied, 1.0, 0.0).astype(BF16))
    take_tie = jnp.where(tied_before + n_above < float(SEL_TOP), 1.0, 0.0)
    sel_t = jnp.where(above, 1.0, jnp.where(tied, take_tie, 0.0))
    bias = ((1.0 - sel_t) * MASK_BIAS).T.astype(BF16)
    for g in range(GQA):
        qa_ref[g * TQ:(g + 1) * TQ, LANES:2 * LANES] = bias

    blocks_per_tile = TK // SEL_BLOCK
    n_full = t0 // TK
    n_act = jnp.int32(0)
    for kt in range(n_sel // blocks_per_tile - 1):
        used = jnp.max(sel_t[kt * blocks_per_tile:(kt + 1) * blocks_per_tile, :]) > 0.5
        tiles_ref[n_act] = kt
        n_act = n_act + jnp.where(used & (kt < n_full), 1, 0)

    qa2 = qa_ref[...]

    def sel_tile(kt, carry, causal):
        m, l, acc = carry
        k0 = pl.multiple_of(kt * TK, TK)
        s = _dot_nt(qa2, ks_ref[pl.ds(k0, TK), :])
        if causal:
            kpos = k0 + lax.broadcasted_iota(jnp.int32, (rows, TK), 1)
            s = jnp.where(kpos <= row_t((rows, TK)), s, NEG)
        m_new = jnp.maximum(m, jnp.max(s, axis=1, keepdims=True))
        alpha = jnp.exp(m - m_new)
        p = jnp.exp(s - m_new)
        l = alpha * l + jnp.sum(p, axis=1, keepdims=True)
        acc = alpha * acc + _dot(p.astype(BF16), vs_ref[pl.ds(k0, TK), :])
        return m_new, l, acc

    init = (jnp.full((rows, 1), NEG, F32), jnp.zeros((rows, 1), F32), jnp.zeros((rows, HEAD_DIM), F32))
    carry = lax.fori_loop(0, n_act, lambda j, c: sel_tile(tiles_ref[j], c, False), init)
    _, l_s, acc_s = sel_tile(n_full, carry, True)
    o_s = acc_s / l_s

    gt = g_ref[...]
    for g in range(GQA):
        sl = slice(g * TQ, (g + 1) * TQ)
        c0 = g * N_BRANCH
        o = (gt[:, c0:c0 + 1] * o_c[sl] + gt[:, c0 + 1:c0 + 2] * o_s[sl]
             + gt[:, c0 + 2:c0 + 3] * o_w[sl])
        o_ref[:, g * HEAD_DIM:(g + 1) * HEAD_DIM] = o


def _nsa(q, qfeat, kc_aug, vc, ks_aug, vs, kw_aug, vw, ov, gates, B, S):
    nq = S // TQ
    gw = GQA * HEAD_DIM
    ncmp = kc_aug.shape[2]
    n_sel = ov.shape[1]
    sp = kw_aug.shape[2]
    kvmap = lambda b, k, i: (b, k, 0, 0)
    return pl.pallas_call(
        _nsa_kernel,
        grid=(B, N_KV_HEADS, nq),
        in_specs=[
            pl.BlockSpec((TQ, gw), lambda b, k, i: (b * nq + i, k)),
            pl.BlockSpec((None, GQA * TQ, HEAD_DIM), lambda b, k, i: (k, 0, 0)),
            pl.BlockSpec((None, None, ncmp, LANES), kvmap),
            pl.BlockSpec((None, None, ncmp, HEAD_DIM), kvmap),
            pl.BlockSpec((None, None, S, 2 * LANES), kvmap),
            pl.BlockSpec((None, None, S, HEAD_DIM), kvmap),
            pl.BlockSpec((None, None, sp, LANES), kvmap),
            pl.BlockSpec((None, None, sp, HEAD_DIM), kvmap),
            pl.BlockSpec((ncmp, n_sel), lambda b, k, i: (0, 0)),
            pl.BlockSpec((TQ, LANES), lambda b, k, i: (b * nq + i, k)),
        ],
        out_specs=pl.BlockSpec((TQ, gw), lambda b, k, i: (b * nq + i, k)),
        out_shape=jax.ShapeDtypeStruct((B * S, N_KV_HEADS * gw), F32),
        scratch_shapes=[
            pltpu.VMEM((GQA * TQ, 2 * LANES), BF16),
            pltpu.SMEM((S // TK,), jnp.int32),
        ],
        compiler_params=pltpu.CompilerParams(
            dimension_semantics=("arbitrary",) * 3, vmem_limit_bytes=VMEM_LIMIT),
        name="nsa",
    )(q, qfeat, kc_aug, vc, ks_aug, vs, kw_aug, vw, ov, gates)


def _mix_kernel(S, tm, cb_ref, cc_ref, ch_ref, ccp_ref, chp_ref, nsa_ref, x_ref, cw_ref,
                gc_ref, gn_ref, wo_ref, h_ref):
    i = pl.program_id(0)
    u = cc_ref[...] * ch_ref[...]
    prev = ccp_ref[...] * chp_ref[...]
    first = (i * tm) % S == 0
    prev = jnp.where(first, 0.0, prev)
    r = lax.broadcasted_iota(jnp.int32, u.shape, 0)
    u1 = jnp.where(r == 0, prev[7:8, :], pltpu.roll(u, 1, axis=0))
    u2 = pltpu.roll(u, 2, axis=0)
    u2 = jnp.where(r == 0, prev[6:7, :], jnp.where(r == 1, prev[7:8, :], u2))
    cw = cw_ref[...]
    conv = cw[0:1, :] * u2 + cw[1:2, :] * u1 + cw[2:3, :] * u
    y = cb_ref[...] * conv
    mixed_c = _rms(y, gc_ref[...]).astype(BF16)
    mixed_n = _rms(nsa_ref[...], gn_ref[...]).astype(BF16)
    h_ref[...] = (x_ref[...] + _dot(mixed_c, wo_ref[0:CONV_CH, :])
                  + _dot(mixed_n, wo_ref[CONV_CH:2 * CONV_CH, :]))


def _mix(conv_in, nsa_out, x2, conv_w, gc, gn, wo, S, tm=512):
    rows = x2.shape[0]
    prev_map = lambda c: (lambda i: (jnp.maximum(i * (tm // 8) - 1, 0), c))
    return pl.pallas_call(
        functools.partial(_mix_kernel, S, tm),
        grid=(rows // tm,),
        in_specs=[
            pl.BlockSpec((tm, CONV_CH), lambda i: (i, 0)),
            pl.BlockSpec((tm, CONV_CH), lambda i: (i, 1)),
            pl.BlockSpec((tm, CONV_CH), lambda i: (i, 2)),
            pl.BlockSpec((8, CONV_CH), prev_map(1)),
            pl.BlockSpec((8, CONV_CH), prev_map(2)),
            pl.BlockSpec((tm, CONV_CH), lambda i: (i, 0)),
            pl.BlockSpec((tm, D_MODEL), lambda i: (i, 0)),
            pl.BlockSpec((CONV_K, CONV_CH), lambda i: (0, 0)),
            pl.BlockSpec((1, CONV_CH), lambda i: (0, 0)),
            pl.BlockSpec((1, CONV_CH), lambda i: (0, 0)),
            pl.BlockSpec((2 * CONV_CH, D_MODEL), lambda i: (0, 0), pipeline_mode=pl.Buffered(1)),
        ],
        out_specs=pl.BlockSpec((tm, D_MODEL), lambda i: (i, 0)),
        out_shape=jax.ShapeDtypeStruct((rows, D_MODEL), F32),
        compiler_params=pltpu.CompilerParams(
            dimension_semantics=("arbitrary",), vmem_limit_bytes=VMEM_LIMIT),
        name="mix",
    )(conv_in, conv_in, conv_in, conv_in, conv_in, nsa_out, x2, conv_w, gc, gn, wo)


_FF_CHUNK = 1408


def _ffn_kernel(h_ref, g2_ref, wg_ref, wu_ref, wd_ref, gf_ref, o_ref):
    h = h_ref[...]
    u = _rms(h, g2_ref[...]).astype(BF16)
    acc = h
    for c in range(D_FF // _FF_CHUNK):
        sl = slice(c * _FF_CHUNK, (c + 1) * _FF_CHUNK)
        a = jax.nn.silu(_dot(u, wg_ref[:, sl])) * _dot(u, wu_ref[:, sl])
        acc = acc + _dot(a.astype(BF16), wd_ref[sl, :])
    o_ref[...] = _rms(acc, gf_ref[...])


def _ffn(h, g2, wg, wu, wd, gf, tm=512):
    rows = h.shape[0]
    const = lambda i: (0, 0)
    return pl.pallas_call(
        _ffn_kernel,
        grid=(rows // tm,),
        in_specs=[
            pl.BlockSpec((tm, D_MODEL), lambda i: (i, 0)),
            pl.BlockSpec((1, D_MODEL), const),
            pl.BlockSpec((D_MODEL, D_FF), const, pipeline_mode=pl.Buffered(1)),
            pl.BlockSpec((D_MODEL, D_FF), const, pipeline_mode=pl.Buffered(1)),
            pl.BlockSpec((D_FF, D_MODEL), const, pipeline_mode=pl.Buffered(1)),
            pl.BlockSpec((1, D_MODEL), const),
        ],
        out_specs=pl.BlockSpec((tm, D_MODEL), lambda i: (i, 0)),
        out_shape=jax.ShapeDtypeStruct((rows, D_MODEL), F32),
        compiler_params=pltpu.CompilerParams(
            dimension_semantics=("arbitrary",), vmem_limit_bytes=VMEM_LIMIT),
        name="ffn",
    )(h, g2, wg, wu, wd, gf)


def _pos_features(pos, width):
    f = np.zeros((pos.shape[0], width), np.float32)
    f[:, 0] = pos // SEL_BLOCK
    f[:, 1] = pos % SEL_BLOCK
    return f


def _query_features():
    f = np.zeros((N_KV_HEADS, GQA * TQ, HEAD_DIM), np.float32)
    for k in range(N_KV_HEADS):
        for g in range(GQA):
            slope = 2.0 ** (-8.0 * (k * GQA + g + 1) / N_HEADS)
            f[k, g * TQ:(g + 1) * TQ, 0] = slope * SEL_BLOCK
            f[k, g * TQ:(g + 1) * TQ, 1] = slope
            f[k, g * TQ:(g + 1) * TQ, 2] = MASK_BIAS
    return f


def _overlap(ncmp_pad, n_sel):
    nc = ncmp_pad - 1
    c_start = CMP_STRIDE * np.arange(nc)
    c_end = c_start + CMP_BLOCK - 1
    s_start = SEL_BLOCK * np.arange(n_sel)
    ov = np.minimum(c_end[:, None] + 1, s_start[None, :] + SEL_BLOCK) - np.maximum(c_start[:, None], s_start[None, :])
    out = np.zeros((ncmp_pad, n_sel), np.float32)
    out[:nc] = np.clip(ov, 0, None).astype(np.float32) / CMP_BLOCK
    return out


def _layer(h, norm1_g, w_in, conv_w, k_cmp_pos, k_cmp_w1, k_cmp_w2, v_cmp_pos, v_cmp_w1, v_cmp_w2,
           gn_conv_g, gn_nsa_g, w_out, norm2_g, w_gate, w_up, w_down):
    B, S, _ = h.shape
    rows = B * S
    x2 = h.reshape(rows, D_MODEL)
    n_sel = S // SEL_BLOCK
    nch = S // CMP_STRIDE

    ng = GQA * N_BRANCH
    w_main, w_g = w_in[:, :_G_OFF], w_in[:, _G_OFF:]
    gpad = jnp.zeros((D_MODEL, LANES - ng), w_in.dtype)
    w_pad = jnp.concatenate([w_main, w_g[:, :ng], gpad, w_g[:, ng:], gpad], axis=1).astype(BF16)

    conv_in, q, kv, gates = _inproj(x2, norm1_g.reshape(1, D_MODEL), w_pad)

    kvt = kv.reshape(B, S, 6, N_KV_HEADS, HEAD_DIM).transpose(2, 0, 3, 1, 4)

    chunks = kvt[0:2].reshape(2, B, N_KV_HEADS, nch, CMP_STRIDE * HEAD_DIM)
    pos = jnp.stack([k_cmp_pos, v_cmp_pos]).reshape(2, 1, CMP_BLOCK * HEAD_DIM)
    pos = jnp.broadcast_to(pos, (2, 8, CMP_BLOCK * HEAD_DIM)).astype(BF16)
    w1 = jnp.stack([k_cmp_w1, v_cmp_w1]).astype(BF16)
    w2 = jnp.stack([k_cmp_w2, v_cmp_w2]).astype(BF16)
    cmp_out = _compress(chunks, pos, w1, w2)

    def bcast(f):
        return jnp.broadcast_to(jnp.asarray(f, BF16), (B, N_KV_HEADS) + f.shape)

    c_end = CMP_STRIDE * np.arange(nch) + CMP_BLOCK - 1
    kc_aug = jnp.concatenate([cmp_out[0].astype(BF16), bcast(_pos_features(c_end, HEAD_DIM))], axis=-1)
    vc = cmp_out[1].astype(BF16)

    tok = np.arange(S)
    onehot = (tok[:, None] // SEL_BLOCK == np.arange(n_sel)[None, :]).astype(np.float32)
    ks_aug = jnp.concatenate([kvt[2], bcast(_pos_features(tok, HEAD_DIM)), bcast(onehot)], axis=-1)
    kw_aug = jnp.concatenate([kvt[4], bcast(_pos_features(tok, HEAD_DIM))], axis=-1)
    pad_rows = np.zeros((WINDOW, LANES), np.float32)
    pad_rows[:, HEAD_DIM + 2] = 1.0
    kw_aug = jnp.concatenate([bcast(pad_rows), kw_aug], axis=2)
    vw = jnp.pad(kvt[5], ((0, 0), (0, 0), (WINDOW, 0), (0, 0)))

    nsa_out = _nsa(q, jnp.asarray(_query_features(), BF16), kc_aug, vc, ks_aug, kvt[3], kw_aug, vw,
                   jnp.asarray(_overlap(nch, n_sel), BF16), gates, B, S)

    h2 = _mix(conv_in, nsa_out, x2, conv_w, gn_conv_g.reshape(1, CONV_CH),
              gn_nsa_g.reshape(1, N_HEADS * HEAD_DIM), w_out.astype(BF16), S)
    return h2


def kernel(x, norm1_g, w_in, conv_w, k_cmp_pos, k_cmp_w1, k_cmp_w2, v_cmp_pos, v_cmp_w1, v_cmp_w2,
           gn_conv_g, gn_nsa_g, w_out, norm2_g, w_gate, w_up, w_down, norm_f_g):
    B, S, _ = x.shape
    depth = norm1_g.shape[0]
    assert depth == 1, "the final norm is fused into the (single) layer's FFN kernel"
    h2 = _layer(x, norm1_g[0], w_in[0], conv_w[0], k_cmp_pos[0], k_cmp_w1[0], k_cmp_w2[0],
                v_cmp_pos[0], v_cmp_w1[0], v_cmp_w2[0], gn_conv_g[0], gn_nsa_g[0], w_out[0],
                norm2_g[0], w_gate[0], w_up[0], w_down[0])
    out = _ffn(h2, norm2_g[0].reshape(1, D_MODEL), w_gate[0].astype(BF16), w_up[0].astype(BF16),
               w_down[0].astype(BF16), norm_f_g.reshape(1, D_MODEL))
    return out.reshape(B, S, D_MODEL)
```

```python
import functools

import numpy as np
import jax
import jax.numpy as jnp
from jax import lax
from jax.experimental import pallas as pl
from jax.experimental.pallas import tpu as pltpu

D_MODEL = 1024
CONV_CH = 512
CONV_K = 3
N_HEADS = 8
HEAD_DIM = 64
N_KV_HEADS = 2
GQA = 4
N_BRANCH = 3
CMP_BLOCK = 32
CMP_STRIDE = 16
CMP_HIDDEN = 256
SEL_BLOCK = 64
SEL_TOP = 16
WINDOW = 512
D_FF = 2816
NORM_EPS = 1e-6

F32 = jnp.float32
BF16 = jnp.bfloat16

LANES = 128
SUBLANES = 8
VMEM_LIMIT = 56 * 1024 * 1024

TQ = 128
TK = 512
NEG = -1e30
MASK_BIAS = -float(2 ** 30)

_CONV_W = 3 * CONV_CH
_Q_OFF = _CONV_W
_KVC_OFF = _Q_OFF + N_HEADS * HEAD_DIM
_KV_OFF = _KVC_OFF + 2 * N_KV_HEADS * HEAD_DIM
_G_OFF = _KV_OFF + 4 * N_KV_HEADS * HEAD_DIM
_G_W = N_KV_HEADS * LANES
_PROJ_W = _G_OFF + _G_W


def _rms(x, g):
    return x * lax.rsqrt(jnp.mean(x * x, axis=-1, keepdims=True) + NORM_EPS) * g


def _dot(a, b):
    return jnp.dot(a, b, preferred_element_type=F32)


def _dot_nt(a, b):
    return lax.dot_general(a, b, (((1,), (1,)), ((), ())), preferred_element_type=F32)


def _inproj_kernel(x_ref, g_ref, w_ref, kf_ref, conv_ref, q_ref, gate_ref, kvc_ref,
                   ks_ref, vs_ref, kw_ref, vw_ref):
    u = _rms(x_ref[...], g_ref[...]).astype(BF16)
    conv_ref[...] = _dot(u, w_ref[:, 0:_CONV_W])
    q_ref[...] = _dot(u, w_ref[:, _Q_OFF:_KVC_OFF]).astype(BF16)
    kvc_ref[...] = _dot(u, w_ref[:, _KVC_OFF:_KV_OFF])
    gate_ref[...] = jax.nn.sigmoid(_dot(u, w_ref[:, _G_OFF:_PROJ_W]))
    kv = _dot(u, w_ref[:, _KV_OFF:_G_OFF]).astype(BF16)
    kf = kf_ref[...]
    hd, kvw = HEAD_DIM, N_KV_HEADS * HEAD_DIM
    for h in range(N_KV_HEADS):
        ks_ref[h] = kf
        ks_ref[h, :, 0:hd] = kv[:, h * hd:(h + 1) * hd]
        vs_ref[h] = kv[:, kvw + h * hd:kvw + (h + 1) * hd]
        kw_ref[h] = kf[:, 0:LANES]
        kw_ref[h, :, 0:hd] = kv[:, 2 * kvw + h * hd:2 * kvw + (h + 1) * hd]
        vw_ref[h] = kv[:, 3 * kvw + h * hd:3 * kvw + (h + 1) * hd]


def _inproj(x2, g, w, kfeat, tm=512):
    rows = x2.shape[0]
    seq_tiles = kfeat.shape[0] // tm
    row = lambda i: (i, 0)
    head_row = lambda i: (0, i, 0)
    return pl.pallas_call(
        _inproj_kernel,
        grid=(rows // tm,),
        in_specs=[
            pl.BlockSpec((tm, D_MODEL), row),
            pl.BlockSpec((1, D_MODEL), lambda i: (0, 0)),
            pl.BlockSpec((D_MODEL, _PROJ_W), lambda i: (0, 0), pipeline_mode=pl.Buffered(1)),
            pl.BlockSpec((tm, 2 * LANES), lambda i: (i % seq_tiles, 0)),
        ],
        out_specs=[
            pl.BlockSpec((tm, _CONV_W), row),
            pl.BlockSpec((tm, N_HEADS * HEAD_DIM), row),
            pl.BlockSpec((tm, _G_W), row),
            pl.BlockSpec((tm, _KV_OFF - _KVC_OFF), row),
            pl.BlockSpec((N_KV_HEADS, tm, 2 * LANES), head_row),
            pl.BlockSpec((N_KV_HEADS, tm, HEAD_DIM), head_row),
            pl.BlockSpec((N_KV_HEADS, tm, LANES), head_row),
            pl.BlockSpec((N_KV_HEADS, tm, HEAD_DIM), head_row),
        ],
        out_shape=[
            jax.ShapeDtypeStruct((rows, _CONV_W), F32),
            jax.ShapeDtypeStruct((rows, N_HEADS * HEAD_DIM), BF16),
            jax.ShapeDtypeStruct((rows, _G_W), F32),
            jax.ShapeDtypeStruct((rows, _KV_OFF - _KVC_OFF), F32),
            jax.ShapeDtypeStruct((N_KV_HEADS, rows, 2 * LANES), BF16),
            jax.ShapeDtypeStruct((N_KV_HEADS, rows, HEAD_DIM), BF16),
            jax.ShapeDtypeStruct((N_KV_HEADS, rows, LANES), BF16),
            jax.ShapeDtypeStruct((N_KV_HEADS, rows, HEAD_DIM), BF16),
        ],
        compiler_params=pltpu.CompilerParams(
            dimension_semantics=("arbitrary",), vmem_limit_bytes=VMEM_LIMIT),
        name="inproj",
    )(x2, g, w, kfeat)


def _compress_kernel(xk_ref, xv_ref, pos_ref, w1_ref, wl_ref, w2_ref, cf_ref, kc_ref, vc_ref):
    nch = xk_ref.shape[0] // CMP_STRIDE
    hd = HEAD_DIM
    acc = [jnp.zeros((nch, 2 * CMP_HIDDEN), F32) for _ in range(2 * N_KV_HEADS)]
    for l in range(CMP_STRIDE):
        for s, x_ref in enumerate((xk_ref, xv_ref)):
            xl = x_ref[pl.ds(l, nch, stride=CMP_STRIDE), :].astype(BF16)
            for h in range(N_KV_HEADS):
                c = s * N_KV_HEADS + h
                acc[c] = acc[c] + _dot(xl[:, h * hd:(h + 1) * hd], wl_ref[s, l])
    for s in range(2):
        pb = _dot(pos_ref[s], w1_ref[s])[0:1, :]
        for h in range(N_KV_HEADS):
            a = acc[s * N_KV_HEADS + h]
            hid = a[:, 0:CMP_HIDDEN] + pltpu.roll(a[:, CMP_HIDDEN:], nch - 1, axis=0) + pb
            out = _dot(jax.nn.gelu(hid, approximate=True).astype(BF16), w2_ref[s]).astype(BF16)
            if s == 0:
                kc_ref[h] = cf_ref[...]
                kc_ref[h, :, 0:hd] = out
            else:
                vc_ref[h] = out


def _compress(kvc, pos, w1, wl, w2, cfeat, B, S):
    nch = S // CMP_STRIDE
    full = lambda *shape: pl.BlockSpec(shape, lambda b: (0,) * len(shape))
    return pl.pallas_call(
        _compress_kernel,
        grid=(B,),
        in_specs=[
            pl.BlockSpec((S, LANES), lambda b: (b, 0)),
            pl.BlockSpec((S, LANES), lambda b: (b, 1)),
            full(*pos.shape), full(*w1.shape), full(*wl.shape), full(*w2.shape), full(*cfeat.shape),
        ],
        out_specs=[
            pl.BlockSpec((None, N_KV_HEADS, nch, LANES), lambda b: (b, 0, 0, 0)),
            pl.BlockSpec((None, N_KV_HEADS, nch, HEAD_DIM), lambda b: (b, 0, 0, 0)),
        ],
        out_shape=[
            jax.ShapeDtypeStruct((B, N_KV_HEADS, nch, LANES), BF16),
            jax.ShapeDtypeStruct((B, N_KV_HEADS, nch, HEAD_DIM), BF16),
        ],
        compiler_params=pltpu.CompilerParams(
            dimension_semantics=("arbitrary",), vmem_limit_bytes=VMEM_LIMIT),
        name="compress",
    )(kvc, kvc, pos, w1, wl, w2, cfeat)


def _softmax_rows(s, mask):
    s = jnp.where(mask, s, NEG)
    m = jnp.max(s, axis=1, keepdims=True)
    e = jnp.where(mask, jnp.exp(s - m), 0.0)
    l = jnp.sum(e, axis=1, keepdims=True)
    return e / jnp.maximum(l, 1e-30)


def _exchange(v, i, j, descending):
    hi, lo = jnp.maximum(v[i], v[j]), jnp.minimum(v[i], v[j])
    v[i], v[j] = (hi, lo) if descending else (lo, hi)


def _bitonic_merge_desc(v):
    n = len(v)
    j = n // 2
    while j >= 1:
        for i in range(n):
            if i ^ j > i:
                _exchange(v, i, i ^ j, True)
        j //= 2


def _kth_largest(x, kth):
    n = x.shape[0]
    assert n == kth * SUBLANES and kth & (kth - 1) == 0
    v = [x[r * SUBLANES:(r + 1) * SUBLANES, :] for r in range(kth)]
    k = 2
    while k <= kth:
        j = k // 2
        while j >= 1:
            for i in range(kth):
                if i ^ j > i:
                    _exchange(v, i, i ^ j, (i & k) == 0)
            j //= 2
        k *= 2
    shift = SUBLANES // 2
    while shift >= 1:
        v = [jnp.maximum(v[i], pltpu.roll(v[kth - 1 - i], shift, axis=0)) for i in range(kth)]
        if shift > 1:
            _bitonic_merge_desc(v)
        shift //= 2
    out = v[0]
    for i in range(1, kth):
        out = jnp.minimum(out, v[i])
    return out[0:1, :]


def _nsa_kernel(q_ref, qf_ref, kc_ref, vc_ref, ks_ref, vs_ref, kw_ref, vw_ref, ov_ref, g_ref,
                o_ref, qa_ref, tiles_ref):
    i = pl.program_id(2)
    t0 = i * TQ
    rows = GQA * TQ
    n_sel = ov_ref.shape[1]
    ncmp = kc_ref.shape[0]

    q = q_ref[...]
    for g in range(GQA):
        qg = q[:, g * HEAD_DIM:(g + 1) * HEAD_DIM].astype(F32) * (HEAD_DIM ** -0.5)
        qa_ref[g * TQ:(g + 1) * TQ, 0:HEAD_DIM] = qg.astype(BF16)
    qa_ref[:, HEAD_DIM:LANES] = qf_ref[...]
    qa1 = qa_ref[:, 0:LANES]

    def row_t(shape):
        r = lax.broadcasted_iota(jnp.int32, shape, 0)
        return t0 + (r & (TQ - 1))

    diag_ok = (lax.broadcasted_iota(jnp.int32, (rows, TQ), 1)
               <= (lax.broadcasted_iota(jnp.int32, (rows, TQ), 0) & (TQ - 1)))

    wk = TQ + WINDOW
    w0 = pl.multiple_of(t0, TQ)
    s_w = _dot_nt(qa1, kw_ref[pl.ds(w0, wk), :])
    s_old = jnp.where(diag_ok, NEG, s_w[:, 0:TQ])
    s_mid = s_w[:, TQ:WINDOW]
    s_new = jnp.where(diag_ok, s_w[:, WINDOW:wk], NEG)
    m_w = jnp.maximum(jnp.maximum(jnp.max(s_old, axis=1, keepdims=True),
                                  jnp.max(s_mid, axis=1, keepdims=True)),
                      jnp.max(s_new, axis=1, keepdims=True))
    e_w = jnp.concatenate([jnp.exp(s_old - m_w), jnp.exp(s_mid - m_w), jnp.exp(s_new - m_w)], axis=1)
    l_w = jnp.sum(e_w, axis=1, keepdims=True)
    o_w = _dot(e_w.astype(BF16), vw_ref[pl.ds(w0, wk), :]) / l_w

    s_c = _dot_nt(qa1, kc_ref[...])
    n_idx = lax.broadcasted_iota(jnp.int32, (rows, ncmp), 1)
    mask_c = (CMP_STRIDE * n_idx + (CMP_BLOCK - 1)) <= row_t((rows, ncmp))
    p_c = _softmax_rows(s_c, mask_c)
    o_c = _dot(p_c.astype(BF16), vc_ref[...])

    p_sum = p_c[0:TQ] + p_c[TQ:2 * TQ] + p_c[2 * TQ:3 * TQ] + p_c[3 * TQ:4 * TQ]
    p_hi = p_sum.astype(BF16)
    p_lo = (p_sum - p_hi.astype(F32)).astype(BF16)
    imp = _dot(p_hi, ov_ref[...]) + _dot(p_lo, ov_ref[...])
    imp_t = imp.T
    jb = lax.broadcasted_iota(jnp.int32, (n_sel, TQ), 0)
    tq = t0 + lax.broadcasted_iota(jnp.int32, (n_sel, TQ), 1)
    imp_t = jnp.where(SEL_BLOCK * jb > tq, -jnp.inf, imp_t)
    forced = (jb == 0) | (jb == (tq >> 6))
    imp_t = jnp.where(forced, jnp.inf, imp_t)

    thr = _kth_largest(imp_t, SEL_TOP)
    above = imp_t > thr
    tied = imp_t == thr
    n_above = jnp.sum(jnp.where(above, 1.0, 0.0), axis=0, keepdims=True)
    lower = (lax.broadcasted_iota(jnp.int32, (n_sel, n_sel), 1)
             < lax.broadcasted_iota(jnp.int32, (n_sel, n_sel), 0))
    tied_before = _dot(jnp.where(lower, 1.0, 0.0).astype(BF16), jnp.where(tied, 1.0, 0.0).astype(BF16))
    take_tie = jnp.where(tied_before + n_above < float(SEL_TOP), 1.0, 0.0)
    sel_t = jnp.where(above, 1.0, jnp.where(tied, take_tie, 0.0))
    bias = ((1.0 - sel_t) * MASK_BIAS).T.astype(BF16)
    for g in range(GQA):
        qa_ref[g * TQ:(g + 1) * TQ, LANES:2 * LANES] = bias

    blocks_per_tile = TK // SEL_BLOCK
    n_full = t0 // TK
    n_act = jnp.int32(0)
    for kt in range(n_sel // blocks_per_tile - 1):
        used = jnp.max(sel_t[kt * blocks_per_tile:(kt + 1) * blocks_per_tile, :]) > 0.5
        tiles_ref[n_act] = kt
        n_act = n_act + jnp.where(used & (kt < n_full), 1, 0)

    qa2 = qa_ref[...]

    def sel_tile(kt, carry, causal):
        m, l, acc = carry
        k0 = pl.multiple_of(kt * TK, TK)
        s = _dot_nt(qa2, ks_ref[pl.ds(k0, TK), :])
        if causal:
            kpos = k0 + lax.broadcasted_iota(jnp.int32, (rows, TK), 1)
            s = jnp.where(kpos <= row_t((rows, TK)), s, NEG)
        m_new = jnp.maximum(m, jnp.max(s, axis=1, keepdims=True))
        alpha = jnp.exp(m - m_new)
        p = jnp.exp(s - m_new)
        l = alpha * l + jnp.sum(p, axis=1, keepdims=True)
        acc = alpha * acc + _dot(p.astype(BF16), vs_ref[pl.ds(k0, TK), :])
        return m_new, l, acc

    init = (jnp.full((rows, 1), NEG, F32), jnp.zeros((rows, 1), F32), jnp.zeros((rows, HEAD_DIM), F32))
    carry = lax.fori_loop(0, n_act, lambda j, c: sel_tile(tiles_ref[j], c, False), init)
    _, l_s, acc_s = sel_tile(n_full, carry, True)
    o_s = acc_s / l_s

    gt = g_ref[...]
    for g in range(GQA):
        sl = slice(g * TQ, (g + 1) * TQ)
        c0 = g * N_BRANCH
        o = (gt[:, c0:c0 + 1] * o_c[sl] + gt[:, c0 + 1:c0 + 2] * o_s[sl]
             + gt[:, c0 + 2:c0 + 3] * o_w[sl])
        o_ref[:, g * HEAD_DIM:(g + 1) * HEAD_DIM] = o


def _nsa(q, qfeat, kc_aug, vc, ks_aug, vs, kw_aug, vw, ov, gates, B, S):
    nq = S // TQ
    gw = GQA * HEAD_DIM
    ncmp = kc_aug.shape[2]
    n_sel = ov.shape[1]
    sp = kw_aug.shape[2]
    bk = lambda b, k, i: (b, k, 0, 0)
    kb = lambda b, k, i: (k, b, 0, 0)
    return pl.pallas_call(
        _nsa_kernel,
        grid=(B, N_KV_HEADS, nq),
        in_specs=[
            pl.BlockSpec((TQ, gw), lambda b, k, i: (b * nq + i, k)),
            pl.BlockSpec((None, GQA * TQ, HEAD_DIM), lambda b, k, i: (k, 0, 0)),
            pl.BlockSpec((None, None, ncmp, LANES), bk),
            pl.BlockSpec((None, None, ncmp, HEAD_DIM), bk),
            pl.BlockSpec((None, None, S, 2 * LANES), kb),
            pl.BlockSpec((None, None, S, HEAD_DIM), kb),
            pl.BlockSpec((None, None, sp, LANES), kb),
            pl.BlockSpec((None, None, sp, HEAD_DIM), kb),
            pl.BlockSpec((ncmp, n_sel), lambda b, k, i: (0, 0)),
            pl.BlockSpec((TQ, LANES), lambda b, k, i: (b * nq + i, k)),
        ],
        out_specs=pl.BlockSpec((TQ, gw), lambda b, k, i: (b * nq + i, k)),
        out_shape=jax.ShapeDtypeStruct((B * S, N_KV_HEADS * gw), F32),
        scratch_shapes=[
            pltpu.VMEM((GQA * TQ, 2 * LANES), BF16),
            pltpu.SMEM((S // TK,), jnp.int32),
        ],
        compiler_params=pltpu.CompilerParams(
            dimension_semantics=("arbitrary",) * 3, vmem_limit_bytes=VMEM_LIMIT),
        name="nsa",
    )(q, qfeat, kc_aug, vc, ks_aug, vs, kw_aug, vw, ov, gates)


def _mix_kernel(S, tm, cb_ref, cc_ref, ch_ref, ccp_ref, chp_ref, nsa_ref, x_ref, cw_ref,
                gc_ref, gn_ref, wo_ref, h_ref):
    i = pl.program_id(0)
    u = cc_ref[...] * ch_ref[...]
    prev = ccp_ref[...] * chp_ref[...]
    first = (i * tm) % S == 0
    prev = jnp.where(first, 0.0, prev)
    r = lax.broadcasted_iota(jnp.int32, u.shape, 0)
    u1 = jnp.where(r == 0, prev[7:8, :], pltpu.roll(u, 1, axis=0))
    u2 = pltpu.roll(u, 2, axis=0)
    u2 = jnp.where(r == 0, prev[6:7, :], jnp.where(r == 1, prev[7:8, :], u2))
    cw = cw_ref[...]
    conv = cw[0:1, :] * u2 + cw[1:2, :] * u1 + cw[2:3, :] * u
    y = cb_ref[...] * conv
    mixed_c = _rms(y, gc_ref[...]).astype(BF16)
    mixed_n = _rms(nsa_ref[...], gn_ref[...]).astype(BF16)
    h_ref[...] = (x_ref[...] + _dot(mixed_c, wo_ref[0:CONV_CH, :])
                  + _dot(mixed_n, wo_ref[CONV_CH:2 * CONV_CH, :]))


def _mix(conv_in, nsa_out, x2, conv_w, gc, gn, wo, S, tm=512):
    rows = x2.shape[0]
    prev_map = lambda c: (lambda i: (jnp.maximum(i * (tm // 8) - 1, 0), c))
    return pl.pallas_call(
        functools.partial(_mix_kernel, S, tm),
        grid=(rows // tm,),
        in_specs=[
            pl.BlockSpec((tm, CONV_CH), lambda i: (i, 0)),
            pl.BlockSpec((tm, CONV_CH), lambda i: (i, 1)),
            pl.BlockSpec((tm, CONV_CH), lambda i: (i, 2)),
            pl.BlockSpec((8, CONV_CH), prev_map(1)),
            pl.BlockSpec((8, CONV_CH), prev_map(2)),
            pl.BlockSpec((tm, CONV_CH), lambda i: (i, 0)),
            pl.BlockSpec((tm, D_MODEL), lambda i: (i, 0)),
            pl.BlockSpec((CONV_K, CONV_CH), lambda i: (0, 0)),
            pl.BlockSpec((1, CONV_CH), lambda i: (0, 0)),
            pl.BlockSpec((1, CONV_CH), lambda i: (0, 0)),
            pl.BlockSpec((2 * CONV_CH, D_MODEL), lambda i: (0, 0), pipeline_mode=pl.Buffered(1)),
        ],
        out_specs=pl.BlockSpec((tm, D_MODEL), lambda i: (i, 0)),
        out_shape=jax.ShapeDtypeStruct((rows, D_MODEL), F32),
        compiler_params=pltpu.CompilerParams(
            dimension_semantics=("arbitrary",), vmem_limit_bytes=VMEM_LIMIT),
        name="mix",
    )(conv_in, conv_in, conv_in, conv_in, conv_in, nsa_out, x2, conv_w, gc, gn, wo)


_FF_CHUNK = 1408


def _ffn_kernel(h_ref, g2_ref, wg_ref, wu_ref, wd_ref, gf_ref, o_ref):
    h = h_ref[...]
    u = _rms(h, g2_ref[...]).astype(BF16)
    acc = h
    for c in range(D_FF // _FF_CHUNK):
        sl = slice(c * _FF_CHUNK, (c + 1) * _FF_CHUNK)
        a = jax.nn.silu(_dot(u, wg_ref[:, sl])) * _dot(u, wu_ref[:, sl])
        acc = acc + _dot(a.astype(BF16), wd_ref[sl, :])
    o_ref[...] = _rms(acc, gf_ref[...])


def _ffn(h, g2, wg, wu, wd, gf, tm=512):
    rows = h.shape[0]
    const = lambda i: (0, 0)
    return pl.pallas_call(
        _ffn_kernel,
        grid=(rows // tm,),
        in_specs=[
            pl.BlockSpec((tm, D_MODEL), lambda i: (i, 0)),
            pl.BlockSpec((1, D_MODEL), const),
            pl.BlockSpec((D_MODEL, D_FF), const, pipeline_mode=pl.Buffered(1)),
            pl.BlockSpec((D_MODEL, D_FF), const, pipeline_mode=pl.Buffered(1)),
            pl.BlockSpec((D_FF, D_MODEL), const, pipeline_mode=pl.Buffered(1)),
            pl.BlockSpec((1, D_MODEL), const),
        ],
        out_specs=pl.BlockSpec((tm, D_MODEL), lambda i: (i, 0)),
        out_shape=jax.ShapeDtypeStruct((rows, D_MODEL), F32),
        compiler_params=pltpu.CompilerParams(
            dimension_semantics=("arbitrary",), vmem_limit_bytes=VMEM_LIMIT),
        name="ffn",
    )(h, g2, wg, wu, wd, gf)


def _key_features(pos, n_sel):
    f = np.zeros((pos.shape[0], LANES + n_sel), np.float32)
    f[:, HEAD_DIM] = pos // SEL_BLOCK
    f[:, HEAD_DIM + 1] = pos % SEL_BLOCK
    if n_sel:
        f[np.arange(pos.shape[0]), LANES + pos // SEL_BLOCK] = 1.0
    return f


def _query_features():
    f = np.zeros((N_KV_HEADS, GQA * TQ, HEAD_DIM), np.float32)
    for k in range(N_KV_HEADS):
        for g in range(GQA):
            slope = 2.0 ** (-8.0 * (k * GQA + g + 1) / N_HEADS)
            f[k, g * TQ:(g + 1) * TQ, 0] = slope * SEL_BLOCK
            f[k, g * TQ:(g + 1) * TQ, 1] = slope
            f[k, g * TQ:(g + 1) * TQ, 2] = MASK_BIAS
    return f


def _overlap(ncmp_pad, n_sel):
    nc = ncmp_pad - 1
    c_start = CMP_STRIDE * np.arange(nc)
    c_end = c_start + CMP_BLOCK - 1
    s_start = SEL_BLOCK * np.arange(n_sel)
    ov = np.minimum(c_end[:, None] + 1, s_start[None, :] + SEL_BLOCK) - np.maximum(c_start[:, None], s_start[None, :])
    out = np.zeros((ncmp_pad, n_sel), np.float32)
    out[:nc] = np.clip(ov, 0, None).astype(np.float32) / CMP_BLOCK
    return out


def _layer(h, norm1_g, w_in, conv_w, k_cmp_pos, k_cmp_w1, k_cmp_w2, v_cmp_pos, v_cmp_w1, v_cmp_w2,
           gn_conv_g, gn_nsa_g, w_out):
    B, S, _ = h.shape
    rows = B * S
    x2 = h.reshape(rows, D_MODEL)
    n_sel = S // SEL_BLOCK
    nch = S // CMP_STRIDE
    hd = HEAD_DIM

    ng = GQA * N_BRANCH
    w_main, w_g = w_in[:, :_G_OFF], w_in[:, _G_OFF:]
    gpad = jnp.zeros((D_MODEL, LANES - ng), w_in.dtype)
    w_pad = jnp.concatenate([w_main, w_g[:, :ng], gpad, w_g[:, ng:], gpad], axis=1).astype(BF16)

    kfeat = jnp.asarray(_key_features(np.arange(S), n_sel), BF16)
    conv_in, q, gates, kvc, ks_aug, vs, kw_aug, vw = _inproj(
        x2, norm1_g.reshape(1, D_MODEL), w_pad, kfeat)

    w1 = jnp.stack([k_cmp_w1, v_cmp_w1]).astype(BF16)
    wl = w1.reshape(2, 2, CMP_STRIDE, hd, CMP_HIDDEN).transpose(0, 2, 3, 1, 4)
    wl = wl.reshape(2, CMP_STRIDE, hd, 2 * CMP_HIDDEN)
    w2 = jnp.stack([k_cmp_w2, v_cmp_w2]).astype(BF16)
    pos = jnp.stack([k_cmp_pos, v_cmp_pos]).reshape(2, 1, CMP_BLOCK * hd)
    pos = jnp.broadcast_to(pos, (2, SUBLANES, CMP_BLOCK * hd)).astype(BF16)
    c_end = CMP_STRIDE * np.arange(nch) + CMP_BLOCK - 1
    cfeat = jnp.asarray(_key_features(c_end, 0), BF16)
    kc_aug, vc = _compress(kvc, pos, w1, wl, w2, cfeat, B, S)

    pad_rows = np.zeros((WINDOW, LANES), np.float32)
    pad_rows[:, hd + 2] = 1.0
    pad_k = jnp.broadcast_to(jnp.asarray(pad_rows, BF16), (N_KV_HEADS, B, WINDOW, LANES))
    kw_aug = jnp.concatenate([pad_k, kw_aug.reshape(N_KV_HEADS, B, S, LANES)], axis=2)
    vw = jnp.pad(vw.reshape(N_KV_HEADS, B, S, hd), ((0, 0), (0, 0), (WINDOW, 0), (0, 0)))

    nsa_out = _nsa(q, jnp.asarray(_query_features(), BF16), kc_aug, vc,
                   ks_aug.reshape(N_KV_HEADS, B, S, 2 * LANES), vs.reshape(N_KV_HEADS, B, S, hd),
                   kw_aug, vw, jnp.asarray(_overlap(nch, n_sel), BF16), gates, B, S)

    return _mix(conv_in, nsa_out, x2, conv_w, gn_conv_g.reshape(1, CONV_CH),
                gn_nsa_g.reshape(1, N_HEADS * hd), w_out.astype(BF16), S)


def kernel(x, norm1_g, w_in, conv_w, k_cmp_pos, k_cmp_w1, k_cmp_w2, v_cmp_pos, v_cmp_w1, v_cmp_w2,
           gn_conv_g, gn_nsa_g, w_out, norm2_g, w_gate, w_up, w_down, norm_f_g):
    B, S, _ = x.shape
    depth = norm1_g.shape[0]
    assert depth == 1, "the final norm is fused into the (single) layer's FFN kernel"
    h2 = _layer(x, norm1_g[0], w_in[0], conv_w[0], k_cmp_pos[0], k_cmp_w1[0], k_cmp_w2[0],
                v_cmp_pos[0], v_cmp_w1[0], v_cmp_w2[0], gn_conv_g[0], gn_nsa_g[0], w_out[0])
    out = _ffn(h2, norm2_g[0].reshape(1, D_MODEL), w_gate[0].astype(BF16), w_up[0].astype(BF16),
               w_down[0].astype(BF16), norm_f_g.reshape(1, D_MODEL))
    return out.reshape(B, S, D_MODEL)
```

```python
import functools

import numpy as np
import jax
import jax.numpy as jnp
from jax import lax
from jax.experimental import pallas as pl
from jax.experimental.pallas import tpu as pltpu

D_MODEL = 1024
CONV_CH = 512
CONV_K = 3
N_HEADS = 8
HEAD_DIM = 64
N_KV_HEADS = 2
GQA = 4
N_BRANCH = 3
CMP_BLOCK = 32
CMP_STRIDE = 16
CMP_HIDDEN = 256
SEL_BLOCK = 64
SEL_TOP = 16
WINDOW = 512
D_FF = 2816
NORM_EPS = 1e-6

F32 = jnp.float32
BF16 = jnp.bfloat16

LANES = 128
SUBLANES = 8
VMEM_LIMIT = 56 * 1024 * 1024

TQ = 128
TK = 512
NEG = -1e30
MASK_BIAS = -float(2 ** 30)

_CONV_W = 3 * CONV_CH
_Q_OFF = _CONV_W
_KVC_OFF = _Q_OFF + N_HEADS * HEAD_DIM
_KV_OFF = _KVC_OFF + 2 * N_KV_HEADS * HEAD_DIM
_G_OFF = _KV_OFF + 4 * N_KV_HEADS * HEAD_DIM
_G_W = N_KV_HEADS * LANES
_PROJ_W = _G_OFF + _G_W


def _rms(x, g):
    return x * lax.rsqrt(jnp.mean(x * x, axis=-1, keepdims=True) + NORM_EPS) * g


def _dot(a, b):
    return jnp.dot(a, b, preferred_element_type=F32)


def _ones_column(n):
    lane = lax.broadcasted_iota(jnp.int32, (n, LANES), 1)
    return jnp.where(lane == HEAD_DIM, 1.0, 0.0).astype(BF16)


def _dot_nt(a, b):
    return lax.dot_general(a, b, (((1,), (1,)), ((), ())), preferred_element_type=F32)


def _inproj_kernel(x_ref, g_ref, w_ref, kf_ref, conv_ref, q_ref, gate_ref, kvc_ref,
                   ks_ref, vs_ref, kw_ref, vw_ref):
    u = _rms(x_ref[...], g_ref[...]).astype(BF16)
    conv_ref[...] = _dot(u, w_ref[:, 0:_CONV_W])
    q_ref[...] = _dot(u, w_ref[:, _Q_OFF:_KVC_OFF]).astype(BF16)
    kvc_ref[...] = _dot(u, w_ref[:, _KVC_OFF:_KV_OFF])
    gate_ref[...] = jax.nn.sigmoid(_dot(u, w_ref[:, _G_OFF:_PROJ_W]))
    kv = _dot(u, w_ref[:, _KV_OFF:_G_OFF]).astype(BF16)
    kf = kf_ref[...]
    ones_col = _ones_column(kv.shape[0])
    hd, kvw = HEAD_DIM, N_KV_HEADS * HEAD_DIM
    for h in range(N_KV_HEADS):
        ks_ref[h] = kf
        ks_ref[h, :, 0:hd] = kv[:, h * hd:(h + 1) * hd]
        vs_ref[h] = ones_col
        vs_ref[h, :, 0:hd] = kv[:, kvw + h * hd:kvw + (h + 1) * hd]
        kw_ref[h] = kf[:, 0:LANES]
        kw_ref[h, :, 0:hd] = kv[:, 2 * kvw + h * hd:2 * kvw + (h + 1) * hd]
        vw_ref[h] = ones_col
        vw_ref[h, :, 0:hd] = kv[:, 3 * kvw + h * hd:3 * kvw + (h + 1) * hd]


def _inproj(x2, g, w, kfeat, tm=512):
    rows = x2.shape[0]
    seq_tiles = kfeat.shape[0] // tm
    row = lambda i: (i, 0)
    head_row = lambda i: (0, i, 0)
    return pl.pallas_call(
        _inproj_kernel,
        grid=(rows // tm,),
        in_specs=[
            pl.BlockSpec((tm, D_MODEL), row),
            pl.BlockSpec((1, D_MODEL), lambda i: (0, 0)),
            pl.BlockSpec((D_MODEL, _PROJ_W), lambda i: (0, 0), pipeline_mode=pl.Buffered(1)),
            pl.BlockSpec((tm, 2 * LANES), lambda i: (i % seq_tiles, 0)),
        ],
        out_specs=[
            pl.BlockSpec((tm, _CONV_W), row),
            pl.BlockSpec((tm, N_HEADS * HEAD_DIM), row),
            pl.BlockSpec((tm, _G_W), row),
            pl.BlockSpec((tm, _KV_OFF - _KVC_OFF), row),
            pl.BlockSpec((N_KV_HEADS, tm, 2 * LANES), head_row),
            pl.BlockSpec((N_KV_HEADS, tm, LANES), head_row),
            pl.BlockSpec((N_KV_HEADS, tm, LANES), head_row),
            pl.BlockSpec((N_KV_HEADS, tm, LANES), head_row),
        ],
        out_shape=[
            jax.ShapeDtypeStruct((rows, _CONV_W), F32),
            jax.ShapeDtypeStruct((rows, N_HEADS * HEAD_DIM), BF16),
            jax.ShapeDtypeStruct((rows, _G_W), F32),
            jax.ShapeDtypeStruct((rows, _KV_OFF - _KVC_OFF), F32),
            jax.ShapeDtypeStruct((N_KV_HEADS, rows, 2 * LANES), BF16),
            jax.ShapeDtypeStruct((N_KV_HEADS, rows, LANES), BF16),
            jax.ShapeDtypeStruct((N_KV_HEADS, rows, LANES), BF16),
            jax.ShapeDtypeStruct((N_KV_HEADS, rows, LANES), BF16),
        ],
        compiler_params=pltpu.CompilerParams(
            dimension_semantics=("arbitrary",), vmem_limit_bytes=VMEM_LIMIT),
        name="inproj",
    )(x2, g, w, kfeat)


def _compress_kernel(xk_ref, xv_ref, pos_ref, w1_ref, wl_ref, w2_ref, cf_ref, kc_ref, vc_ref):
    nch = xk_ref.shape[0] // CMP_STRIDE
    hd = HEAD_DIM
    acc = [jnp.zeros((nch, 2 * CMP_HIDDEN), F32) for _ in range(2 * N_KV_HEADS)]
    for l in range(CMP_STRIDE):
        for s, x_ref in enumerate((xk_ref, xv_ref)):
            xl = x_ref[pl.ds(l, nch, stride=CMP_STRIDE), :].astype(BF16)
            for h in range(N_KV_HEADS):
                c = s * N_KV_HEADS + h
                acc[c] = acc[c] + _dot(xl[:, h * hd:(h + 1) * hd], wl_ref[s, l])
    for s in range(2):
        pb = _dot(pos_ref[s], w1_ref[s])[0:1, :]
        for h in range(N_KV_HEADS):
            a = acc[s * N_KV_HEADS + h]
            hid = a[:, 0:CMP_HIDDEN] + pltpu.roll(a[:, CMP_HIDDEN:], nch - 1, axis=0) + pb
            out = _dot(jax.nn.gelu(hid, approximate=True).astype(BF16), w2_ref[s]).astype(BF16)
            if s == 0:
                kc_ref[h] = cf_ref[...]
                kc_ref[h, :, 0:hd] = out
            else:
                vc_ref[h] = _ones_column(nch)
                vc_ref[h, :, 0:hd] = out


def _compress(kvc, pos, w1, wl, w2, cfeat, B, S):
    nch = S // CMP_STRIDE
    full = lambda *shape: pl.BlockSpec(shape, lambda b: (0,) * len(shape))
    return pl.pallas_call(
        _compress_kernel,
        grid=(B,),
        in_specs=[
            pl.BlockSpec((S, LANES), lambda b: (b, 0)),
            pl.BlockSpec((S, LANES), lambda b: (b, 1)),
            full(*pos.shape), full(*w1.shape), full(*wl.shape), full(*w2.shape), full(*cfeat.shape),
        ],
        out_specs=[
            pl.BlockSpec((None, N_KV_HEADS, nch, LANES), lambda b: (b, 0, 0, 0)),
            pl.BlockSpec((None, N_KV_HEADS, nch, LANES), lambda b: (b, 0, 0, 0)),
        ],
        out_shape=[
            jax.ShapeDtypeStruct((B, N_KV_HEADS, nch, LANES), BF16),
            jax.ShapeDtypeStruct((B, N_KV_HEADS, nch, LANES), BF16),
        ],
        compiler_params=pltpu.CompilerParams(
            dimension_semantics=("arbitrary",), vmem_limit_bytes=VMEM_LIMIT),
        name="compress",
    )(kvc, kvc, pos, w1, wl, w2, cfeat)


def _normalised(o):
    return o[:, 0:HEAD_DIM] / o[:, HEAD_DIM:HEAD_DIM + 1]


def _exchange(v, i, j, descending):
    hi, lo = jnp.maximum(v[i], v[j]), jnp.minimum(v[i], v[j])
    v[i], v[j] = (hi, lo) if descending else (lo, hi)


def _bitonic_merge_desc(v):
    n = len(v)
    j = n // 2
    while j >= 1:
        for i in range(n):
            if i ^ j > i:
                _exchange(v, i, i ^ j, True)
        j //= 2


def _kth_largest(x, kth):
    n = x.shape[0]
    assert n == kth * SUBLANES and kth & (kth - 1) == 0
    v = [x[r * SUBLANES:(r + 1) * SUBLANES, :] for r in range(kth)]
    k = 2
    while k <= kth:
        j = k // 2
        while j >= 1:
            for i in range(kth):
                if i ^ j > i:
                    _exchange(v, i, i ^ j, (i & k) == 0)
            j //= 2
        k *= 2
    shift = SUBLANES // 2
    while shift >= 1:
        v = [jnp.maximum(v[i], pltpu.roll(v[kth - 1 - i], shift, axis=0)) for i in range(kth)]
        if shift > 1:
            _bitonic_merge_desc(v)
        shift //= 2
    out = v[0]
    for i in range(1, kth):
        out = jnp.minimum(out, v[i])
    return out[0:1, :]


def _nsa_kernel(q_ref, qf_ref, kc_ref, vc_ref, ks_ref, vs_ref, kw_ref, vw_ref, ov_ref, cm_ref, dm_ref,
                g_ref, o_ref, qa_ref, tiles_ref):
    i = pl.program_id(2)
    t0 = i * TQ
    rows = GQA * TQ
    n_sel = ov_ref.shape[1]
    ncmp = kc_ref.shape[0]

    q = q_ref[...]
    for g in range(GQA):
        qg = q[:, g * HEAD_DIM:(g + 1) * HEAD_DIM].astype(F32) * (HEAD_DIM ** -0.5)
        qa_ref[g * TQ:(g + 1) * TQ, 0:HEAD_DIM] = qg.astype(BF16)
    qa_ref[:, HEAD_DIM:LANES] = qf_ref[...]
    qa1 = qa_ref[:, 0:LANES]

    diag_ok = (lax.broadcasted_iota(jnp.int32, (rows, TQ), 1)
               <= (lax.broadcasted_iota(jnp.int32, (rows, TQ), 0) & (TQ - 1)))

    wk = TQ + WINDOW
    w0 = pl.multiple_of(t0, TQ)
    s_w = _dot_nt(qa1, kw_ref[pl.ds(w0, wk), :])
    s_old = jnp.where(diag_ok, NEG, s_w[:, 0:TQ])
    s_mid = s_w[:, TQ:WINDOW]
    s_new = jnp.where(diag_ok, s_w[:, WINDOW:wk], NEG)
    m_w = jnp.maximum(jnp.maximum(jnp.max(s_old, axis=1, keepdims=True),
                                  jnp.max(s_mid, axis=1, keepdims=True)),
                      jnp.max(s_new, axis=1, keepdims=True))
    e_w = jnp.concatenate([jnp.exp(s_old - m_w), jnp.exp(s_mid - m_w), jnp.exp(s_new - m_w)], axis=1)
    o_w = _normalised(_dot(e_w.astype(BF16), vw_ref[pl.ds(w0, wk), :]))

    s_c = _dot_nt(qa1, kc_ref[...])
    s_c = jnp.where(cm_ref[...] <= (TQ // CMP_STRIDE) * i - 2, s_c, NEG)
    m_c = jnp.maximum(jnp.max(s_c, axis=1, keepdims=True), 0.5 * NEG)
    e_c = jnp.exp(s_c - m_c)
    oc = _dot(e_c.astype(BF16), vc_ref[...])
    inv_c = 1.0 / jnp.maximum(oc[:, HEAD_DIM:HEAD_DIM + 1], 1e-30)
    o_c = oc[:, 0:HEAD_DIM] * inv_c
    p_c = e_c * inv_c

    p_sum = p_c[0:TQ] + p_c[TQ:2 * TQ] + p_c[2 * TQ:3 * TQ] + p_c[3 * TQ:4 * TQ]
    p_hi = p_sum.astype(BF16)
    p_lo = (p_sum - p_hi.astype(F32)).astype(BF16)
    imp = _dot(p_hi, ov_ref[...]) + _dot(p_lo, ov_ref[...])
    imp_t = imp.T
    jb = lax.broadcasted_iota(jnp.int32, (n_sel, TQ), 0)
    tq = t0 + lax.broadcasted_iota(jnp.int32, (n_sel, TQ), 1)
    imp_t = jnp.where(SEL_BLOCK * jb > tq, -jnp.inf, imp_t)
    forced = (jb == 0) | (jb == (tq >> 6))
    imp_t = jnp.where(forced, jnp.inf, imp_t)

    thr = _kth_largest(imp_t, SEL_TOP)
    above = imp_t > thr
    tied = imp_t == thr
    n_above = jnp.sum(jnp.where(above, 1.0, 0.0), axis=0, keepdims=True)
    lower = (lax.broadcasted_iota(jnp.int32, (n_sel, n_sel), 1)
             < lax.broadcasted_iota(jnp.int32, (n_sel, n_sel), 0))
    tied_before = _dot(jnp.where(lower, 1.0, 0.0).astype(BF16), jnp.where(tied, 1.0, 0.0).astype(BF16))
    take_tie = jnp.where(tied_before + n_above < float(SEL_TOP), 1.0, 0.0)
    sel_t = jnp.where(above, 1.0, jnp.where(tied, take_tie, 0.0))
    bias = ((1.0 - sel_t) * MASK_BIAS).T.astype(BF16)
    for g in range(GQA):
        qa_ref[g * TQ:(g + 1) * TQ, LANES:2 * LANES] = bias

    blocks_per_tile = TK // SEL_BLOCK
    n_full = t0 // TK
    n_act = jnp.int32(0)
    for kt in range(n_sel // blocks_per_tile - 1):
        used = jnp.max(sel_t[kt * blocks_per_tile:(kt + 1) * blocks_per_tile, :]) > 0.5
        tiles_ref[n_act] = kt
        n_act = n_act + jnp.where(used & (kt < n_full), 1, 0)

    qa2 = qa_ref[...]

    def sel_tile(kt, carry, causal):
        m, acc = carry
        k0 = pl.multiple_of(kt * TK, TK)
        s = _dot_nt(qa2, ks_ref[pl.ds(k0, TK), :])
        if causal:
            s = jnp.where(dm_ref[...] <= t0 - k0, s, NEG)
        m_new = jnp.maximum(m, jnp.max(s, axis=1, keepdims=True))
        p = jnp.exp(s - m_new)
        acc = jnp.exp(m - m_new) * acc + _dot(p.astype(BF16), vs_ref[pl.ds(k0, TK), :])
        return m_new, acc

    init = (jnp.full((rows, 1), NEG, F32), jnp.zeros((rows, LANES), F32))
    carry = lax.fori_loop(0, n_act, lambda j, c: sel_tile(tiles_ref[j], c, False), init)
    o_s = _normalised(sel_tile(n_full, carry, True)[1])

    gt = g_ref[...]
    for g in range(GQA):
        sl = slice(g * TQ, (g + 1) * TQ)
        c0 = g * N_BRANCH
        o = (gt[:, c0:c0 + 1] * o_c[sl] + gt[:, c0 + 1:c0 + 2] * o_s[sl]
             + gt[:, c0 + 2:c0 + 3] * o_w[sl])
        o_ref[:, g * HEAD_DIM:(g + 1) * HEAD_DIM] = o


def _nsa(q, qfeat, kc_aug, vc, ks_aug, vs, kw_aug, vw, ov, cmask, dmask, gates, B, S):
    nq = S // TQ
    gw = GQA * HEAD_DIM
    ncmp = kc_aug.shape[2]
    n_sel = ov.shape[1]
    sp = kw_aug.shape[2]
    bk = lambda b, k, i: (b, k, 0, 0)
    kb = lambda b, k, i: (k, b, 0, 0)
    return pl.pallas_call(
        _nsa_kernel,
        grid=(B, N_KV_HEADS, nq),
        in_specs=[
            pl.BlockSpec((TQ, gw), lambda b, k, i: (b * nq + i, k)),
            pl.BlockSpec((None, GQA * TQ, HEAD_DIM), lambda b, k, i: (k, 0, 0)),
            pl.BlockSpec((None, None, ncmp, LANES), bk),
            pl.BlockSpec((None, None, ncmp, LANES), bk),
            pl.BlockSpec((None, None, S, 2 * LANES), kb),
            pl.BlockSpec((None, None, S, LANES), kb),
            pl.BlockSpec((None, None, sp, LANES), kb),
            pl.BlockSpec((None, None, sp, LANES), kb),
            pl.BlockSpec((ncmp, n_sel), lambda b, k, i: (0, 0)),
            pl.BlockSpec((GQA * TQ, ncmp), lambda b, k, i: (0, 0)),
            pl.BlockSpec((GQA * TQ, TK), lambda b, k, i: (0, 0)),
            pl.BlockSpec((TQ, LANES), lambda b, k, i: (b * nq + i, k)),
        ],
        out_specs=pl.BlockSpec((TQ, gw), lambda b, k, i: (b * nq + i, k)),
        out_shape=jax.ShapeDtypeStruct((B * S, N_KV_HEADS * gw), F32),
        scratch_shapes=[
            pltpu.VMEM((GQA * TQ, 2 * LANES), BF16),
            pltpu.SMEM((S // TK,), jnp.int32),
        ],
        compiler_params=pltpu.CompilerParams(
            dimension_semantics=("arbitrary",) * 3, vmem_limit_bytes=VMEM_LIMIT),
        name="nsa",
    )(q, qfeat, kc_aug, vc, ks_aug, vs, kw_aug, vw, ov, cmask, dmask, gates)


def _mix_kernel(S, tm, cb_ref, cc_ref, ch_ref, ccp_ref, chp_ref, nsa_ref, x_ref, cw_ref,
                gc_ref, gn_ref, wo_ref, h_ref):
    i = pl.program_id(0)
    u = cc_ref[...] * ch_ref[...]
    prev = ccp_ref[...] * chp_ref[...]
    first = (i * tm) % S == 0
    prev = jnp.where(first, 0.0, prev)
    r = lax.broadcasted_iota(jnp.int32, u.shape, 0)
    u1 = jnp.where(r == 0, prev[7:8, :], pltpu.roll(u, 1, axis=0))
    u2 = pltpu.roll(u, 2, axis=0)
    u2 = jnp.where(r == 0, prev[6:7, :], jnp.where(r == 1, prev[7:8, :], u2))
    cw = cw_ref[...]
    conv = cw[0:1, :] * u2 + cw[1:2, :] * u1 + cw[2:3, :] * u
    y = cb_ref[...] * conv
    mixed_c = _rms(y, gc_ref[...]).astype(BF16)
    mixed_n = _rms(nsa_ref[...], gn_ref[...]).astype(BF16)
    h_ref[...] = (x_ref[...] + _dot(mixed_c, wo_ref[0:CONV_CH, :])
                  + _dot(mixed_n, wo_ref[CONV_CH:2 * CONV_CH, :]))


def _mix(conv_in, nsa_out, x2, conv_w, gc, gn, wo, S, tm=512):
    rows = x2.shape[0]
    prev_map = lambda c: (lambda i: (jnp.maximum(i * (tm // 8) - 1, 0), c))
    return pl.pallas_call(
        functools.partial(_mix_kernel, S, tm),
        grid=(rows // tm,),
        in_specs=[
            pl.BlockSpec((tm, CONV_CH), lambda i: (i, 0)),
            pl.BlockSpec((tm, CONV_CH), lambda i: (i, 1)),
            pl.BlockSpec((tm, CONV_CH), lambda i: (i, 2)),
            pl.BlockSpec((8, CONV_CH), prev_map(1)),
            pl.BlockSpec((8, CONV_CH), prev_map(2)),
            pl.BlockSpec((tm, CONV_CH), lambda i: (i, 0)),
            pl.BlockSpec((tm, D_MODEL), lambda i: (i, 0)),
            pl.BlockSpec((CONV_K, CONV_CH), lambda i: (0, 0)),
            pl.BlockSpec((1, CONV_CH), lambda i: (0, 0)),
            pl.BlockSpec((1, CONV_CH), lambda i: (0, 0)),
            pl.BlockSpec((2 * CONV_CH, D_MODEL), lambda i: (0, 0), pipeline_mode=pl.Buffered(1)),
        ],
        out_specs=pl.BlockSpec((tm, D_MODEL), lambda i: (i, 0)),
        out_shape=jax.ShapeDtypeStruct((rows, D_MODEL), F32),
        compiler_params=pltpu.CompilerParams(
            dimension_semantics=("arbitrary",), vmem_limit_bytes=VMEM_LIMIT),
        name="mix",
    )(conv_in, conv_in, conv_in, conv_in, conv_in, nsa_out, x2, conv_w, gc, gn, wo)


_FF_CHUNK = 1408


def _ffn_kernel(h_ref, g2_ref, wg_ref, wu_ref, wd_ref, gf_ref, o_ref):
    h = h_ref[...]
    u = _rms(h, g2_ref[...]).astype(BF16)
    acc = h
    for c in range(D_FF // _FF_CHUNK):
        sl = slice(c * _FF_CHUNK, (c + 1) * _FF_CHUNK)
        a = jax.nn.silu(_dot(u, wg_ref[:, sl])) * _dot(u, wu_ref[:, sl])
        acc = acc + _dot(a.astype(BF16), wd_ref[sl, :])
    o_ref[...] = _rms(acc, gf_ref[...])


def _ffn(h, g2, wg, wu, wd, gf, tm=512):
    rows = h.shape[0]
    const = lambda i: (0, 0)
    return pl.pallas_call(
        _ffn_kernel,
        grid=(rows // tm,),
        in_specs=[
            pl.BlockSpec((tm, D_MODEL), lambda i: (i, 0)),
            pl.BlockSpec((1, D_MODEL), const),
            pl.BlockSpec((D_MODEL, D_FF), const, pipeline_mode=pl.Buffered(1)),
            pl.BlockSpec((D_MODEL, D_FF), const, pipeline_mode=pl.Buffered(1)),
            pl.BlockSpec((D_FF, D_MODEL), const, pipeline_mode=pl.Buffered(1)),
            pl.BlockSpec((1, D_MODEL), const),
        ],
        out_specs=pl.BlockSpec((tm, D_MODEL), lambda i: (i, 0)),
        out_shape=jax.ShapeDtypeStruct((rows, D_MODEL), F32),
        compiler_params=pltpu.CompilerParams(
            dimension_semantics=("arbitrary",), vmem_limit_bytes=VMEM_LIMIT),
        name="ffn",
    )(h, g2, wg, wu, wd, gf)


def _key_features(pos, n_sel):
    f = np.zeros((pos.shape[0], LANES + n_sel), np.float32)
    f[:, HEAD_DIM] = pos // SEL_BLOCK
    f[:, HEAD_DIM + 1] = pos % SEL_BLOCK
    if n_sel:
        f[np.arange(pos.shape[0]), LANES + pos // SEL_BLOCK] = 1.0
    return f


def _query_features():
    f = np.zeros((N_KV_HEADS, GQA * TQ, HEAD_DIM), np.float32)
    for k in range(N_KV_HEADS):
        for g in range(GQA):
            slope = 2.0 ** (-8.0 * (k * GQA + g + 1) / N_HEADS)
            f[k, g * TQ:(g + 1) * TQ, 0] = slope * SEL_BLOCK
            f[k, g * TQ:(g + 1) * TQ, 1] = slope
            f[k, g * TQ:(g + 1) * TQ, 2] = MASK_BIAS
    return f


def _mask_offsets(ncmp):
    r = np.arange(GQA * TQ)[:, None] % TQ
    cmask = np.arange(ncmp)[None, :] - ((r + 1) // CMP_STRIDE)
    dmask = np.arange(TK)[None, :] - r
    return cmask.astype(np.int32), dmask.astype(np.int32)


def _overlap(ncmp_pad, n_sel):
    nc = ncmp_pad - 1
    c_start = CMP_STRIDE * np.arange(nc)
    c_end = c_start + CMP_BLOCK - 1
    s_start = SEL_BLOCK * np.arange(n_sel)
    ov = np.minimum(c_end[:, None] + 1, s_start[None, :] + SEL_BLOCK) - np.maximum(c_start[:, None], s_start[None, :])
    out = np.zeros((ncmp_pad, n_sel), np.float32)
    out[:nc] = np.clip(ov, 0, None).astype(np.float32) / CMP_BLOCK
    return out


def _layer(h, norm1_g, w_in, conv_w, k_cmp_pos, k_cmp_w1, k_cmp_w2, v_cmp_pos, v_cmp_w1, v_cmp_w2,
           gn_conv_g, gn_nsa_g, w_out):
    B, S, _ = h.shape
    rows = B * S
    x2 = h.reshape(rows, D_MODEL)
    n_sel = S // SEL_BLOCK
    nch = S // CMP_STRIDE
    hd = HEAD_DIM

    ng = GQA * N_BRANCH
    w_main, w_g = w_in[:, :_G_OFF], w_in[:, _G_OFF:]
    gpad = jnp.zeros((D_MODEL, LANES - ng), w_in.dtype)
    w_pad = jnp.concatenate([w_main, w_g[:, :ng], gpad, w_g[:, ng:], gpad], axis=1).astype(BF16)

    kfeat = jnp.asarray(_key_features(np.arange(S), n_sel), BF16)
    conv_in, q, gates, kvc, ks_aug, vs, kw_aug, vw = _inproj(
        x2, norm1_g.reshape(1, D_MODEL), w_pad, kfeat)

    w1 = jnp.stack([k_cmp_w1, v_cmp_w1]).astype(BF16)
    wl = w1.reshape(2, 2, CMP_STRIDE, hd, CMP_HIDDEN).transpose(0, 2, 3, 1, 4)
    wl = wl.reshape(2, CMP_STRIDE, hd, 2 * CMP_HIDDEN)
    w2 = jnp.stack([k_cmp_w2, v_cmp_w2]).astype(BF16)
    pos = jnp.stack([k_cmp_pos, v_cmp_pos]).reshape(2, 1, CMP_BLOCK * hd)
    pos = jnp.broadcast_to(pos, (2, SUBLANES, CMP_BLOCK * hd)).astype(BF16)
    c_end = CMP_STRIDE * np.arange(nch) + CMP_BLOCK - 1
    cfeat = jnp.asarray(_key_features(c_end, 0), BF16)
    kc_aug, vc = _compress(kvc, pos, w1, wl, w2, cfeat, B, S)

    pad_rows = np.zeros((WINDOW, LANES), np.float32)
    pad_rows[:, hd + 2] = 1.0
    pad_k = jnp.broadcast_to(jnp.asarray(pad_rows, BF16), (N_KV_HEADS, B, WINDOW, LANES))
    kw_aug = jnp.concatenate([pad_k, kw_aug.reshape(N_KV_HEADS, B, S, LANES)], axis=2)
    vw = jnp.pad(vw.reshape(N_KV_HEADS, B, S, LANES), ((0, 0), (0, 0), (WINDOW, 0), (0, 0)))
    cmask, dmask = _mask_offsets(nch)

    nsa_out = _nsa(q, jnp.asarray(_query_features(), BF16), kc_aug, vc,
                   ks_aug.reshape(N_KV_HEADS, B, S, 2 * LANES), vs.reshape(N_KV_HEADS, B, S, LANES),
                   kw_aug, vw, jnp.asarray(_overlap(nch, n_sel), BF16), jnp.asarray(cmask),
                   jnp.asarray(dmask), gates, B, S)

    return _mix(conv_in, nsa_out, x2, conv_w, gn_conv_g.reshape(1, CONV_CH),
                gn_nsa_g.reshape(1, N_HEADS * hd), w_out.astype(BF16), S)


def kernel(x, norm1_g, w_in, conv_w, k_cmp_pos, k_cmp_w1, k_cmp_w2, v_cmp_pos, v_cmp_w1, v_cmp_w2,
           gn_conv_g, gn_nsa_g, w_out, norm2_g, w_gate, w_up, w_down, norm_f_g):
    B, S, _ = x.shape
    depth = norm1_g.shape[0]
    assert depth == 1, "the final norm is fused into the (single) layer's FFN kernel"
    h2 = _layer(x, norm1_g[0], w_in[0], conv_w[0], k_cmp_pos[0], k_cmp_w1[0], k_cmp_w2[0],
                v_cmp_pos[0], v_cmp_w1[0], v_cmp_w2[0], gn_conv_g[0], gn_nsa_g[0], w_out[0])
    out = _ffn(h2, norm2_g[0].reshape(1, D_MODEL), w_gate[0].astype(BF16), w_up[0].astype(BF16),
               w_down[0].astype(BF16), norm_f_g.reshape(1, D_MODEL))
    return out.reshape(B, S, D_MODEL)
```

```python
import functools

import numpy as np
import jax
import jax.numpy as jnp
from jax import lax
from jax.experimental import pallas as pl
from jax.experimental.pallas import tpu as pltpu

D_MODEL = 1024
CONV_CH = 512
CONV_K = 3
N_HEADS = 8
HEAD_DIM = 64
N_KV_HEADS = 2
GQA = 4
N_BRANCH = 3
CMP_BLOCK = 32
CMP_STRIDE = 16
CMP_HIDDEN = 256
SEL_BLOCK = 64
SEL_TOP = 16
WINDOW = 512
D_FF = 2816
NORM_EPS = 1e-6

F32 = jnp.float32
BF16 = jnp.bfloat16

LANES = 128
SUBLANES = 8
VMEM_LIMIT = 56 * 1024 * 1024

TQ = 128
TK = 256
NEG = -1e30
MASK_BIAS = -float(2 ** 30)

_CONV_W = 3 * CONV_CH
_Q_OFF = _CONV_W
_KVC_OFF = _Q_OFF + N_HEADS * HEAD_DIM
_KV_OFF = _KVC_OFF + 2 * N_KV_HEADS * HEAD_DIM
_G_OFF = _KV_OFF + 4 * N_KV_HEADS * HEAD_DIM
_G_W = N_KV_HEADS * LANES
_PROJ_W = _G_OFF + _G_W


def _rms(x, g):
    return x * lax.rsqrt(jnp.mean(x * x, axis=-1, keepdims=True) + NORM_EPS) * g


def _dot(a, b):
    return jnp.dot(a, b, preferred_element_type=F32)


def _ones_column(n):
    lane = lax.broadcasted_iota(jnp.int32, (n, LANES), 1)
    return jnp.where(lane == HEAD_DIM, 1.0, 0.0).astype(BF16)


def _dot_nt(a, b):
    return lax.dot_general(a, b, (((1,), (1,)), ((), ())), preferred_element_type=F32)


def _inproj_kernel(x_ref, g_ref, w_ref, kf_ref, conv_ref, q_ref, gate_ref, kvc_ref,
                   ks_ref, vs_ref, kw_ref, vw_ref):
    u = _rms(x_ref[...], g_ref[...]).astype(BF16)
    conv_ref[...] = _dot(u, w_ref[:, 0:_CONV_W])
    q_ref[...] = _dot(u, w_ref[:, _Q_OFF:_KVC_OFF]).astype(BF16)
    kvc_ref[...] = _dot(u, w_ref[:, _KVC_OFF:_KV_OFF])
    gate_ref[...] = jax.nn.sigmoid(_dot(u, w_ref[:, _G_OFF:_PROJ_W]))
    kv = _dot(u, w_ref[:, _KV_OFF:_G_OFF]).astype(BF16)
    kf = kf_ref[...]
    ones_col = _ones_column(kv.shape[0])
    hd, kvw = HEAD_DIM, N_KV_HEADS * HEAD_DIM
    for h in range(N_KV_HEADS):
        ks_ref[h] = kf
        ks_ref[h, :, 0:hd] = kv[:, h * hd:(h + 1) * hd]
        vs_ref[h] = ones_col
        vs_ref[h, :, 0:hd] = kv[:, kvw + h * hd:kvw + (h + 1) * hd]
        kw_ref[h] = kf[:, 0:LANES]
        kw_ref[h, :, 0:hd] = kv[:, 2 * kvw + h * hd:2 * kvw + (h + 1) * hd]
        vw_ref[h] = ones_col
        vw_ref[h, :, 0:hd] = kv[:, 3 * kvw + h * hd:3 * kvw + (h + 1) * hd]


def _inproj(x2, g, w, kfeat, tm=512):
    rows = x2.shape[0]
    seq_tiles = kfeat.shape[0] // tm
    row = lambda i: (i, 0)
    head_row = lambda i: (0, i, 0)
    return pl.pallas_call(
        _inproj_kernel,
        grid=(rows // tm,),
        in_specs=[
            pl.BlockSpec((tm, D_MODEL), row),
            pl.BlockSpec((1, D_MODEL), lambda i: (0, 0)),
            pl.BlockSpec((D_MODEL, _PROJ_W), lambda i: (0, 0), pipeline_mode=pl.Buffered(1)),
            pl.BlockSpec((tm, 2 * LANES), lambda i: (i % seq_tiles, 0)),
        ],
        out_specs=[
            pl.BlockSpec((tm, _CONV_W), row),
            pl.BlockSpec((tm, N_HEADS * HEAD_DIM), row),
            pl.BlockSpec((tm, _G_W), row),
            pl.BlockSpec((tm, _KV_OFF - _KVC_OFF), row),
            pl.BlockSpec((N_KV_HEADS, tm, 2 * LANES), head_row),
            pl.BlockSpec((N_KV_HEADS, tm, LANES), head_row),
            pl.BlockSpec((N_KV_HEADS, tm, LANES), head_row),
            pl.BlockSpec((N_KV_HEADS, tm, LANES), head_row),
        ],
        out_shape=[
            jax.ShapeDtypeStruct((rows, _CONV_W), F32),
            jax.ShapeDtypeStruct((rows, N_HEADS * HEAD_DIM), BF16),
            jax.ShapeDtypeStruct((rows, _G_W), F32),
            jax.ShapeDtypeStruct((rows, _KV_OFF - _KVC_OFF), F32),
            jax.ShapeDtypeStruct((N_KV_HEADS, rows, 2 * LANES), BF16),
            jax.ShapeDtypeStruct((N_KV_HEADS, rows, LANES), BF16),
            jax.ShapeDtypeStruct((N_KV_HEADS, rows, LANES), BF16),
            jax.ShapeDtypeStruct((N_KV_HEADS, rows, LANES), BF16),
        ],
        compiler_params=pltpu.CompilerParams(
            dimension_semantics=("arbitrary",), vmem_limit_bytes=VMEM_LIMIT),
        name="inproj",
    )(x2, g, w, kfeat)


def _compress_kernel(xk_ref, xv_ref, pos_ref, w1_ref, wl_ref, w2_ref, cf_ref, kc_ref, vc_ref):
    nch = xk_ref.shape[0] // CMP_STRIDE
    hd = HEAD_DIM
    acc = [jnp.zeros((nch, 2 * CMP_HIDDEN), F32) for _ in range(2 * N_KV_HEADS)]
    for l in range(CMP_STRIDE):
        for s, x_ref in enumerate((xk_ref, xv_ref)):
            xl = x_ref[pl.ds(l, nch, stride=CMP_STRIDE), :].astype(BF16)
            for h in range(N_KV_HEADS):
                c = s * N_KV_HEADS + h
                acc[c] = acc[c] + _dot(xl[:, h * hd:(h + 1) * hd], wl_ref[s, l])
    for s in range(2):
        pb = _dot(pos_ref[s], w1_ref[s])[0:1, :]
        for h in range(N_KV_HEADS):
            a = acc[s * N_KV_HEADS + h]
            hid = a[:, 0:CMP_HIDDEN] + pltpu.roll(a[:, CMP_HIDDEN:], nch - 1, axis=0) + pb
            out = _dot(jax.nn.gelu(hid, approximate=True).astype(BF16), w2_ref[s]).astype(BF16)
            if s == 0:
                kc_ref[h] = cf_ref[...]
                kc_ref[h, :, 0:hd] = out
            else:
                vc_ref[h] = _ones_column(nch)
                vc_ref[h, :, 0:hd] = out


def _compress(kvc, pos, w1, wl, w2, cfeat, B, S):
    nch = S // CMP_STRIDE
    full = lambda *shape: pl.BlockSpec(shape, lambda b: (0,) * len(shape))
    return pl.pallas_call(
        _compress_kernel,
        grid=(B,),
        in_specs=[
            pl.BlockSpec((S, LANES), lambda b: (b, 0)),
            pl.BlockSpec((S, LANES), lambda b: (b, 1)),
            full(*pos.shape), full(*w1.shape), full(*wl.shape), full(*w2.shape), full(*cfeat.shape),
        ],
        out_specs=[
            pl.BlockSpec((None, N_KV_HEADS, nch, LANES), lambda b: (b, 0, 0, 0)),
            pl.BlockSpec((None, N_KV_HEADS, nch, LANES), lambda b: (b, 0, 0, 0)),
        ],
        out_shape=[
            jax.ShapeDtypeStruct((B, N_KV_HEADS, nch, LANES), BF16),
            jax.ShapeDtypeStruct((B, N_KV_HEADS, nch, LANES), BF16),
        ],
        compiler_params=pltpu.CompilerParams(
            dimension_semantics=("arbitrary",), vmem_limit_bytes=VMEM_LIMIT),
        name="compress",
    )(kvc, kvc, pos, w1, wl, w2, cfeat)


def _normalised(o):
    return o[:, 0:HEAD_DIM] / o[:, HEAD_DIM:HEAD_DIM + 1]


def _exchange(v, i, j, descending):
    hi, lo = jnp.maximum(v[i], v[j]), jnp.minimum(v[i], v[j])
    v[i], v[j] = (hi, lo) if descending else (lo, hi)


def _bitonic_merge_desc(v):
    n = len(v)
    j = n // 2
    while j >= 1:
        for i in range(n):
            if i ^ j > i:
                _exchange(v, i, i ^ j, True)
        j //= 2


def _kth_largest(x, kth):
    n = x.shape[0]
    assert n == kth * SUBLANES and kth & (kth - 1) == 0
    v = [x[r * SUBLANES:(r + 1) * SUBLANES, :] for r in range(kth)]
    k = 2
    while k <= kth:
        j = k // 2
        while j >= 1:
            for i in range(kth):
                if i ^ j > i:
                    _exchange(v, i, i ^ j, (i & k) == 0)
            j //= 2
        k *= 2
    shift = SUBLANES // 2
    while shift >= 1:
        v = [jnp.maximum(v[i], pltpu.roll(v[kth - 1 - i], shift, axis=0)) for i in range(kth)]
        if shift > 1:
            _bitonic_merge_desc(v)
        shift //= 2
    out = v[0]
    for i in range(1, kth):
        out = jnp.minimum(out, v[i])
    return out[0:1, :]


def _nsa_kernel(q_ref, qf_ref, kc_ref, vc_ref, ks_ref, vs_ref, kw_ref, vw_ref, ov_ref, cm_ref, mb_ref,
                g_ref, o_ref, qa_ref, tiles_ref, m_ref, acc_ref, sa_ref, sb_ref):
    i = pl.program_id(2)
    t0 = i * TQ
    rows = GQA * TQ
    n_sel = ov_ref.shape[1]
    ncmp = kc_ref.shape[0]

    q = q_ref[...]
    for g in range(GQA):
        qg = q[:, g * HEAD_DIM:(g + 1) * HEAD_DIM].astype(F32) * (HEAD_DIM ** -0.5)
        qa_ref[g * TQ:(g + 1) * TQ, 0:HEAD_DIM] = qg.astype(BF16)
    qa_ref[:, HEAD_DIM:LANES] = qf_ref[...]
    qa1 = qa_ref[:, 0:LANES]

    diag_ok = (lax.broadcasted_iota(jnp.int32, (rows, TQ), 1)
               <= (lax.broadcasted_iota(jnp.int32, (rows, TQ), 0) & (TQ - 1)))

    wk = TQ + WINDOW
    w0 = pl.multiple_of(t0, TQ)
    s_w = _dot_nt(qa1, kw_ref[pl.ds(w0, wk), :])
    s_old = jnp.where(diag_ok, NEG, s_w[:, 0:TQ])
    s_mid = s_w[:, TQ:WINDOW]
    s_new = jnp.where(diag_ok, s_w[:, WINDOW:wk], NEG)
    m_w = jnp.maximum(jnp.maximum(jnp.max(s_old, axis=1, keepdims=True),
                                  jnp.max(s_mid, axis=1, keepdims=True)),
                      jnp.max(s_new, axis=1, keepdims=True))
    e_w = jnp.concatenate([jnp.exp(s_old - m_w), jnp.exp(s_mid - m_w), jnp.exp(s_new - m_w)], axis=1)
    o_w = _normalised(_dot(e_w.astype(BF16), vw_ref[pl.ds(w0, wk), :]))

    s_c = _dot_nt(qa1, kc_ref[...])
    s_c = jnp.where(cm_ref[...] <= (TQ // CMP_STRIDE) * i - 2, s_c, NEG)
    m_c = jnp.maximum(jnp.max(s_c, axis=1, keepdims=True), 0.5 * NEG)
    e_c = jnp.exp(s_c - m_c)
    oc = _dot(e_c.astype(BF16), vc_ref[...])
    inv_c = 1.0 / jnp.maximum(oc[:, HEAD_DIM:HEAD_DIM + 1], 1e-30)
    o_c = oc[:, 0:HEAD_DIM] * inv_c
    p_c = e_c * inv_c

    p_sum = p_c[0:TQ] + p_c[TQ:2 * TQ] + p_c[2 * TQ:3 * TQ] + p_c[3 * TQ:4 * TQ]
    p_hi = p_sum.astype(BF16)
    p_lo = (p_sum - p_hi.astype(F32)).astype(BF16)
    imp = _dot(p_hi, ov_ref[...]) + _dot(p_lo, ov_ref[...])
    imp_t = imp.T
    jb = lax.broadcasted_iota(jnp.int32, (n_sel, TQ), 0)
    tq = t0 + lax.broadcasted_iota(jnp.int32, (n_sel, TQ), 1)
    imp_t = jnp.where(SEL_BLOCK * jb > tq, -jnp.inf, imp_t)
    forced = (jb == 0) | (jb == (tq >> 6))
    imp_t = jnp.where(forced, jnp.inf, imp_t)

    thr = _kth_largest(imp_t, SEL_TOP)
    above = imp_t > thr
    tied = imp_t == thr
    n_above = jnp.sum(jnp.where(above, 1.0, 0.0), axis=0, keepdims=True)
    lower = (lax.broadcasted_iota(jnp.int32, (n_sel, n_sel), 1)
             < lax.broadcasted_iota(jnp.int32, (n_sel, n_sel), 0))
    tied_before = _dot(jnp.where(lower, 1.0, 0.0).astype(BF16), jnp.where(tied, 1.0, 0.0).astype(BF16))
    take_tie = jnp.where(tied_before + n_above < float(SEL_TOP), 1.0, 0.0)
    sel_t = jnp.where(above, 1.0, jnp.where(tied, take_tie, 0.0))
    bias = ((1.0 - sel_t) * MASK_BIAS).T.astype(BF16)
    for g in range(GQA):
        qa_ref[g * TQ:(g + 1) * TQ, LANES:2 * LANES] = bias

    blocks_per_tile = TK // SEL_BLOCK
    n_past = t0 // TK
    blk_used = jnp.max(sel_t, axis=1, keepdims=True)
    n_act = jnp.int32(0)
    for kt in range(n_sel // blocks_per_tile - 1):
        used = jnp.max(blk_used[kt * blocks_per_tile:(kt + 1) * blocks_per_tile, :]) > 0.5
        tiles_ref[n_act] = kt
        n_act = n_act + jnp.where(used & (kt < n_past), 1, 0)
    tiles_ref[n_act] = n_past
    tiles_ref[n_act + 1] = 0
    tiles_ref[n_act + 2] = 0
    last_mask = 1 + (t0 - n_past * TK) // TQ

    m_ref[...] = jnp.full(m_ref.shape, NEG, F32)
    acc_ref[...] = jnp.zeros(acc_ref.shape, F32)

    def scores(j, s_ref):
        k0 = pl.multiple_of(tiles_ref[j] * TK, TK)
        s_ref[...] = _dot_nt(qa_ref[...], ks_ref[pl.ds(k0, TK), :])

    def accumulate(j, s_ref):
        k0 = pl.multiple_of(tiles_ref[j] * TK, TK)
        slot = jnp.where(j <= n_act, 0, 1)
        s = s_ref[...] + mb_ref[jnp.where(j == n_act, last_mask, 0)]
        m = m_ref[slot]
        m_new = jnp.maximum(m, jnp.max(s, axis=1, keepdims=True))
        p = jnp.concatenate([jnp.exp(s[:, c:c + LANES] - m_new) for c in range(0, TK, LANES)], axis=1)
        acc_ref[slot] = (jnp.exp(m - m_new) * acc_ref[slot]
                         + _dot(p.astype(BF16), vs_ref[pl.ds(k0, TK), :]))
        m_ref[slot] = m_new

    scores(0, sa_ref)

    def pair(jj, c):
        a = 2 * jj
        scores(a + 1, sb_ref)
        accumulate(a, sa_ref)
        scores(a + 2, sa_ref)
        accumulate(a + 1, sb_ref)
        return c

    lax.fori_loop(0, (n_act + 2) // 2, pair, 0)
    o_s = _normalised(acc_ref[0])

    gt = g_ref[...]
    for g in range(GQA):
        sl = slice(g * TQ, (g + 1) * TQ)
        c0 = g * N_BRANCH
        o = (gt[:, c0:c0 + 1] * o_c[sl] + gt[:, c0 + 1:c0 + 2] * o_s[sl]
             + gt[:, c0 + 2:c0 + 3] * o_w[sl])
        o_ref[:, g * HEAD_DIM:(g + 1) * HEAD_DIM] = o


def _nsa(q, qfeat, kc_aug, vc, ks_aug, vs, kw_aug, vw, ov, cmask, tile_masks, gates, B, S):
    nq = S // TQ
    gw = GQA * HEAD_DIM
    ncmp = kc_aug.shape[2]
    n_sel = ov.shape[1]
    sp = kw_aug.shape[2]
    bk = lambda b, k, i: (b, k, 0, 0)
    kb = lambda b, k, i: (k, b, 0, 0)
    return pl.pallas_call(
        _nsa_kernel,
        grid=(B, N_KV_HEADS, nq),
        in_specs=[
            pl.BlockSpec((TQ, gw), lambda b, k, i: (b * nq + i, k)),
            pl.BlockSpec((None, GQA * TQ, HEAD_DIM), lambda b, k, i: (k, 0, 0)),
            pl.BlockSpec((None, None, ncmp, LANES), bk),
            pl.BlockSpec((None, None, ncmp, LANES), bk),
            pl.BlockSpec((None, None, S, 2 * LANES), kb),
            pl.BlockSpec((None, None, S, LANES), kb),
            pl.BlockSpec((None, None, sp, LANES), kb),
            pl.BlockSpec((None, None, sp, LANES), kb),
            pl.BlockSpec((ncmp, n_sel), lambda b, k, i: (0, 0)),
            pl.BlockSpec((GQA * TQ, ncmp), lambda b, k, i: (0, 0)),
            pl.BlockSpec(tile_masks.shape, lambda b, k, i: (0, 0, 0)),
            pl.BlockSpec((TQ, LANES), lambda b, k, i: (b * nq + i, k)),
        ],
        out_specs=pl.BlockSpec((TQ, gw), lambda b, k, i: (b * nq + i, k)),
        out_shape=jax.ShapeDtypeStruct((B * S, N_KV_HEADS * gw), F32),
        scratch_shapes=[
            pltpu.VMEM((GQA * TQ, 2 * LANES), BF16),
            pltpu.SMEM((S // TK + 2,), jnp.int32),
            pltpu.VMEM((2, GQA * TQ, LANES), F32),
            pltpu.VMEM((2, GQA * TQ, LANES), F32),
            pltpu.VMEM((GQA * TQ, TK), F32),
            pltpu.VMEM((GQA * TQ, TK), F32),
        ],
        compiler_params=pltpu.CompilerParams(
            dimension_semantics=("arbitrary",) * 3, vmem_limit_bytes=VMEM_LIMIT),
        name="nsa",
    )(q, qfeat, kc_aug, vc, ks_aug, vs, kw_aug, vw, ov, cmask, tile_masks, gates)


def _mix_kernel(S, tm, cb_ref, cc_ref, ch_ref, ccp_ref, chp_ref, nsa_ref, x_ref, cw_ref,
                gc_ref, gn_ref, wo_ref, h_ref):
    i = pl.program_id(0)
    u = cc_ref[...] * ch_ref[...]
    prev = ccp_ref[...] * chp_ref[...]
    first = (i * tm) % S == 0
    prev = jnp.where(first, 0.0, prev)
    r = lax.broadcasted_iota(jnp.int32, u.shape, 0)
    u1 = jnp.where(r == 0, prev[7:8, :], pltpu.roll(u, 1, axis=0))
    u2 = pltpu.roll(u, 2, axis=0)
    u2 = jnp.where(r == 0, prev[6:7, :], jnp.where(r == 1, prev[7:8, :], u2))
    cw = cw_ref[...]
    conv = cw[0:1, :] * u2 + cw[1:2, :] * u1 + cw[2:3, :] * u
    y = cb_ref[...] * conv
    mixed_c = _rms(y, gc_ref[...]).astype(BF16)
    mixed_n = _rms(nsa_ref[...], gn_ref[...]).astype(BF16)
    h_ref[...] = (x_ref[...] + _dot(mixed_c, wo_ref[0:CONV_CH, :])
                  + _dot(mixed_n, wo_ref[CONV_CH:2 * CONV_CH, :]))


def _mix(conv_in, nsa_out, x2, conv_w, gc, gn, wo, S, tm=512):
    rows = x2.shape[0]
    prev_map = lambda c: (lambda i: (jnp.maximum(i * (tm // 8) - 1, 0), c))
    return pl.pallas_call(
        functools.partial(_mix_kernel, S, tm),
        grid=(rows // tm,),
        in_specs=[
            pl.BlockSpec((tm, CONV_CH), lambda i: (i, 0)),
            pl.BlockSpec((tm, CONV_CH), lambda i: (i, 1)),
            pl.BlockSpec((tm, CONV_CH), lambda i: (i, 2)),
            pl.BlockSpec((8, CONV_CH), prev_map(1)),
            pl.BlockSpec((8, CONV_CH), prev_map(2)),
            pl.BlockSpec((tm, CONV_CH), lambda i: (i, 0)),
            pl.BlockSpec((tm, D_MODEL), lambda i: (i, 0)),
            pl.BlockSpec((CONV_K, CONV_CH), lambda i: (0, 0)),
            pl.BlockSpec((1, CONV_CH), lambda i: (0, 0)),
            pl.BlockSpec((1, CONV_CH), lambda i: (0, 0)),
            pl.BlockSpec((2 * CONV_CH, D_MODEL), lambda i: (0, 0), pipeline_mode=pl.Buffered(1)),
        ],
        out_specs=pl.BlockSpec((tm, D_MODEL), lambda i: (i, 0)),
        out_shape=jax.ShapeDtypeStruct((rows, D_MODEL), F32),
        compiler_params=pltpu.CompilerParams(
            dimension_semantics=("arbitrary",), vmem_limit_bytes=VMEM_LIMIT),
        name="mix",
    )(conv_in, conv_in, conv_in, conv_in, conv_in, nsa_out, x2, conv_w, gc, gn, wo)


_FF_CHUNK = 1408


def _ffn_kernel(h_ref, g2_ref, wg_ref, wu_ref, wd_ref, gf_ref, o_ref):
    h = h_ref[...]
    u = _rms(h, g2_ref[...]).astype(BF16)
    acc = h
    for c in range(D_FF // _FF_CHUNK):
        sl = slice(c * _FF_CHUNK, (c + 1) * _FF_CHUNK)
        a = jax.nn.silu(_dot(u, wg_ref[:, sl])) * _dot(u, wu_ref[:, sl])
        acc = acc + _dot(a.astype(BF16), wd_ref[sl, :])
    o_ref[...] = _rms(acc, gf_ref[...])


def _ffn(h, g2, wg, wu, wd, gf, tm=512):
    rows = h.shape[0]
    const = lambda i: (0, 0)
    return pl.pallas_call(
        _ffn_kernel,
        grid=(rows // tm,),
        in_specs=[
            pl.BlockSpec((tm, D_MODEL), lambda i: (i, 0)),
            pl.BlockSpec((1, D_MODEL), const),
            pl.BlockSpec((D_MODEL, D_FF), const, pipeline_mode=pl.Buffered(1)),
            pl.BlockSpec((D_MODEL, D_FF), const, pipeline_mode=pl.Buffered(1)),
            pl.BlockSpec((D_FF, D_MODEL), const, pipeline_mode=pl.Buffered(1)),
            pl.BlockSpec((1, D_MODEL), const),
        ],
        out_specs=pl.BlockSpec((tm, D_MODEL), lambda i: (i, 0)),
        out_shape=jax.ShapeDtypeStruct((rows, D_MODEL), F32),
        compiler_params=pltpu.CompilerParams(
            dimension_semantics=("arbitrary",), vmem_limit_bytes=VMEM_LIMIT),
        name="ffn",
    )(h, g2, wg, wu, wd, gf)


def _key_features(pos, n_sel):
    f = np.zeros((pos.shape[0], LANES + n_sel), np.float32)
    f[:, HEAD_DIM] = pos // SEL_BLOCK
    f[:, HEAD_DIM + 1] = pos % SEL_BLOCK
    if n_sel:
        f[np.arange(pos.shape[0]), LANES + pos // SEL_BLOCK] = 1.0
    return f


def _query_features():
    f = np.zeros((N_KV_HEADS, GQA * TQ, HEAD_DIM), np.float32)
    for k in range(N_KV_HEADS):
        for g in range(GQA):
            slope = 2.0 ** (-8.0 * (k * GQA + g + 1) / N_HEADS)
            f[k, g * TQ:(g + 1) * TQ, 0] = slope * SEL_BLOCK
            f[k, g * TQ:(g + 1) * TQ, 1] = slope
            f[k, g * TQ:(g + 1) * TQ, 2] = MASK_BIAS
    return f


def _causal_masks(ncmp):
    r = np.arange(GQA * TQ)[:, None] % TQ
    cmask = (np.arange(ncmp)[None, :] - ((r + 1) // CMP_STRIDE)).astype(np.int32)
    c = np.arange(TK)[None, :]
    tile_masks = [np.zeros((GQA * TQ, TK), np.float32)]
    for d in range(TK // TQ):
        tile_masks.append(np.where(c - r <= d * TQ, 0.0, NEG).astype(np.float32))
    return cmask, np.stack(tile_masks)


def _overlap(ncmp_pad, n_sel):
    nc = ncmp_pad - 1
    c_start = CMP_STRIDE * np.arange(nc)
    c_end = c_start + CMP_BLOCK - 1
    s_start = SEL_BLOCK * np.arange(n_sel)
    ov = np.minimum(c_end[:, None] + 1, s_start[None, :] + SEL_BLOCK) - np.maximum(c_start[:, None], s_start[None, :])
    out = np.zeros((ncmp_pad, n_sel), np.float32)
    out[:nc] = np.clip(ov, 0, None).astype(np.float32) / CMP_BLOCK
    return out


def _layer(h, norm1_g, w_in, conv_w, k_cmp_pos, k_cmp_w1, k_cmp_w2, v_cmp_pos, v_cmp_w1, v_cmp_w2,
           gn_conv_g, gn_nsa_g, w_out):
    B, S, _ = h.shape
    rows = B * S
    x2 = h.reshape(rows, D_MODEL)
    n_sel = S // SEL_BLOCK
    nch = S // CMP_STRIDE
    hd = HEAD_DIM

    ng = GQA * N_BRANCH
    w_main, w_g = w_in[:, :_G_OFF], w_in[:, _G_OFF:]
    gpad = jnp.zeros((D_MODEL, LANES - ng), w_in.dtype)
    w_pad = jnp.concatenate([w_main, w_g[:, :ng], gpad, w_g[:, ng:], gpad], axis=1).astype(BF16)

    kfeat = jnp.asarray(_key_features(np.arange(S), n_sel), BF16)
    conv_in, q, gates, kvc, ks_aug, vs, kw_aug, vw = _inproj(
        x2, norm1_g.reshape(1, D_MODEL), w_pad, kfeat)

    w1 = jnp.stack([k_cmp_w1, v_cmp_w1]).astype(BF16)
    wl = w1.reshape(2, 2, CMP_STRIDE, hd, CMP_HIDDEN).transpose(0, 2, 3, 1, 4)
    wl = wl.reshape(2, CMP_STRIDE, hd, 2 * CMP_HIDDEN)
    w2 = jnp.stack([k_cmp_w2, v_cmp_w2]).astype(BF16)
    pos = jnp.stack([k_cmp_pos, v_cmp_pos]).reshape(2, 1, CMP_BLOCK * hd)
    pos = jnp.broadcast_to(pos, (2, SUBLANES, CMP_BLOCK * hd)).astype(BF16)
    c_end = CMP_STRIDE * np.arange(nch) + CMP_BLOCK - 1
    cfeat = jnp.asarray(_key_features(c_end, 0), BF16)
    kc_aug, vc = _compress(kvc, pos, w1, wl, w2, cfeat, B, S)

    pad_rows = np.zeros((WINDOW, LANES), np.float32)
    pad_rows[:, hd + 2] = 1.0
    pad_k = jnp.broadcast_to(jnp.asarray(pad_rows, BF16), (N_KV_HEADS, B, WINDOW, LANES))
    kw_aug = jnp.concatenate([pad_k, kw_aug.reshape(N_KV_HEADS, B, S, LANES)], axis=2)
    vw = jnp.pad(vw.reshape(N_KV_HEADS, B, S, LANES), ((0, 0), (0, 0), (WINDOW, 0), (0, 0)))
    cmask, tile_masks = _causal_masks(nch)

    nsa_out = _nsa(q, jnp.asarray(_query_features(), BF16), kc_aug, vc,
                   ks_aug.reshape(N_KV_HEADS, B, S, 2 * LANES), vs.reshape(N_KV_HEADS, B, S, LANES),
                   kw_aug, vw, jnp.asarray(_overlap(nch, n_sel), BF16), jnp.asarray(cmask),
                   jnp.asarray(tile_masks), gates, B, S)

    return _mix(conv_in, nsa_out, x2, conv_w, gn_conv_g.reshape(1, CONV_CH),
                gn_nsa_g.reshape(1, N_HEADS * hd), w_out.astype(BF16), S)


def kernel(x, norm1_g, w_in, conv_w, k_cmp_pos, k_cmp_w1, k_cmp_w2, v_cmp_pos, v_cmp_w1, v_cmp_w2,
           gn_conv_g, gn_nsa_g, w_out, norm2_g, w_gate, w_up, w_down, norm_f_g):
    B, S, _ = x.shape
    depth = norm1_g.shape[0]
    assert depth == 1, "the final norm is fused into the (single) layer's FFN kernel"
    h2 = _layer(x, norm1_g[0], w_in[0], conv_w[0], k_cmp_pos[0], k_cmp_w1[0], k_cmp_w2[0],
                v_cmp_pos[0], v_cmp_w1[0], v_cmp_w2[0], gn_conv_g[0], gn_nsa_g[0], w_out[0])
    out = _ffn(h2, norm2_g[0].reshape(1, D_MODEL), w_gate[0].astype(BF16), w_up[0].astype(BF16),
               w_down[0].astype(BF16), norm_f_g.reshape(1, D_MODEL))
    return out.reshape(B, S, D_MODEL)
```

```python
import functools

import numpy as np
import jax
import jax.numpy as jnp
from jax import lax
from jax.experimental import pallas as pl
from jax.experimental.pallas import tpu as pltpu

D_MODEL = 1024
CONV_CH = 512
CONV_K = 3
N_HEADS = 8
HEAD_DIM = 64
N_KV_HEADS = 2
GQA = 4
N_BRANCH = 3
CMP_BLOCK = 32
CMP_STRIDE = 16
CMP_HIDDEN = 256
SEL_BLOCK = 64
SEL_TOP = 16
WINDOW = 512
D_FF = 2816
NORM_EPS = 1e-6

F32 = jnp.float32
BF16 = jnp.bfloat16

LANES = 128
SUBLANES = 8
VMEM_LIMIT = 56 * 1024 * 1024

TQ = 256
TK = 256
NEG = -1e30
MASK_BIAS = -float(2 ** 30)

_CONV_W = 3 * CONV_CH
_Q_OFF = _CONV_W
_KVC_OFF = _Q_OFF + N_HEADS * HEAD_DIM
_KV_OFF = _KVC_OFF + 2 * N_KV_HEADS * HEAD_DIM
_G_OFF = _KV_OFF + 4 * N_KV_HEADS * HEAD_DIM
_G_W = N_KV_HEADS * LANES
_PROJ_W = _G_OFF + _G_W


def _rms(x, g):
    return x * lax.rsqrt(jnp.mean(x * x, axis=-1, keepdims=True) + NORM_EPS) * g


def _dot(a, b):
    return jnp.dot(a, b, preferred_element_type=F32)


def _ones_column(n):
    lane = lax.broadcasted_iota(jnp.int32, (n, LANES), 1)
    return jnp.where(lane == HEAD_DIM, 1.0, 0.0).astype(BF16)


def _dot_nt(a, b):
    return lax.dot_general(a, b, (((1,), (1,)), ((), ())), preferred_element_type=F32)


def _inproj_kernel(x_ref, g_ref, w_ref, kf_ref, conv_ref, q_ref, gate_ref, kvc_ref,
                   ks_ref, vs_ref, kw_ref, vw_ref):
    u = _rms(x_ref[...], g_ref[...]).astype(BF16)
    conv_ref[...] = _dot(u, w_ref[:, 0:_CONV_W])
    q_ref[...] = _dot(u, w_ref[:, _Q_OFF:_KVC_OFF]).astype(BF16)
    kvc_ref[...] = _dot(u, w_ref[:, _KVC_OFF:_KV_OFF])
    gate_ref[...] = jax.nn.sigmoid(_dot(u, w_ref[:, _G_OFF:_PROJ_W]))
    kv = _dot(u, w_ref[:, _KV_OFF:_G_OFF]).astype(BF16)
    kf = kf_ref[...]
    ones_col = _ones_column(kv.shape[0])
    hd, kvw = HEAD_DIM, N_KV_HEADS * HEAD_DIM
    for h in range(N_KV_HEADS):
        ks_ref[h] = kf
        ks_ref[h, :, 0:hd] = kv[:, h * hd:(h + 1) * hd]
        vs_ref[h] = ones_col
        vs_ref[h, :, 0:hd] = kv[:, kvw + h * hd:kvw + (h + 1) * hd]
        kw_ref[h] = kf[:, 0:LANES]
        kw_ref[h, :, 0:hd] = kv[:, 2 * kvw + h * hd:2 * kvw + (h + 1) * hd]
        vw_ref[h] = ones_col
        vw_ref[h, :, 0:hd] = kv[:, 3 * kvw + h * hd:3 * kvw + (h + 1) * hd]


def _inproj(x2, g, w, kfeat, tm=512):
    rows = x2.shape[0]
    seq_tiles = kfeat.shape[0] // tm
    row = lambda i: (i, 0)
    head_row = lambda i: (0, i, 0)
    return pl.pallas_call(
        _inproj_kernel,
        grid=(rows // tm,),
        in_specs=[
            pl.BlockSpec((tm, D_MODEL), row),
            pl.BlockSpec((1, D_MODEL), lambda i: (0, 0)),
            pl.BlockSpec((D_MODEL, _PROJ_W), lambda i: (0, 0), pipeline_mode=pl.Buffered(1)),
            pl.BlockSpec((tm, 2 * LANES), lambda i: (i % seq_tiles, 0)),
        ],
        out_specs=[
            pl.BlockSpec((tm, _CONV_W), row),
            pl.BlockSpec((tm, N_HEADS * HEAD_DIM), row),
            pl.BlockSpec((tm, _G_W), row),
            pl.BlockSpec((tm, _KV_OFF - _KVC_OFF), row),
            pl.BlockSpec((N_KV_HEADS, tm, 2 * LANES), head_row),
            pl.BlockSpec((N_KV_HEADS, tm, LANES), head_row),
            pl.BlockSpec((N_KV_HEADS, tm, LANES), head_row),
            pl.BlockSpec((N_KV_HEADS, tm, LANES), head_row),
        ],
        out_shape=[
            jax.ShapeDtypeStruct((rows, _CONV_W), F32),
            jax.ShapeDtypeStruct((rows, N_HEADS * HEAD_DIM), BF16),
            jax.ShapeDtypeStruct((rows, _G_W), F32),
            jax.ShapeDtypeStruct((rows, _KV_OFF - _KVC_OFF), F32),
            jax.ShapeDtypeStruct((N_KV_HEADS, rows, 2 * LANES), BF16),
            jax.ShapeDtypeStruct((N_KV_HEADS, rows, LANES), BF16),
            jax.ShapeDtypeStruct((N_KV_HEADS, rows, LANES), BF16),
            jax.ShapeDtypeStruct((N_KV_HEADS, rows, LANES), BF16),
        ],
        compiler_params=pltpu.CompilerParams(
            dimension_semantics=("arbitrary",), vmem_limit_bytes=VMEM_LIMIT),
        name="inproj",
    )(x2, g, w, kfeat)


def _compress_kernel(xk_ref, xv_ref, pos_ref, w1_ref, wl_ref, w2_ref, cf_ref, kc_ref, vc_ref):
    nch = xk_ref.shape[0] // CMP_STRIDE
    hd = HEAD_DIM
    acc = [jnp.zeros((nch, 2 * CMP_HIDDEN), F32) for _ in range(2 * N_KV_HEADS)]
    for l in range(CMP_STRIDE):
        for s, x_ref in enumerate((xk_ref, xv_ref)):
            xl = x_ref[pl.ds(l, nch, stride=CMP_STRIDE), :].astype(BF16)
            for h in range(N_KV_HEADS):
                c = s * N_KV_HEADS + h
                acc[c] = acc[c] + _dot(xl[:, h * hd:(h + 1) * hd], wl_ref[s, l])
    for s in range(2):
        pb = _dot(pos_ref[s], w1_ref[s])[0:1, :]
        for h in range(N_KV_HEADS):
            a = acc[s * N_KV_HEADS + h]
            hid = a[:, 0:CMP_HIDDEN] + pltpu.roll(a[:, CMP_HIDDEN:], nch - 1, axis=0) + pb
            out = _dot(jax.nn.gelu(hid, approximate=True).astype(BF16), w2_ref[s]).astype(BF16)
            if s == 0:
                kc_ref[h] = cf_ref[...]
                kc_ref[h, :, 0:hd] = out
            else:
                vc_ref[h] = _ones_column(nch)
                vc_ref[h, :, 0:hd] = out


def _compress(kvc, pos, w1, wl, w2, cfeat, B, S):
    nch = S // CMP_STRIDE
    full = lambda *shape: pl.BlockSpec(shape, lambda b: (0,) * len(shape))
    return pl.pallas_call(
        _compress_kernel,
        grid=(B,),
        in_specs=[
            pl.BlockSpec((S, LANES), lambda b: (b, 0)),
            pl.BlockSpec((S, LANES), lambda b: (b, 1)),
            full(*pos.shape), full(*w1.shape), full(*wl.shape), full(*w2.shape), full(*cfeat.shape),
        ],
        out_specs=[
            pl.BlockSpec((None, N_KV_HEADS, nch, LANES), lambda b: (b, 0, 0, 0)),
            pl.BlockSpec((None, N_KV_HEADS, nch, LANES), lambda b: (b, 0, 0, 0)),
        ],
        out_shape=[
            jax.ShapeDtypeStruct((B, N_KV_HEADS, nch, LANES), BF16),
            jax.ShapeDtypeStruct((B, N_KV_HEADS, nch, LANES), BF16),
        ],
        compiler_params=pltpu.CompilerParams(
            dimension_semantics=("arbitrary",), vmem_limit_bytes=VMEM_LIMIT),
        name="compress",
    )(kvc, kvc, pos, w1, wl, w2, cfeat)


def _normalised(o):
    return o[:, 0:HEAD_DIM] / o[:, HEAD_DIM:HEAD_DIM + 1]


def _exchange(v, i, j, descending):
    hi, lo = jnp.maximum(v[i], v[j]), jnp.minimum(v[i], v[j])
    v[i], v[j] = (hi, lo) if descending else (lo, hi)


def _bitonic_merge_desc(v):
    n = len(v)
    j = n // 2
    while j >= 1:
        for i in range(n):
            if i ^ j > i:
                _exchange(v, i, i ^ j, True)
        j //= 2


def _kth_largest(x, kth):
    n = x.shape[0]
    assert n == kth * SUBLANES and kth & (kth - 1) == 0
    v = [x[r * SUBLANES:(r + 1) * SUBLANES, :] for r in range(kth)]
    k = 2
    while k <= kth:
        j = k // 2
        while j >= 1:
            for i in range(kth):
                if i ^ j > i:
                    _exchange(v, i, i ^ j, (i & k) == 0)
            j //= 2
        k *= 2
    shift = SUBLANES // 2
    while shift >= 1:
        v = [jnp.maximum(v[i], pltpu.roll(v[kth - 1 - i], shift, axis=0)) for i in range(kth)]
        if shift > 1:
            _bitonic_merge_desc(v)
        shift //= 2
    out = v[0]
    for i in range(1, kth):
        out = jnp.minimum(out, v[i])
    return out[0:1, :]


def _nsa_kernel(q_ref, qf_ref, kc_ref, vc_ref, ks_ref, vs_ref, kw_ref, vw_ref, ov_ref, cm_ref, mb_ref,
                g_ref, o_ref, qa_ref, tiles_ref, m_ref, acc_ref, sa_ref, sb_ref):
    i = pl.program_id(2)
    t0 = i * TQ
    rows = GQA * TQ
    n_sel = ov_ref.shape[1]
    ncmp = kc_ref.shape[0]

    q = q_ref[...]
    for g in range(GQA):
        qg = q[:, g * HEAD_DIM:(g + 1) * HEAD_DIM].astype(F32) * (HEAD_DIM ** -0.5)
        qa_ref[g * TQ:(g + 1) * TQ, 0:HEAD_DIM] = qg.astype(BF16)
    qa_ref[:, HEAD_DIM:LANES] = qf_ref[...]
    qa1 = qa_ref[:, 0:LANES]

    diag_ok = (lax.broadcasted_iota(jnp.int32, (rows, TQ), 1)
               <= (lax.broadcasted_iota(jnp.int32, (rows, TQ), 0) & (TQ - 1)))

    wk = TQ + WINDOW
    w0 = pl.multiple_of(t0, TQ)
    s_w = _dot_nt(qa1, kw_ref[pl.ds(w0, wk), :])
    s_old = jnp.where(diag_ok, NEG, s_w[:, 0:TQ])
    s_mid = s_w[:, TQ:WINDOW]
    s_new = jnp.where(diag_ok, s_w[:, WINDOW:wk], NEG)
    m_w = jnp.maximum(jnp.maximum(jnp.max(s_old, axis=1, keepdims=True),
                                  jnp.max(s_mid, axis=1, keepdims=True)),
                      jnp.max(s_new, axis=1, keepdims=True))
    e_w = jnp.concatenate([jnp.exp(s_old - m_w), jnp.exp(s_mid - m_w), jnp.exp(s_new - m_w)], axis=1)
    o_w = _normalised(_dot(e_w.astype(BF16), vw_ref[pl.ds(w0, wk), :]))

    s_c = _dot_nt(qa1, kc_ref[...])
    s_c = jnp.where(cm_ref[...] <= (TQ // CMP_STRIDE) * i - 2, s_c, NEG)
    m_c = jnp.maximum(jnp.max(s_c, axis=1, keepdims=True), 0.5 * NEG)
    e_c = jnp.exp(s_c - m_c)
    oc = _dot(e_c.astype(BF16), vc_ref[...])
    inv_c = 1.0 / jnp.maximum(oc[:, HEAD_DIM:HEAD_DIM + 1], 1e-30)
    o_c = oc[:, 0:HEAD_DIM] * inv_c
    p_c = e_c * inv_c

    p_sum = p_c[0:TQ] + p_c[TQ:2 * TQ] + p_c[2 * TQ:3 * TQ] + p_c[3 * TQ:4 * TQ]
    p_hi = p_sum.astype(BF16)
    p_lo = (p_sum - p_hi.astype(F32)).astype(BF16)
    imp = _dot(p_hi, ov_ref[...]) + _dot(p_lo, ov_ref[...])
    imp_t = imp.T
    jb = lax.broadcasted_iota(jnp.int32, (n_sel, TQ), 0)
    tq = t0 + lax.broadcasted_iota(jnp.int32, (n_sel, TQ), 1)
    imp_t = jnp.where(SEL_BLOCK * jb > tq, -jnp.inf, imp_t)
    forced = (jb == 0) | (jb == (tq >> 6))
    imp_t = jnp.where(forced, jnp.inf, imp_t)

    thr = _kth_largest(imp_t, SEL_TOP)
    above = imp_t > thr
    tied = imp_t == thr
    n_above = jnp.sum(jnp.where(above, 1.0, 0.0), axis=0, keepdims=True)
    lower = (lax.broadcasted_iota(jnp.int32, (n_sel, n_sel), 1)
             < lax.broadcasted_iota(jnp.int32, (n_sel, n_sel), 0))
    tied_before = _dot(jnp.where(lower, 1.0, 0.0).astype(BF16), jnp.where(tied, 1.0, 0.0).astype(BF16))
    take_tie = jnp.where(tied_before + n_above < float(SEL_TOP), 1.0, 0.0)
    sel_t = jnp.where(above, 1.0, jnp.where(tied, take_tie, 0.0))
    bias = ((1.0 - sel_t) * MASK_BIAS).T.astype(BF16)
    for g in range(GQA):
        qa_ref[g * TQ:(g + 1) * TQ, LANES:2 * LANES] = bias

    blocks_per_tile = TK // SEL_BLOCK
    n_past = t0 // TK
    blk_used = jnp.max(sel_t, axis=1, keepdims=True)
    n_act = jnp.int32(0)
    for kt in range(n_sel // blocks_per_tile - 1):
        used = jnp.max(blk_used[kt * blocks_per_tile:(kt + 1) * blocks_per_tile, :]) > 0.5
        tiles_ref[n_act] = kt
        n_act = n_act + jnp.where(used & (kt < n_past), 1, 0)
    tiles_ref[n_act] = n_past
    tiles_ref[n_act + 1] = 0
    tiles_ref[n_act + 2] = 0
    last_mask = 1 + (t0 - n_past * TK) // TQ

    m_ref[...] = jnp.full(m_ref.shape, NEG, F32)
    acc_ref[...] = jnp.zeros(acc_ref.shape, F32)

    def scores(j, s_ref):
        k0 = pl.multiple_of(tiles_ref[j] * TK, TK)
        s_ref[...] = _dot_nt(qa_ref[...], ks_ref[pl.ds(k0, TK), :])

    def accumulate(j, s_ref):
        k0 = pl.multiple_of(tiles_ref[j] * TK, TK)
        slot = jnp.where(j <= n_act, 0, 1)
        s = s_ref[...] + mb_ref[jnp.where(j == n_act, last_mask, 0)]
        m = m_ref[slot]
        m_new = jnp.maximum(m, jnp.max(s, axis=1, keepdims=True))
        p = jnp.concatenate([jnp.exp(s[:, c:c + LANES] - m_new) for c in range(0, TK, LANES)], axis=1)
        acc_ref[slot] = (jnp.exp(m - m_new) * acc_ref[slot]
                         + _dot(p.astype(BF16), vs_ref[pl.ds(k0, TK), :]))
        m_ref[slot] = m_new

    scores(0, sa_ref)

    def pair(jj, c):
        a = 2 * jj
        scores(a + 1, sb_ref)
        accumulate(a, sa_ref)
        scores(a + 2, sa_ref)
        accumulate(a + 1, sb_ref)
        return c

    lax.fori_loop(0, (n_act + 2) // 2, pair, 0)
    o_s = _normalised(acc_ref[0])

    gt = g_ref[...]
    for g in range(GQA):
        sl = slice(g * TQ, (g + 1) * TQ)
        c0 = g * N_BRANCH
        o = (gt[:, c0:c0 + 1] * o_c[sl] + gt[:, c0 + 1:c0 + 2] * o_s[sl]
             + gt[:, c0 + 2:c0 + 3] * o_w[sl])
        o_ref[:, g * HEAD_DIM:(g + 1) * HEAD_DIM] = o


def _nsa(q, qfeat, kc_aug, vc, ks_aug, vs, kw_aug, vw, ov, cmask, tile_masks, gates, B, S):
    nq = S // TQ
    gw = GQA * HEAD_DIM
    ncmp = kc_aug.shape[2]
    n_sel = ov.shape[1]
    sp = kw_aug.shape[2]
    bk = lambda b, k, i: (b, k, 0, 0)
    kb = lambda b, k, i: (k, b, 0, 0)
    return pl.pallas_call(
        _nsa_kernel,
        grid=(B, N_KV_HEADS, nq),
        in_specs=[
            pl.BlockSpec((TQ, gw), lambda b, k, i: (b * nq + i, k)),
            pl.BlockSpec((None, GQA * TQ, HEAD_DIM), lambda b, k, i: (k, 0, 0)),
            pl.BlockSpec((None, None, ncmp, LANES), bk),
            pl.BlockSpec((None, None, ncmp, LANES), bk),
            pl.BlockSpec((None, None, S, 2 * LANES), kb),
            pl.BlockSpec((None, None, S, LANES), kb),
            pl.BlockSpec((None, None, sp, LANES), kb),
            pl.BlockSpec((None, None, sp, LANES), kb),
            pl.BlockSpec((ncmp, n_sel), lambda b, k, i: (0, 0)),
            pl.BlockSpec((GQA * TQ, ncmp), lambda b, k, i: (0, 0)),
            pl.BlockSpec(tile_masks.shape, lambda b, k, i: (0, 0, 0)),
            pl.BlockSpec((TQ, LANES), lambda b, k, i: (b * nq + i, k)),
        ],
        out_specs=pl.BlockSpec((TQ, gw), lambda b, k, i: (b * nq + i, k)),
        out_shape=jax.ShapeDtypeStruct((B * S, N_KV_HEADS * gw), F32),
        scratch_shapes=[
            pltpu.VMEM((GQA * TQ, 2 * LANES), BF16),
            pltpu.SMEM((S // TK + 2,), jnp.int32),
            pltpu.VMEM((2, GQA * TQ, LANES), F32),
            pltpu.VMEM((2, GQA * TQ, LANES), F32),
            pltpu.VMEM((GQA * TQ, TK), F32),
            pltpu.VMEM((GQA * TQ, TK), F32),
        ],
        compiler_params=pltpu.CompilerParams(
            dimension_semantics=("arbitrary",) * 3, vmem_limit_bytes=VMEM_LIMIT),
        name="nsa",
    )(q, qfeat, kc_aug, vc, ks_aug, vs, kw_aug, vw, ov, cmask, tile_masks, gates)


def _mix_kernel(S, tm, cb_ref, cc_ref, ch_ref, ccp_ref, chp_ref, nsa_ref, x_ref, cw_ref,
                gc_ref, gn_ref, wo_ref, h_ref):
    i = pl.program_id(0)
    u = cc_ref[...] * ch_ref[...]
    prev = ccp_ref[...] * chp_ref[...]
    first = (i * tm) % S == 0
    prev = jnp.where(first, 0.0, prev)
    r = lax.broadcasted_iota(jnp.int32, u.shape, 0)
    u1 = jnp.where(r == 0, prev[7:8, :], pltpu.roll(u, 1, axis=0))
    u2 = pltpu.roll(u, 2, axis=0)
    u2 = jnp.where(r == 0, prev[6:7, :], jnp.where(r == 1, prev[7:8, :], u2))
    cw = cw_ref[...]
    conv = cw[0:1, :] * u2 + cw[1:2, :] * u1 + cw[2:3, :] * u
    y = cb_ref[...] * conv
    mixed_c = _rms(y, gc_ref[...]).astype(BF16)
    mixed_n = _rms(nsa_ref[...], gn_ref[...]).astype(BF16)
    h_ref[...] = (x_ref[...] + _dot(mixed_c, wo_ref[0:CONV_CH, :])
                  + _dot(mixed_n, wo_ref[CONV_CH:2 * CONV_CH, :]))


def _mix(conv_in, nsa_out, x2, conv_w, gc, gn, wo, S, tm=512):
    rows = x2.shape[0]
    prev_map = lambda c: (lambda i: (jnp.maximum(i * (tm // 8) - 1, 0), c))
    return pl.pallas_call(
        functools.partial(_mix_kernel, S, tm),
        grid=(rows // tm,),
        in_specs=[
            pl.BlockSpec((tm, CONV_CH), lambda i: (i, 0)),
            pl.BlockSpec((tm, CONV_CH), lambda i: (i, 1)),
            pl.BlockSpec((tm, CONV_CH), lambda i: (i, 2)),
            pl.BlockSpec((8, CONV_CH), prev_map(1)),
            pl.BlockSpec((8, CONV_CH), prev_map(2)),
            pl.BlockSpec((tm, CONV_CH), lambda i: (i, 0)),
            pl.BlockSpec((tm, D_MODEL), lambda i: (i, 0)),
            pl.BlockSpec((CONV_K, CONV_CH), lambda i: (0, 0)),
            pl.BlockSpec((1, CONV_CH), lambda i: (0, 0)),
            pl.BlockSpec((1, CONV_CH), lambda i: (0, 0)),
            pl.BlockSpec((2 * CONV_CH, D_MODEL), lambda i: (0, 0), pipeline_mode=pl.Buffered(1)),
        ],
        out_specs=pl.BlockSpec((tm, D_MODEL), lambda i: (i, 0)),
        out_shape=jax.ShapeDtypeStruct((rows, D_MODEL), F32),
        compiler_params=pltpu.CompilerParams(
            dimension_semantics=("arbitrary",), vmem_limit_bytes=VMEM_LIMIT),
        name="mix",
    )(conv_in, conv_in, conv_in, conv_in, conv_in, nsa_out, x2, conv_w, gc, gn, wo)


_FF_CHUNK = 1408


def _ffn_kernel(h_ref, g2_ref, wg_ref, wu_ref, wd_ref, gf_ref, o_ref):
    h = h_ref[...]
    u = _rms(h, g2_ref[...]).astype(BF16)
    acc = h
    for c in range(D_FF // _FF_CHUNK):
        sl = slice(c * _FF_CHUNK, (c + 1) * _FF_CHUNK)
        a = jax.nn.silu(_dot(u, wg_ref[:, sl])) * _dot(u, wu_ref[:, sl])
        acc = acc + _dot(a.astype(BF16), wd_ref[sl, :])
    o_ref[...] = _rms(acc, gf_ref[...])


def _ffn(h, g2, wg, wu, wd, gf, tm=512):
    rows = h.shape[0]
    const = lambda i: (0, 0)
    return pl.pallas_call(
        _ffn_kernel,
        grid=(rows // tm,),
        in_specs=[
            pl.BlockSpec((tm, D_MODEL), lambda i: (i, 0)),
            pl.BlockSpec((1, D_MODEL), const),
            pl.BlockSpec((D_MODEL, D_FF), const, pipeline_mode=pl.Buffered(1)),
            pl.BlockSpec((D_MODEL, D_FF), const, pipeline_mode=pl.Buffered(1)),
            pl.BlockSpec((D_FF, D_MODEL), const, pipeline_mode=pl.Buffered(1)),
            pl.BlockSpec((1, D_MODEL), const),
        ],
        out_specs=pl.BlockSpec((tm, D_MODEL), lambda i: (i, 0)),
        out_shape=jax.ShapeDtypeStruct((rows, D_MODEL), F32),
        compiler_params=pltpu.CompilerParams(
            dimension_semantics=("arbitrary",), vmem_limit_bytes=VMEM_LIMIT),
        name="ffn",
    )(h, g2, wg, wu, wd, gf)


def _key_features(pos, n_sel):
    f = np.zeros((pos.shape[0], LANES + n_sel), np.float32)
    f[:, HEAD_DIM] = pos // SEL_BLOCK
    f[:, HEAD_DIM + 1] = pos % SEL_BLOCK
    if n_sel:
        f[np.arange(pos.shape[0]), LANES + pos // SEL_BLOCK] = 1.0
    return f


def _query_features():
    f = np.zeros((N_KV_HEADS, GQA * TQ, HEAD_DIM), np.float32)
    for k in range(N_KV_HEADS):
        for g in range(GQA):
            slope = 2.0 ** (-8.0 * (k * GQA + g + 1) / N_HEADS)
            f[k, g * TQ:(g + 1) * TQ, 0] = slope * SEL_BLOCK
            f[k, g * TQ:(g + 1) * TQ, 1] = slope
            f[k, g * TQ:(g + 1) * TQ, 2] = MASK_BIAS
    return f


def _causal_masks(ncmp):
    r = np.arange(GQA * TQ)[:, None] % TQ
    cmask = (np.arange(ncmp)[None, :] - ((r + 1) // CMP_STRIDE)).astype(np.int32)
    c = np.arange(TK)[None, :]
    tile_masks = [np.zeros((GQA * TQ, TK), np.float32)]
    for d in range(TK // TQ):
        tile_masks.append(np.where(c - r <= d * TQ, 0.0, NEG).astype(np.float32))
    return cmask, np.stack(tile_masks)


def _overlap(ncmp_pad, n_sel):
    nc = ncmp_pad - 1
    c_start = CMP_STRIDE * np.arange(nc)
    c_end = c_start + CMP_BLOCK - 1
    s_start = SEL_BLOCK * np.arange(n_sel)
    ov = np.minimum(c_end[:, None] + 1, s_start[None, :] + SEL_BLOCK) - np.maximum(c_start[:, None], s_start[None, :])
    out = np.zeros((ncmp_pad, n_sel), np.float32)
    out[:nc] = np.clip(ov, 0, None).astype(np.float32) / CMP_BLOCK
    return out


def _layer(h, norm1_g, w_in, conv_w, k_cmp_pos, k_cmp_w1, k_cmp_w2, v_cmp_pos, v_cmp_w1, v_cmp_w2,
           gn_conv_g, gn_nsa_g, w_out):
    B, S, _ = h.shape
    rows = B * S
    x2 = h.reshape(rows, D_MODEL)
    n_sel = S // SEL_BLOCK
    nch = S // CMP_STRIDE
    hd = HEAD_DIM

    ng = GQA * N_BRANCH
    w_main, w_g = w_in[:, :_G_OFF], w_in[:, _G_OFF:]
    gpad = jnp.zeros((D_MODEL, LANES - ng), w_in.dtype)
    w_pad = jnp.concatenate([w_main, w_g[:, :ng], gpad, w_g[:, ng:], gpad], axis=1).astype(BF16)

    kfeat = jnp.asarray(_key_features(np.arange(S), n_sel), BF16)
    conv_in, q, gates, kvc, ks_aug, vs, kw_aug, vw = _inproj(
        x2, norm1_g.reshape(1, D_MODEL), w_pad, kfeat)

    w1 = jnp.stack([k_cmp_w1, v_cmp_w1]).astype(BF16)
    wl = w1.reshape(2, 2, CMP_STRIDE, hd, CMP_HIDDEN).transpose(0, 2, 3, 1, 4)
    wl = wl.reshape(2, CMP_STRIDE, hd, 2 * CMP_HIDDEN)
    w2 = jnp.stack([k_cmp_w2, v_cmp_w2]).astype(BF16)
    pos = jnp.stack([k_cmp_pos, v_cmp_pos]).reshape(2, 1, CMP_BLOCK * hd)
    pos = jnp.broadcast_to(pos, (2, SUBLANES, CMP_BLOCK * hd)).astype(BF16)
    c_end = CMP_STRIDE * np.arange(nch) + CMP_BLOCK - 1
    cfeat = jnp.asarray(_key_features(c_end, 0), BF16)
    kc_aug, vc = _compress(kvc, pos, w1, wl, w2, cfeat, B, S)

    pad_rows = np.zeros((WINDOW, LANES), np.float32)
    pad_rows[:, hd + 2] = 1.0
    pad_k = jnp.broadcast_to(jnp.asarray(pad_rows, BF16), (N_KV_HEADS, B, WINDOW, LANES))
    kw_aug = jnp.concatenate([pad_k, kw_aug.reshape(N_KV_HEADS, B, S, LANES)], axis=2)
    vw = jnp.pad(vw.reshape(N_KV_HEADS, B, S, LANES), ((0, 0), (0, 0), (WINDOW, 0), (0, 0)))
    cmask, tile_masks = _causal_masks(nch)

    nsa_out = _nsa(q, jnp.asarray(_query_features(), BF16), kc_aug, vc,
                   ks_aug.reshape(N_KV_HEADS, B, S, 2 * LANES), vs.reshape(N_KV_HEADS, B, S, LANES),
                   kw_aug, vw, jnp.asarray(_overlap(nch, n_sel), BF16), jnp.asarray(cmask),
                   jnp.asarray(tile_masks), gates, B, S)

    return _mix(conv_in, nsa_out, x2, conv_w, gn_conv_g.reshape(1, CONV_CH),
                gn_nsa_g.reshape(1, N_HEADS * hd), w_out.astype(BF16), S)


def kernel(x, norm1_g, w_in, conv_w, k_cmp_pos, k_cmp_w1, k_cmp_w2, v_cmp_pos, v_cmp_w1, v_cmp_w2,
           gn_conv_g, gn_nsa_g, w_out, norm2_g, w_gate, w_up, w_down, norm_f_g):
    B, S, _ = x.shape
    depth = norm1_g.shape[0]
    assert depth == 1, "the final norm is fused into the (single) layer's FFN kernel"
    h2 = _layer(x, norm1_g[0], w_in[0], conv_w[0], k_cmp_pos[0], k_cmp_w1[0], k_cmp_w2[0],
                v_cmp_pos[0], v_cmp_w1[0], v_cmp_w2[0], gn_conv_g[0], gn_nsa_g[0], w_out[0])
    out = _ffn(h2, norm2_g[0].reshape(1, D_MODEL), w_gate[0].astype(BF16), w_up[0].astype(BF16),
               w_down[0].astype(BF16), norm_f_g.reshape(1, D_MODEL))
    return out.reshape(B, S, D_MODEL)
```

```python
import functools

import numpy as np
import jax
import jax.numpy as jnp
from jax import lax
from jax.experimental import pallas as pl
from jax.experimental.pallas import tpu as pltpu

D_MODEL = 1024
CONV_CH = 512
CONV_K = 3
N_HEADS = 8
HEAD_DIM = 64
N_KV_HEADS = 2
GQA = 4
N_BRANCH = 3
CMP_BLOCK = 32
CMP_STRIDE = 16
CMP_HIDDEN = 256
SEL_BLOCK = 64
SEL_TOP = 16
WINDOW = 512
D_FF = 2816
NORM_EPS = 1e-6

F32 = jnp.float32
BF16 = jnp.bfloat16

LANES = 128
VW = 2 * LANES
SUBLANES = 8
VMEM_LIMIT = 56 * 1024 * 1024

TQ = 256
TK = 256
NEG = -1e30
MASK_BIAS = -float(2 ** 30)

_CONV_W = 3 * CONV_CH
_Q_OFF = _CONV_W
_KVC_OFF = _Q_OFF + N_HEADS * HEAD_DIM
_KV_OFF = _KVC_OFF + 2 * N_KV_HEADS * HEAD_DIM
_G_OFF = _KV_OFF + 4 * N_KV_HEADS * HEAD_DIM
_G_W = N_KV_HEADS * LANES
_PROJ_W = _G_OFF + _G_W


def _rms(x, g):
    return x * lax.rsqrt(jnp.mean(x * x, axis=-1, keepdims=True) + NORM_EPS) * g


def _dot(a, b):
    return jnp.dot(a, b, preferred_element_type=F32)


def _value_rows(v2, h, wide):
    lane = lax.broadcasted_iota(jnp.int32, v2.shape, 1)
    swapped = pltpu.roll(v2, HEAD_DIM, axis=1)
    if not wide:
        return jnp.where(lane < HEAD_DIM, v2 if h == 0 else swapped, 1.0).astype(BF16)
    dup = jnp.where((lane < HEAD_DIM) == (h == 0), v2, swapped)
    return jnp.concatenate([dup.astype(BF16), jnp.ones(v2.shape, BF16)], axis=1)


def _dot_nt(a, b):
    return lax.dot_general(a, b, (((1,), (1,)), ((), ())), preferred_element_type=F32)


def _inproj_kernel(x_ref, g_ref, w_ref, kf_ref, conv_ref, q_ref, gate_ref, kvc_ref,
                   ks_ref, vs_ref, kw_ref, vw_ref):
    u = _rms(x_ref[...], g_ref[...]).astype(BF16)
    conv_ref[...] = _dot(u, w_ref[:, 0:_CONV_W])
    q_ref[...] = _dot(u, w_ref[:, _Q_OFF:_KVC_OFF]).astype(BF16)
    kvc_ref[...] = _dot(u, w_ref[:, _KVC_OFF:_KV_OFF])
    gate_ref[...] = jax.nn.sigmoid(_dot(u, w_ref[:, _G_OFF:_PROJ_W]))
    kv = _dot(u, w_ref[:, _KV_OFF:_G_OFF])
    kf = kf_ref[...]
    hd, kvw = HEAD_DIM, N_KV_HEADS * HEAD_DIM
    for h in range(N_KV_HEADS):
        ks_ref[h] = kf
        ks_ref[h, :, 0:hd] = kv[:, h * hd:(h + 1) * hd].astype(BF16)
        vs_ref[h] = _value_rows(kv[:, kvw:2 * kvw], h, False)
        kw_ref[h] = kf[:, 0:LANES]
        kw_ref[h, :, 0:hd] = kv[:, 2 * kvw + h * hd:2 * kvw + (h + 1) * hd].astype(BF16)
        vw_ref[h] = _value_rows(kv[:, 3 * kvw:4 * kvw], h, True)


def _inproj(x2, g, w, kfeat, tm=512):
    rows = x2.shape[0]
    seq_tiles = kfeat.shape[0] // tm
    row = lambda i: (i, 0)
    head_row = lambda i: (0, i, 0)
    return pl.pallas_call(
        _inproj_kernel,
        grid=(rows // tm,),
        in_specs=[
            pl.BlockSpec((tm, D_MODEL), row),
            pl.BlockSpec((1, D_MODEL), lambda i: (0, 0)),
            pl.BlockSpec((D_MODEL, _PROJ_W), lambda i: (0, 0), pipeline_mode=pl.Buffered(1)),
            pl.BlockSpec((tm, 2 * LANES), lambda i: (i % seq_tiles, 0)),
        ],
        out_specs=[
            pl.BlockSpec((tm, _CONV_W), row),
            pl.BlockSpec((tm, N_HEADS * HEAD_DIM), row),
            pl.BlockSpec((tm, _G_W), row),
            pl.BlockSpec((tm, _KV_OFF - _KVC_OFF), row),
            pl.BlockSpec((N_KV_HEADS, tm, 2 * LANES), head_row),
            pl.BlockSpec((N_KV_HEADS, tm, LANES), head_row),
            pl.BlockSpec((N_KV_HEADS, tm, LANES), head_row),
            pl.BlockSpec((N_KV_HEADS, tm, VW), head_row),
        ],
        out_shape=[
            jax.ShapeDtypeStruct((rows, _CONV_W), F32),
            jax.ShapeDtypeStruct((rows, N_HEADS * HEAD_DIM), BF16),
            jax.ShapeDtypeStruct((rows, _G_W), F32),
            jax.ShapeDtypeStruct((rows, _KV_OFF - _KVC_OFF), F32),
            jax.ShapeDtypeStruct((N_KV_HEADS, rows, 2 * LANES), BF16),
            jax.ShapeDtypeStruct((N_KV_HEADS, rows, LANES), BF16),
            jax.ShapeDtypeStruct((N_KV_HEADS, rows, LANES), BF16),
            jax.ShapeDtypeStruct((N_KV_HEADS, rows, VW), BF16),
        ],
        compiler_params=pltpu.CompilerParams(
            dimension_semantics=("arbitrary",), vmem_limit_bytes=VMEM_LIMIT),
        name="inproj",
    )(x2, g, w, kfeat)


def _compress_kernel(xk_ref, xv_ref, pos_ref, w1_ref, wl_ref, w2_ref, cf_ref, kc_ref, vc_ref):
    nch = xk_ref.shape[0] // CMP_STRIDE
    hd = HEAD_DIM
    acc = [jnp.zeros((nch, 2 * CMP_HIDDEN), F32) for _ in range(2 * N_KV_HEADS)]
    for l in range(CMP_STRIDE):
        for s, x_ref in enumerate((xk_ref, xv_ref)):
            xl = x_ref[pl.ds(l, nch, stride=CMP_STRIDE), :].astype(BF16)
            for h in range(N_KV_HEADS):
                c = s * N_KV_HEADS + h
                acc[c] = acc[c] + _dot(xl[:, h * hd:(h + 1) * hd], wl_ref[s, l])
    for s in range(2):
        pb = _dot(pos_ref[s], w1_ref[s])[0:1, :]
        outs = []
        for h in range(N_KV_HEADS):
            a = acc[s * N_KV_HEADS + h]
            hid = a[:, 0:CMP_HIDDEN] + pltpu.roll(a[:, CMP_HIDDEN:], nch - 1, axis=0) + pb
            outs.append(_dot(jax.nn.gelu(hid, approximate=True).astype(BF16), w2_ref[s]))
        for h in range(N_KV_HEADS):
            if s == 0:
                kc_ref[h] = cf_ref[...]
                kc_ref[h, :, 0:hd] = outs[h].astype(BF16)
            else:
                vc_ref[h] = _value_rows(jnp.concatenate(outs, axis=1), h, True)


def _compress(kvc, pos, w1, wl, w2, cfeat, B, S):
    nch = S // CMP_STRIDE
    full = lambda *shape: pl.BlockSpec(shape, lambda b: (0,) * len(shape))
    return pl.pallas_call(
        _compress_kernel,
        grid=(B,),
        in_specs=[
            pl.BlockSpec((S, LANES), lambda b: (b, 0)),
            pl.BlockSpec((S, LANES), lambda b: (b, 1)),
            full(*pos.shape), full(*w1.shape), full(*wl.shape), full(*w2.shape), full(*cfeat.shape),
        ],
        out_specs=[
            pl.BlockSpec((None, N_KV_HEADS, nch, LANES), lambda b: (b, 0, 0, 0)),
            pl.BlockSpec((None, N_KV_HEADS, nch, VW), lambda b: (b, 0, 0, 0)),
        ],
        out_shape=[
            jax.ShapeDtypeStruct((B, N_KV_HEADS, nch, LANES), BF16),
            jax.ShapeDtypeStruct((B, N_KV_HEADS, nch, VW), BF16),
        ],
        compiler_params=pltpu.CompilerParams(
            dimension_semantics=("arbitrary",), vmem_limit_bytes=VMEM_LIMIT),
        name="compress",
    )(kvc, kvc, pos, w1, wl, w2, cfeat)


def _hi_lo(x):
    hi = x.astype(BF16)
    return jnp.concatenate([hi, (x - hi.astype(F32)).astype(BF16)], axis=1)


def _head_layout(acc):
    low_half = lax.broadcasted_iota(jnp.int32, (TQ, LANES), 1) < HEAD_DIM
    head = lambda x, g: x[g * TQ:(g + 1) * TQ]
    if acc.shape[1] == VW:
        x = acc[:, 0:LANES] * (1.0 / jnp.maximum(acc[:, LANES:VW], 1e-30))
        pairs = [jnp.where(low_half, head(x, 2 * j), head(x, 2 * j + 1)) for j in range(GQA // 2)]
    else:
        rot = pltpu.roll(acc, HEAD_DIM, axis=1)
        pairs = [jnp.where(low_half, head(acc, 2 * j) * (1.0 / head(rot, 2 * j)),
                           head(rot, 2 * j + 1) * (1.0 / head(acc, 2 * j + 1))) for j in range(GQA // 2)]
    return jnp.concatenate(pairs, axis=1)


def _exchange(v, i, j, descending):
    hi, lo = jnp.maximum(v[i], v[j]), jnp.minimum(v[i], v[j])
    v[i], v[j] = (hi, lo) if descending else (lo, hi)


def _bitonic_merge_desc(v):
    n = len(v)
    j = n // 2
    while j >= 1:
        for i in range(n):
            if i ^ j > i:
                _exchange(v, i, i ^ j, True)
        j //= 2


def _kth_largest(x, kth):
    n = x.shape[0]
    assert n == kth * SUBLANES and kth & (kth - 1) == 0
    v = [x[r * SUBLANES:(r + 1) * SUBLANES, :] for r in range(kth)]
    k = 2
    while k <= kth:
        j = k // 2
        while j >= 1:
            for i in range(kth):
                if i ^ j > i:
                    _exchange(v, i, i ^ j, (i & k) == 0)
            j //= 2
        k *= 2
    shift = SUBLANES // 2
    while shift >= 1:
        v = [jnp.maximum(v[i], pltpu.roll(v[kth - 1 - i], shift, axis=0)) for i in range(kth)]
        if shift > 1:
            _bitonic_merge_desc(v)
        shift //= 2
    out = v[0]
    for i in range(1, kth):
        out = jnp.minimum(out, v[i])
    return out[0:1, :]


def _nsa_kernel(q_ref, qf_ref, kc_ref, vc_ref, ks_ref, vs_ref, kw_ref, vw_ref, ov_ref, cm_ref, mb_ref,
                ge_ref, g_ref, o_ref, qa_ref, tiles_ref, m_ref, acc_ref, sa_ref, sb_ref):
    i = pl.program_id(2)
    t0 = i * TQ
    rows = GQA * TQ
    n_sel = ov_ref.shape[1]
    ncmp = kc_ref.shape[0]

    q = q_ref[...]
    for g in range(GQA):
        qg = q[:, g * HEAD_DIM:(g + 1) * HEAD_DIM].astype(F32) * (HEAD_DIM ** -0.5)
        qa_ref[g * TQ:(g + 1) * TQ, 0:HEAD_DIM] = qg.astype(BF16)
    qa_ref[:, HEAD_DIM:LANES] = qf_ref[...]
    qa1 = qa_ref[:, 0:LANES]

    diag_ok = (lax.broadcasted_iota(jnp.int32, (rows, TQ), 1)
               <= (lax.broadcasted_iota(jnp.int32, (rows, TQ), 0) & (TQ - 1)))

    wk = TQ + WINDOW
    w0 = pl.multiple_of(t0, TQ)
    s_w = _dot_nt(qa1, kw_ref[pl.ds(w0, wk), :])
    s_old = jnp.where(diag_ok, NEG, s_w[:, 0:TQ])
    s_mid = s_w[:, TQ:WINDOW]
    s_new = jnp.where(diag_ok, s_w[:, WINDOW:wk], NEG)
    m_w = jnp.maximum(jnp.maximum(jnp.max(s_old, axis=1, keepdims=True),
                                  jnp.max(s_mid, axis=1, keepdims=True)),
                      jnp.max(s_new, axis=1, keepdims=True))
    e_w = jnp.concatenate([jnp.exp(s_old - m_w), jnp.exp(s_mid - m_w), jnp.exp(s_new - m_w)], axis=1)
    acc_w = _dot(e_w.astype(BF16), vw_ref[pl.ds(w0, wk), :])

    s_c = _dot_nt(qa1, kc_ref[...])
    s_c = jnp.where(cm_ref[...] <= (TQ // CMP_STRIDE) * i - 2, s_c, NEG)
    m_c = jnp.maximum(jnp.max(s_c, axis=1, keepdims=True), 0.5 * NEG)
    e_c = jnp.exp(s_c - m_c)
    acc_c = _dot(e_c.astype(BF16), vc_ref[...])
    inv_c = 1.0 / jnp.maximum(acc_c[:, LANES:VW], 1e-30)
    p_c = e_c * jnp.concatenate([inv_c] * (ncmp // LANES), axis=1)

    p_sum = p_c[0:TQ] + p_c[TQ:2 * TQ] + p_c[2 * TQ:3 * TQ] + p_c[3 * TQ:4 * TQ]
    p_hi = p_sum.astype(BF16)
    p_lo = (p_sum - p_hi.astype(F32)).astype(BF16)
    imp = _dot(p_hi, ov_ref[...]) + _dot(p_lo, ov_ref[...])
    imp_t = imp.T
    jb = lax.broadcasted_iota(jnp.int32, (n_sel, TQ), 0)
    tq = t0 + lax.broadcasted_iota(jnp.int32, (n_sel, TQ), 1)
    imp_t = jnp.where(SEL_BLOCK * jb > tq, -jnp.inf, imp_t)
    forced = (jb == 0) | (jb == (tq >> 6))
    imp_t = jnp.where(forced, jnp.inf, imp_t)

    thr = _kth_largest(imp_t, SEL_TOP)
    above = imp_t > thr
    tied = imp_t == thr
    n_above = jnp.sum(jnp.where(above, 1.0, 0.0), axis=0, keepdims=True)
    lower = (lax.broadcasted_iota(jnp.int32, (n_sel, n_sel), 1)
             < lax.broadcasted_iota(jnp.int32, (n_sel, n_sel), 0))
    tied_before = _dot(jnp.where(lower, 1.0, 0.0).astype(BF16), jnp.where(tied, 1.0, 0.0).astype(BF16))
    take_tie = jnp.where(tied_before + n_above < float(SEL_TOP), 1.0, 0.0)
    sel_t = jnp.where(above, 1.0, jnp.where(tied, take_tie, 0.0))
    sel_q = sel_t.T
    bias = ((1.0 - sel_q) * MASK_BIAS).astype(BF16)
    for g in range(GQA):
        qa_ref[g * TQ:(g + 1) * TQ, LANES:2 * LANES] = bias

    n_past = t0 // TK
    first_mask = 1 + (t0 - n_past * TK) // TQ
    tiles_ref[0] = n_past

    def scores(j, s_ref):
        k0 = pl.multiple_of(tiles_ref[j] * TK, TK)
        s_ref[...] = _dot_nt(qa_ref[...], ks_ref[pl.ds(k0, TK), :])

    scores(0, sa_ref)

    bpt = TK // SEL_BLOCK
    n_tiles = n_sel // bpt
    blk_cnt = _dot(jnp.ones((SUBLANES, TQ), BF16), sel_q.astype(BF16))
    in_tile = (lax.broadcasted_iota(jnp.int32, (n_sel, LANES), 0) // bpt
               == lax.broadcasted_iota(jnp.int32, (n_sel, LANES), 1))
    tile_cnt = _dot(jnp.where(blk_cnt > 0.5, 1.0, 0.0).astype(BF16),
                    jnp.where(in_tile, 1.0, 0.0).astype(BF16))
    lane = lax.broadcasted_iota(jnp.int32, (1, LANES), 1)
    bit = jnp.where(tile_cnt[0:1, :] > 0.5, jnp.left_shift(1, lane & 15).astype(F32), 0.0)
    words = [jnp.sum(jnp.where((lane >= 16 * w) & (lane < 16 * (w + 1)), bit, 0.0)).astype(jnp.int32)
             for w in range((n_tiles + 15) // 16)]
    n_tasks = jnp.int32(1)
    for kt in range(n_tiles - 1):
        used = (words[kt // 16] >> (kt % 16)) & 1
        tiles_ref[n_tasks] = kt
        n_tasks = n_tasks + jnp.where(kt < n_past, used, 0)
    tiles_ref[n_tasks] = 0
    tiles_ref[n_tasks + 1] = 0

    m_ref[...] = jnp.full(m_ref.shape, NEG, F32)
    acc_ref[...] = jnp.zeros(acc_ref.shape, F32)

    def accumulate(j, s_ref):
        k0 = pl.multiple_of(tiles_ref[j] * TK, TK)
        slot = jnp.where(j < n_tasks, 0, 1)
        s = s_ref[...] + mb_ref[jnp.where(j == 0, first_mask, 0)]
        m = m_ref[slot]
        m_new = jnp.maximum(m, jnp.max(s, axis=1, keepdims=True))
        p = jnp.concatenate([jnp.exp(s[:, c:c + LANES] - m_new) for c in range(0, TK, LANES)], axis=1)
        acc_ref[slot] = (jnp.exp(m - m_new) * acc_ref[slot]
                         + _dot(p.astype(BF16), vs_ref[pl.ds(k0, TK), :]))
        m_ref[slot] = m_new

    def pair(jj, c):
        a = 2 * jj
        scores(a + 1, sb_ref)
        accumulate(a, sa_ref)
        scores(a + 2, sa_ref)
        accumulate(a + 1, sb_ref)
        return c

    lax.fori_loop(0, (n_tasks + 1) // 2, pair, 0)

    gates = _dot(_hi_lo(g_ref[...]), ge_ref[...])
    gw = GQA * HEAD_DIM
    o_ref[...] = (gates[:, 0:gw] * _head_layout(acc_c) + gates[:, gw:2 * gw] * _head_layout(acc_ref[0])
                  + gates[:, 2 * gw:3 * gw] * _head_layout(acc_w))


def _nsa(q, qfeat, kc_aug, vc, ks_aug, vs, kw_aug, vw, ov, cmask, tile_masks, gate_expand, gates, B, S):
    nq = S // TQ
    gw = GQA * HEAD_DIM
    ncmp = kc_aug.shape[2]
    n_sel = ov.shape[1]
    sp = kw_aug.shape[2]
    bk = lambda b, k, i: (b, k, 0, 0)
    kb = lambda b, k, i: (k, b, 0, 0)
    return pl.pallas_call(
        _nsa_kernel,
        grid=(B, N_KV_HEADS, nq),
        in_specs=[
            pl.BlockSpec((TQ, gw), lambda b, k, i: (b * nq + i, k)),
            pl.BlockSpec((None, GQA * TQ, HEAD_DIM), lambda b, k, i: (k, 0, 0)),
            pl.BlockSpec((None, None, ncmp, LANES), bk),
            pl.BlockSpec((None, None, ncmp, VW), bk),
            pl.BlockSpec((None, None, S, 2 * LANES), kb),
            pl.BlockSpec((None, None, S, LANES), kb),
            pl.BlockSpec((None, None, sp, LANES), kb),
            pl.BlockSpec((None, None, sp, VW), kb),
            pl.BlockSpec((ncmp, n_sel), lambda b, k, i: (0, 0)),
            pl.BlockSpec((GQA * TQ, ncmp), lambda b, k, i: (0, 0)),
            pl.BlockSpec(tile_masks.shape, lambda b, k, i: (0, 0, 0)),
            pl.BlockSpec(gate_expand.shape, lambda b, k, i: (0, 0)),
            pl.BlockSpec((TQ, LANES), lambda b, k, i: (b * nq + i, k)),
        ],
        out_specs=pl.BlockSpec((TQ, gw), lambda b, k, i: (b * nq + i, k)),
        out_shape=jax.ShapeDtypeStruct((B * S, N_KV_HEADS * gw), F32),
        scratch_shapes=[
            pltpu.VMEM((GQA * TQ, 2 * LANES), BF16),
            pltpu.SMEM((S // TK + 2,), jnp.int32),
            pltpu.VMEM((2, GQA * TQ, LANES), F32),
            pltpu.VMEM((2, GQA * TQ, LANES), F32),
            pltpu.VMEM((GQA * TQ, TK), F32),
            pltpu.VMEM((GQA * TQ, TK), F32),
        ],
        compiler_params=pltpu.CompilerParams(
            dimension_semantics=("arbitrary",) * 3, vmem_limit_bytes=VMEM_LIMIT),
        name="nsa",
    )(q, qfeat, kc_aug, vc, ks_aug, vs, kw_aug, vw, ov, cmask, tile_masks, gate_expand, gates)


def _mix_kernel(S, tm, cb_ref, cc_ref, ch_ref, ccp_ref, chp_ref, nsa_ref, x_ref, cw_ref,
                gc_ref, gn_ref, wo_ref, h_ref):
    i = pl.program_id(0)
    u = cc_ref[...] * ch_ref[...]
    prev = ccp_ref[...] * chp_ref[...]
    first = (i * tm) % S == 0
    prev = jnp.where(first, 0.0, prev)
    r = lax.broadcasted_iota(jnp.int32, u.shape, 0)
    u1 = jnp.where(r == 0, prev[7:8, :], pltpu.roll(u, 1, axis=0))
    u2 = pltpu.roll(u, 2, axis=0)
    u2 = jnp.where(r == 0, prev[6:7, :], jnp.where(r == 1, prev[7:8, :], u2))
    cw = cw_ref[...]
    conv = cw[0:1, :] * u2 + cw[1:2, :] * u1 + cw[2:3, :] * u
    y = cb_ref[...] * conv
    mixed_c = _rms(y, gc_ref[...]).astype(BF16)
    mixed_n = _rms(nsa_ref[...], gn_ref[...]).astype(BF16)
    h_ref[...] = (x_ref[...] + _dot(mixed_c, wo_ref[0:CONV_CH, :])
                  + _dot(mixed_n, wo_ref[CONV_CH:2 * CONV_CH, :]))


def _mix(conv_in, nsa_out, x2, conv_w, gc, gn, wo, S, tm=512):
    rows = x2.shape[0]
    prev_map = lambda c: (lambda i: (jnp.maximum(i * (tm // 8) - 1, 0), c))
    return pl.pallas_call(
        functools.partial(_mix_kernel, S, tm),
        grid=(rows // tm,),
        in_specs=[
            pl.BlockSpec((tm, CONV_CH), lambda i: (i, 0)),
            pl.BlockSpec((tm, CONV_CH), lambda i: (i, 1)),
            pl.BlockSpec((tm, CONV_CH), lambda i: (i, 2)),
            pl.BlockSpec((8, CONV_CH), prev_map(1)),
            pl.BlockSpec((8, CONV_CH), prev_map(2)),
            pl.BlockSpec((tm, CONV_CH), lambda i: (i, 0)),
            pl.BlockSpec((tm, D_MODEL), lambda i: (i, 0)),
            pl.BlockSpec((CONV_K, CONV_CH), lambda i: (0, 0)),
            pl.BlockSpec((1, CONV_CH), lambda i: (0, 0)),
            pl.BlockSpec((1, CONV_CH), lambda i: (0, 0)),
            pl.BlockSpec((2 * CONV_CH, D_MODEL), lambda i: (0, 0), pipeline_mode=pl.Buffered(1)),
        ],
        out_specs=pl.BlockSpec((tm, D_MODEL), lambda i: (i, 0)),
        out_shape=jax.ShapeDtypeStruct((rows, D_MODEL), F32),
        compiler_params=pltpu.CompilerParams(
            dimension_semantics=("arbitrary",), vmem_limit_bytes=VMEM_LIMIT),
        name="mix",
    )(conv_in, conv_in, conv_in, conv_in, conv_in, nsa_out, x2, conv_w, gc, gn, wo)


_FF_CHUNK = 1408


def _ffn_kernel(h_ref, g2_ref, wg_ref, wu_ref, wd_ref, gf_ref, o_ref):
    h = h_ref[...]
    u = _rms(h, g2_ref[...]).astype(BF16)
    acc = h
    for c in range(D_FF // _FF_CHUNK):
        sl = slice(c * _FF_CHUNK, (c + 1) * _FF_CHUNK)
        a = jax.nn.silu(_dot(u, wg_ref[:, sl])) * _dot(u, wu_ref[:, sl])
        acc = acc + _dot(a.astype(BF16), wd_ref[sl, :])
    o_ref[...] = _rms(acc, gf_ref[...])


def _ffn(h, g2, wg, wu, wd, gf, tm=512):
    rows = h.shape[0]
    const = lambda i: (0, 0)
    return pl.pallas_call(
        _ffn_kernel,
        grid=(rows // tm,),
        in_specs=[
            pl.BlockSpec((tm, D_MODEL), lambda i: (i, 0)),
            pl.BlockSpec((1, D_MODEL), const),
            pl.BlockSpec((D_MODEL, D_FF), const, pipeline_mode=pl.Buffered(1)),
            pl.BlockSpec((D_MODEL, D_FF), const, pipeline_mode=pl.Buffered(1)),
            pl.BlockSpec((D_FF, D_MODEL), const, pipeline_mode=pl.Buffered(1)),
            pl.BlockSpec((1, D_MODEL), const),
        ],
        out_specs=pl.BlockSpec((tm, D_MODEL), lambda i: (i, 0)),
        out_shape=jax.ShapeDtypeStruct((rows, D_MODEL), F32),
        compiler_params=pltpu.CompilerParams(
            dimension_semantics=("arbitrary",), vmem_limit_bytes=VMEM_LIMIT),
        name="ffn",
    )(h, g2, wg, wu, wd, gf)


def _key_features(pos, n_sel):
    f = np.zeros((pos.shape[0], LANES + n_sel), np.float32)
    f[:, HEAD_DIM] = pos // SEL_BLOCK
    f[:, HEAD_DIM + 1] = pos % SEL_BLOCK
    if n_sel:
        f[np.arange(pos.shape[0]), LANES + pos // SEL_BLOCK] = 1.0
    return f


def _query_features():
    f = np.zeros((N_KV_HEADS, GQA * TQ, HEAD_DIM), np.float32)
    for k in range(N_KV_HEADS):
        for g in range(GQA):
            slope = 2.0 ** (-8.0 * (k * GQA + g + 1) / N_HEADS)
            f[k, g * TQ:(g + 1) * TQ, 0] = slope * SEL_BLOCK
            f[k, g * TQ:(g + 1) * TQ, 1] = slope
            f[k, g * TQ:(g + 1) * TQ, 2] = MASK_BIAS
    return f


def _causal_masks(ncmp):
    r = np.arange(GQA * TQ)[:, None] % TQ
    cmask = (np.arange(ncmp)[None, :] - ((r + 1) // CMP_STRIDE)).astype(np.int32)
    c = np.arange(TK)[None, :]
    tile_masks = [np.zeros((GQA * TQ, TK), np.float32)]
    for d in range(TK // TQ):
        tile_masks.append(np.where(c - r <= d * TQ, 0.0, NEG).astype(np.float32))
    return cmask, np.stack(tile_masks)


def _gate_expand():
    gw = GQA * HEAD_DIM
    e = np.zeros((2 * LANES, N_BRANCH * gw), np.float32)
    for g in range(GQA):
        for br in range(N_BRANCH):
            for half in range(2):
                e[half * LANES + g * N_BRANCH + br, br * gw + g * HEAD_DIM:br * gw + (g + 1) * HEAD_DIM] = 1.0
    return e


def _overlap(ncmp_pad, n_sel):
    nc = ncmp_pad - 1
    c_start = CMP_STRIDE * np.arange(nc)
    c_end = c_start + CMP_BLOCK - 1
    s_start = SEL_BLOCK * np.arange(n_sel)
    ov = np.minimum(c_end[:, None] + 1, s_start[None, :] + SEL_BLOCK) - np.maximum(c_start[:, None], s_start[None, :])
    out = np.zeros((ncmp_pad, n_sel), np.float32)
    out[:nc] = np.clip(ov, 0, None).astype(np.float32) / CMP_BLOCK
    return out


def _layer(h, norm1_g, w_in, conv_w, k_cmp_pos, k_cmp_w1, k_cmp_w2, v_cmp_pos, v_cmp_w1, v_cmp_w2,
           gn_conv_g, gn_nsa_g, w_out):
    B, S, _ = h.shape
    rows = B * S
    x2 = h.reshape(rows, D_MODEL)
    n_sel = S // SEL_BLOCK
    nch = S // CMP_STRIDE
    hd = HEAD_DIM

    ng = GQA * N_BRANCH
    w_main, w_g = w_in[:, :_G_OFF], w_in[:, _G_OFF:]
    gpad = jnp.zeros((D_MODEL, LANES - ng), w_in.dtype)
    w_pad = jnp.concatenate([w_main, w_g[:, :ng], gpad, w_g[:, ng:], gpad], axis=1).astype(BF16)

    kfeat = jnp.asarray(_key_features(np.arange(S), n_sel), BF16)
    conv_in, q, gates, kvc, ks_aug, vs, kw_aug, vw = _inproj(
        x2, norm1_g.reshape(1, D_MODEL), w_pad, kfeat)

    w1 = jnp.stack([k_cmp_w1, v_cmp_w1]).astype(BF16)
    wl = w1.reshape(2, 2, CMP_STRIDE, hd, CMP_HIDDEN).transpose(0, 2, 3, 1, 4)
    wl = wl.reshape(2, CMP_STRIDE, hd, 2 * CMP_HIDDEN)
    w2 = jnp.stack([k_cmp_w2, v_cmp_w2]).astype(BF16)
    pos = jnp.stack([k_cmp_pos, v_cmp_pos]).reshape(2, 1, CMP_BLOCK * hd)
    pos = jnp.broadcast_to(pos, (2, SUBLANES, CMP_BLOCK * hd)).astype(BF16)
    c_end = CMP_STRIDE * np.arange(nch) + CMP_BLOCK - 1
    cfeat = jnp.asarray(_key_features(c_end, 0), BF16)
    kc_aug, vc = _compress(kvc, pos, w1, wl, w2, cfeat, B, S)

    pad_rows = np.zeros((WINDOW, LANES), np.float32)
    pad_rows[:, hd + 2] = 1.0
    pad_k = jnp.broadcast_to(jnp.asarray(pad_rows, BF16), (N_KV_HEADS, B, WINDOW, LANES))
    kw_aug = jnp.concatenate([pad_k, kw_aug.reshape(N_KV_HEADS, B, S, LANES)], axis=2)
    vw = jnp.pad(vw.reshape(N_KV_HEADS, B, S, VW), ((0, 0), (0, 0), (WINDOW, 0), (0, 0)))
    cmask, tile_masks = _causal_masks(nch)

    nsa_out = _nsa(q, jnp.asarray(_query_features(), BF16), kc_aug, vc,
                   ks_aug.reshape(N_KV_HEADS, B, S, 2 * LANES), vs.reshape(N_KV_HEADS, B, S, LANES),
                   kw_aug, vw, jnp.asarray(_overlap(nch, n_sel), BF16), jnp.asarray(cmask),
                   jnp.asarray(tile_masks), jnp.asarray(_gate_expand(), BF16), gates, B, S)

    return _mix(conv_in, nsa_out, x2, conv_w, gn_conv_g.reshape(1, CONV_CH),
                gn_nsa_g.reshape(1, N_HEADS * hd), w_out.astype(BF16), S)


def kernel(x, norm1_g, w_in, conv_w, k_cmp_pos, k_cmp_w1, k_cmp_w2, v_cmp_pos, v_cmp_w1, v_cmp_w2,
           gn_conv_g, gn_nsa_g, w_out, norm2_g, w_gate, w_up, w_down, norm_f_g):
    B, S, _ = x.shape
    depth = norm1_g.shape[0]
    assert depth == 1, "the final norm is fused into the (single) layer's FFN kernel"
    h2 = _layer(x, norm1_g[0], w_in[0], conv_w[0], k_cmp_pos[0], k_cmp_w1[0], k_cmp_w2[0],
                v_cmp_pos[0], v_cmp_w1[0], v_cmp_w2[0], gn_conv_g[0], gn_nsa_g[0], w_out[0])
    out = _ffn(h2, norm2_g[0].reshape(1, D_MODEL), w_gate[0].astype(BF16), w_up[0].astype(BF16),
               w_down[0].astype(BF16), norm_f_g.reshape(1, D_MODEL))
    return out.reshape(B, S, D_MODEL)
```

```python
import functools

import numpy as np
import jax
import jax.numpy as jnp
from jax import lax
from jax.experimental import pallas as pl
from jax.experimental.pallas import tpu as pltpu

D_MODEL = 1024
CONV_CH = 512
CONV_K = 3
N_HEADS = 8
HEAD_DIM = 64
N_KV_HEADS = 2
GQA = 4
N_BRANCH = 3
CMP_BLOCK = 32
CMP_STRIDE = 16
CMP_HIDDEN = 256
SEL_BLOCK = 64
SEL_TOP = 16
WINDOW = 512
D_FF = 2816
NORM_EPS = 1e-6

F32 = jnp.float32
BF16 = jnp.bfloat16

LANES = 128
VW = 2 * LANES
SUBLANES = 8
VMEM_LIMIT = 56 * 1024 * 1024

TQ = 256
TK = 256
NEG = -1e30
MASK_BIAS = -float(2 ** 30)

_CONV_W = 3 * CONV_CH
_Q_OFF = _CONV_W
_KVC_OFF = _Q_OFF + N_HEADS * HEAD_DIM
_KV_OFF = _KVC_OFF + 2 * N_KV_HEADS * HEAD_DIM
_G_OFF = _KV_OFF + 4 * N_KV_HEADS * HEAD_DIM
_G_W = N_KV_HEADS * LANES
_PROJ_W = _G_OFF + _G_W


def _rms(x, g):
    return x * lax.rsqrt(jnp.mean(x * x, axis=-1, keepdims=True) + NORM_EPS) * g


def _dot(a, b):
    return jnp.dot(a, b, preferred_element_type=F32)


def _value_rows(v2, h, wide):
    lane = lax.broadcasted_iota(jnp.int32, v2.shape, 1)
    swapped = pltpu.roll(v2, HEAD_DIM, axis=1)
    if not wide:
        return jnp.where(lane < HEAD_DIM, v2 if h == 0 else swapped, 1.0).astype(BF16)
    dup = jnp.where((lane < HEAD_DIM) == (h == 0), v2, swapped)
    return jnp.concatenate([dup.astype(BF16), jnp.ones(v2.shape, BF16)], axis=1)


def _dot_nt(a, b):
    return lax.dot_general(a, b, (((1,), (1,)), ((), ())), preferred_element_type=F32)


def _inproj_kernel(seq_tiles, x_ref, xp_ref, g_ref, w_ref, kf_ref, cw_ref, gc_ref, mixc_ref, q_ref, gate_ref,
                   kvc_ref, ks_ref, vs_ref, kw_ref, vw_ref):
    u = _rms(x_ref[...], g_ref[...]).astype(BF16)

    cbch = _dot(u, w_ref[:, 0:_CONV_W])
    cu = cbch[:, CONV_CH:2 * CONV_CH] * cbch[:, 2 * CONV_CH:3 * CONV_CH]
    up = _rms(xp_ref[...], g_ref[...]).astype(BF16)
    chp = _dot(up, w_ref[:, CONV_CH:_CONV_W])
    starts_sequence = pl.program_id(0) % seq_tiles == 0
    prev = jnp.where(starts_sequence, 0.0, chp[:, 0:CONV_CH] * chp[:, CONV_CH:2 * CONV_CH])
    r = lax.broadcasted_iota(jnp.int32, cu.shape, 0)
    cu1 = jnp.where(r == 0, prev[7:8, :], pltpu.roll(cu, 1, axis=0))
    cu2 = pltpu.roll(cu, 2, axis=0)
    cu2 = jnp.where(r == 0, prev[6:7, :], jnp.where(r == 1, prev[7:8, :], cu2))
    cw = cw_ref[...]
    y = cbch[:, 0:CONV_CH] * (cw[0:1, :] * cu2 + cw[1:2, :] * cu1 + cw[2:3, :] * cu)
    mixc_ref[...] = _rms(y, gc_ref[...])

    q_ref[...] = _dot(u, w_ref[:, _Q_OFF:_KVC_OFF]).astype(BF16)
    kvc_ref[...] = _dot(u, w_ref[:, _KVC_OFF:_KV_OFF])
    gate_ref[...] = jax.nn.sigmoid(_dot(u, w_ref[:, _G_OFF:_PROJ_W]))
    kv = _dot(u, w_ref[:, _KV_OFF:_G_OFF])
    kf = kf_ref[...]
    hd, kvw = HEAD_DIM, N_KV_HEADS * HEAD_DIM
    for h in range(N_KV_HEADS):
        ks_ref[h] = kf
        ks_ref[h, :, 0:hd] = kv[:, h * hd:(h + 1) * hd].astype(BF16)
        vs_ref[h] = _value_rows(kv[:, kvw:2 * kvw], h, False)
        kw_ref[h] = kf[:, 0:LANES]
        kw_ref[h, :, 0:hd] = kv[:, 2 * kvw + h * hd:2 * kvw + (h + 1) * hd].astype(BF16)
        vw_ref[h] = _value_rows(kv[:, 3 * kvw:4 * kvw], h, True)


def _inproj(x2, g, w, kfeat, conv_w, gc, tm=512):
    rows = x2.shape[0]
    seq_tiles = kfeat.shape[0] // tm
    row = lambda i: (i, 0)
    head_row = lambda i: (0, i, 0)
    const = lambda i: (0, 0)
    return pl.pallas_call(
        functools.partial(_inproj_kernel, seq_tiles),
        grid=(rows // tm,),
        in_specs=[
            pl.BlockSpec((tm, D_MODEL), row),
            pl.BlockSpec((SUBLANES, D_MODEL), lambda i: (jnp.maximum(i * (tm // SUBLANES) - 1, 0), 0)),
            pl.BlockSpec((1, D_MODEL), const),
            pl.BlockSpec((D_MODEL, _PROJ_W), const, pipeline_mode=pl.Buffered(1)),
            pl.BlockSpec((tm, 2 * LANES), lambda i: (i % seq_tiles, 0)),
            pl.BlockSpec((CONV_K, CONV_CH), const),
            pl.BlockSpec((1, CONV_CH), const),
        ],
        out_specs=[
            pl.BlockSpec((tm, CONV_CH), row),
            pl.BlockSpec((tm, N_HEADS * HEAD_DIM), row),
            pl.BlockSpec((tm, _G_W), row),
            pl.BlockSpec((tm, _KV_OFF - _KVC_OFF), row),
            pl.BlockSpec((N_KV_HEADS, tm, 2 * LANES), head_row),
            pl.BlockSpec((N_KV_HEADS, tm, LANES), head_row),
            pl.BlockSpec((N_KV_HEADS, tm, LANES), head_row),
            pl.BlockSpec((N_KV_HEADS, tm, VW), head_row),
        ],
        out_shape=[
            jax.ShapeDtypeStruct((rows, CONV_CH), F32),
            jax.ShapeDtypeStruct((rows, N_HEADS * HEAD_DIM), BF16),
            jax.ShapeDtypeStruct((rows, _G_W), F32),
            jax.ShapeDtypeStruct((rows, _KV_OFF - _KVC_OFF), F32),
            jax.ShapeDtypeStruct((N_KV_HEADS, rows, 2 * LANES), BF16),
            jax.ShapeDtypeStruct((N_KV_HEADS, rows, LANES), BF16),
            jax.ShapeDtypeStruct((N_KV_HEADS, rows, LANES), BF16),
            jax.ShapeDtypeStruct((N_KV_HEADS, rows, VW), BF16),
        ],
        compiler_params=pltpu.CompilerParams(
            dimension_semantics=("arbitrary",), vmem_limit_bytes=VMEM_LIMIT),
        name="inproj",
    )(x2, x2, g, w, kfeat, conv_w, gc)


def _compress_kernel(xk_ref, xv_ref, pos_ref, w1_ref, wl_ref, w2_ref, cf_ref, kc_ref, vc_ref):
    nch = xk_ref.shape[0] // CMP_STRIDE
    hd = HEAD_DIM
    acc = [jnp.zeros((nch, 2 * CMP_HIDDEN), F32) for _ in range(2 * N_KV_HEADS)]
    for l in range(CMP_STRIDE):
        for s, x_ref in enumerate((xk_ref, xv_ref)):
            xl = x_ref[pl.ds(l, nch, stride=CMP_STRIDE), :].astype(BF16)
            for h in range(N_KV_HEADS):
                c = s * N_KV_HEADS + h
                acc[c] = acc[c] + _dot(xl[:, h * hd:(h + 1) * hd], wl_ref[s, l])
    for s in range(2):
        pb = _dot(pos_ref[s], w1_ref[s])[0:1, :]
        outs = []
        for h in range(N_KV_HEADS):
            a = acc[s * N_KV_HEADS + h]
            hid = a[:, 0:CMP_HIDDEN] + pltpu.roll(a[:, CMP_HIDDEN:], nch - 1, axis=0) + pb
            outs.append(_dot(jax.nn.gelu(hid, approximate=True).astype(BF16), w2_ref[s]))
        for h in range(N_KV_HEADS):
            if s == 0:
                kc_ref[h] = cf_ref[...]
                kc_ref[h, :, 0:hd] = outs[h].astype(BF16)
            else:
                vc_ref[h] = _value_rows(jnp.concatenate(outs, axis=1), h, True)


def _compress(kvc, pos, w1, wl, w2, cfeat, B, S):
    nch = S // CMP_STRIDE
    full = lambda *shape: pl.BlockSpec(shape, lambda b: (0,) * len(shape))
    return pl.pallas_call(
        _compress_kernel,
        grid=(B,),
        in_specs=[
            pl.BlockSpec((S, LANES), lambda b: (b, 0)),
            pl.BlockSpec((S, LANES), lambda b: (b, 1)),
            full(*pos.shape), full(*w1.shape), full(*wl.shape), full(*w2.shape), full(*cfeat.shape),
        ],
        out_specs=[
            pl.BlockSpec((None, N_KV_HEADS, nch, LANES), lambda b: (b, 0, 0, 0)),
            pl.BlockSpec((None, N_KV_HEADS, nch, VW), lambda b: (b, 0, 0, 0)),
        ],
        out_shape=[
            jax.ShapeDtypeStruct((B, N_KV_HEADS, nch, LANES), BF16),
            jax.ShapeDtypeStruct((B, N_KV_HEADS, nch, VW), BF16),
        ],
        compiler_params=pltpu.CompilerParams(
            dimension_semantics=("arbitrary",), vmem_limit_bytes=VMEM_LIMIT),
        name="compress",
    )(kvc, kvc, pos, w1, wl, w2, cfeat)


def _hi_lo(x):
    hi = x.astype(BF16)
    return jnp.concatenate([hi, (x - hi.astype(F32)).astype(BF16)], axis=1)


def _head_layout(acc):
    low_half = lax.broadcasted_iota(jnp.int32, (TQ, LANES), 1) < HEAD_DIM
    head = lambda x, g: x[g * TQ:(g + 1) * TQ]
    if acc.shape[1] == VW:
        x = acc[:, 0:LANES] * (1.0 / jnp.maximum(acc[:, LANES:VW], 1e-30))
        pairs = [jnp.where(low_half, head(x, 2 * j), head(x, 2 * j + 1)) for j in range(GQA // 2)]
    else:
        rot = pltpu.roll(acc, HEAD_DIM, axis=1)
        pairs = [jnp.where(low_half, head(acc, 2 * j) * (1.0 / head(rot, 2 * j)),
                           head(rot, 2 * j + 1) * (1.0 / head(acc, 2 * j + 1))) for j in range(GQA // 2)]
    return jnp.concatenate(pairs, axis=1)


def _exchange(v, i, j, descending):
    hi, lo = jnp.maximum(v[i], v[j]), jnp.minimum(v[i], v[j])
    v[i], v[j] = (hi, lo) if descending else (lo, hi)


def _bitonic_merge_desc(v):
    n = len(v)
    j = n // 2
    while j >= 1:
        for i in range(n):
            if i ^ j > i:
                _exchange(v, i, i ^ j, True)
        j //= 2


def _kth_largest(x, kth):
    n = x.shape[0]
    assert n == kth * SUBLANES and kth & (kth - 1) == 0
    v = [x[r * SUBLANES:(r + 1) * SUBLANES, :] for r in range(kth)]
    k = 2
    while k <= kth:
        j = k // 2
        while j >= 1:
            for i in range(kth):
                if i ^ j > i:
                    _exchange(v, i, i ^ j, (i & k) == 0)
            j //= 2
        k *= 2
    shift = SUBLANES // 2
    while shift >= 1:
        v = [jnp.maximum(v[i], pltpu.roll(v[kth - 1 - i], shift, axis=0)) for i in range(kth)]
        if shift > 1:
            _bitonic_merge_desc(v)
        shift //= 2
    out = v[0]
    for i in range(1, kth):
        out = jnp.minimum(out, v[i])
    return out[0:1, :]


def _nsa_kernel(q_ref, qf_ref, kc_ref, vc_ref, ks_ref, vs_ref, kw_ref, vw_ref, ov_ref, cm_ref, mb_ref,
                ge_ref, g_ref, o_ref, qa_ref, tiles_ref, m_ref, acc_ref, sa_ref, sb_ref):
    i = pl.program_id(2)
    t0 = i * TQ
    rows = GQA * TQ
    n_sel = ov_ref.shape[1]
    ncmp = kc_ref.shape[0]

    q = q_ref[...]
    for g in range(GQA):
        qg = q[:, g * HEAD_DIM:(g + 1) * HEAD_DIM].astype(F32) * (HEAD_DIM ** -0.5)
        qa_ref[g * TQ:(g + 1) * TQ, 0:HEAD_DIM] = qg.astype(BF16)
    qa_ref[:, HEAD_DIM:LANES] = qf_ref[...]
    qa1 = qa_ref[:, 0:LANES]

    diag_ok = (lax.broadcasted_iota(jnp.int32, (rows, TQ), 1)
               <= (lax.broadcasted_iota(jnp.int32, (rows, TQ), 0) & (TQ - 1)))

    wk = TQ + WINDOW
    w0 = pl.multiple_of(t0, TQ)
    s_w = _dot_nt(qa1, kw_ref[pl.ds(w0, wk), :])
    s_old = jnp.where(diag_ok, NEG, s_w[:, 0:TQ])
    s_mid = s_w[:, TQ:WINDOW]
    s_new = jnp.where(diag_ok, s_w[:, WINDOW:wk], NEG)
    m_w = jnp.maximum(jnp.maximum(jnp.max(s_old, axis=1, keepdims=True),
                                  jnp.max(s_mid, axis=1, keepdims=True)),
                      jnp.max(s_new, axis=1, keepdims=True))
    e_w = jnp.concatenate([jnp.exp(s_old - m_w), jnp.exp(s_mid - m_w), jnp.exp(s_new - m_w)], axis=1)
    acc_w = _dot(e_w.astype(BF16), vw_ref[pl.ds(w0, wk), :])

    s_c = _dot_nt(qa1, kc_ref[...])
    s_c = jnp.where(cm_ref[...] <= (TQ // CMP_STRIDE) * i - 2, s_c, NEG)
    m_c = jnp.maximum(jnp.max(s_c, axis=1, keepdims=True), 0.5 * NEG)
    e_c = jnp.exp(s_c - m_c)
    acc_c = _dot(e_c.astype(BF16), vc_ref[...])
    inv_c = 1.0 / jnp.maximum(acc_c[:, LANES:VW], 1e-30)
    p_c = e_c * jnp.concatenate([inv_c] * (ncmp // LANES), axis=1)

    p_sum = p_c[0:TQ] + p_c[TQ:2 * TQ] + p_c[2 * TQ:3 * TQ] + p_c[3 * TQ:4 * TQ]
    p_hi = p_sum.astype(BF16)
    p_lo = (p_sum - p_hi.astype(F32)).astype(BF16)
    imp = _dot(p_hi, ov_ref[...]) + _dot(p_lo, ov_ref[...])
    imp_t = imp.T
    jb = lax.broadcasted_iota(jnp.int32, (n_sel, TQ), 0)
    tq = t0 + lax.broadcasted_iota(jnp.int32, (n_sel, TQ), 1)
    imp_t = jnp.where(SEL_BLOCK * jb > tq, -jnp.inf, imp_t)
    forced = (jb == 0) | (jb == (tq >> 6))
    imp_t = jnp.where(forced, jnp.inf, imp_t)

    thr = _kth_largest(imp_t, SEL_TOP)
    above = imp_t > thr
    tied = imp_t == thr
    n_above = jnp.sum(jnp.where(above, 1.0, 0.0), axis=0, keepdims=True)
    lower = (lax.broadcasted_iota(jnp.int32, (n_sel, n_sel), 1)
             < lax.broadcasted_iota(jnp.int32, (n_sel, n_sel), 0))
    tied_before = _dot(jnp.where(lower, 1.0, 0.0).astype(BF16), jnp.where(tied, 1.0, 0.0).astype(BF16))
    take_tie = jnp.where(tied_before + n_above < float(SEL_TOP), 1.0, 0.0)
    sel_t = jnp.where(above, 1.0, jnp.where(tied, take_tie, 0.0))
    sel_q = sel_t.T
    bias = ((1.0 - sel_q) * MASK_BIAS).astype(BF16)
    for g in range(GQA):
        qa_ref[g * TQ:(g + 1) * TQ, LANES:2 * LANES] = bias

    n_past = t0 // TK
    first_mask = 1 + (t0 - n_past * TK) // TQ
    tiles_ref[0] = n_past

    def scores(j, s_ref):
        k0 = pl.multiple_of(tiles_ref[j] * TK, TK)
        s_ref[...] = _dot_nt(qa_ref[...], ks_ref[pl.ds(k0, TK), :])

    scores(0, sa_ref)

    bpt = TK // SEL_BLOCK
    n_tiles = n_sel // bpt
    blk_cnt = _dot(jnp.ones((SUBLANES, TQ), BF16), sel_q.astype(BF16))
    in_tile = (lax.broadcasted_iota(jnp.int32, (n_sel, LANES), 0) // bpt
               == lax.broadcasted_iota(jnp.int32, (n_sel, LANES), 1))
    tile_cnt = _dot(jnp.where(blk_cnt > 0.5, 1.0, 0.0).astype(BF16),
                    jnp.where(in_tile, 1.0, 0.0).astype(BF16))
    lane = lax.broadcasted_iota(jnp.int32, (1, LANES), 1)
    bit = jnp.where(tile_cnt[0:1, :] > 0.5, jnp.left_shift(1, lane & 15).astype(F32), 0.0)
    words = [jnp.sum(jnp.where((lane >= 16 * w) & (lane < 16 * (w + 1)), bit, 0.0)).astype(jnp.int32)
             for w in range((n_tiles + 15) // 16)]
    n_tasks = jnp.int32(1)
    for kt in range(n_tiles - 1):
        used = (words[kt // 16] >> (kt % 16)) & 1
        tiles_ref[n_tasks] = kt
        n_tasks = n_tasks + jnp.where(kt < n_past, used, 0)
    tiles_ref[n_tasks] = 0
    tiles_ref[n_tasks + 1] = 0

    m_ref[...] = jnp.full(m_ref.shape, NEG, F32)
    acc_ref[...] = jnp.zeros(acc_ref.shape, F32)

    def accumulate(j, s_ref):
        k0 = pl.multiple_of(tiles_ref[j] * TK, TK)
        slot = jnp.where(j < n_tasks, 0, 1)
        s = s_ref[...] + mb_ref[jnp.where(j == 0, first_mask, 0)]
        m = m_ref[slot]
        m_new = jnp.maximum(m, jnp.max(s, axis=1, keepdims=True))
        p = jnp.concatenate([jnp.exp(s[:, c:c + LANES] - m_new) for c in range(0, TK, LANES)], axis=1)
        acc_ref[slot] = (jnp.exp(m - m_new) * acc_ref[slot]
                         + _dot(p.astype(BF16), vs_ref[pl.ds(k0, TK), :]))
        m_ref[slot] = m_new

    def pair(jj, c):
        a = 2 * jj
        scores(a + 1, sb_ref)
        accumulate(a, sa_ref)
        scores(a + 2, sa_ref)
        accumulate(a + 1, sb_ref)
        return c

    lax.fori_loop(0, (n_tasks + 1) // 2, pair, 0)

    gates = _dot(_hi_lo(g_ref[...]), ge_ref[...])
    gw = GQA * HEAD_DIM
    o_ref[...] = (gates[:, 0:gw] * _head_layout(acc_c) + gates[:, gw:2 * gw] * _head_layout(acc_ref[0])
                  + gates[:, 2 * gw:3 * gw] * _head_layout(acc_w))


def _nsa(q, qfeat, kc_aug, vc, ks_aug, vs, kw_aug, vw, ov, cmask, tile_masks, gate_expand, gates, B, S):
    nq = S // TQ
    gw = GQA * HEAD_DIM
    ncmp = kc_aug.shape[2]
    n_sel = ov.shape[1]
    sp = kw_aug.shape[2]
    bk = lambda b, k, i: (b, k, 0, 0)
    kb = lambda b, k, i: (k, b, 0, 0)
    return pl.pallas_call(
        _nsa_kernel,
        grid=(B, N_KV_HEADS, nq),
        in_specs=[
            pl.BlockSpec((TQ, gw), lambda b, k, i: (b * nq + i, k)),
            pl.BlockSpec((None, GQA * TQ, HEAD_DIM), lambda b, k, i: (k, 0, 0)),
            pl.BlockSpec((None, None, ncmp, LANES), bk),
            pl.BlockSpec((None, None, ncmp, VW), bk),
            pl.BlockSpec((None, None, S, 2 * LANES), kb),
            pl.BlockSpec((None, None, S, LANES), kb),
            pl.BlockSpec((None, None, sp, LANES), kb),
            pl.BlockSpec((None, None, sp, VW), kb),
            pl.BlockSpec((ncmp, n_sel), lambda b, k, i: (0, 0)),
            pl.BlockSpec((GQA * TQ, ncmp), lambda b, k, i: (0, 0)),
            pl.BlockSpec(tile_masks.shape, lambda b, k, i: (0, 0, 0)),
            pl.BlockSpec(gate_expand.shape, lambda b, k, i: (0, 0)),
            pl.BlockSpec((TQ, LANES), lambda b, k, i: (b * nq + i, k)),
        ],
        out_specs=pl.BlockSpec((TQ, gw), lambda b, k, i: (b * nq + i, k)),
        out_shape=jax.ShapeDtypeStruct((B * S, N_KV_HEADS * gw), F32),
        scratch_shapes=[
            pltpu.VMEM((GQA * TQ, 2 * LANES), BF16),
            pltpu.SMEM((S // TK + 2,), jnp.int32),
            pltpu.VMEM((2, GQA * TQ, LANES), F32),
            pltpu.VMEM((2, GQA * TQ, LANES), F32),
            pltpu.VMEM((GQA * TQ, TK), F32),
            pltpu.VMEM((GQA * TQ, TK), F32),
        ],
        compiler_params=pltpu.CompilerParams(
            dimension_semantics=("arbitrary",) * 3, vmem_limit_bytes=VMEM_LIMIT),
        name="nsa",
    )(q, qfeat, kc_aug, vc, ks_aug, vs, kw_aug, vw, ov, cmask, tile_masks, gate_expand, gates)


_FF_CHUNK = 1408


def _outproj_kernel(mc_ref, nsa_ref, x_ref, gn_ref, wo_ref, h_ref):
    mixed_n = _rms(nsa_ref[...], gn_ref[...]).astype(BF16)
    h_ref[...] = (x_ref[...] + _dot(mc_ref[...].astype(BF16), wo_ref[0:CONV_CH, :])
                  + _dot(mixed_n, wo_ref[CONV_CH:2 * CONV_CH, :]))


def _outproj(mixed_c, nsa_out, x2, gn, wo, tm=512):
    rows = x2.shape[0]
    row = lambda i: (i, 0)
    const = lambda i: (0, 0)
    return pl.pallas_call(
        _outproj_kernel,
        grid=(rows // tm,),
        in_specs=[
            pl.BlockSpec((tm, CONV_CH), row),
            pl.BlockSpec((tm, N_HEADS * HEAD_DIM), row),
            pl.BlockSpec((tm, D_MODEL), row),
            pl.BlockSpec((1, N_HEADS * HEAD_DIM), const),
            pl.BlockSpec((2 * CONV_CH, D_MODEL), const, pipeline_mode=pl.Buffered(1)),
        ],
        out_specs=pl.BlockSpec((tm, D_MODEL), row),
        out_shape=jax.ShapeDtypeStruct((rows, D_MODEL), F32),
        compiler_params=pltpu.CompilerParams(
            dimension_semantics=("arbitrary",), vmem_limit_bytes=VMEM_LIMIT),
        name="outproj",
    )(mixed_c, nsa_out, x2, gn, wo)


def _outffn_kernel(h_ref, g2_ref, wg_ref, wu_ref, wd_ref, gf_ref, o_ref):
    h = h_ref[...]
    u = _rms(h, g2_ref[...]).astype(BF16)
    acc = h
    for c in range(D_FF // _FF_CHUNK):
        sl = slice(c * _FF_CHUNK, (c + 1) * _FF_CHUNK)
        a = jax.nn.silu(_dot(u, wg_ref[:, sl])) * _dot(u, wu_ref[:, sl])
        acc = acc + _dot(a.astype(BF16), wd_ref[sl, :])
    o_ref[...] = _rms(acc, gf_ref[...])


def _outffn(h, g2, wg, wu, wd, gf, tm=512):
    rows = h.shape[0]
    row = lambda i: (i, 0)
    const = lambda i: (0, 0)
    resident = lambda shape: pl.BlockSpec(shape, const, pipeline_mode=pl.Buffered(1))
    return pl.pallas_call(
        _outffn_kernel,
        grid=(rows // tm,),
        in_specs=[
            pl.BlockSpec((tm, D_MODEL), row),
            pl.BlockSpec((1, D_MODEL), const),
            resident((D_MODEL, D_FF)),
            resident((D_MODEL, D_FF)),
            resident((D_FF, D_MODEL)),
            pl.BlockSpec((1, D_MODEL), const),
        ],
        out_specs=pl.BlockSpec((tm, D_MODEL), row),
        out_shape=jax.ShapeDtypeStruct((rows, D_MODEL), F32),
        compiler_params=pltpu.CompilerParams(
            dimension_semantics=("arbitrary",), vmem_limit_bytes=VMEM_LIMIT),
        name="outffn",
    )(h, g2, wg, wu, wd, gf)


def _key_features(pos, n_sel):
    f = np.zeros((pos.shape[0], LANES + n_sel), np.float32)
    f[:, HEAD_DIM] = pos // SEL_BLOCK
    f[:, HEAD_DIM + 1] = pos % SEL_BLOCK
    if n_sel:
        f[np.arange(pos.shape[0]), LANES + pos // SEL_BLOCK] = 1.0
    return f


def _query_features():
    f = np.zeros((N_KV_HEADS, GQA * TQ, HEAD_DIM), np.float32)
    for k in range(N_KV_HEADS):
        for g in range(GQA):
            slope = 2.0 ** (-8.0 * (k * GQA + g + 1) / N_HEADS)
            f[k, g * TQ:(g + 1) * TQ, 0] = slope * SEL_BLOCK
            f[k, g * TQ:(g + 1) * TQ, 1] = slope
            f[k, g * TQ:(g + 1) * TQ, 2] = MASK_BIAS
    return f


def _causal_masks(ncmp):
    r = np.arange(GQA * TQ)[:, None] % TQ
    cmask = (np.arange(ncmp)[None, :] - ((r + 1) // CMP_STRIDE)).astype(np.int32)
    c = np.arange(TK)[None, :]
    tile_masks = [np.zeros((GQA * TQ, TK), np.float32)]
    for d in range(TK // TQ):
        tile_masks.append(np.where(c - r <= d * TQ, 0.0, NEG).astype(np.float32))
    return cmask, np.stack(tile_masks)


def _gate_expand():
    gw = GQA * HEAD_DIM
    e = np.zeros((2 * LANES, N_BRANCH * gw), np.float32)
    for g in range(GQA):
        for br in range(N_BRANCH):
            for half in range(2):
                e[half * LANES + g * N_BRANCH + br, br * gw + g * HEAD_DIM:br * gw + (g + 1) * HEAD_DIM] = 1.0
    return e


def _overlap(ncmp_pad, n_sel):
    nc = ncmp_pad - 1
    c_start = CMP_STRIDE * np.arange(nc)
    c_end = c_start + CMP_BLOCK - 1
    s_start = SEL_BLOCK * np.arange(n_sel)
    ov = np.minimum(c_end[:, None] + 1, s_start[None, :] + SEL_BLOCK) - np.maximum(c_start[:, None], s_start[None, :])
    out = np.zeros((ncmp_pad, n_sel), np.float32)
    out[:nc] = np.clip(ov, 0, None).astype(np.float32) / CMP_BLOCK
    return out


def _layer(x2, B, S, norm1_g, w_in, conv_w, k_cmp_pos, k_cmp_w1, k_cmp_w2, v_cmp_pos, v_cmp_w1, v_cmp_w2,
           gn_conv_g):
    n_sel = S // SEL_BLOCK
    nch = S // CMP_STRIDE
    hd = HEAD_DIM

    ng = GQA * N_BRANCH
    w_main, w_g = w_in[:, :_G_OFF], w_in[:, _G_OFF:]
    gpad = jnp.zeros((D_MODEL, LANES - ng), w_in.dtype)
    w_pad = jnp.concatenate([w_main, w_g[:, :ng], gpad, w_g[:, ng:], gpad], axis=1).astype(BF16)

    kfeat = jnp.asarray(_key_features(np.arange(S), n_sel), BF16)
    mixed_c, q, gates, kvc, ks_aug, vs, kw_aug, vw = _inproj(
        x2, norm1_g.reshape(1, D_MODEL), w_pad, kfeat, conv_w, gn_conv_g.reshape(1, CONV_CH))

    w1 = jnp.stack([k_cmp_w1, v_cmp_w1]).astype(BF16)
    wl = w1.reshape(2, 2, CMP_STRIDE, hd, CMP_HIDDEN).transpose(0, 2, 3, 1, 4)
    wl = wl.reshape(2, CMP_STRIDE, hd, 2 * CMP_HIDDEN)
    w2 = jnp.stack([k_cmp_w2, v_cmp_w2]).astype(BF16)
    pos = jnp.stack([k_cmp_pos, v_cmp_pos]).reshape(2, 1, CMP_BLOCK * hd)
    pos = jnp.broadcast_to(pos, (2, SUBLANES, CMP_BLOCK * hd)).astype(BF16)
    c_end = CMP_STRIDE * np.arange(nch) + CMP_BLOCK - 1
    cfeat = jnp.asarray(_key_features(c_end, 0), BF16)
    kc_aug, vc = _compress(kvc, pos, w1, wl, w2, cfeat, B, S)

    pad_rows = np.zeros((WINDOW, LANES), np.float32)
    pad_rows[:, hd + 2] = 1.0
    pad_k = jnp.broadcast_to(jnp.asarray(pad_rows, BF16), (N_KV_HEADS, B, WINDOW, LANES))
    kw_aug = jnp.concatenate([pad_k, kw_aug.reshape(N_KV_HEADS, B, S, LANES)], axis=2)
    vw = jnp.pad(vw.reshape(N_KV_HEADS, B, S, VW), ((0, 0), (0, 0), (WINDOW, 0), (0, 0)))
    cmask, tile_masks = _causal_masks(nch)

    nsa_out = _nsa(q, jnp.asarray(_query_features(), BF16), kc_aug, vc,
                   ks_aug.reshape(N_KV_HEADS, B, S, 2 * LANES), vs.reshape(N_KV_HEADS, B, S, LANES),
                   kw_aug, vw, jnp.asarray(_overlap(nch, n_sel), BF16), jnp.asarray(cmask),
                   jnp.asarray(tile_masks), jnp.asarray(_gate_expand(), BF16), gates, B, S)
    return mixed_c, nsa_out


def kernel(x, norm1_g, w_in, conv_w, k_cmp_pos, k_cmp_w1, k_cmp_w2, v_cmp_pos, v_cmp_w1, v_cmp_w2,
           gn_conv_g, gn_nsa_g, w_out, norm2_g, w_gate, w_up, w_down, norm_f_g):
    B, S, _ = x.shape
    depth = norm1_g.shape[0]
    assert depth == 1, "the final norm is fused into the (single) layer's FFN kernel"
    x2 = x.reshape(B * S, D_MODEL)
    mixed_c, nsa_out = _layer(x2, B, S, norm1_g[0], w_in[0], conv_w[0], k_cmp_pos[0], k_cmp_w1[0],
                              k_cmp_w2[0], v_cmp_pos[0], v_cmp_w1[0], v_cmp_w2[0], gn_conv_g[0])
    h = _outproj(mixed_c, nsa_out, x2, gn_nsa_g[0].reshape(1, N_HEADS * HEAD_DIM), w_out[0].astype(BF16))
    out = _outffn(h, norm2_g[0].reshape(1, D_MODEL), w_gate[0].astype(BF16), w_up[0].astype(BF16),
                  w_down[0].astype(BF16), norm_f_g.reshape(1, D_MODEL))
    return out.reshape(B, S, D_MODEL)
```

```python
import functools

import numpy as np
import jax
import jax.numpy as jnp
from jax import lax
from jax.experimental import pallas as pl
from jax.experimental.pallas import tpu as pltpu

D_MODEL = 1024
CONV_CH = 512
CONV_K = 3
N_HEADS = 8
HEAD_DIM = 64
N_KV_HEADS = 2
GQA = 4
N_BRANCH = 3
CMP_BLOCK = 32
CMP_STRIDE = 16
CMP_HIDDEN = 256
SEL_BLOCK = 64
SEL_TOP = 16
WINDOW = 512
D_FF = 2816
NORM_EPS = 1e-6

F32 = jnp.float32
BF16 = jnp.bfloat16

LANES = 128
VW = 2 * LANES
SUBLANES = 8
VMEM_LIMIT = 56 * 1024 * 1024

TQ = 256
TK = 256
NEG = -1e30
MASK_BIAS = -float(2 ** 30)

_CONV_W = 3 * CONV_CH
_Q_OFF = _CONV_W
_KVC_OFF = _Q_OFF + N_HEADS * HEAD_DIM
_KV_OFF = _KVC_OFF + 2 * N_KV_HEADS * HEAD_DIM
_G_OFF = _KV_OFF + 4 * N_KV_HEADS * HEAD_DIM
_G_W = N_KV_HEADS * LANES
_PROJ_W = _G_OFF + _G_W


def _rms(x, g):
    return x * lax.rsqrt(jnp.mean(x * x, axis=-1, keepdims=True) + NORM_EPS) * g


def _dot(a, b):
    return jnp.dot(a, b, preferred_element_type=F32)


def _value_rows(v2, h, wide):
    lane = lax.broadcasted_iota(jnp.int32, v2.shape, 1)
    swapped = pltpu.roll(v2, HEAD_DIM, axis=1)
    if not wide:
        return jnp.where(lane < HEAD_DIM, v2 if h == 0 else swapped, 1.0).astype(BF16)
    dup = jnp.where((lane < HEAD_DIM) == (h == 0), v2, swapped)
    return jnp.concatenate([dup.astype(BF16), jnp.ones(v2.shape, BF16)], axis=1)


def _dot_nt(a, b):
    return lax.dot_general(a, b, (((1,), (1,)), ((), ())), preferred_element_type=F32)


def _inproj_kernel(seq_tiles, x_ref, xp_ref, g_ref, w_ref, kf_ref, cw_ref, gc_ref, mixc_ref, q_ref, gate_ref,
                   kvc_ref, ks_ref, vs_ref, kw_ref, vw_ref):
    u = _rms(x_ref[...], g_ref[...]).astype(BF16)

    cbch = _dot(u, w_ref[:, 0:_CONV_W])
    cu = cbch[:, CONV_CH:2 * CONV_CH] * cbch[:, 2 * CONV_CH:3 * CONV_CH]
    up = _rms(xp_ref[...], g_ref[...]).astype(BF16)
    chp = _dot(up, w_ref[:, CONV_CH:_CONV_W])
    starts_sequence = pl.program_id(0) % seq_tiles == 0
    prev = jnp.where(starts_sequence, 0.0, chp[:, 0:CONV_CH] * chp[:, CONV_CH:2 * CONV_CH])
    r = lax.broadcasted_iota(jnp.int32, cu.shape, 0)
    cu1 = jnp.where(r == 0, prev[7:8, :], pltpu.roll(cu, 1, axis=0))
    cu2 = pltpu.roll(cu, 2, axis=0)
    cu2 = jnp.where(r == 0, prev[6:7, :], jnp.where(r == 1, prev[7:8, :], cu2))
    cw = cw_ref[...]
    y = cbch[:, 0:CONV_CH] * (cw[0:1, :] * cu2 + cw[1:2, :] * cu1 + cw[2:3, :] * cu)
    mixc_ref[...] = _rms(y, gc_ref[...])

    q_ref[...] = _dot(u, w_ref[:, _Q_OFF:_KVC_OFF]).astype(BF16)
    kvc_ref[...] = _dot(u, w_ref[:, _KVC_OFF:_KV_OFF])
    gate_ref[...] = jax.nn.sigmoid(_dot(u, w_ref[:, _G_OFF:_PROJ_W]))
    kv = _dot(u, w_ref[:, _KV_OFF:_G_OFF])
    kf = kf_ref[...]
    hd, kvw = HEAD_DIM, N_KV_HEADS * HEAD_DIM
    for h in range(N_KV_HEADS):
        ks_ref[h] = kf
        ks_ref[h, :, 0:hd] = kv[:, h * hd:(h + 1) * hd].astype(BF16)
        vs_ref[h] = _value_rows(kv[:, kvw:2 * kvw], h, False)
        kw_ref[h] = kf[:, 0:LANES]
        kw_ref[h, :, 0:hd] = kv[:, 2 * kvw + h * hd:2 * kvw + (h + 1) * hd].astype(BF16)
        vw_ref[h] = _value_rows(kv[:, 3 * kvw:4 * kvw], h, True)


def _inproj(x2, g, w, kfeat, conv_w, gc, tm=512):
    rows = x2.shape[0]
    seq_tiles = kfeat.shape[0] // tm
    row = lambda i: (i, 0)
    head_row = lambda i: (0, i, 0)
    const = lambda i: (0, 0)
    return pl.pallas_call(
        functools.partial(_inproj_kernel, seq_tiles),
        grid=(rows // tm,),
        in_specs=[
            pl.BlockSpec((tm, D_MODEL), row),
            pl.BlockSpec((SUBLANES, D_MODEL), lambda i: (jnp.maximum(i * (tm // SUBLANES) - 1, 0), 0)),
            pl.BlockSpec((1, D_MODEL), const),
            pl.BlockSpec((D_MODEL, _PROJ_W), const, pipeline_mode=pl.Buffered(1)),
            pl.BlockSpec((tm, 2 * LANES), lambda i: (i % seq_tiles, 0)),
            pl.BlockSpec((CONV_K, CONV_CH), const),
            pl.BlockSpec((1, CONV_CH), const),
        ],
        out_specs=[
            pl.BlockSpec((tm, CONV_CH), row),
            pl.BlockSpec((tm, N_HEADS * HEAD_DIM), row),
            pl.BlockSpec((tm, _G_W), row),
            pl.BlockSpec((tm, _KV_OFF - _KVC_OFF), row),
            pl.BlockSpec((N_KV_HEADS, tm, 2 * LANES), head_row),
            pl.BlockSpec((N_KV_HEADS, tm, LANES), head_row),
            pl.BlockSpec((N_KV_HEADS, tm, LANES), head_row),
            pl.BlockSpec((N_KV_HEADS, tm, VW), head_row),
        ],
        out_shape=[
            jax.ShapeDtypeStruct((rows, CONV_CH), F32),
            jax.ShapeDtypeStruct((rows, N_HEADS * HEAD_DIM), BF16),
            jax.ShapeDtypeStruct((rows, _G_W), F32),
            jax.ShapeDtypeStruct((rows, _KV_OFF - _KVC_OFF), F32),
            jax.ShapeDtypeStruct((N_KV_HEADS, rows, 2 * LANES), BF16),
            jax.ShapeDtypeStruct((N_KV_HEADS, rows, LANES), BF16),
            jax.ShapeDtypeStruct((N_KV_HEADS, rows, LANES), BF16),
            jax.ShapeDtypeStruct((N_KV_HEADS, rows, VW), BF16),
        ],
        compiler_params=pltpu.CompilerParams(
            dimension_semantics=("arbitrary",), vmem_limit_bytes=VMEM_LIMIT),
        name="inproj",
    )(x2, x2, g, w, kfeat, conv_w, gc)


def _compress_kernel(xk_ref, xv_ref, pos_ref, w1_ref, wl_ref, w2_ref, cf_ref, kc_ref, vc_ref):
    nch = xk_ref.shape[0] // CMP_STRIDE
    hd = HEAD_DIM
    acc = [jnp.zeros((nch, 2 * CMP_HIDDEN), F32) for _ in range(2 * N_KV_HEADS)]
    for l in range(CMP_STRIDE):
        for s, x_ref in enumerate((xk_ref, xv_ref)):
            xl = x_ref[pl.ds(l, nch, stride=CMP_STRIDE), :].astype(BF16)
            for h in range(N_KV_HEADS):
                c = s * N_KV_HEADS + h
                acc[c] = acc[c] + _dot(xl[:, h * hd:(h + 1) * hd], wl_ref[s, l])
    for s in range(2):
        pb = _dot(pos_ref[s], w1_ref[s])[0:1, :]
        outs = []
        for h in range(N_KV_HEADS):
            a = acc[s * N_KV_HEADS + h]
            hid = a[:, 0:CMP_HIDDEN] + pltpu.roll(a[:, CMP_HIDDEN:], nch - 1, axis=0) + pb
            outs.append(_dot(jax.nn.gelu(hid, approximate=True).astype(BF16), w2_ref[s]))
        for h in range(N_KV_HEADS):
            if s == 0:
                kc_ref[h] = cf_ref[...]
                kc_ref[h, :, 0:hd] = outs[h].astype(BF16)
            else:
                vc_ref[h] = _value_rows(jnp.concatenate(outs, axis=1), h, True)


def _compress(kvc, pos, w1, wl, w2, cfeat, B, S):
    nch = S // CMP_STRIDE
    full = lambda *shape: pl.BlockSpec(shape, lambda b: (0,) * len(shape))
    return pl.pallas_call(
        _compress_kernel,
        grid=(B,),
        in_specs=[
            pl.BlockSpec((S, LANES), lambda b: (b, 0)),
            pl.BlockSpec((S, LANES), lambda b: (b, 1)),
            full(*pos.shape), full(*w1.shape), full(*wl.shape), full(*w2.shape), full(*cfeat.shape),
        ],
        out_specs=[
            pl.BlockSpec((None, N_KV_HEADS, nch, LANES), lambda b: (b, 0, 0, 0)),
            pl.BlockSpec((None, N_KV_HEADS, nch, VW), lambda b: (b, 0, 0, 0)),
        ],
        out_shape=[
            jax.ShapeDtypeStruct((B, N_KV_HEADS, nch, LANES), BF16),
            jax.ShapeDtypeStruct((B, N_KV_HEADS, nch, VW), BF16),
        ],
        compiler_params=pltpu.CompilerParams(
            dimension_semantics=("arbitrary",), vmem_limit_bytes=VMEM_LIMIT),
        name="compress",
    )(kvc, kvc, pos, w1, wl, w2, cfeat)


def _hi_lo(x):
    hi = x.astype(BF16)
    return jnp.concatenate([hi, (x - hi.astype(F32)).astype(BF16)], axis=1)


def _head_layout(acc):
    low_half = lax.broadcasted_iota(jnp.int32, (TQ, LANES), 1) < HEAD_DIM
    head = lambda x, g: x[g * TQ:(g + 1) * TQ]
    if acc.shape[1] == VW:
        x = acc[:, 0:LANES] * (1.0 / jnp.maximum(acc[:, LANES:VW], 1e-30))
        pairs = [jnp.where(low_half, head(x, 2 * j), head(x, 2 * j + 1)) for j in range(GQA // 2)]
    else:
        rot = pltpu.roll(acc, HEAD_DIM, axis=1)
        pairs = [jnp.where(low_half, head(acc, 2 * j) * (1.0 / head(rot, 2 * j)),
                           head(rot, 2 * j + 1) * (1.0 / head(acc, 2 * j + 1))) for j in range(GQA // 2)]
    return jnp.concatenate(pairs, axis=1)


def _exchange(v, i, j, descending):
    hi, lo = jnp.maximum(v[i], v[j]), jnp.minimum(v[i], v[j])
    v[i], v[j] = (hi, lo) if descending else (lo, hi)


def _bitonic_merge_desc(v):
    n = len(v)
    j = n // 2
    while j >= 1:
        for i in range(n):
            if i ^ j > i:
                _exchange(v, i, i ^ j, True)
        j //= 2


def _kth_largest(x, kth):
    n = x.shape[0]
    assert n == kth * SUBLANES and kth & (kth - 1) == 0
    v = [x[r * SUBLANES:(r + 1) * SUBLANES, :] for r in range(kth)]
    k = 2
    while k <= kth:
        j = k // 2
        while j >= 1:
            for i in range(kth):
                if i ^ j > i:
                    _exchange(v, i, i ^ j, (i & k) == 0)
            j //= 2
        k *= 2
    shift = SUBLANES // 2
    while shift >= 1:
        v = [jnp.maximum(v[i], pltpu.roll(v[kth - 1 - i], shift, axis=0)) for i in range(kth)]
        if shift > 1:
            _bitonic_merge_desc(v)
        shift //= 2
    out = v[0]
    for i in range(1, kth):
        out = jnp.minimum(out, v[i])
    return out[0:1, :]


def _nsa_kernel(q_ref, qf_ref, kc_ref, vc_ref, ks_ref, vs_ref, kw_ref, vw_ref, ov_ref, cm_ref, mb_ref,
                ge_ref, g_ref, o_ref, qa_ref, tiles_ref, m_ref, acc_ref, sa_ref, sb_ref):
    i = pl.program_id(2)
    t0 = i * TQ
    rows = GQA * TQ
    n_sel = ov_ref.shape[1]
    ncmp = kc_ref.shape[0]

    q = q_ref[...]
    for g in range(GQA):
        qg = q[:, g * HEAD_DIM:(g + 1) * HEAD_DIM].astype(F32) * (HEAD_DIM ** -0.5)
        qa_ref[g * TQ:(g + 1) * TQ, 0:HEAD_DIM] = qg.astype(BF16)
    qa_ref[:, HEAD_DIM:LANES] = qf_ref[...]
    qa1 = qa_ref[:, 0:LANES]

    diag_ok = (lax.broadcasted_iota(jnp.int32, (rows, TQ), 1)
               <= (lax.broadcasted_iota(jnp.int32, (rows, TQ), 0) & (TQ - 1)))

    wk = TQ + WINDOW
    w0 = pl.multiple_of(t0, TQ)
    s_w = _dot_nt(qa1, kw_ref[pl.ds(w0, wk), :])
    s_old = jnp.where(diag_ok, NEG, s_w[:, 0:TQ])
    s_mid = s_w[:, TQ:WINDOW]
    s_new = jnp.where(diag_ok, s_w[:, WINDOW:wk], NEG)
    m_w = jnp.maximum(jnp.maximum(jnp.max(s_old, axis=1, keepdims=True),
                                  jnp.max(s_mid, axis=1, keepdims=True)),
                      jnp.max(s_new, axis=1, keepdims=True))
    e_w = jnp.concatenate([jnp.exp(s_old - m_w), jnp.exp(s_mid - m_w), jnp.exp(s_new - m_w)], axis=1)
    acc_w = _dot(e_w.astype(BF16), vw_ref[pl.ds(w0, wk), :])

    s_c = _dot_nt(qa1, kc_ref[...])
    s_c = jnp.where(cm_ref[...] <= (TQ // CMP_STRIDE) * i - 2, s_c, NEG)
    m_c = jnp.maximum(jnp.max(s_c, axis=1, keepdims=True), 0.5 * NEG)
    e_c = jnp.exp(s_c - m_c)
    acc_c = _dot(e_c.astype(BF16), vc_ref[...])
    inv_c = 1.0 / jnp.maximum(acc_c[:, LANES:VW], 1e-30)
    p_c = e_c * jnp.concatenate([inv_c] * (ncmp // LANES), axis=1)

    p_sum = p_c[0:TQ] + p_c[TQ:2 * TQ] + p_c[2 * TQ:3 * TQ] + p_c[3 * TQ:4 * TQ]
    p_hi = p_sum.astype(BF16)
    p_lo = (p_sum - p_hi.astype(F32)).astype(BF16)
    imp = _dot(p_hi, ov_ref[...]) + _dot(p_lo, ov_ref[...])
    imp_t = imp.T
    jb = lax.broadcasted_iota(jnp.int32, (n_sel, TQ), 0)
    tq = t0 + lax.broadcasted_iota(jnp.int32, (n_sel, TQ), 1)
    imp_t = jnp.where(SEL_BLOCK * jb > tq, -jnp.inf, imp_t)
    forced = (jb == 0) | (jb == (tq >> 6))
    imp_t = jnp.where(forced, jnp.inf, imp_t)

    thr = _kth_largest(imp_t, SEL_TOP)
    above = imp_t > thr
    tied = imp_t == thr
    n_above = jnp.sum(jnp.where(above, 1.0, 0.0), axis=0, keepdims=True)
    lower = (lax.broadcasted_iota(jnp.int32, (n_sel, n_sel), 1)
             < lax.broadcasted_iota(jnp.int32, (n_sel, n_sel), 0))
    tied_before = _dot(jnp.where(lower, 1.0, 0.0).astype(BF16), jnp.where(tied, 1.0, 0.0).astype(BF16))
    take_tie = jnp.where(tied_before + n_above < float(SEL_TOP), 1.0, 0.0)
    sel_t = jnp.where(above, 1.0, jnp.where(tied, take_tie, 0.0))
    sel_q = sel_t.T
    bias = ((1.0 - sel_q) * MASK_BIAS).astype(BF16)
    for g in range(GQA):
        qa_ref[g * TQ:(g + 1) * TQ, LANES:2 * LANES] = bias

    n_past = t0 // TK
    first_mask = (t0 - n_past * TK) // TQ
    tiles_ref[0] = n_past

    def scores(j, s_ref, mask=None):
        k0 = pl.multiple_of(tiles_ref[j] * TK, TK)
        s = _dot_nt(qa_ref[...], ks_ref[pl.ds(k0, TK), :])
        s_ref[...] = s if mask is None else s + mask

    scores(0, sa_ref, mb_ref[first_mask])

    bpt = TK // SEL_BLOCK
    n_tiles = n_sel // bpt
    blk_cnt = _dot(jnp.ones((SUBLANES, TQ), BF16), sel_q.astype(BF16))
    in_tile = (lax.broadcasted_iota(jnp.int32, (n_sel, LANES), 0) // bpt
               == lax.broadcasted_iota(jnp.int32, (n_sel, LANES), 1))
    tile_cnt = _dot(jnp.where(blk_cnt > 0.5, 1.0, 0.0).astype(BF16),
                    jnp.where(in_tile, 1.0, 0.0).astype(BF16))
    lane = lax.broadcasted_iota(jnp.int32, (1, LANES), 1)
    bit = jnp.where(tile_cnt[0:1, :] > 0.5, jnp.left_shift(1, lane & 15).astype(F32), 0.0)
    words = [jnp.sum(jnp.where((lane >= 16 * w) & (lane < 16 * (w + 1)), bit, 0.0)).astype(jnp.int32)
             for w in range((n_tiles + 15) // 16)]
    n_tasks = jnp.int32(1)
    for kt in range(n_tiles - 1):
        used = (words[kt // 16] >> (kt % 16)) & 1
        tiles_ref[n_tasks] = kt
        n_tasks = n_tasks + jnp.where(kt < n_past, used, 0)
    tiles_ref[n_tasks] = 0
    tiles_ref[n_tasks + 1] = 0

    m_ref[...] = jnp.full(m_ref.shape, NEG, F32)
    acc_ref[...] = jnp.zeros(acc_ref.shape, F32)

    def accumulate(j, s_ref):
        k0 = pl.multiple_of(tiles_ref[j] * TK, TK)
        slot = jnp.where(j < n_tasks, 0, 1)
        s = s_ref[...]
        m = m_ref[slot]
        m_new = jnp.maximum(m, jnp.max(s, axis=1, keepdims=True))
        p = jnp.concatenate([jnp.exp(s[:, c:c + LANES] - m_new) for c in range(0, TK, LANES)], axis=1)
        acc_ref[slot] = (jnp.exp(m - m_new) * acc_ref[slot]
                         + _dot(p.astype(BF16), vs_ref[pl.ds(k0, TK), :]))
        m_ref[slot] = m_new

    def pair(jj, c):
        a = 2 * jj
        scores(a + 1, sb_ref)
        accumulate(a, sa_ref)
        scores(a + 2, sa_ref)
        accumulate(a + 1, sb_ref)
        return c

    lax.fori_loop(0, (n_tasks + 1) // 2, pair, 0)

    gates = _dot(_hi_lo(g_ref[...]), ge_ref[...])
    gw = GQA * HEAD_DIM
    o_ref[...] = (gates[:, 0:gw] * _head_layout(acc_c) + gates[:, gw:2 * gw] * _head_layout(acc_ref[0])
                  + gates[:, 2 * gw:3 * gw] * _head_layout(acc_w))


def _nsa(q, qfeat, kc_aug, vc, ks_aug, vs, kw_aug, vw, ov, cmask, tile_masks, gate_expand, gates, B, S):
    nq = S // TQ
    gw = GQA * HEAD_DIM
    ncmp = kc_aug.shape[2]
    n_sel = ov.shape[1]
    sp = kw_aug.shape[2]
    bk = lambda b, k, i: (b, k, 0, 0)
    kb = lambda b, k, i: (k, b, 0, 0)
    return pl.pallas_call(
        _nsa_kernel,
        grid=(B, N_KV_HEADS, nq),
        in_specs=[
            pl.BlockSpec((TQ, gw), lambda b, k, i: (b * nq + i, k)),
            pl.BlockSpec((None, GQA * TQ, HEAD_DIM), lambda b, k, i: (k, 0, 0)),
            pl.BlockSpec((None, None, ncmp, LANES), bk),
            pl.BlockSpec((None, None, ncmp, VW), bk),
            pl.BlockSpec((None, None, S, 2 * LANES), kb),
            pl.BlockSpec((None, None, S, LANES), kb),
            pl.BlockSpec((None, None, sp, LANES), kb),
            pl.BlockSpec((None, None, sp, VW), kb),
            pl.BlockSpec((ncmp, n_sel), lambda b, k, i: (0, 0)),
            pl.BlockSpec((GQA * TQ, ncmp), lambda b, k, i: (0, 0)),
            pl.BlockSpec(tile_masks.shape, lambda b, k, i: (0, 0, 0)),
            pl.BlockSpec(gate_expand.shape, lambda b, k, i: (0, 0)),
            pl.BlockSpec((TQ, LANES), lambda b, k, i: (b * nq + i, k)),
        ],
        out_specs=pl.BlockSpec((TQ, gw), lambda b, k, i: (b * nq + i, k)),
        out_shape=jax.ShapeDtypeStruct((B * S, N_KV_HEADS * gw), F32),
        scratch_shapes=[
            pltpu.VMEM((GQA * TQ, 2 * LANES), BF16),
            pltpu.SMEM((S // TK + 2,), jnp.int32),
            pltpu.VMEM((2, GQA * TQ, LANES), F32),
            pltpu.VMEM((2, GQA * TQ, LANES), F32),
            pltpu.VMEM((GQA * TQ, TK), F32),
            pltpu.VMEM((GQA * TQ, TK), F32),
        ],
        compiler_params=pltpu.CompilerParams(
            dimension_semantics=("arbitrary",) * 3, vmem_limit_bytes=VMEM_LIMIT),
        name="nsa",
    )(q, qfeat, kc_aug, vc, ks_aug, vs, kw_aug, vw, ov, cmask, tile_masks, gate_expand, gates)


_FF_CHUNK = 1408


def _outproj_kernel(mc_ref, nsa_ref, x_ref, gn_ref, wo_ref, h_ref):
    mixed_n = _rms(nsa_ref[...], gn_ref[...]).astype(BF16)
    h_ref[...] = (x_ref[...] + _dot(mc_ref[...].astype(BF16), wo_ref[0:CONV_CH, :])
                  + _dot(mixed_n, wo_ref[CONV_CH:2 * CONV_CH, :]))


def _outproj(mixed_c, nsa_out, x2, gn, wo, tm=512):
    rows = x2.shape[0]
    row = lambda i: (i, 0)
    const = lambda i: (0, 0)
    return pl.pallas_call(
        _outproj_kernel,
        grid=(rows // tm,),
        in_specs=[
            pl.BlockSpec((tm, CONV_CH), row),
            pl.BlockSpec((tm, N_HEADS * HEAD_DIM), row),
            pl.BlockSpec((tm, D_MODEL), row),
            pl.BlockSpec((1, N_HEADS * HEAD_DIM), const),
            pl.BlockSpec((2 * CONV_CH, D_MODEL), const, pipeline_mode=pl.Buffered(1)),
        ],
        out_specs=pl.BlockSpec((tm, D_MODEL), row),
        out_shape=jax.ShapeDtypeStruct((rows, D_MODEL), F32),
        compiler_params=pltpu.CompilerParams(
            dimension_semantics=("arbitrary",), vmem_limit_bytes=VMEM_LIMIT),
        name="outproj",
    )(mixed_c, nsa_out, x2, gn, wo)


def _outffn_kernel(h_ref, g2_ref, wg_ref, wu_ref, wd_ref, gf_ref, o_ref):
    h = h_ref[...]
    u = _rms(h, g2_ref[...]).astype(BF16)
    acc = h
    for c in range(D_FF // _FF_CHUNK):
        sl = slice(c * _FF_CHUNK, (c + 1) * _FF_CHUNK)
        a = jax.nn.silu(_dot(u, wg_ref[:, sl])) * _dot(u, wu_ref[:, sl])
        acc = acc + _dot(a.astype(BF16), wd_ref[sl, :])
    o_ref[...] = _rms(acc, gf_ref[...])


def _outffn(h, g2, wg, wu, wd, gf, tm=512):
    rows = h.shape[0]
    row = lambda i: (i, 0)
    const = lambda i: (0, 0)
    resident = lambda shape: pl.BlockSpec(shape, const, pipeline_mode=pl.Buffered(1))
    return pl.pallas_call(
        _outffn_kernel,
        grid=(rows // tm,),
        in_specs=[
            pl.BlockSpec((tm, D_MODEL), row),
            pl.BlockSpec((1, D_MODEL), const),
            resident((D_MODEL, D_FF)),
            resident((D_MODEL, D_FF)),
            resident((D_FF, D_MODEL)),
            pl.BlockSpec((1, D_MODEL), const),
        ],
        out_specs=pl.BlockSpec((tm, D_MODEL), row),
        out_shape=jax.ShapeDtypeStruct((rows, D_MODEL), F32),
        compiler_params=pltpu.CompilerParams(
            dimension_semantics=("arbitrary",), vmem_limit_bytes=VMEM_LIMIT),
        name="outffn",
    )(h, g2, wg, wu, wd, gf)


def _key_features(pos, n_sel):
    f = np.zeros((pos.shape[0], LANES + n_sel), np.float32)
    f[:, HEAD_DIM] = pos // SEL_BLOCK
    f[:, HEAD_DIM + 1] = pos % SEL_BLOCK
    if n_sel:
        f[np.arange(pos.shape[0]), LANES + pos // SEL_BLOCK] = 1.0
    return f


def _query_features():
    f = np.zeros((N_KV_HEADS, GQA * TQ, HEAD_DIM), np.float32)
    for k in range(N_KV_HEADS):
        for g in range(GQA):
            slope = 2.0 ** (-8.0 * (k * GQA + g + 1) / N_HEADS)
            f[k, g * TQ:(g + 1) * TQ, 0] = slope * SEL_BLOCK
            f[k, g * TQ:(g + 1) * TQ, 1] = slope
            f[k, g * TQ:(g + 1) * TQ, 2] = MASK_BIAS
    return f


def _causal_masks(ncmp):
    r = np.arange(GQA * TQ)[:, None] % TQ
    cmask = (np.arange(ncmp)[None, :] - ((r + 1) // CMP_STRIDE)).astype(np.int32)
    c = np.arange(TK)[None, :]
    tile_masks = [np.where(c - r <= d * TQ, 0.0, NEG).astype(np.float32) for d in range(max(TK // TQ, 1))]
    return cmask, np.stack(tile_masks)


def _gate_expand():
    gw = GQA * HEAD_DIM
    e = np.zeros((2 * LANES, N_BRANCH * gw), np.float32)
    for g in range(GQA):
        for br in range(N_BRANCH):
            for half in range(2):
                e[half * LANES + g * N_BRANCH + br, br * gw + g * HEAD_DIM:br * gw + (g + 1) * HEAD_DIM] = 1.0
    return e


def _overlap(ncmp_pad, n_sel):
    nc = ncmp_pad - 1
    c_start = CMP_STRIDE * np.arange(nc)
    c_end = c_start + CMP_BLOCK - 1
    s_start = SEL_BLOCK * np.arange(n_sel)
    ov = np.minimum(c_end[:, None] + 1, s_start[None, :] + SEL_BLOCK) - np.maximum(c_start[:, None], s_start[None, :])
    out = np.zeros((ncmp_pad, n_sel), np.float32)
    out[:nc] = np.clip(ov, 0, None).astype(np.float32) / CMP_BLOCK
    return out


def _layer(x2, B, S, norm1_g, w_in, conv_w, k_cmp_pos, k_cmp_w1, k_cmp_w2, v_cmp_pos, v_cmp_w1, v_cmp_w2,
           gn_conv_g):
    n_sel = S // SEL_BLOCK
    nch = S // CMP_STRIDE
    hd = HEAD_DIM

    ng = GQA * N_BRANCH
    w_main, w_g = w_in[:, :_G_OFF], w_in[:, _G_OFF:]
    gpad = jnp.zeros((D_MODEL, LANES - ng), w_in.dtype)
    w_pad = jnp.concatenate([w_main, w_g[:, :ng], gpad, w_g[:, ng:], gpad], axis=1).astype(BF16)

    kfeat = jnp.asarray(_key_features(np.arange(S), n_sel), BF16)
    mixed_c, q, gates, kvc, ks_aug, vs, kw_aug, vw = _inproj(
        x2, norm1_g.reshape(1, D_MODEL), w_pad, kfeat, conv_w, gn_conv_g.reshape(1, CONV_CH))

    w1 = jnp.stack([k_cmp_w1, v_cmp_w1]).astype(BF16)
    wl = w1.reshape(2, 2, CMP_STRIDE, hd, CMP_HIDDEN).transpose(0, 2, 3, 1, 4)
    wl = wl.reshape(2, CMP_STRIDE, hd, 2 * CMP_HIDDEN)
    w2 = jnp.stack([k_cmp_w2, v_cmp_w2]).astype(BF16)
    pos = jnp.stack([k_cmp_pos, v_cmp_pos]).reshape(2, 1, CMP_BLOCK * hd)
    pos = jnp.broadcast_to(pos, (2, SUBLANES, CMP_BLOCK * hd)).astype(BF16)
    c_end = CMP_STRIDE * np.arange(nch) + CMP_BLOCK - 1
    cfeat = jnp.asarray(_key_features(c_end, 0), BF16)
    kc_aug, vc = _compress(kvc, pos, w1, wl, w2, cfeat, B, S)

    pad_rows = np.zeros((WINDOW, LANES), np.float32)
    pad_rows[:, hd + 2] = 1.0
    pad_k = jnp.broadcast_to(jnp.asarray(pad_rows, BF16), (N_KV_HEADS, B, WINDOW, LANES))
    kw_aug = jnp.concatenate([pad_k, kw_aug.reshape(N_KV_HEADS, B, S, LANES)], axis=2)
    vw = jnp.pad(vw.reshape(N_KV_HEADS, B, S, VW), ((0, 0), (0, 0), (WINDOW, 0), (0, 0)))
    cmask, tile_masks = _causal_masks(nch)

    nsa_out = _nsa(q, jnp.asarray(_query_features(), BF16), kc_aug, vc,
                   ks_aug.reshape(N_KV_HEADS, B, S, 2 * LANES), vs.reshape(N_KV_HEADS, B, S, LANES),
                   kw_aug, vw, jnp.asarray(_overlap(nch, n_sel), BF16), jnp.asarray(cmask),
                   jnp.asarray(tile_masks), jnp.asarray(_gate_expand(), BF16), gates, B, S)
    return mixed_c, nsa_out


def kernel(x, norm1_g, w_in, conv_w, k_cmp_pos, k_cmp_w1, k_cmp_w2, v_cmp_pos, v_cmp_w1, v_cmp_w2,
           gn_conv_g, gn_nsa_g, w_out, norm2_g, w_gate, w_up, w_down, norm_f_g):
    B, S, _ = x.shape
    depth = norm1_g.shape[0]
    assert depth == 1, "the final norm is fused into the (single) layer's FFN kernel"
    x2 = x.reshape(B * S, D_MODEL)
    mixed_c, nsa_out = _layer(x2, B, S, norm1_g[0], w_in[0], conv_w[0], k_cmp_pos[0], k_cmp_w1[0],
                              k_cmp_w2[0], v_cmp_pos[0], v_cmp_w1[0], v_cmp_w2[0], gn_conv_g[0])
    h = _outproj(mixed_c, nsa_out, x2, gn_nsa_g[0].reshape(1, N_HEADS * HEAD_DIM), w_out[0].astype(BF16))
    out = _outffn(h, norm2_g[0].reshape(1, D_MODEL), w_gate[0].astype(BF16), w_up[0].astype(BF16),
                  w_down[0].astype(BF16), norm_f_g.reshape(1, D_MODEL))
    return out.reshape(B, S, D_MODEL)
```

```python
import functools

import numpy as np
import jax
import jax.numpy as jnp
from jax import lax
from jax.experimental import pallas as pl
from jax.experimental.pallas import tpu as pltpu

D_MODEL = 1024
CONV_CH = 512
CONV_K = 3
N_HEADS = 8
HEAD_DIM = 64
N_KV_HEADS = 2
GQA = 4
N_BRANCH = 3
CMP_BLOCK = 32
CMP_STRIDE = 16
CMP_HIDDEN = 256
SEL_BLOCK = 64
SEL_TOP = 16
WINDOW = 512
D_FF = 2816
NORM_EPS = 1e-6

F32 = jnp.float32
BF16 = jnp.bfloat16

LANES = 128
VW = 2 * LANES
SUBLANES = 8
VMEM_LIMIT = 56 * 1024 * 1024

TQ = 256
TK = 256
NEG = -1e30
MASK_BIAS = -float(2 ** 30)

_CONV_W = 3 * CONV_CH
_Q_OFF = _CONV_W
_KVC_OFF = _Q_OFF + N_HEADS * HEAD_DIM
_KV_OFF = _KVC_OFF + 2 * N_KV_HEADS * HEAD_DIM
_G_OFF = _KV_OFF + 4 * N_KV_HEADS * HEAD_DIM
_G_W = N_KV_HEADS * LANES
_PROJ_W = _G_OFF + _G_W


def _rms(x, g):
    return x * lax.rsqrt(jnp.mean(x * x, axis=-1, keepdims=True) + NORM_EPS) * g


def _dot(a, b):
    return jnp.dot(a, b, preferred_element_type=F32)


def _value_rows(v2, h, wide):
    lane = lax.broadcasted_iota(jnp.int32, v2.shape, 1)
    swapped = pltpu.roll(v2, HEAD_DIM, axis=1)
    if not wide:
        return jnp.where(lane < HEAD_DIM, v2 if h == 0 else swapped, 1.0).astype(BF16)
    dup = jnp.where((lane < HEAD_DIM) == (h == 0), v2, swapped)
    return jnp.concatenate([dup.astype(BF16), jnp.ones(v2.shape, BF16)], axis=1)


def _dot_nt(a, b):
    return lax.dot_general(a, b, (((1,), (1,)), ((), ())), preferred_element_type=F32)


def _inproj_kernel(seq_tiles, x_ref, xp_ref, g_ref, w_ref, wgt_ref, kf_ref, cw_ref, gc_ref, mixc_ref, q_ref, gate_ref,
                   kvc_ref, ks_ref, vs_ref, kw_ref, vw_ref):
    u = _rms(x_ref[...], g_ref[...]).astype(BF16)

    cbch = _dot(u, w_ref[:, 0:_CONV_W])
    cu = cbch[:, CONV_CH:2 * CONV_CH] * cbch[:, 2 * CONV_CH:3 * CONV_CH]
    up = _rms(xp_ref[...], g_ref[...]).astype(BF16)
    chp = _dot(up, w_ref[:, CONV_CH:_CONV_W])
    starts_sequence = pl.program_id(0) % seq_tiles == 0
    prev = jnp.where(starts_sequence, 0.0, chp[:, 0:CONV_CH] * chp[:, CONV_CH:2 * CONV_CH])
    r = lax.broadcasted_iota(jnp.int32, cu.shape, 0)
    cu1 = jnp.where(r == 0, prev[7:8, :], pltpu.roll(cu, 1, axis=0))
    cu2 = pltpu.roll(cu, 2, axis=0)
    cu2 = jnp.where(r == 0, prev[6:7, :], jnp.where(r == 1, prev[7:8, :], cu2))
    cw = cw_ref[...]
    y = cbch[:, 0:CONV_CH] * (cw[0:1, :] * cu2 + cw[1:2, :] * cu1 + cw[2:3, :] * cu)
    mixc_ref[...] = _rms(y, gc_ref[...]).astype(BF16)

    q_ref[...] = _dot(u, w_ref[:, _Q_OFF:_KVC_OFF]).astype(BF16)
    kvc_ref[...] = _dot(u, w_ref[:, _KVC_OFF:_KV_OFF])
    gate_ref[...] = jax.nn.sigmoid(_dot(u, wgt_ref[...]))
    kv = _dot(u, w_ref[:, _KV_OFF:_G_OFF])
    kf = kf_ref[...]
    hd, kvw = HEAD_DIM, N_KV_HEADS * HEAD_DIM
    for h in range(N_KV_HEADS):
        ks_ref[h] = kf
        ks_ref[h, :, 0:hd] = kv[:, h * hd:(h + 1) * hd].astype(BF16)
        vs_ref[h] = _value_rows(kv[:, kvw:2 * kvw], h, False)
        kw_ref[h] = kf[:, 0:LANES]
        kw_ref[h, :, 0:hd] = kv[:, 2 * kvw + h * hd:2 * kvw + (h + 1) * hd].astype(BF16)
        vw_ref[h] = _value_rows(kv[:, 3 * kvw:4 * kvw], h, True)


def _inproj(x2, g, w, w_gates, kfeat, conv_w, gc, tm=512):
    rows = x2.shape[0]
    seq_tiles = kfeat.shape[0] // tm
    row = lambda i: (i, 0)
    head_row = lambda i: (0, i, 0)
    const = lambda i: (0, 0)
    return pl.pallas_call(
        functools.partial(_inproj_kernel, seq_tiles),
        grid=(rows // tm,),
        in_specs=[
            pl.BlockSpec((tm, D_MODEL), row),
            pl.BlockSpec((SUBLANES, D_MODEL), lambda i: (jnp.maximum(i * (tm // SUBLANES) - 1, 0), 0)),
            pl.BlockSpec((1, D_MODEL), const),
            pl.BlockSpec((D_MODEL, _G_OFF), const, pipeline_mode=pl.Buffered(1)),
            pl.BlockSpec((D_MODEL, _G_W), const, pipeline_mode=pl.Buffered(1)),
            pl.BlockSpec((tm, 2 * LANES), lambda i: (i % seq_tiles, 0)),
            pl.BlockSpec((CONV_K, CONV_CH), const),
            pl.BlockSpec((1, CONV_CH), const),
        ],
        out_specs=[
            pl.BlockSpec((tm, CONV_CH), row),
            pl.BlockSpec((tm, N_HEADS * HEAD_DIM), row),
            pl.BlockSpec((tm, _G_W), row),
            pl.BlockSpec((tm, _KV_OFF - _KVC_OFF), row),
            pl.BlockSpec((N_KV_HEADS, tm, 2 * LANES), head_row),
            pl.BlockSpec((N_KV_HEADS, tm, LANES), head_row),
            pl.BlockSpec((N_KV_HEADS, tm, LANES), head_row),
            pl.BlockSpec((N_KV_HEADS, tm, VW), head_row),
        ],
        out_shape=[
            jax.ShapeDtypeStruct((rows, CONV_CH), BF16),
            jax.ShapeDtypeStruct((rows, N_HEADS * HEAD_DIM), BF16),
            jax.ShapeDtypeStruct((rows, _G_W), F32),
            jax.ShapeDtypeStruct((rows, _KV_OFF - _KVC_OFF), F32),
            jax.ShapeDtypeStruct((N_KV_HEADS, rows, 2 * LANES), BF16),
            jax.ShapeDtypeStruct((N_KV_HEADS, rows, LANES), BF16),
            jax.ShapeDtypeStruct((N_KV_HEADS, rows, LANES), BF16),
            jax.ShapeDtypeStruct((N_KV_HEADS, rows, VW), BF16),
        ],
        compiler_params=pltpu.CompilerParams(
            dimension_semantics=("arbitrary",), vmem_limit_bytes=VMEM_LIMIT),
        name="inproj",
    )(x2, x2, g, w, w_gates, kfeat, conv_w, gc)


def _compress_kernel(xk_ref, xv_ref, pos_ref, w1_ref, wl_ref, w2_ref, cf_ref, kc_ref, vc_ref):
    nch = xk_ref.shape[0] // CMP_STRIDE
    hd = HEAD_DIM
    acc = [jnp.zeros((nch, 2 * CMP_HIDDEN), F32) for _ in range(2 * N_KV_HEADS)]
    for l in range(CMP_STRIDE):
        for s, x_ref in enumerate((xk_ref, xv_ref)):
            xl = x_ref[pl.ds(l, nch, stride=CMP_STRIDE), :].astype(BF16)
            for h in range(N_KV_HEADS):
                c = s * N_KV_HEADS + h
                acc[c] = acc[c] + _dot(xl[:, h * hd:(h + 1) * hd], wl_ref[s, l])
    for s in range(2):
        pb = _dot(pos_ref[s], w1_ref[s])[0:1, :]
        outs = []
        for h in range(N_KV_HEADS):
            a = acc[s * N_KV_HEADS + h]
            hid = a[:, 0:CMP_HIDDEN] + pltpu.roll(a[:, CMP_HIDDEN:], nch - 1, axis=0) + pb
            outs.append(_dot(jax.nn.gelu(hid, approximate=True).astype(BF16), w2_ref[s]))
        for h in range(N_KV_HEADS):
            if s == 0:
                kc_ref[h] = cf_ref[...]
                kc_ref[h, :, 0:hd] = outs[h].astype(BF16)
            else:
                vc_ref[h] = _value_rows(jnp.concatenate(outs, axis=1), h, True)


def _compress(kvc, pos, w1, wl, w2, cfeat, B, S):
    nch = S // CMP_STRIDE
    full = lambda *shape: pl.BlockSpec(shape, lambda b: (0,) * len(shape))
    return pl.pallas_call(
        _compress_kernel,
        grid=(B,),
        in_specs=[
            pl.BlockSpec((S, LANES), lambda b: (b, 0)),
            pl.BlockSpec((S, LANES), lambda b: (b, 1)),
            full(*pos.shape), full(*w1.shape), full(*wl.shape), full(*w2.shape), full(*cfeat.shape),
        ],
        out_specs=[
            pl.BlockSpec((None, N_KV_HEADS, nch, LANES), lambda b: (b, 0, 0, 0)),
            pl.BlockSpec((None, N_KV_HEADS, nch, VW), lambda b: (b, 0, 0, 0)),
        ],
        out_shape=[
            jax.ShapeDtypeStruct((B, N_KV_HEADS, nch, LANES), BF16),
            jax.ShapeDtypeStruct((B, N_KV_HEADS, nch, VW), BF16),
        ],
        compiler_params=pltpu.CompilerParams(
            dimension_semantics=("arbitrary",), vmem_limit_bytes=VMEM_LIMIT),
        name="compress",
    )(kvc, kvc, pos, w1, wl, w2, cfeat)


def _hi_lo(x):
    hi = x.astype(BF16)
    return jnp.concatenate([hi, (x - hi.astype(F32)).astype(BF16)], axis=1)


def _head_layout(acc):
    low_half = lax.broadcasted_iota(jnp.int32, (TQ, LANES), 1) < HEAD_DIM
    head = lambda x, g: x[g * TQ:(g + 1) * TQ]
    if acc.shape[1] == VW:
        x = acc[:, 0:LANES] * (1.0 / jnp.maximum(acc[:, LANES:VW], 1e-30))
        pairs = [jnp.where(low_half, head(x, 2 * j), head(x, 2 * j + 1)) for j in range(GQA // 2)]
    else:
        rot = pltpu.roll(acc, HEAD_DIM, axis=1)
        pairs = [jnp.where(low_half, head(acc, 2 * j) * (1.0 / head(rot, 2 * j)),
                           head(rot, 2 * j + 1) * (1.0 / head(acc, 2 * j + 1))) for j in range(GQA // 2)]
    return jnp.concatenate(pairs, axis=1)


def _exchange(v, i, j, descending):
    hi, lo = jnp.maximum(v[i], v[j]), jnp.minimum(v[i], v[j])
    v[i], v[j] = (hi, lo) if descending else (lo, hi)


def _bitonic_merge_desc(v):
    n = len(v)
    j = n // 2
    while j >= 1:
        for i in range(n):
            if i ^ j > i:
                _exchange(v, i, i ^ j, True)
        j //= 2


def _kth_largest(x, kth, fillers=()):
    n = x.shape[0]
    assert n == kth * SUBLANES and kth & (kth - 1) == 0
    v = [x[r * SUBLANES:(r + 1) * SUBLANES, :] for r in range(kth)]
    fillers = list(fillers)
    layers = 0
    k = 2
    while k <= kth:
        j = k // 2
        while j >= 1:
            for i in range(kth):
                if i ^ j > i:
                    _exchange(v, i, i ^ j, (i & k) == 0)
            j //= 2
            layers += 1
            if layers % 3 == 0 and fillers:
                fillers.pop(0)()
        k *= 2
    shift = SUBLANES // 2
    while shift >= 1:
        v = [jnp.maximum(v[i], pltpu.roll(v[kth - 1 - i], shift, axis=0)) for i in range(kth)]
        if shift > 1:
            _bitonic_merge_desc(v)
        if fillers:
            fillers.pop(0)()
        shift //= 2
    for filler in fillers:
        filler()
    out = v[0]
    for i in range(1, kth):
        out = jnp.minimum(out, v[i])
    return out[0:1, :]


def _nsa_kernel(q_ref, qf_ref, kc_ref, vc_ref, ks_ref, vs_ref, kw_ref, vw_ref, ov_ref, cm_ref, mb_ref,
                ge_ref, g_ref, o_ref, qa_ref, tiles_ref, m_ref, acc_ref, sa_ref, sb_ref):
    i = pl.program_id(2)
    t0 = i * TQ
    rows = GQA * TQ
    n_sel = ov_ref.shape[1]
    ncmp = kc_ref.shape[0]

    q = q_ref[...]
    for g in range(GQA):
        qg = q[:, g * HEAD_DIM:(g + 1) * HEAD_DIM].astype(F32) * (HEAD_DIM ** -0.5)
        qa_ref[g * TQ:(g + 1) * TQ, 0:HEAD_DIM] = qg.astype(BF16)
    qa_ref[:, HEAD_DIM:LANES] = qf_ref[...]
    qa1 = qa_ref[:, 0:LANES]

    diag_ok = (lax.broadcasted_iota(jnp.int32, (rows, TQ), 1)
               <= (lax.broadcasted_iota(jnp.int32, (rows, TQ), 0) & (TQ - 1)))

    wk = TQ + WINDOW
    w0 = pl.multiple_of(t0, TQ)
    win = {}

    def win_scores():
        s_w = _dot_nt(qa1, kw_ref[pl.ds(w0, wk), :])
        win["s"] = (jnp.where(diag_ok, NEG, s_w[:, 0:TQ]),
                    s_w[:, TQ:WINDOW], jnp.where(diag_ok, s_w[:, WINDOW:wk], NEG))

    def win_max():
        s_old, s_mid, s_new = win["s"]
        win["m"] = jnp.maximum(jnp.maximum(jnp.max(s_old, axis=1, keepdims=True),
                                           jnp.max(s_mid, axis=1, keepdims=True)),
                               jnp.max(s_new, axis=1, keepdims=True))

    def win_exp():
        win["e"] = jnp.concatenate([jnp.exp(s - win["m"]) for s in win["s"]], axis=1).astype(BF16)

    def win_pv():
        win["acc"] = _dot(win["e"], vw_ref[pl.ds(w0, wk), :])

    s_c = _dot_nt(qa1, kc_ref[...])
    s_c = jnp.where(cm_ref[...] <= (TQ // CMP_STRIDE) * i - 2, s_c, NEG)
    m_c = jnp.maximum(jnp.max(s_c, axis=1, keepdims=True), 0.5 * NEG)
    e_c = jnp.exp(s_c - m_c)
    acc_c = _dot(e_c.astype(BF16), vc_ref[...])
    inv_c = 1.0 / jnp.maximum(acc_c[:, LANES:VW], 1e-30)
    p_c = e_c * jnp.concatenate([inv_c] * (ncmp // LANES), axis=1)

    p_sum = p_c[0:TQ] + p_c[TQ:2 * TQ] + p_c[2 * TQ:3 * TQ] + p_c[3 * TQ:4 * TQ]
    p_hi = p_sum.astype(BF16)
    p_lo = (p_sum - p_hi.astype(F32)).astype(BF16)
    imp = _dot(p_hi, ov_ref[...]) + _dot(p_lo, ov_ref[...])
    imp_t = imp.T
    jb = lax.broadcasted_iota(jnp.int32, (n_sel, TQ), 0)
    tq = t0 + lax.broadcasted_iota(jnp.int32, (n_sel, TQ), 1)
    imp_t = jnp.where(SEL_BLOCK * jb > tq, -jnp.inf, imp_t)
    forced = (jb == 0) | (jb == (tq >> 6))
    imp_t = jnp.where(forced, jnp.inf, imp_t)

    thr = _kth_largest(imp_t, SEL_TOP, (win_scores, win_max, win_exp, win_pv))
    acc_w = win["acc"]
    above = imp_t > thr
    tied = imp_t == thr
    n_above = jnp.sum(jnp.where(above, 1.0, 0.0), axis=0, keepdims=True)
    lower = (lax.broadcasted_iota(jnp.int32, (n_sel, n_sel), 1)
             < lax.broadcasted_iota(jnp.int32, (n_sel, n_sel), 0))
    tied_before = _dot(jnp.where(lower, 1.0, 0.0).astype(BF16), jnp.where(tied, 1.0, 0.0).astype(BF16))
    take_tie = jnp.where(tied_before + n_above < float(SEL_TOP), 1.0, 0.0)
    sel_t = jnp.where(above, 1.0, jnp.where(tied, take_tie, 0.0))
    sel_q = sel_t.T
    bias = ((1.0 - sel_q) * MASK_BIAS).astype(BF16)
    for g in range(GQA):
        qa_ref[g * TQ:(g + 1) * TQ, LANES:2 * LANES] = bias

    n_past = t0 // TK
    first_mask = (t0 - n_past * TK) // TQ
    tiles_ref[0] = n_past

    def scores(j, s_ref, mask=None):
        k0 = pl.multiple_of(tiles_ref[j] * TK, TK)
        s = _dot_nt(qa_ref[...], ks_ref[pl.ds(k0, TK), :])
        s_ref[...] = s if mask is None else s + mask

    scores(0, sa_ref, mb_ref[first_mask])

    bpt = TK // SEL_BLOCK
    n_tiles = n_sel // bpt
    blk_cnt = _dot(jnp.ones((SUBLANES, TQ), BF16), sel_q.astype(BF16))
    in_tile = (lax.broadcasted_iota(jnp.int32, (n_sel, LANES), 0) // bpt
               == lax.broadcasted_iota(jnp.int32, (n_sel, LANES), 1))
    tile_cnt = _dot(jnp.where(blk_cnt > 0.5, 1.0, 0.0).astype(BF16),
                    jnp.where(in_tile, 1.0, 0.0).astype(BF16))
    lane = lax.broadcasted_iota(jnp.int32, (1, LANES), 1)
    bit = jnp.where(tile_cnt[0:1, :] > 0.5, jnp.left_shift(1, lane & 15).astype(F32), 0.0)
    words = [jnp.sum(jnp.where((lane >= 16 * w) & (lane < 16 * (w + 1)), bit, 0.0)).astype(jnp.int32)
             for w in range((n_tiles + 15) // 16)]
    n_tasks = jnp.int32(1)
    for kt in range(n_tiles - 1):
        used = (words[kt // 16] >> (kt % 16)) & 1
        tiles_ref[n_tasks] = kt
        n_tasks = n_tasks + jnp.where(kt < n_past, used, 0)
    tiles_ref[n_tasks] = 0
    tiles_ref[n_tasks + 1] = 0

    m_ref[...] = jnp.full(m_ref.shape, NEG, F32)
    acc_ref[...] = jnp.zeros(acc_ref.shape, F32)

    def accumulate(j, s_ref):
        k0 = pl.multiple_of(tiles_ref[j] * TK, TK)
        slot = jnp.where(j < n_tasks, 0, 1)
        s = s_ref[...]
        m = m_ref[slot]
        m_new = jnp.maximum(m, jnp.max(s, axis=1, keepdims=True))
        p = jnp.concatenate([jnp.exp(s[:, c:c + LANES] - m_new) for c in range(0, TK, LANES)], axis=1)
        acc_ref[slot] = (jnp.exp(m - m_new) * acc_ref[slot]
                         + _dot(p.astype(BF16), vs_ref[pl.ds(k0, TK), :]))
        m_ref[slot] = m_new

    def pair(jj, c):
        a = 2 * jj
        scores(a + 1, sb_ref)
        accumulate(a, sa_ref)
        scores(a + 2, sa_ref)
        accumulate(a + 1, sb_ref)
        return c

    lax.fori_loop(0, (n_tasks + 1) // 2, pair, 0)

    gates = _dot(_hi_lo(g_ref[...]), ge_ref[...])
    gw = GQA * HEAD_DIM
    o_ref[...] = (gates[:, 0:gw] * _head_layout(acc_c) + gates[:, gw:2 * gw] * _head_layout(acc_ref[0])
                  + gates[:, 2 * gw:3 * gw] * _head_layout(acc_w)).astype(o_ref.dtype)


def _nsa(q, qfeat, kc_aug, vc, ks_aug, vs, kw_aug, vw, ov, cmask, tile_masks, gate_expand, gates, B, S):
    nq = S // TQ
    gw = GQA * HEAD_DIM
    ncmp = kc_aug.shape[2]
    n_sel = ov.shape[1]
    sp = kw_aug.shape[2]
    bk = lambda b, k, i: (b, k, 0, 0)
    kb = lambda b, k, i: (k, b, 0, 0)
    return pl.pallas_call(
        _nsa_kernel,
        grid=(B, N_KV_HEADS, nq),
        in_specs=[
            pl.BlockSpec((TQ, gw), lambda b, k, i: (b * nq + i, k)),
            pl.BlockSpec((None, GQA * TQ, HEAD_DIM), lambda b, k, i: (k, 0, 0)),
            pl.BlockSpec((None, None, ncmp, LANES), bk),
            pl.BlockSpec((None, None, ncmp, VW), bk),
            pl.BlockSpec((None, None, S, 2 * LANES), kb),
            pl.BlockSpec((None, None, S, LANES), kb),
            pl.BlockSpec((None, None, sp, LANES), kb),
            pl.BlockSpec((None, None, sp, VW), kb),
            pl.BlockSpec((ncmp, n_sel), lambda b, k, i: (0, 0)),
            pl.BlockSpec((GQA * TQ, ncmp), lambda b, k, i: (0, 0)),
            pl.BlockSpec(tile_masks.shape, lambda b, k, i: (0, 0, 0)),
            pl.BlockSpec(gate_expand.shape, lambda b, k, i: (0, 0)),
            pl.BlockSpec((TQ, LANES), lambda b, k, i: (b * nq + i, k)),
        ],
        out_specs=pl.BlockSpec((TQ, gw), lambda b, k, i: (b * nq + i, k)),
        out_shape=jax.ShapeDtypeStruct((B * S, N_KV_HEADS * gw), BF16),
        scratch_shapes=[
            pltpu.VMEM((GQA * TQ, 2 * LANES), BF16),
            pltpu.SMEM((S // TK + 2,), jnp.int32),
            pltpu.VMEM((2, GQA * TQ, LANES), F32),
            pltpu.VMEM((2, GQA * TQ, LANES), F32),
            pltpu.VMEM((GQA * TQ, TK), F32),
            pltpu.VMEM((GQA * TQ, TK), F32),
        ],
        compiler_params=pltpu.CompilerParams(
            dimension_semantics=("arbitrary",) * 3, vmem_limit_bytes=VMEM_LIMIT),
        name="nsa",
    )(q, qfeat, kc_aug, vc, ks_aug, vs, kw_aug, vw, ov, cmask, tile_masks, gate_expand, gates)


_FF_CHUNK = 1408


def _outproj_kernel(mc_ref, nsa_ref, x_ref, gn_ref, wo_ref, h_ref):
    mixed_n = _rms(nsa_ref[...].astype(F32), gn_ref[...]).astype(BF16)
    h_ref[...] = (x_ref[...] + _dot(mc_ref[...], wo_ref[0:CONV_CH, :])
                  + _dot(mixed_n, wo_ref[CONV_CH:2 * CONV_CH, :]))


def _outproj(mixed_c, nsa_out, x2, gn, wo, tm=512):
    rows = x2.shape[0]
    row = lambda i: (i, 0)
    const = lambda i: (0, 0)
    return pl.pallas_call(
        _outproj_kernel,
        grid=(rows // tm,),
        in_specs=[
            pl.BlockSpec((tm, CONV_CH), row),
            pl.BlockSpec((tm, N_HEADS * HEAD_DIM), row),
            pl.BlockSpec((tm, D_MODEL), row),
            pl.BlockSpec((1, N_HEADS * HEAD_DIM), const),
            pl.BlockSpec((2 * CONV_CH, D_MODEL), const, pipeline_mode=pl.Buffered(1)),
        ],
        out_specs=pl.BlockSpec((tm, D_MODEL), row),
        out_shape=jax.ShapeDtypeStruct((rows, D_MODEL), F32),
        compiler_params=pltpu.CompilerParams(
            dimension_semantics=("arbitrary",), vmem_limit_bytes=VMEM_LIMIT),
        name="outproj",
    )(mixed_c, nsa_out, x2, gn, wo)


def _outffn_kernel(h_ref, g2_ref, wg_ref, wu_ref, wd_ref, gf_ref, o_ref):
    h = h_ref[...]
    u = _rms(h, g2_ref[...]).astype(BF16)
    acc = h
    for c in range(D_FF // _FF_CHUNK):
        sl = slice(c * _FF_CHUNK, (c + 1) * _FF_CHUNK)
        a = jax.nn.silu(_dot(u, wg_ref[:, sl])) * _dot(u, wu_ref[:, sl])
        acc = acc + _dot(a.astype(BF16), wd_ref[sl, :])
    o_ref[...] = _rms(acc, gf_ref[...])


def _outffn(h, g2, wg, wu, wd, gf, tm=512):
    rows = h.shape[0]
    row = lambda i: (i, 0)
    const = lambda i: (0, 0)
    resident = lambda shape: pl.BlockSpec(shape, const, pipeline_mode=pl.Buffered(1))
    return pl.pallas_call(
        _outffn_kernel,
        grid=(rows // tm,),
        in_specs=[
            pl.BlockSpec((tm, D_MODEL), row),
            pl.BlockSpec((1, D_MODEL), const),
            resident((D_MODEL, D_FF)),
            resident((D_MODEL, D_FF)),
            resident((D_FF, D_MODEL)),
            pl.BlockSpec((1, D_MODEL), const),
        ],
        out_specs=pl.BlockSpec((tm, D_MODEL), row),
        out_shape=jax.ShapeDtypeStruct((rows, D_MODEL), F32),
        compiler_params=pltpu.CompilerParams(
            dimension_semantics=("arbitrary",), vmem_limit_bytes=VMEM_LIMIT),
        name="outffn",
    )(h, g2, wg, wu, wd, gf)


def _key_features(pos, n_sel):
    f = np.zeros((pos.shape[0], LANES + n_sel), np.float32)
    f[:, HEAD_DIM] = pos // SEL_BLOCK
    f[:, HEAD_DIM + 1] = pos % SEL_BLOCK
    if n_sel:
        f[np.arange(pos.shape[0]), LANES + pos // SEL_BLOCK] = 1.0
    return f


def _query_features():
    f = np.zeros((N_KV_HEADS, GQA * TQ, HEAD_DIM), np.float32)
    for k in range(N_KV_HEADS):
        for g in range(GQA):
            slope = 2.0 ** (-8.0 * (k * GQA + g + 1) / N_HEADS)
            f[k, g * TQ:(g + 1) * TQ, 0] = slope * SEL_BLOCK
            f[k, g * TQ:(g + 1) * TQ, 1] = slope
            f[k, g * TQ:(g + 1) * TQ, 2] = MASK_BIAS
    return f


def _causal_masks(ncmp):
    r = np.arange(GQA * TQ)[:, None] % TQ
    cmask = (np.arange(ncmp)[None, :] - ((r + 1) // CMP_STRIDE)).astype(np.int32)
    c = np.arange(TK)[None, :]
    tile_masks = [np.where(c - r <= d * TQ, 0.0, NEG).astype(np.float32) for d in range(max(TK // TQ, 1))]
    return cmask, np.stack(tile_masks)


def _gate_expand():
    gw = GQA * HEAD_DIM
    e = np.zeros((2 * LANES, N_BRANCH * gw), np.float32)
    for g in range(GQA):
        for br in range(N_BRANCH):
            for half in range(2):
                e[half * LANES + g * N_BRANCH + br, br * gw + g * HEAD_DIM:br * gw + (g + 1) * HEAD_DIM] = 1.0
    return e


def _overlap(ncmp_pad, n_sel):
    nc = ncmp_pad - 1
    c_start = CMP_STRIDE * np.arange(nc)
    c_end = c_start + CMP_BLOCK - 1
    s_start = SEL_BLOCK * np.arange(n_sel)
    ov = np.minimum(c_end[:, None] + 1, s_start[None, :] + SEL_BLOCK) - np.maximum(c_start[:, None], s_start[None, :])
    out = np.zeros((ncmp_pad, n_sel), np.float32)
    out[:nc] = np.clip(ov, 0, None).astype(np.float32) / CMP_BLOCK
    return out


def _layer(x2, B, S, norm1_g, w_in, conv_w, k_cmp_pos, k_cmp_w1, k_cmp_w2, v_cmp_pos, v_cmp_w1, v_cmp_w2,
           gn_conv_g):
    n_sel = S // SEL_BLOCK
    nch = S // CMP_STRIDE
    hd = HEAD_DIM

    ng = GQA * N_BRANCH
    w_g = w_in[:, _G_OFF:].astype(BF16)
    gpad = jnp.zeros((D_MODEL, LANES - ng), BF16)
    w_gates = jnp.concatenate([w_g[:, :ng], gpad, w_g[:, ng:], gpad], axis=1)

    kfeat = jnp.asarray(_key_features(np.arange(S), n_sel), BF16)
    mixed_c, q, gates, kvc, ks_aug, vs, kw_aug, vw = _inproj(
        x2, norm1_g.reshape(1, D_MODEL), w_in[:, :_G_OFF].astype(BF16), w_gates, kfeat, conv_w,
        gn_conv_g.reshape(1, CONV_CH))

    w1 = jnp.stack([k_cmp_w1, v_cmp_w1]).astype(BF16)
    wl = w1.reshape(2, 2, CMP_STRIDE, hd, CMP_HIDDEN).transpose(0, 2, 3, 1, 4)
    wl = wl.reshape(2, CMP_STRIDE, hd, 2 * CMP_HIDDEN)
    w2 = jnp.stack([k_cmp_w2, v_cmp_w2]).astype(BF16)
    pos = jnp.stack([k_cmp_pos, v_cmp_pos]).reshape(2, 1, CMP_BLOCK * hd)
    pos = jnp.broadcast_to(pos, (2, SUBLANES, CMP_BLOCK * hd)).astype(BF16)
    c_end = CMP_STRIDE * np.arange(nch) + CMP_BLOCK - 1
    cfeat = jnp.asarray(_key_features(c_end, 0), BF16)
    kc_aug, vc = _compress(kvc, pos, w1, wl, w2, cfeat, B, S)

    pad_rows = np.zeros((WINDOW, LANES), np.float32)
    pad_rows[:, hd + 2] = 1.0
    pad_k = jnp.broadcast_to(jnp.asarray(pad_rows, BF16), (N_KV_HEADS, B, WINDOW, LANES))
    kw_aug = jnp.concatenate([pad_k, kw_aug.reshape(N_KV_HEADS, B, S, LANES)], axis=2)
    vw = jnp.pad(vw.reshape(N_KV_HEADS, B, S, VW), ((0, 0), (0, 0), (WINDOW, 0), (0, 0)))
    cmask, tile_masks = _causal_masks(nch)

    nsa_out = _nsa(q, jnp.asarray(_query_features(), BF16), kc_aug, vc,
                   ks_aug.reshape(N_KV_HEADS, B, S, 2 * LANES), vs.reshape(N_KV_HEADS, B, S, LANES),
                   kw_aug, vw, jnp.asarray(_overlap(nch, n_sel), BF16), jnp.asarray(cmask),
                   jnp.asarray(tile_masks), jnp.asarray(_gate_expand(), BF16), gates, B, S)
    return mixed_c, nsa_out


def kernel(x, norm1_g, w_in, conv_w, k_cmp_pos, k_cmp_w1, k_cmp_w2, v_cmp_pos, v_cmp_w1, v_cmp_w2,
           gn_conv_g, gn_nsa_g, w_out, norm2_g, w_gate, w_up, w_down, norm_f_g):
    B, S, _ = x.shape
    depth = norm1_g.shape[0]
    assert depth == 1, "the final norm is fused into the (single) layer's FFN kernel"
    x2 = x.reshape(B * S, D_MODEL)
    mixed_c, nsa_out = _layer(x2, B, S, norm1_g[0], w_in[0], conv_w[0], k_cmp_pos[0], k_cmp_w1[0],
                              k_cmp_w2[0], v_cmp_pos[0], v_cmp_w1[0], v_cmp_w2[0], gn_conv_g[0])
    h = _outproj(mixed_c, nsa_out, x2, gn_nsa_g[0].reshape(1, N_HEADS * HEAD_DIM), w_out[0].astype(BF16))
    out = _outffn(h, norm2_g[0].reshape(1, D_MODEL), w_gate[0].astype(BF16), w_up[0].astype(BF16),
                  w_down[0].astype(BF16), norm_f_g.reshape(1, D_MODEL))
    return out.reshape(B, S, D_MODEL)
```

```python
import functools

import numpy as np
import jax
import jax.numpy as jnp
from jax import lax
from jax.experimental import pallas as pl
from jax.experimental.pallas import tpu as pltpu

D_MODEL = 1024
CONV_CH = 512
CONV_K = 3
N_HEADS = 8
HEAD_DIM = 64
N_KV_HEADS = 2
GQA = 4
N_BRANCH = 3
CMP_BLOCK = 32
CMP_STRIDE = 16
CMP_HIDDEN = 256
SEL_BLOCK = 64
SEL_TOP = 16
WINDOW = 512
D_FF = 2816
NORM_EPS = 1e-6

F32 = jnp.float32
BF16 = jnp.bfloat16

LANES = 128
VW = 2 * LANES
SUBLANES = 8
VMEM_LIMIT = 56 * 1024 * 1024

TQ = 256
TK = 256
NEG = -1e30
MASK_BIAS = -float(2 ** 30)

_CONV_W = 3 * CONV_CH
_Q_OFF = _CONV_W
_KVC_OFF = _Q_OFF + N_HEADS * HEAD_DIM
_KV_OFF = _KVC_OFF + 2 * N_KV_HEADS * HEAD_DIM
_G_OFF = _KV_OFF + 4 * N_KV_HEADS * HEAD_DIM
_G_W = N_KV_HEADS * LANES
_PROJ_W = _G_OFF + _G_W


def _rms(x, g):
    return x * lax.rsqrt(jnp.mean(x * x, axis=-1, keepdims=True) + NORM_EPS) * g


def _dot(a, b):
    return jnp.dot(a, b, preferred_element_type=F32)


def _value_rows(v2, h, wide):
    lane = lax.broadcasted_iota(jnp.int32, v2.shape, 1)
    swapped = pltpu.roll(v2, HEAD_DIM, axis=1)
    if not wide:
        return jnp.where(lane < HEAD_DIM, v2 if h == 0 else swapped, 1.0).astype(BF16)
    dup = jnp.where((lane < HEAD_DIM) == (h == 0), v2, swapped)
    return jnp.concatenate([dup.astype(BF16), jnp.ones(v2.shape, BF16)], axis=1)


def _dot_nt(a, b):
    return lax.dot_general(a, b, (((1,), (1,)), ((), ())), preferred_element_type=F32)


def _inproj_kernel(seq_tiles, x_ref, xp_ref, g_ref, w_ref, wgt_ref, kf_ref, cw_ref, gc_ref, mixc_ref, q_ref, gate_ref,
                   kvc_ref, ks_ref, vs_ref, kw_ref, vw_ref):
    u = _rms(x_ref[...], g_ref[...]).astype(BF16)

    cbch = _dot(u, w_ref[:, 0:_CONV_W])
    cu = cbch[:, CONV_CH:2 * CONV_CH] * cbch[:, 2 * CONV_CH:3 * CONV_CH]
    up = _rms(xp_ref[...], g_ref[...]).astype(BF16)
    chp = _dot(up, w_ref[:, CONV_CH:_CONV_W])
    starts_sequence = pl.program_id(0) % seq_tiles == 0
    prev = jnp.where(starts_sequence, 0.0, chp[:, 0:CONV_CH] * chp[:, CONV_CH:2 * CONV_CH])
    r = lax.broadcasted_iota(jnp.int32, cu.shape, 0)
    cu1 = jnp.where(r == 0, prev[7:8, :], pltpu.roll(cu, 1, axis=0))
    cu2 = pltpu.roll(cu, 2, axis=0)
    cu2 = jnp.where(r == 0, prev[6:7, :], jnp.where(r == 1, prev[7:8, :], cu2))
    cw = cw_ref[...]
    y = cbch[:, 0:CONV_CH] * (cw[0:1, :] * cu2 + cw[1:2, :] * cu1 + cw[2:3, :] * cu)
    mixc_ref[...] = _rms(y, gc_ref[...]).astype(BF16)

    q_ref[...] = _dot(u, w_ref[:, _Q_OFF:_KVC_OFF]).astype(BF16)
    kvc_ref[...] = _dot(u, w_ref[:, _KVC_OFF:_KV_OFF])
    gate_ref[...] = jax.nn.sigmoid(_dot(u, wgt_ref[...]))
    kv = _dot(u, w_ref[:, _KV_OFF:_G_OFF])
    kf = kf_ref[...]
    hd, kvw = HEAD_DIM, N_KV_HEADS * HEAD_DIM
    for h in range(N_KV_HEADS):
        ks_ref[h] = kf
        ks_ref[h, :, 0:hd] = kv[:, h * hd:(h + 1) * hd].astype(BF16)
        vs_ref[h] = _value_rows(kv[:, kvw:2 * kvw], h, False)
        kw_ref[h] = kf[:, 0:LANES]
        kw_ref[h, :, 0:hd] = kv[:, 2 * kvw + h * hd:2 * kvw + (h + 1) * hd].astype(BF16)
        vw_ref[h] = _value_rows(kv[:, 3 * kvw:4 * kvw], h, True)


def _inproj(x2, g, w, w_gates, kfeat, conv_w, gc, tm=1024):
    rows = x2.shape[0]
    seq_tiles = kfeat.shape[0] // tm
    row = lambda i: (i, 0)
    head_row = lambda i: (0, i, 0)
    const = lambda i: (0, 0)
    return pl.pallas_call(
        functools.partial(_inproj_kernel, seq_tiles),
        grid=(rows // tm,),
        in_specs=[
            pl.BlockSpec((tm, D_MODEL), row),
            pl.BlockSpec((SUBLANES, D_MODEL), lambda i: (jnp.maximum(i * (tm // SUBLANES) - 1, 0), 0)),
            pl.BlockSpec((1, D_MODEL), const),
            pl.BlockSpec((D_MODEL, _G_OFF), const, pipeline_mode=pl.Buffered(1)),
            pl.BlockSpec((D_MODEL, _G_W), const, pipeline_mode=pl.Buffered(1)),
            pl.BlockSpec((tm, 2 * LANES), lambda i: (i % seq_tiles, 0)),
            pl.BlockSpec((CONV_K, CONV_CH), const),
            pl.BlockSpec((1, CONV_CH), const),
        ],
        out_specs=[
            pl.BlockSpec((tm, CONV_CH), row),
            pl.BlockSpec((tm, N_HEADS * HEAD_DIM), row),
            pl.BlockSpec((tm, _G_W), row),
            pl.BlockSpec((tm, _KV_OFF - _KVC_OFF), row),
            pl.BlockSpec((N_KV_HEADS, tm, 2 * LANES), head_row),
            pl.BlockSpec((N_KV_HEADS, tm, LANES), head_row),
            pl.BlockSpec((N_KV_HEADS, tm, LANES), head_row),
            pl.BlockSpec((N_KV_HEADS, tm, VW), head_row),
        ],
        out_shape=[
            jax.ShapeDtypeStruct((rows, CONV_CH), BF16),
            jax.ShapeDtypeStruct((rows, N_HEADS * HEAD_DIM), BF16),
            jax.ShapeDtypeStruct((rows, _G_W), F32),
            jax.ShapeDtypeStruct((rows, _KV_OFF - _KVC_OFF), F32),
            jax.ShapeDtypeStruct((N_KV_HEADS, rows, 2 * LANES), BF16),
            jax.ShapeDtypeStruct((N_KV_HEADS, rows, LANES), BF16),
            jax.ShapeDtypeStruct((N_KV_HEADS, rows, LANES), BF16),
            jax.ShapeDtypeStruct((N_KV_HEADS, rows, VW), BF16),
        ],
        compiler_params=pltpu.CompilerParams(
            dimension_semantics=("arbitrary",), vmem_limit_bytes=VMEM_LIMIT),
        name="inproj",
    )(x2, x2, g, w, w_gates, kfeat, conv_w, gc)


def _compress_kernel(xk_ref, xv_ref, pos_ref, w1_ref, wl_ref, w2_ref, cf_ref, kc_ref, vc_ref):
    nch = xk_ref.shape[0] // CMP_STRIDE
    hd = HEAD_DIM
    acc = [jnp.zeros((nch, 2 * CMP_HIDDEN), F32) for _ in range(2 * N_KV_HEADS)]
    for l in range(CMP_STRIDE):
        for s, x_ref in enumerate((xk_ref, xv_ref)):
            xl = x_ref[pl.ds(l, nch, stride=CMP_STRIDE), :].astype(BF16)
            for h in range(N_KV_HEADS):
                c = s * N_KV_HEADS + h
                acc[c] = acc[c] + _dot(xl[:, h * hd:(h + 1) * hd], wl_ref[s, l])
    for s in range(2):
        pb = _dot(pos_ref[s], w1_ref[s])[0:1, :]
        outs = []
        for h in range(N_KV_HEADS):
            a = acc[s * N_KV_HEADS + h]
            hid = a[:, 0:CMP_HIDDEN] + pltpu.roll(a[:, CMP_HIDDEN:], nch - 1, axis=0) + pb
            outs.append(_dot(jax.nn.gelu(hid, approximate=True).astype(BF16), w2_ref[s]))
        for h in range(N_KV_HEADS):
            if s == 0:
                kc_ref[h] = cf_ref[...]
                kc_ref[h, :, 0:hd] = outs[h].astype(BF16)
            else:
                vc_ref[h] = _value_rows(jnp.concatenate(outs, axis=1), h, True)


def _compress(kvc, pos, w1, wl, w2, cfeat, B, S):
    nch = S // CMP_STRIDE
    full = lambda *shape: pl.BlockSpec(shape, lambda b: (0,) * len(shape))
    return pl.pallas_call(
        _compress_kernel,
        grid=(B,),
        in_specs=[
            pl.BlockSpec((S, LANES), lambda b: (b, 0)),
            pl.BlockSpec((S, LANES), lambda b: (b, 1)),
            full(*pos.shape), full(*w1.shape), full(*wl.shape), full(*w2.shape), full(*cfeat.shape),
        ],
        out_specs=[
            pl.BlockSpec((None, N_KV_HEADS, nch, LANES), lambda b: (b, 0, 0, 0)),
            pl.BlockSpec((None, N_KV_HEADS, nch, VW), lambda b: (b, 0, 0, 0)),
        ],
        out_shape=[
            jax.ShapeDtypeStruct((B, N_KV_HEADS, nch, LANES), BF16),
            jax.ShapeDtypeStruct((B, N_KV_HEADS, nch, VW), BF16),
        ],
        compiler_params=pltpu.CompilerParams(
            dimension_semantics=("arbitrary",), vmem_limit_bytes=VMEM_LIMIT),
        name="compress",
    )(kvc, kvc, pos, w1, wl, w2, cfeat)


def _hi_lo(x):
    hi = x.astype(BF16)
    return jnp.concatenate([hi, (x - hi.astype(F32)).astype(BF16)], axis=1)


def _head_layout(acc):
    low_half = lax.broadcasted_iota(jnp.int32, (TQ, LANES), 1) < HEAD_DIM
    head = lambda x, g: x[g * TQ:(g + 1) * TQ]
    if acc.shape[1] == VW:
        x = acc[:, 0:LANES] * (1.0 / jnp.maximum(acc[:, LANES:VW], 1e-30))
        pairs = [jnp.where(low_half, head(x, 2 * j), head(x, 2 * j + 1)) for j in range(GQA // 2)]
    else:
        rot = pltpu.roll(acc, HEAD_DIM, axis=1)
        pairs = [jnp.where(low_half, head(acc, 2 * j) * (1.0 / head(rot, 2 * j)),
                           head(rot, 2 * j + 1) * (1.0 / head(acc, 2 * j + 1))) for j in range(GQA // 2)]
    return jnp.concatenate(pairs, axis=1)


def _exchange(v, i, j, descending):
    hi, lo = jnp.maximum(v[i], v[j]), jnp.minimum(v[i], v[j])
    v[i], v[j] = (hi, lo) if descending else (lo, hi)


def _bitonic_merge_desc(v):
    n = len(v)
    j = n // 2
    while j >= 1:
        for i in range(n):
            if i ^ j > i:
                _exchange(v, i, i ^ j, True)
        j //= 2


def _kth_largest(x, kth, fillers=()):
    n = x.shape[0]
    assert n == kth * SUBLANES and kth & (kth - 1) == 0
    v = [x[r * SUBLANES:(r + 1) * SUBLANES, :] for r in range(kth)]
    fillers = list(fillers)
    layers = 0
    k = 2
    while k <= kth:
        j = k // 2
        while j >= 1:
            for i in range(kth):
                if i ^ j > i:
                    _exchange(v, i, i ^ j, (i & k) == 0)
            j //= 2
            layers += 1
            if layers % 3 == 0 and fillers:
                fillers.pop(0)()
        k *= 2
    shift = SUBLANES // 2
    while shift >= 1:
        v = [jnp.maximum(v[i], pltpu.roll(v[kth - 1 - i], shift, axis=0)) for i in range(kth)]
        if shift > 1:
            _bitonic_merge_desc(v)
        if fillers:
            fillers.pop(0)()
        shift //= 2
    for filler in fillers:
        filler()
    out = v[0]
    for i in range(1, kth):
        out = jnp.minimum(out, v[i])
    return out[0:1, :]


def _nsa_kernel(q_ref, qf_ref, kc_ref, vc_ref, ks_ref, vs_ref, kw_ref, vw_ref, ov_ref, cm_ref, mb_ref,
                ge_ref, g_ref, o_ref, qa_ref, tiles_ref, m_ref, acc_ref, sa_ref, sb_ref):
    i = pl.program_id(2)
    t0 = i * TQ
    rows = GQA * TQ
    n_sel = ov_ref.shape[1]
    ncmp = kc_ref.shape[0]

    q = q_ref[...]
    for g in range(GQA):
        qg = q[:, g * HEAD_DIM:(g + 1) * HEAD_DIM].astype(F32) * (HEAD_DIM ** -0.5)
        qa_ref[g * TQ:(g + 1) * TQ, 0:HEAD_DIM] = qg.astype(BF16)
    qa_ref[:, HEAD_DIM:LANES] = qf_ref[...]
    qa1 = qa_ref[:, 0:LANES]

    diag_ok = (lax.broadcasted_iota(jnp.int32, (rows, TQ), 1)
               <= (lax.broadcasted_iota(jnp.int32, (rows, TQ), 0) & (TQ - 1)))

    wk = TQ + WINDOW
    w0 = pl.multiple_of(t0, TQ)
    win = {}

    def win_scores():
        s_w = _dot_nt(qa1, kw_ref[pl.ds(w0, wk), :])
        win["s"] = (jnp.where(diag_ok, NEG, s_w[:, 0:TQ]),
                    s_w[:, TQ:WINDOW], jnp.where(diag_ok, s_w[:, WINDOW:wk], NEG))

    def win_max():
        s_old, s_mid, s_new = win["s"]
        win["m"] = jnp.maximum(jnp.maximum(jnp.max(s_old, axis=1, keepdims=True),
                                           jnp.max(s_mid, axis=1, keepdims=True)),
                               jnp.max(s_new, axis=1, keepdims=True))

    def win_exp():
        win["e"] = jnp.concatenate([jnp.exp(s - win["m"]) for s in win["s"]], axis=1).astype(BF16)

    def win_pv():
        win["acc"] = _dot(win["e"], vw_ref[pl.ds(w0, wk), :])

    s_c = _dot_nt(qa1, kc_ref[...])
    s_c = jnp.where(cm_ref[...] <= (TQ // CMP_STRIDE) * i - 2, s_c, NEG)
    m_c = jnp.maximum(jnp.max(s_c, axis=1, keepdims=True), 0.5 * NEG)
    e_c = jnp.exp(s_c - m_c)
    acc_c = _dot(e_c.astype(BF16), vc_ref[...])
    inv_c = 1.0 / jnp.maximum(acc_c[:, LANES:VW], 1e-30)
    p_c = e_c * jnp.concatenate([inv_c] * (ncmp // LANES), axis=1)

    p_sum = p_c[0:TQ] + p_c[TQ:2 * TQ] + p_c[2 * TQ:3 * TQ] + p_c[3 * TQ:4 * TQ]
    p_hi = p_sum.astype(BF16)
    p_lo = (p_sum - p_hi.astype(F32)).astype(BF16)
    imp = _dot(p_hi, ov_ref[...]) + _dot(p_lo, ov_ref[...])
    imp_t = imp.T
    jb = lax.broadcasted_iota(jnp.int32, (n_sel, TQ), 0)
    tq = t0 + lax.broadcasted_iota(jnp.int32, (n_sel, TQ), 1)
    imp_t = jnp.where(SEL_BLOCK * jb > tq, -jnp.inf, imp_t)
    forced = (jb == 0) | (jb == (tq >> 6))
    imp_t = jnp.where(forced, jnp.inf, imp_t)

    thr = _kth_largest(imp_t, SEL_TOP, (win_scores, win_max, win_exp, win_pv))
    acc_w = win["acc"]
    above = imp_t > thr
    tied = imp_t == thr
    n_above = jnp.sum(jnp.where(above, 1.0, 0.0), axis=0, keepdims=True)
    lower = (lax.broadcasted_iota(jnp.int32, (n_sel, n_sel), 1)
             < lax.broadcasted_iota(jnp.int32, (n_sel, n_sel), 0))
    tied_before = _dot(jnp.where(lower, 1.0, 0.0).astype(BF16), jnp.where(tied, 1.0, 0.0).astype(BF16))
    take_tie = jnp.where(tied_before + n_above < float(SEL_TOP), 1.0, 0.0)
    sel_t = jnp.where(above, 1.0, jnp.where(tied, take_tie, 0.0))
    sel_q = sel_t.T
    bias = ((1.0 - sel_q) * MASK_BIAS).astype(BF16)
    for g in range(GQA):
        qa_ref[g * TQ:(g + 1) * TQ, LANES:2 * LANES] = bias

    n_past = t0 // TK
    first_mask = (t0 - n_past * TK) // TQ
    tiles_ref[0] = n_past

    def scores(j, s_ref, mask=None):
        k0 = pl.multiple_of(tiles_ref[j] * TK, TK)
        s = _dot_nt(qa_ref[...], ks_ref[pl.ds(k0, TK), :])
        s_ref[...] = s if mask is None else s + mask

    scores(0, sa_ref, mb_ref[first_mask])

    bpt = TK // SEL_BLOCK
    n_tiles = n_sel // bpt
    blk_cnt = _dot(jnp.ones((SUBLANES, TQ), BF16), sel_q.astype(BF16))
    in_tile = (lax.broadcasted_iota(jnp.int32, (n_sel, LANES), 0) // bpt
               == lax.broadcasted_iota(jnp.int32, (n_sel, LANES), 1))
    tile_cnt = _dot(jnp.where(blk_cnt > 0.5, 1.0, 0.0).astype(BF16),
                    jnp.where(in_tile, 1.0, 0.0).astype(BF16))
    lane = lax.broadcasted_iota(jnp.int32, (1, LANES), 1)
    bit = jnp.where(tile_cnt[0:1, :] > 0.5, jnp.left_shift(1, lane & 15).astype(F32), 0.0)
    words = [jnp.sum(jnp.where((lane >= 16 * w) & (lane < 16 * (w + 1)), bit, 0.0)).astype(jnp.int32)
             for w in range((n_tiles + 15) // 16)]
    n_tasks = jnp.int32(1)
    for kt in range(n_tiles - 1):
        used = (words[kt // 16] >> (kt % 16)) & 1
        tiles_ref[n_tasks] = kt
        n_tasks = n_tasks + jnp.where(kt < n_past, used, 0)
    tiles_ref[n_tasks] = 0
    tiles_ref[n_tasks + 1] = 0

    m_ref[...] = jnp.full(m_ref.shape, NEG, F32)
    acc_ref[...] = jnp.zeros(acc_ref.shape, F32)

    def accumulate(j, s_ref):
        k0 = pl.multiple_of(tiles_ref[j] * TK, TK)
        slot = jnp.where(j < n_tasks, 0, 1)
        s = s_ref[...]
        m = m_ref[slot]
        m_new = jnp.maximum(m, jnp.max(s, axis=1, keepdims=True))
        p = jnp.concatenate([jnp.exp(s[:, c:c + LANES] - m_new) for c in range(0, TK, LANES)], axis=1)
        acc_ref[slot] = (jnp.exp(m - m_new) * acc_ref[slot]
                         + _dot(p.astype(BF16), vs_ref[pl.ds(k0, TK), :]))
        m_ref[slot] = m_new

    def pair(jj, c):
        a = 2 * jj
        scores(a + 1, sb_ref)
        accumulate(a, sa_ref)
        scores(a + 2, sa_ref)
        accumulate(a + 1, sb_ref)
        return c

    lax.fori_loop(0, (n_tasks + 1) // 2, pair, 0)

    gates = _dot(_hi_lo(g_ref[...]), ge_ref[...])
    gw = GQA * HEAD_DIM
    o_ref[...] = (gates[:, 0:gw] * _head_layout(acc_c) + gates[:, gw:2 * gw] * _head_layout(acc_ref[0])
                  + gates[:, 2 * gw:3 * gw] * _head_layout(acc_w)).astype(o_ref.dtype)


def _nsa(q, qfeat, kc_aug, vc, ks_aug, vs, kw_aug, vw, ov, cmask, tile_masks, gate_expand, gates, B, S):
    nq = S // TQ
    gw = GQA * HEAD_DIM
    ncmp = kc_aug.shape[2]
    n_sel = ov.shape[1]
    sp = kw_aug.shape[2]
    bk = lambda b, k, i: (b, k, 0, 0)
    kb = lambda b, k, i: (k, b, 0, 0)
    return pl.pallas_call(
        _nsa_kernel,
        grid=(B, N_KV_HEADS, nq),
        in_specs=[
            pl.BlockSpec((TQ, gw), lambda b, k, i: (b * nq + i, k)),
            pl.BlockSpec((None, GQA * TQ, HEAD_DIM), lambda b, k, i: (k, 0, 0)),
            pl.BlockSpec((None, None, ncmp, LANES), bk),
            pl.BlockSpec((None, None, ncmp, VW), bk),
            pl.BlockSpec((None, None, S, 2 * LANES), kb),
            pl.BlockSpec((None, None, S, LANES), kb),
            pl.BlockSpec((None, None, sp, LANES), kb),
            pl.BlockSpec((None, None, sp, VW), kb),
            pl.BlockSpec((ncmp, n_sel), lambda b, k, i: (0, 0)),
            pl.BlockSpec((GQA * TQ, ncmp), lambda b, k, i: (0, 0)),
            pl.BlockSpec(tile_masks.shape, lambda b, k, i: (0, 0, 0)),
            pl.BlockSpec(gate_expand.shape, lambda b, k, i: (0, 0)),
            pl.BlockSpec((TQ, LANES), lambda b, k, i: (b * nq + i, k)),
        ],
        out_specs=pl.BlockSpec((TQ, gw), lambda b, k, i: (b * nq + i, k)),
        out_shape=jax.ShapeDtypeStruct((B * S, N_KV_HEADS * gw), BF16),
        scratch_shapes=[
            pltpu.VMEM((GQA * TQ, 2 * LANES), BF16),
            pltpu.SMEM((S // TK + 2,), jnp.int32),
            pltpu.VMEM((2, GQA * TQ, LANES), F32),
            pltpu.VMEM((2, GQA * TQ, LANES), F32),
            pltpu.VMEM((GQA * TQ, TK), F32),
            pltpu.VMEM((GQA * TQ, TK), F32),
        ],
        compiler_params=pltpu.CompilerParams(
            dimension_semantics=("arbitrary",) * 3, vmem_limit_bytes=VMEM_LIMIT),
        name="nsa",
    )(q, qfeat, kc_aug, vc, ks_aug, vs, kw_aug, vw, ov, cmask, tile_masks, gate_expand, gates)


_FF_CHUNK = 768


def _outproj_kernel(mc_ref, nsa_ref, x_ref, gn_ref, wo_ref, h_ref):
    mixed_n = _rms(nsa_ref[...].astype(F32), gn_ref[...]).astype(BF16)
    h_ref[...] = (x_ref[...] + _dot(mc_ref[...], wo_ref[0:CONV_CH, :])
                  + _dot(mixed_n, wo_ref[CONV_CH:2 * CONV_CH, :]))


def _outproj(mixed_c, nsa_out, x2, gn, wo, tm=512):
    rows = x2.shape[0]
    row = lambda i: (i, 0)
    const = lambda i: (0, 0)
    return pl.pallas_call(
        _outproj_kernel,
        grid=(rows // tm,),
        in_specs=[
            pl.BlockSpec((tm, CONV_CH), row),
            pl.BlockSpec((tm, N_HEADS * HEAD_DIM), row),
            pl.BlockSpec((tm, D_MODEL), row),
            pl.BlockSpec((1, N_HEADS * HEAD_DIM), const),
            pl.BlockSpec((2 * CONV_CH, D_MODEL), const, pipeline_mode=pl.Buffered(1)),
        ],
        out_specs=pl.BlockSpec((tm, D_MODEL), row),
        out_shape=jax.ShapeDtypeStruct((rows, D_MODEL), F32),
        compiler_params=pltpu.CompilerParams(
            dimension_semantics=("arbitrary",), vmem_limit_bytes=VMEM_LIMIT),
        name="outproj",
    )(mixed_c, nsa_out, x2, gn, wo)


def _outffn_kernel(h_ref, g2_ref, wg_ref, wu_ref, wd_ref, gf_ref, o_ref):
    h = h_ref[...]
    u = _rms(h, g2_ref[...]).astype(BF16)
    acc = h
    for c in range(0, D_FF, _FF_CHUNK):
        sl = slice(c, min(c + _FF_CHUNK, D_FF))
        a = jax.nn.silu(_dot(u, wg_ref[:, sl])) * _dot(u, wu_ref[:, sl])
        acc = acc + _dot(a.astype(BF16), wd_ref[sl, :])
    o_ref[...] = _rms(acc, gf_ref[...])


def _outffn(h, g2, wg, wu, wd, gf, tm=512):
    rows = h.shape[0]
    row = lambda i: (i, 0)
    const = lambda i: (0, 0)
    resident = lambda shape: pl.BlockSpec(shape, const, pipeline_mode=pl.Buffered(1))
    return pl.pallas_call(
        _outffn_kernel,
        grid=(rows // tm,),
        in_specs=[
            pl.BlockSpec((tm, D_MODEL), row),
            pl.BlockSpec((1, D_MODEL), const),
            resident((D_MODEL, D_FF)),
            resident((D_MODEL, D_FF)),
            resident((D_FF, D_MODEL)),
            pl.BlockSpec((1, D_MODEL), const),
        ],
        out_specs=pl.BlockSpec((tm, D_MODEL), row),
        out_shape=jax.ShapeDtypeStruct((rows, D_MODEL), F32),
        compiler_params=pltpu.CompilerParams(
            dimension_semantics=("arbitrary",), vmem_limit_bytes=VMEM_LIMIT),
        name="outffn",
    )(h, g2, wg, wu, wd, gf)


def _key_features(pos, n_sel):
    f = np.zeros((pos.shape[0], LANES + n_sel), np.float32)
    f[:, HEAD_DIM] = pos // SEL_BLOCK
    f[:, HEAD_DIM + 1] = pos % SEL_BLOCK
    if n_sel:
        f[np.arange(pos.shape[0]), LANES + pos // SEL_BLOCK] = 1.0
    return f


def _query_features():
    f = np.zeros((N_KV_HEADS, GQA * TQ, HEAD_DIM), np.float32)
    for k in range(N_KV_HEADS):
        for g in range(GQA):
            slope = 2.0 ** (-8.0 * (k * GQA + g + 1) / N_HEADS)
            f[k, g * TQ:(g + 1) * TQ, 0] = slope * SEL_BLOCK
            f[k, g * TQ:(g + 1) * TQ, 1] = slope
            f[k, g * TQ:(g + 1) * TQ, 2] = MASK_BIAS
    return f


def _causal_masks(ncmp):
    r = np.arange(GQA * TQ)[:, None] % TQ
    cmask = (np.arange(ncmp)[None, :] - ((r + 1) // CMP_STRIDE)).astype(np.int32)
    c = np.arange(TK)[None, :]
    tile_masks = [np.where(c - r <= d * TQ, 0.0, NEG).astype(np.float32) for d in range(max(TK // TQ, 1))]
    return cmask, np.stack(tile_masks)


def _gate_expand():
    gw = GQA * HEAD_DIM
    e = np.zeros((2 * LANES, N_BRANCH * gw), np.float32)
    for g in range(GQA):
        for br in range(N_BRANCH):
            for half in range(2):
                e[half * LANES + g * N_BRANCH + br, br * gw + g * HEAD_DIM:br * gw + (g + 1) * HEAD_DIM] = 1.0
    return e


def _overlap(ncmp_pad, n_sel):
    nc = ncmp_pad - 1
    c_start = CMP_STRIDE * np.arange(nc)
    c_end = c_start + CMP_BLOCK - 1
    s_start = SEL_BLOCK * np.arange(n_sel)
    ov = np.minimum(c_end[:, None] + 1, s_start[None, :] + SEL_BLOCK) - np.maximum(c_start[:, None], s_start[None, :])
    out = np.zeros((ncmp_pad, n_sel), np.float32)
    out[:nc] = np.clip(ov, 0, None).astype(np.float32) / CMP_BLOCK
    return out


def _layer(x2, B, S, norm1_g, w_in, conv_w, k_cmp_pos, k_cmp_w1, k_cmp_w2, v_cmp_pos, v_cmp_w1, v_cmp_w2,
           gn_conv_g):
    n_sel = S // SEL_BLOCK
    nch = S // CMP_STRIDE
    hd = HEAD_DIM

    ng = GQA * N_BRANCH
    w_g = w_in[:, _G_OFF:].astype(BF16)
    gpad = jnp.zeros((D_MODEL, LANES - ng), BF16)
    w_gates = jnp.concatenate([w_g[:, :ng], gpad, w_g[:, ng:], gpad], axis=1)

    kfeat = jnp.asarray(_key_features(np.arange(S), n_sel), BF16)
    mixed_c, q, gates, kvc, ks_aug, vs, kw_aug, vw = _inproj(
        x2, norm1_g.reshape(1, D_MODEL), w_in[:, :_G_OFF].astype(BF16), w_gates, kfeat, conv_w,
        gn_conv_g.reshape(1, CONV_CH))

    w1 = jnp.stack([k_cmp_w1, v_cmp_w1]).astype(BF16)
    wl = w1.reshape(2, 2, CMP_STRIDE, hd, CMP_HIDDEN).transpose(0, 2, 3, 1, 4)
    wl = wl.reshape(2, CMP_STRIDE, hd, 2 * CMP_HIDDEN)
    w2 = jnp.stack([k_cmp_w2, v_cmp_w2]).astype(BF16)
    pos = jnp.stack([k_cmp_pos, v_cmp_pos]).reshape(2, 1, CMP_BLOCK * hd)
    pos = jnp.broadcast_to(pos, (2, SUBLANES, CMP_BLOCK * hd)).astype(BF16)
    c_end = CMP_STRIDE * np.arange(nch) + CMP_BLOCK - 1
    cfeat = jnp.asarray(_key_features(c_end, 0), BF16)
    kc_aug, vc = _compress(kvc, pos, w1, wl, w2, cfeat, B, S)

    pad_rows = np.zeros((WINDOW, LANES), np.float32)
    pad_rows[:, hd + 2] = 1.0
    pad_k = jnp.broadcast_to(jnp.asarray(pad_rows, BF16), (N_KV_HEADS, B, WINDOW, LANES))
    kw_aug = jnp.concatenate([pad_k, kw_aug.reshape(N_KV_HEADS, B, S, LANES)], axis=2)
    vw = jnp.pad(vw.reshape(N_KV_HEADS, B, S, VW), ((0, 0), (0, 0), (WINDOW, 0), (0, 0)))
    cmask, tile_masks = _causal_masks(nch)

    nsa_out = _nsa(q, jnp.asarray(_query_features(), BF16), kc_aug, vc,
                   ks_aug.reshape(N_KV_HEADS, B, S, 2 * LANES), vs.reshape(N_KV_HEADS, B, S, LANES),
                   kw_aug, vw, jnp.asarray(_overlap(nch, n_sel), BF16), jnp.asarray(cmask),
                   jnp.asarray(tile_masks), jnp.asarray(_gate_expand(), BF16), gates, B, S)
    return mixed_c, nsa_out


def kernel(x, norm1_g, w_in, conv_w, k_cmp_pos, k_cmp_w1, k_cmp_w2, v_cmp_pos, v_cmp_w1, v_cmp_w2,
           gn_conv_g, gn_nsa_g, w_out, norm2_g, w_gate, w_up, w_down, norm_f_g):
    B, S, _ = x.shape
    depth = norm1_g.shape[0]
    assert depth == 1, "the final norm is fused into the (single) layer's FFN kernel"
    x2 = x.reshape(B * S, D_MODEL)
    mixed_c, nsa_out = _layer(x2, B, S, norm1_g[0], w_in[0], conv_w[0], k_cmp_pos[0], k_cmp_w1[0],
                              k_cmp_w2[0], v_cmp_pos[0], v_cmp_w1[0], v_cmp_w2[0], gn_conv_g[0])
    h = _outproj(mixed_c, nsa_out, x2, gn_nsa_g[0].reshape(1, N_HEADS * HEAD_DIM), w_out[0].astype(BF16))
    out = _outffn(h, norm2_g[0].reshape(1, D_MODEL), w_gate[0].astype(BF16), w_up[0].astype(BF16),
                  w_down[0].astype(BF16), norm_f_g.reshape(1, D_MODEL))
    return out.reshape(B, S, D_MODEL)
```

```python
import functools

import numpy as np
import jax
import jax.numpy as jnp
from jax import lax
from jax.experimental import pallas as pl
from jax.experimental.pallas import tpu as pltpu

D_MODEL = 1024
CONV_CH = 512
CONV_K = 3
N_HEADS = 8
HEAD_DIM = 64
N_KV_HEADS = 2
GQA = 4
N_BRANCH = 3
CMP_BLOCK = 32
CMP_STRIDE = 16
CMP_HIDDEN = 256
SEL_BLOCK = 64
SEL_TOP = 16
WINDOW = 512
D_FF = 2816
NORM_EPS = 1e-6

F32 = jnp.float32
BF16 = jnp.bfloat16

LANES = 128
VW = 2 * LANES
SUBLANES = 8
VMEM_LIMIT = 56 * 1024 * 1024

TQ = 256
TK = 256
NEG = -1e30
MASK_BIAS = -float(2 ** 30)

_CONV_W = 3 * CONV_CH
_Q_OFF = _CONV_W
_KVC_OFF = _Q_OFF + N_HEADS * HEAD_DIM
_KV_OFF = _KVC_OFF + 2 * N_KV_HEADS * HEAD_DIM
_G_OFF = _KV_OFF + 4 * N_KV_HEADS * HEAD_DIM
_G_W = N_HEADS * N_BRANCH
_PROJ_W = _G_OFF + _G_W


def _rms(x, g):
    return x * lax.rsqrt(jnp.mean(x * x, axis=-1, keepdims=True) + NORM_EPS) * g


def _dot(a, b):
    return jnp.dot(a, b, preferred_element_type=F32)


def _value_rows(v2, h, wide):
    lane = lax.broadcasted_iota(jnp.int32, v2.shape, 1)
    swapped = pltpu.roll(v2, HEAD_DIM, axis=1)
    if not wide:
        return jnp.where(lane < HEAD_DIM, v2 if h == 0 else swapped, 1.0).astype(BF16)
    dup = jnp.where((lane < HEAD_DIM) == (h == 0), v2, swapped)
    return jnp.concatenate([dup.astype(BF16), jnp.ones(v2.shape, BF16)], axis=1)


def _dot_nt(a, b):
    return lax.dot_general(a, b, (((1,), (1,)), ((), ())), preferred_element_type=F32)


def _inproj_kernel(seq_tiles, x_ref, xp_ref, g_ref, w_ref, kf_ref, cw_ref, gc_ref, mixc_ref, q_ref, gate_ref,
                   kvc_ref, ks_ref, vs_ref, kw_ref, vw_ref):
    u = _rms(x_ref[...], g_ref[...]).astype(BF16)

    cbch = _dot(u, w_ref[:, 0:_CONV_W])
    cu = cbch[:, CONV_CH:2 * CONV_CH] * cbch[:, 2 * CONV_CH:3 * CONV_CH]
    up = _rms(xp_ref[...], g_ref[...]).astype(BF16)
    chp = _dot(up, w_ref[:, CONV_CH:_CONV_W])
    starts_sequence = pl.program_id(0) % seq_tiles == 0
    prev = jnp.where(starts_sequence, 0.0, chp[:, 0:CONV_CH] * chp[:, CONV_CH:2 * CONV_CH])
    r = lax.broadcasted_iota(jnp.int32, cu.shape, 0)
    cu1 = jnp.where(r == 0, prev[7:8, :], pltpu.roll(cu, 1, axis=0))
    cu2 = pltpu.roll(cu, 2, axis=0)
    cu2 = jnp.where(r == 0, prev[6:7, :], jnp.where(r == 1, prev[7:8, :], cu2))
    cw = cw_ref[...]
    y = cbch[:, 0:CONV_CH] * (cw[0:1, :] * cu2 + cw[1:2, :] * cu1 + cw[2:3, :] * cu)
    mixc_ref[...] = _rms(y, gc_ref[...]).astype(BF16)

    q_ref[...] = _dot(u, w_ref[:, _Q_OFF:_KVC_OFF]).astype(BF16)
    kvc_ref[...] = _dot(u, w_ref[:, _KVC_OFF:_KV_OFF])
    gate_ref[...] = jax.nn.sigmoid(_dot(u, w_ref[:, _G_OFF:_PROJ_W]))
    kv = _dot(u, w_ref[:, _KV_OFF:_G_OFF])
    kf = kf_ref[...]
    hd, kvw = HEAD_DIM, N_KV_HEADS * HEAD_DIM
    for h in range(N_KV_HEADS):
        ks_ref[h] = kf
        ks_ref[h, :, 0:hd] = kv[:, h * hd:(h + 1) * hd].astype(BF16)
        vs_ref[h] = _value_rows(kv[:, kvw:2 * kvw], h, False)
        kw_ref[h] = kf[:, 0:LANES]
        kw_ref[h, :, 0:hd] = kv[:, 2 * kvw + h * hd:2 * kvw + (h + 1) * hd].astype(BF16)
        vw_ref[h] = _value_rows(kv[:, 3 * kvw:4 * kvw], h, True)


def _inproj(x2, g, w, kfeat, conv_w, gc, tm=1024):
    rows = x2.shape[0]
    seq_tiles = kfeat.shape[0] // tm
    row = lambda i: (i, 0)
    head_row = lambda i: (0, i, 0)
    const = lambda i: (0, 0)
    return pl.pallas_call(
        functools.partial(_inproj_kernel, seq_tiles),
        grid=(rows // tm,),
        in_specs=[
            pl.BlockSpec((tm, D_MODEL), row),
            pl.BlockSpec((SUBLANES, D_MODEL), lambda i: (jnp.maximum(i * (tm // SUBLANES) - 1, 0), 0)),
            pl.BlockSpec((1, D_MODEL), const),
            pl.BlockSpec((D_MODEL, _PROJ_W), const, pipeline_mode=pl.Buffered(1)),
            pl.BlockSpec((tm, 2 * LANES), lambda i: (i % seq_tiles, 0)),
            pl.BlockSpec((CONV_K, CONV_CH), const),
            pl.BlockSpec((1, CONV_CH), const),
        ],
        out_specs=[
            pl.BlockSpec((tm, CONV_CH), row),
            pl.BlockSpec((tm, N_HEADS * HEAD_DIM), row),
            pl.BlockSpec((tm, _G_W), row),
            pl.BlockSpec((tm, _KV_OFF - _KVC_OFF), row),
            pl.BlockSpec((N_KV_HEADS, tm, 2 * LANES), head_row),
            pl.BlockSpec((N_KV_HEADS, tm, LANES), head_row),
            pl.BlockSpec((N_KV_HEADS, tm, LANES), head_row),
            pl.BlockSpec((N_KV_HEADS, tm, VW), head_row),
        ],
        out_shape=[
            jax.ShapeDtypeStruct((rows, CONV_CH), BF16),
            jax.ShapeDtypeStruct((rows, N_HEADS * HEAD_DIM), BF16),
            jax.ShapeDtypeStruct((rows, _G_W), F32),
            jax.ShapeDtypeStruct((rows, _KV_OFF - _KVC_OFF), F32),
            jax.ShapeDtypeStruct((N_KV_HEADS, rows, 2 * LANES), BF16),
            jax.ShapeDtypeStruct((N_KV_HEADS, rows, LANES), BF16),
            jax.ShapeDtypeStruct((N_KV_HEADS, rows, LANES), BF16),
            jax.ShapeDtypeStruct((N_KV_HEADS, rows, VW), BF16),
        ],
        compiler_params=pltpu.CompilerParams(
            dimension_semantics=("arbitrary",), vmem_limit_bytes=VMEM_LIMIT),
        name="inproj",
    )(x2, x2, g, w, kfeat, conv_w, gc)


def _compress_kernel(xk_ref, xv_ref, pos_ref, w1_ref, w2_ref, cf_ref, kc_ref, vc_ref):
    nch = xk_ref.shape[0] // CMP_STRIDE
    hd = HEAD_DIM
    first = [jnp.zeros((nch, CMP_HIDDEN), F32) for _ in range(2 * N_KV_HEADS)]
    second = [jnp.zeros((nch, CMP_HIDDEN), F32) for _ in range(2 * N_KV_HEADS)]
    for l in range(CMP_STRIDE):
        for s, x_ref in enumerate((xk_ref, xv_ref)):
            xl = x_ref[pl.ds(l, nch, stride=CMP_STRIDE), :].astype(BF16)
            for h in range(N_KV_HEADS):
                c = s * N_KV_HEADS + h
                xh = xl[:, h * hd:(h + 1) * hd]
                first[c] = first[c] + _dot(xh, w1_ref[s, l * hd:(l + 1) * hd, :])
                second[c] = second[c] + _dot(xh, w1_ref[s, (CMP_STRIDE + l) * hd:(CMP_STRIDE + l + 1) * hd, :])
    for s in range(2):
        pb = _dot(pos_ref[s], w1_ref[s])[0:1, :]
        outs = []
        for h in range(N_KV_HEADS):
            c = s * N_KV_HEADS + h
            hid = first[c] + pltpu.roll(second[c], nch - 1, axis=0) + pb
            outs.append(_dot(jax.nn.gelu(hid, approximate=True).astype(BF16), w2_ref[s]))
        for h in range(N_KV_HEADS):
            if s == 0:
                kc_ref[h] = cf_ref[...]
                kc_ref[h, :, 0:hd] = outs[h].astype(BF16)
            else:
                vc_ref[h] = _value_rows(jnp.concatenate(outs, axis=1), h, True)


def _compress(kvc, pos, w1, w2, cfeat, B, S):
    nch = S // CMP_STRIDE
    full = lambda *shape: pl.BlockSpec(shape, lambda b: (0,) * len(shape))
    return pl.pallas_call(
        _compress_kernel,
        grid=(B,),
        in_specs=[
            pl.BlockSpec((S, LANES), lambda b: (b, 0)),
            pl.BlockSpec((S, LANES), lambda b: (b, 1)),
            full(*pos.shape), full(*w1.shape), full(*w2.shape), full(*cfeat.shape),
        ],
        out_specs=[
            pl.BlockSpec((None, N_KV_HEADS, nch, LANES), lambda b: (b, 0, 0, 0)),
            pl.BlockSpec((None, N_KV_HEADS, nch, VW), lambda b: (b, 0, 0, 0)),
        ],
        out_shape=[
            jax.ShapeDtypeStruct((B, N_KV_HEADS, nch, LANES), BF16),
            jax.ShapeDtypeStruct((B, N_KV_HEADS, nch, VW), BF16),
        ],
        compiler_params=pltpu.CompilerParams(
            dimension_semantics=("arbitrary",), vmem_limit_bytes=VMEM_LIMIT),
        name="compress",
    )(kvc, kvc, pos, w1, w2, cfeat)


def _hi_lo(x):
    hi = x.astype(BF16)
    return jnp.concatenate([hi, (x - hi.astype(F32)).astype(BF16)], axis=1)


def _head_layout(acc):
    low_half = lax.broadcasted_iota(jnp.int32, (TQ, LANES), 1) < HEAD_DIM
    head = lambda x, g: x[g * TQ:(g + 1) * TQ]
    if acc.shape[1] == VW:
        x = acc[:, 0:LANES] * (1.0 / jnp.maximum(acc[:, LANES:VW], 1e-30))
        pairs = [jnp.where(low_half, head(x, 2 * j), head(x, 2 * j + 1)) for j in range(GQA // 2)]
    else:
        rot = pltpu.roll(acc, HEAD_DIM, axis=1)
        pairs = [jnp.where(low_half, head(acc, 2 * j) * (1.0 / head(rot, 2 * j)),
                           head(rot, 2 * j + 1) * (1.0 / head(acc, 2 * j + 1))) for j in range(GQA // 2)]
    return jnp.concatenate(pairs, axis=1)


def _exchange(v, i, j, descending):
    hi, lo = jnp.maximum(v[i], v[j]), jnp.minimum(v[i], v[j])
    v[i], v[j] = (hi, lo) if descending else (lo, hi)


def _bitonic_merge_desc(v):
    n = len(v)
    j = n // 2
    while j >= 1:
        for i in range(n):
            if i ^ j > i:
                _exchange(v, i, i ^ j, True)
        j //= 2


def _kth_largest(x, kth, fillers=()):
    n = x.shape[0]
    assert n == kth * SUBLANES and kth & (kth - 1) == 0
    v = [x[r * SUBLANES:(r + 1) * SUBLANES, :] for r in range(kth)]
    fillers = list(fillers)
    layers = 0
    k = 2
    while k <= kth:
        j = k // 2
        while j >= 1:
            for i in range(kth):
                if i ^ j > i:
                    _exchange(v, i, i ^ j, (i & k) == 0)
            j //= 2
            layers += 1
            if layers % 3 == 0 and fillers:
                fillers.pop(0)()
        k *= 2
    shift = SUBLANES // 2
    while shift >= 1:
        v = [jnp.maximum(v[i], pltpu.roll(v[kth - 1 - i], shift, axis=0)) for i in range(kth)]
        if shift > 1:
            _bitonic_merge_desc(v)
        if fillers:
            fillers.pop(0)()
        shift //= 2
    for filler in fillers:
        filler()
    out = v[0]
    for i in range(1, kth):
        out = jnp.minimum(out, v[i])
    return out[0:1, :]


def _nsa_kernel(q_ref, qf_ref, kc_ref, vc_ref, ks_ref, vs_ref, kw_ref, vw_ref, ov_ref, cm_ref, mb_ref,
                ge_ref, g_ref, o_ref, qa_ref, tiles_ref, m_ref, acc_ref, sa_ref, sb_ref):
    i = pl.program_id(2)
    t0 = i * TQ
    rows = GQA * TQ
    n_sel = ov_ref.shape[1]
    ncmp = kc_ref.shape[0]

    q = q_ref[...]
    for g in range(GQA):
        qg = q[:, g * HEAD_DIM:(g + 1) * HEAD_DIM].astype(F32) * (HEAD_DIM ** -0.5)
        qa_ref[g * TQ:(g + 1) * TQ, 0:HEAD_DIM] = qg.astype(BF16)
    qa_ref[:, HEAD_DIM:LANES] = qf_ref[...]
    qa1 = qa_ref[:, 0:LANES]

    diag_ok = (lax.broadcasted_iota(jnp.int32, (rows, TQ), 1)
               <= (lax.broadcasted_iota(jnp.int32, (rows, TQ), 0) & (TQ - 1)))

    wk = TQ + WINDOW
    w0 = pl.multiple_of(t0, TQ)
    win = {}

    def win_scores():
        s_w = _dot_nt(qa1, kw_ref[pl.ds(w0, wk), :])
        win["s"] = (jnp.where(diag_ok, NEG, s_w[:, 0:TQ]),
                    s_w[:, TQ:WINDOW], jnp.where(diag_ok, s_w[:, WINDOW:wk], NEG))

    def win_max():
        s_old, s_mid, s_new = win["s"]
        win["m"] = jnp.maximum(jnp.maximum(jnp.max(s_old, axis=1, keepdims=True),
                                           jnp.max(s_mid, axis=1, keepdims=True)),
                               jnp.max(s_new, axis=1, keepdims=True))

    def win_exp():
        win["e"] = jnp.concatenate([jnp.exp(s - win["m"]) for s in win["s"]], axis=1).astype(BF16)

    def win_pv():
        win["acc"] = _dot(win["e"], vw_ref[pl.ds(w0, wk), :])

    s_c = _dot_nt(qa1, kc_ref[...])
    s_c = jnp.where(cm_ref[...] <= (TQ // CMP_STRIDE) * i - 2, s_c, NEG)
    m_c = jnp.maximum(jnp.max(s_c, axis=1, keepdims=True), 0.5 * NEG)
    e_c = jnp.exp(s_c - m_c)
    acc_c = _dot(e_c.astype(BF16), vc_ref[...])
    inv_c = 1.0 / jnp.maximum(acc_c[:, LANES:VW], 1e-30)
    p_c = e_c * jnp.concatenate([inv_c] * (ncmp // LANES), axis=1)

    p_sum = p_c[0:TQ] + p_c[TQ:2 * TQ] + p_c[2 * TQ:3 * TQ] + p_c[3 * TQ:4 * TQ]
    p_hi = p_sum.astype(BF16)
    p_lo = (p_sum - p_hi.astype(F32)).astype(BF16)
    imp = _dot(p_hi, ov_ref[...]) + _dot(p_lo, ov_ref[...])
    imp_t = imp.T
    jb = lax.broadcasted_iota(jnp.int32, (n_sel, TQ), 0)
    tq = t0 + lax.broadcasted_iota(jnp.int32, (n_sel, TQ), 1)
    imp_t = jnp.where(SEL_BLOCK * jb > tq, -jnp.inf, imp_t)
    forced = (jb == 0) | (jb == (tq >> 6))
    imp_t = jnp.where(forced, jnp.inf, imp_t)

    thr = _kth_largest(imp_t, SEL_TOP, (win_scores, win_max, win_exp, win_pv))
    acc_w = win["acc"]
    above = imp_t > thr
    tied = imp_t == thr
    n_above = jnp.sum(jnp.where(above, 1.0, 0.0), axis=0, keepdims=True)
    lower = (lax.broadcasted_iota(jnp.int32, (n_sel, n_sel), 1)
             < lax.broadcasted_iota(jnp.int32, (n_sel, n_sel), 0))
    tied_before = _dot(jnp.where(lower, 1.0, 0.0).astype(BF16), jnp.where(tied, 1.0, 0.0).astype(BF16))
    take_tie = jnp.where(tied_before + n_above < float(SEL_TOP), 1.0, 0.0)
    sel_t = jnp.where(above, 1.0, jnp.where(tied, take_tie, 0.0))
    sel_q = sel_t.T
    bias = ((1.0 - sel_q) * MASK_BIAS).astype(BF16)
    for g in range(GQA):
        qa_ref[g * TQ:(g + 1) * TQ, LANES:2 * LANES] = bias

    n_past = t0 // TK
    first_mask = (t0 - n_past * TK) // TQ
    tiles_ref[0] = n_past

    def scores(j, s_ref, mask=None):
        k0 = pl.multiple_of(tiles_ref[j] * TK, TK)
        s = _dot_nt(qa_ref[...], ks_ref[pl.ds(k0, TK), :])
        s_ref[...] = s if mask is None else s + mask

    scores(0, sa_ref, mb_ref[first_mask])

    bpt = TK // SEL_BLOCK
    n_tiles = n_sel // bpt
    blk_cnt = _dot(jnp.ones((SUBLANES, TQ), BF16), sel_q.astype(BF16))
    in_tile = (lax.broadcasted_iota(jnp.int32, (n_sel, LANES), 0) // bpt
               == lax.broadcasted_iota(jnp.int32, (n_sel, LANES), 1))
    tile_cnt = _dot(jnp.where(blk_cnt > 0.5, 1.0, 0.0).astype(BF16),
                    jnp.where(in_tile, 1.0, 0.0).astype(BF16))
    lane = lax.broadcasted_iota(jnp.int32, (1, LANES), 1)
    bit = jnp.where(tile_cnt[0:1, :] > 0.5, jnp.left_shift(1, lane & 15).astype(F32), 0.0)
    words = [jnp.sum(jnp.where((lane >= 16 * w) & (lane < 16 * (w + 1)), bit, 0.0)).astype(jnp.int32)
             for w in range((n_tiles + 15) // 16)]
    n_tasks = jnp.int32(1)
    for kt in range(n_tiles - 1):
        used = (words[kt // 16] >> (kt % 16)) & 1
        tiles_ref[n_tasks] = kt
        n_tasks = n_tasks + jnp.where(kt < n_past, used, 0)
    tiles_ref[n_tasks] = 0
    tiles_ref[n_tasks + 1] = 0

    m_ref[...] = jnp.full(m_ref.shape, NEG, F32)
    acc_ref[...] = jnp.zeros(acc_ref.shape, F32)

    def accumulate(j, s_ref):
        k0 = pl.multiple_of(tiles_ref[j] * TK, TK)
        slot = jnp.where(j < n_tasks, 0, 1)
        s = s_ref[...]
        m = m_ref[slot]
        m_new = jnp.maximum(m, jnp.max(s, axis=1, keepdims=True))
        p = jnp.concatenate([jnp.exp(s[:, c:c + LANES] - m_new) for c in range(0, TK, LANES)], axis=1)
        acc_ref[slot] = (jnp.exp(m - m_new) * acc_ref[slot]
                         + _dot(p.astype(BF16), vs_ref[pl.ds(k0, TK), :]))
        m_ref[slot] = m_new

    def pair(jj, c):
        a = 2 * jj
        scores(a + 1, sb_ref)
        accumulate(a, sa_ref)
        scores(a + 2, sa_ref)
        accumulate(a + 1, sb_ref)
        return c

    lax.fori_loop(0, (n_tasks + 1) // 2, pair, 0)

    gates = _dot(_hi_lo(g_ref[...]), ge_ref[...])
    gw = GQA * HEAD_DIM
    o_ref[...] = (gates[:, 0:gw] * _head_layout(acc_c) + gates[:, gw:2 * gw] * _head_layout(acc_ref[0])
                  + gates[:, 2 * gw:3 * gw] * _head_layout(acc_w)).astype(o_ref.dtype)


def _nsa(q, qfeat, kc_aug, vc, ks_aug, vs, kw_aug, vw, ov, cmask, tile_masks, gate_expand, gates, B, S):
    nq = S // TQ
    gw = GQA * HEAD_DIM
    ncmp = kc_aug.shape[2]
    n_sel = ov.shape[1]
    sp = kw_aug.shape[2]
    bk = lambda b, k, i: (b, k, 0, 0)
    kb = lambda b, k, i: (k, b, 0, 0)
    return pl.pallas_call(
        _nsa_kernel,
        grid=(B, N_KV_HEADS, nq),
        in_specs=[
            pl.BlockSpec((TQ, gw), lambda b, k, i: (b * nq + i, k)),
            pl.BlockSpec((None, GQA * TQ, HEAD_DIM), lambda b, k, i: (k, 0, 0)),
            pl.BlockSpec((None, None, ncmp, LANES), bk),
            pl.BlockSpec((None, None, ncmp, VW), bk),
            pl.BlockSpec((None, None, S, 2 * LANES), kb),
            pl.BlockSpec((None, None, S, LANES), kb),
            pl.BlockSpec((None, None, sp, LANES), kb),
            pl.BlockSpec((None, None, sp, VW), kb),
            pl.BlockSpec((ncmp, n_sel), lambda b, k, i: (0, 0)),
            pl.BlockSpec((GQA * TQ, ncmp), lambda b, k, i: (0, 0)),
            pl.BlockSpec(tile_masks.shape, lambda b, k, i: (0, 0, 0)),
            pl.BlockSpec((None,) + gate_expand.shape[1:], lambda b, k, i: (k, 0, 0)),
            pl.BlockSpec((TQ, _G_W), lambda b, k, i: (b * nq + i, 0)),
        ],
        out_specs=pl.BlockSpec((TQ, gw), lambda b, k, i: (b * nq + i, k)),
        out_shape=jax.ShapeDtypeStruct((B * S, N_KV_HEADS * gw), BF16),
        scratch_shapes=[
            pltpu.VMEM((GQA * TQ, 2 * LANES), BF16),
            pltpu.SMEM((S // TK + 2,), jnp.int32),
            pltpu.VMEM((2, GQA * TQ, LANES), F32),
            pltpu.VMEM((2, GQA * TQ, LANES), F32),
            pltpu.VMEM((GQA * TQ, TK), F32),
            pltpu.VMEM((GQA * TQ, TK), F32),
        ],
        compiler_params=pltpu.CompilerParams(
            dimension_semantics=("arbitrary",) * 3, vmem_limit_bytes=VMEM_LIMIT),
        name="nsa",
    )(q, qfeat, kc_aug, vc, ks_aug, vs, kw_aug, vw, ov, cmask, tile_masks, gate_expand, gates)


_FF_CHUNK = 768


def _outproj_kernel(mc_ref, nsa_ref, x_ref, gn_ref, wo_ref, h_ref):
    mixed_n = _rms(nsa_ref[...].astype(F32), gn_ref[...]).astype(BF16)
    h_ref[...] = (x_ref[...] + _dot(mc_ref[...], wo_ref[0:CONV_CH, :])
                  + _dot(mixed_n, wo_ref[CONV_CH:2 * CONV_CH, :]))


def _outproj(mixed_c, nsa_out, x2, gn, wo, tm=512):
    rows = x2.shape[0]
    row = lambda i: (i, 0)
    const = lambda i: (0, 0)
    return pl.pallas_call(
        _outproj_kernel,
        grid=(rows // tm,),
        in_specs=[
            pl.BlockSpec((tm, CONV_CH), row),
            pl.BlockSpec((tm, N_HEADS * HEAD_DIM), row),
            pl.BlockSpec((tm, D_MODEL), row),
            pl.BlockSpec((1, N_HEADS * HEAD_DIM), const),
            pl.BlockSpec((2 * CONV_CH, D_MODEL), const, pipeline_mode=pl.Buffered(1)),
        ],
        out_specs=pl.BlockSpec((tm, D_MODEL), row),
        out_shape=jax.ShapeDtypeStruct((rows, D_MODEL), F32),
        compiler_params=pltpu.CompilerParams(
            dimension_semantics=("arbitrary",), vmem_limit_bytes=VMEM_LIMIT),
        name="outproj",
    )(mixed_c, nsa_out, x2, gn, wo)


def _outffn_kernel(h_ref, g2_ref, wg_ref, wu_ref, wd_ref, gf_ref, o_ref):
    h = h_ref[...]
    u = _rms(h, g2_ref[...]).astype(BF16)
    acc = h
    for c in range(0, D_FF, _FF_CHUNK):
        sl = slice(c, min(c + _FF_CHUNK, D_FF))
        a = jax.nn.silu(_dot(u, wg_ref[:, sl])) * _dot(u, wu_ref[:, sl])
        acc = acc + _dot(a.astype(BF16), wd_ref[sl, :])
    o_ref[...] = _rms(acc, gf_ref[...])


def _outffn(h, g2, wg, wu, wd, gf, tm=512):
    rows = h.shape[0]
    row = lambda i: (i, 0)
    const = lambda i: (0, 0)
    resident = lambda shape: pl.BlockSpec(shape, const, pipeline_mode=pl.Buffered(1))
    return pl.pallas_call(
        _outffn_kernel,
        grid=(rows // tm,),
        in_specs=[
            pl.BlockSpec((tm, D_MODEL), row),
            pl.BlockSpec((1, D_MODEL), const),
            resident((D_MODEL, D_FF)),
            resident((D_MODEL, D_FF)),
            resident((D_FF, D_MODEL)),
            pl.BlockSpec((1, D_MODEL), const),
        ],
        out_specs=pl.BlockSpec((tm, D_MODEL), row),
        out_shape=jax.ShapeDtypeStruct((rows, D_MODEL), F32),
        compiler_params=pltpu.CompilerParams(
            dimension_semantics=("arbitrary",), vmem_limit_bytes=VMEM_LIMIT),
        name="outffn",
    )(h, g2, wg, wu, wd, gf)


def _key_features(pos, n_sel):
    f = np.zeros((pos.shape[0], LANES + n_sel), np.float32)
    f[:, HEAD_DIM] = pos // SEL_BLOCK
    f[:, HEAD_DIM + 1] = pos % SEL_BLOCK
    if n_sel:
        f[np.arange(pos.shape[0]), LANES + pos // SEL_BLOCK] = 1.0
    return f


def _query_features():
    f = np.zeros((N_KV_HEADS, GQA * TQ, HEAD_DIM), np.float32)
    for k in range(N_KV_HEADS):
        for g in range(GQA):
            slope = 2.0 ** (-8.0 * (k * GQA + g + 1) / N_HEADS)
            f[k, g * TQ:(g + 1) * TQ, 0] = slope * SEL_BLOCK
            f[k, g * TQ:(g + 1) * TQ, 1] = slope
            f[k, g * TQ:(g + 1) * TQ, 2] = MASK_BIAS
    return f


def _causal_masks(ncmp):
    r = np.arange(GQA * TQ)[:, None] % TQ
    cmask = (np.arange(ncmp)[None, :] - ((r + 1) // CMP_STRIDE)).astype(np.int32)
    c = np.arange(TK)[None, :]
    tile_masks = [np.where(c - r <= d * TQ, 0.0, NEG).astype(np.float32) for d in range(max(TK // TQ, 1))]
    return cmask, np.stack(tile_masks)


def _gate_expand():
    gw = GQA * HEAD_DIM
    e = np.zeros((N_KV_HEADS, 2 * _G_W, N_BRANCH * gw), np.float32)
    for k in range(N_KV_HEADS):
        for g in range(GQA):
            for br in range(N_BRANCH):
                for half in range(2):
                    e[k, half * _G_W + (k * GQA + g) * N_BRANCH + br,
                      br * gw + g * HEAD_DIM:br * gw + (g + 1) * HEAD_DIM] = 1.0
    return e


def _overlap(ncmp_pad, n_sel):
    nc = ncmp_pad - 1
    c_start = CMP_STRIDE * np.arange(nc)
    c_end = c_start + CMP_BLOCK - 1
    s_start = SEL_BLOCK * np.arange(n_sel)
    ov = np.minimum(c_end[:, None] + 1, s_start[None, :] + SEL_BLOCK) - np.maximum(c_start[:, None], s_start[None, :])
    out = np.zeros((ncmp_pad, n_sel), np.float32)
    out[:nc] = np.clip(ov, 0, None).astype(np.float32) / CMP_BLOCK
    return out


def _layer(x2, B, S, norm1_g, w_in, conv_w, k_cmp_pos, k_cmp_w1, k_cmp_w2, v_cmp_pos, v_cmp_w1, v_cmp_w2,
           gn_conv_g):
    n_sel = S // SEL_BLOCK
    nch = S // CMP_STRIDE
    hd = HEAD_DIM

    kfeat = jnp.asarray(_key_features(np.arange(S), n_sel), BF16)
    mixed_c, q, gates, kvc, ks_aug, vs, kw_aug, vw = _inproj(
        x2, norm1_g.reshape(1, D_MODEL), w_in.astype(BF16), kfeat, conv_w, gn_conv_g.reshape(1, CONV_CH))

    w1 = jnp.stack([k_cmp_w1, v_cmp_w1]).astype(BF16)
    w2 = jnp.stack([k_cmp_w2, v_cmp_w2]).astype(BF16)
    pos = jnp.stack([k_cmp_pos, v_cmp_pos]).reshape(2, 1, CMP_BLOCK * hd)
    pos = jnp.broadcast_to(pos, (2, SUBLANES, CMP_BLOCK * hd)).astype(BF16)
    c_end = CMP_STRIDE * np.arange(nch) + CMP_BLOCK - 1
    cfeat = jnp.asarray(_key_features(c_end, 0), BF16)
    kc_aug, vc = _compress(kvc, pos, w1, w2, cfeat, B, S)

    pad_rows = np.zeros((WINDOW, LANES), np.float32)
    pad_rows[:, hd + 2] = 1.0
    pad_k = jnp.broadcast_to(jnp.asarray(pad_rows, BF16), (N_KV_HEADS, B, WINDOW, LANES))
    kw_aug = jnp.concatenate([pad_k, kw_aug.reshape(N_KV_HEADS, B, S, LANES)], axis=2)
    vw = jnp.pad(vw.reshape(N_KV_HEADS, B, S, VW), ((0, 0), (0, 0), (WINDOW, 0), (0, 0)))
    cmask, tile_masks = _causal_masks(nch)

    nsa_out = _nsa(q, jnp.asarray(_query_features(), BF16), kc_aug, vc,
                   ks_aug.reshape(N_KV_HEADS, B, S, 2 * LANES), vs.reshape(N_KV_HEADS, B, S, LANES),
                   kw_aug, vw, jnp.asarray(_overlap(nch, n_sel), BF16), jnp.asarray(cmask),
                   jnp.asarray(tile_masks), jnp.asarray(_gate_expand(), BF16), gates, B, S)
    return mixed_c, nsa_out


def kernel(x, norm1_g, w_in, conv_w, k_cmp_pos, k_cmp_w1, k_cmp_w2, v_cmp_pos, v_cmp_w1, v_cmp_w2,
           gn_conv_g, gn_nsa_g, w_out, norm2_g, w_gate, w_up, w_down, norm_f_g):
    B, S, _ = x.shape
    depth = norm1_g.shape[0]
    assert depth == 1, "the final norm is fused into the (single) layer's FFN kernel"
    x2 = x.reshape(B * S, D_MODEL)
    mixed_c, nsa_out = _layer(x2, B, S, norm1_g[0], w_in[0], conv_w[0], k_cmp_pos[0], k_cmp_w1[0],
                              k_cmp_w2[0], v_cmp_pos[0], v_cmp_w1[0], v_cmp_w2[0], gn_conv_g[0])
    h = _outproj(mixed_c, nsa_out, x2, gn_nsa_g[0].reshape(1, N_HEADS * HEAD_DIM), w_out[0].astype(BF16))
    out = _outffn(h, norm2_g[0].reshape(1, D_MODEL), w_gate[0].astype(BF16), w_up[0].astype(BF16),
                  w_down[0].astype(BF16), norm_f_g.reshape(1, D_MODEL))
    return out.reshape(B, S, D_MODEL)
```

```python
import functools

import numpy as np
import jax
import jax.numpy as jnp
from jax import lax
from jax.experimental import pallas as pl
from jax.experimental.pallas import tpu as pltpu

D_MODEL = 1024
CONV_CH = 512
CONV_K = 3
N_HEADS = 8
HEAD_DIM = 64
N_KV_HEADS = 2
GQA = 4
N_BRANCH = 3
CMP_BLOCK = 32
CMP_STRIDE = 16
CMP_HIDDEN = 256
SEL_BLOCK = 64
SEL_TOP = 16
WINDOW = 512
D_FF = 2816
NORM_EPS = 1e-6

F32 = jnp.float32
BF16 = jnp.bfloat16

LANES = 128
VW = 2 * LANES
SUBLANES = 8
VMEM_LIMIT = 56 * 1024 * 1024

TQ = 256
TK = 256
NEG = -1e30
MASK_BIAS = -float(2 ** 30)

_CONV_W = 3 * CONV_CH
_Q_OFF = _CONV_W
_KVC_OFF = _Q_OFF + N_HEADS * HEAD_DIM
_KV_OFF = _KVC_OFF + 2 * N_KV_HEADS * HEAD_DIM
_G_OFF = _KV_OFF + 4 * N_KV_HEADS * HEAD_DIM
_G_W = N_HEADS * N_BRANCH
_PROJ_W = _G_OFF + _G_W


def _rms(x, g):
    return x * lax.rsqrt(jnp.mean(x * x, axis=-1, keepdims=True) + NORM_EPS) * g


def _dot(a, b):
    return jnp.dot(a, b, preferred_element_type=F32)


def _value_rows(v2, h, wide):
    lane = lax.broadcasted_iota(jnp.int32, v2.shape, 1)
    swapped = pltpu.roll(v2, HEAD_DIM, axis=1)
    if not wide:
        return jnp.where(lane < HEAD_DIM, v2 if h == 0 else swapped, 1.0).astype(BF16)
    dup = jnp.where((lane < HEAD_DIM) == (h == 0), v2, swapped)
    return jnp.concatenate([dup.astype(BF16), jnp.ones(v2.shape, BF16)], axis=1)


def _dot_nt(a, b):
    return lax.dot_general(a, b, (((1,), (1,)), ((), ())), preferred_element_type=F32)


def _inproj_kernel(seq_tiles, x_ref, xp_ref, g_ref, w_ref, kf_ref, cw_ref, gc_ref, mixc_ref, q_ref, gate_ref,
                   kvc_ref, ks_ref, vs_ref, kw_ref, vw_ref):
    u = _rms(x_ref[...], g_ref[...]).astype(BF16)

    cbch = _dot(u, w_ref[:, 0:_CONV_W])
    cu = cbch[:, CONV_CH:2 * CONV_CH] * cbch[:, 2 * CONV_CH:3 * CONV_CH]
    up = _rms(xp_ref[...], g_ref[...]).astype(BF16)
    chp = _dot(up, w_ref[:, CONV_CH:_CONV_W])
    starts_sequence = pl.program_id(0) % seq_tiles == 0
    prev = jnp.where(starts_sequence, 0.0, chp[:, 0:CONV_CH] * chp[:, CONV_CH:2 * CONV_CH])
    r = lax.broadcasted_iota(jnp.int32, cu.shape, 0)
    cu1 = jnp.where(r == 0, prev[7:8, :], pltpu.roll(cu, 1, axis=0))
    cu2 = pltpu.roll(cu, 2, axis=0)
    cu2 = jnp.where(r == 0, prev[6:7, :], jnp.where(r == 1, prev[7:8, :], cu2))
    cw = cw_ref[...]
    y = cbch[:, 0:CONV_CH] * (cw[0:1, :] * cu2 + cw[1:2, :] * cu1 + cw[2:3, :] * cu)
    mixc_ref[...] = _rms(y, gc_ref[...]).astype(BF16)

    q_ref[...] = _dot(u, w_ref[:, _Q_OFF:_KVC_OFF]).astype(BF16)
    kvc_ref[...] = _dot(u, w_ref[:, _KVC_OFF:_KV_OFF])
    gate_ref[...] = jax.nn.sigmoid(_dot(u, w_ref[:, _G_OFF:_PROJ_W]))
    kv = _dot(u, w_ref[:, _KV_OFF:_G_OFF])
    kf = kf_ref[...]
    hd, kvw = HEAD_DIM, N_KV_HEADS * HEAD_DIM
    for h in range(N_KV_HEADS):
        ks_ref[h] = kf
        ks_ref[h, :, 0:hd] = kv[:, h * hd:(h + 1) * hd].astype(BF16)
        vs_ref[h] = _value_rows(kv[:, kvw:2 * kvw], h, False)
        kw_ref[h] = kf[:, 0:LANES]
        kw_ref[h, :, 0:hd] = kv[:, 2 * kvw + h * hd:2 * kvw + (h + 1) * hd].astype(BF16)
        vw_ref[h] = _value_rows(kv[:, 3 * kvw:4 * kvw], h, True)


def _inproj(x2, g, w, kfeat, conv_w, gc, tm=1024):
    rows = x2.shape[0]
    seq_tiles = kfeat.shape[0] // tm
    row = lambda i: (i, 0)
    head_row = lambda i: (0, i, 0)
    const = lambda i: (0, 0)
    return pl.pallas_call(
        functools.partial(_inproj_kernel, seq_tiles),
        grid=(rows // tm,),
        in_specs=[
            pl.BlockSpec((tm, D_MODEL), row),
            pl.BlockSpec((SUBLANES, D_MODEL), lambda i: (jnp.maximum(i * (tm // SUBLANES) - 1, 0), 0)),
            pl.BlockSpec((1, D_MODEL), const),
            pl.BlockSpec((D_MODEL, _PROJ_W), const, pipeline_mode=pl.Buffered(1)),
            pl.BlockSpec((tm, 2 * LANES), lambda i: (i % seq_tiles, 0)),
            pl.BlockSpec((CONV_K, CONV_CH), const),
            pl.BlockSpec((1, CONV_CH), const),
        ],
        out_specs=[
            pl.BlockSpec((tm, CONV_CH), row),
            pl.BlockSpec((tm, N_HEADS * HEAD_DIM), row),
            pl.BlockSpec((tm, _G_W), row),
            pl.BlockSpec((tm, _KV_OFF - _KVC_OFF), row),
            pl.BlockSpec((N_KV_HEADS, tm, 2 * LANES), head_row),
            pl.BlockSpec((N_KV_HEADS, tm, LANES), head_row),
            pl.BlockSpec((N_KV_HEADS, tm, LANES), head_row),
            pl.BlockSpec((N_KV_HEADS, tm, VW), head_row),
        ],
        out_shape=[
            jax.ShapeDtypeStruct((rows, CONV_CH), BF16),
            jax.ShapeDtypeStruct((rows, N_HEADS * HEAD_DIM), BF16),
            jax.ShapeDtypeStruct((rows, _G_W), F32),
            jax.ShapeDtypeStruct((rows, _KV_OFF - _KVC_OFF), F32),
            jax.ShapeDtypeStruct((N_KV_HEADS, rows, 2 * LANES), BF16),
            jax.ShapeDtypeStruct((N_KV_HEADS, rows, LANES), BF16),
            jax.ShapeDtypeStruct((N_KV_HEADS, rows, LANES), BF16),
            jax.ShapeDtypeStruct((N_KV_HEADS, rows, VW), BF16),
        ],
        compiler_params=pltpu.CompilerParams(
            dimension_semantics=("arbitrary",), vmem_limit_bytes=VMEM_LIMIT),
        name="inproj",
    )(x2, x2, g, w, kfeat, conv_w, gc)


def _compress_kernel(xk_ref, xv_ref, pos_ref, w1_ref, w2_ref, cf_ref, kc_ref, vc_ref):
    nch = xk_ref.shape[0] // CMP_STRIDE
    hd = HEAD_DIM
    first = [jnp.zeros((nch, CMP_HIDDEN), F32) for _ in range(2 * N_KV_HEADS)]
    second = [jnp.zeros((nch, CMP_HIDDEN), F32) for _ in range(2 * N_KV_HEADS)]
    for l in range(CMP_STRIDE):
        for s, x_ref in enumerate((xk_ref, xv_ref)):
            xl = x_ref[pl.ds(l, nch, stride=CMP_STRIDE), :].astype(BF16)
            for h in range(N_KV_HEADS):
                c = s * N_KV_HEADS + h
                xh = xl[:, h * hd:(h + 1) * hd]
                first[c] = first[c] + _dot(xh, w1_ref[s, l * hd:(l + 1) * hd, :])
                second[c] = second[c] + _dot(xh, w1_ref[s, (CMP_STRIDE + l) * hd:(CMP_STRIDE + l + 1) * hd, :])
    for s in range(2):
        pb = _dot(pos_ref[s], w1_ref[s])[0:1, :]
        outs = []
        for h in range(N_KV_HEADS):
            c = s * N_KV_HEADS + h
            hid = first[c] + pltpu.roll(second[c], nch - 1, axis=0) + pb
            outs.append(_dot(jax.nn.gelu(hid, approximate=True).astype(BF16), w2_ref[s]))
        for h in range(N_KV_HEADS):
            if s == 0:
                kc_ref[h] = cf_ref[...]
                kc_ref[h, :, 0:hd] = outs[h].astype(BF16)
            else:
                vc_ref[h] = _value_rows(jnp.concatenate(outs, axis=1), h, True)


def _compress(kvc, pos, w1, w2, cfeat, B, S):
    nch = S // CMP_STRIDE
    full = lambda *shape: pl.BlockSpec(shape, lambda b: (0,) * len(shape))
    return pl.pallas_call(
        _compress_kernel,
        grid=(B,),
        in_specs=[
            pl.BlockSpec((S, LANES), lambda b: (b, 0)),
            pl.BlockSpec((S, LANES), lambda b: (b, 1)),
            full(*pos.shape), full(*w1.shape), full(*w2.shape), full(*cfeat.shape),
        ],
        out_specs=[
            pl.BlockSpec((None, N_KV_HEADS, nch, LANES), lambda b: (b, 0, 0, 0)),
            pl.BlockSpec((None, N_KV_HEADS, nch, VW), lambda b: (b, 0, 0, 0)),
        ],
        out_shape=[
            jax.ShapeDtypeStruct((B, N_KV_HEADS, nch, LANES), BF16),
            jax.ShapeDtypeStruct((B, N_KV_HEADS, nch, VW), BF16),
        ],
        compiler_params=pltpu.CompilerParams(
            dimension_semantics=("arbitrary",), vmem_limit_bytes=VMEM_LIMIT),
        name="compress",
    )(kvc, kvc, pos, w1, w2, cfeat)


def _hi_lo(x):
    hi = x.astype(BF16)
    return jnp.concatenate([hi, (x - hi.astype(F32)).astype(BF16)], axis=1)


def _head_layout(acc):
    low_half = lax.broadcasted_iota(jnp.int32, (TQ, LANES), 1) < HEAD_DIM
    head = lambda x, g: x[g * TQ:(g + 1) * TQ]
    if acc.shape[1] == VW:
        x = acc[:, 0:LANES] * (1.0 / jnp.maximum(acc[:, LANES:VW], 1e-30))
        pairs = [jnp.where(low_half, head(x, 2 * j), head(x, 2 * j + 1)) for j in range(GQA // 2)]
    else:
        rot = pltpu.roll(acc, HEAD_DIM, axis=1)
        pairs = [jnp.where(low_half, head(acc, 2 * j) * (1.0 / head(rot, 2 * j)),
                           head(rot, 2 * j + 1) * (1.0 / head(acc, 2 * j + 1))) for j in range(GQA // 2)]
    return jnp.concatenate(pairs, axis=1)


def _exchange(v, i, j, descending):
    hi, lo = jnp.maximum(v[i], v[j]), jnp.minimum(v[i], v[j])
    v[i], v[j] = (hi, lo) if descending else (lo, hi)


def _bitonic_merge_desc(v):
    n = len(v)
    j = n // 2
    while j >= 1:
        for i in range(n):
            if i ^ j > i:
                _exchange(v, i, i ^ j, True)
        j //= 2


def _kth_largest(x, kth, fillers=()):
    n = x.shape[0]
    assert n == kth * SUBLANES and kth & (kth - 1) == 0
    v = [x[r * SUBLANES:(r + 1) * SUBLANES, :] for r in range(kth)]
    fillers = list(fillers)
    layers = 0
    k = 2
    while k <= kth:
        j = k // 2
        while j >= 1:
            for i in range(kth):
                if i ^ j > i:
                    _exchange(v, i, i ^ j, (i & k) == 0)
            j //= 2
            layers += 1
            if layers % 3 == 0 and fillers:
                fillers.pop(0)()
        k *= 2
    shift = SUBLANES // 2
    while shift >= 1:
        v = [jnp.maximum(v[i], pltpu.roll(v[kth - 1 - i], shift, axis=0)) for i in range(kth)]
        if shift > 1:
            _bitonic_merge_desc(v)
        if fillers:
            fillers.pop(0)()
        shift //= 2
    for filler in fillers:
        filler()
    out = v[0]
    for i in range(1, kth):
        out = jnp.minimum(out, v[i])
    return out[0:1, :]


def _nsa_kernel(q_ref, qf_ref, kc_ref, vc_ref, ks_ref, vs_ref, kw_ref, vw_ref, ov_ref, cm_ref, mb_ref,
                ge_ref, g_ref, o_ref, qa_ref, tiles_ref, m_ref, acc_ref, sa_ref, sb_ref):
    i = pl.program_id(2)
    t0 = i * TQ
    rows = GQA * TQ
    n_sel = ov_ref.shape[1]
    ncmp = kc_ref.shape[0]

    q = q_ref[...]
    for g in range(GQA):
        qg = q[:, g * HEAD_DIM:(g + 1) * HEAD_DIM].astype(F32) * (HEAD_DIM ** -0.5)
        qa_ref[g * TQ:(g + 1) * TQ, 0:HEAD_DIM] = qg.astype(BF16)
    qa_ref[:, HEAD_DIM:LANES] = qf_ref[...]
    qa1 = qa_ref[:, 0:LANES]

    diag_ok = (lax.broadcasted_iota(jnp.int32, (rows, TQ), 1)
               <= (lax.broadcasted_iota(jnp.int32, (rows, TQ), 0) & (TQ - 1)))

    wk = TQ + WINDOW
    w0 = pl.multiple_of(t0, TQ)
    win = {}

    def win_scores():
        s_w = _dot_nt(qa1, kw_ref[pl.ds(w0, wk), :])
        win["s"] = (jnp.where(diag_ok, NEG, s_w[:, 0:TQ]),
                    s_w[:, TQ:WINDOW], jnp.where(diag_ok, s_w[:, WINDOW:wk], NEG))

    def win_max():
        s_old, s_mid, s_new = win["s"]
        win["m"] = jnp.maximum(jnp.maximum(jnp.max(s_old, axis=1, keepdims=True),
                                           jnp.max(s_mid, axis=1, keepdims=True)),
                               jnp.max(s_new, axis=1, keepdims=True))

    def win_exp():
        win["e"] = jnp.concatenate([jnp.exp(s - win["m"]) for s in win["s"]], axis=1).astype(BF16)

    def win_pv():
        win["acc"] = _dot(win["e"], vw_ref[pl.ds(w0, wk), :])

    s_c = _dot_nt(qa1, kc_ref[...])
    s_c = jnp.where(cm_ref[...] <= (TQ // CMP_STRIDE) * i - 2, s_c, NEG)
    m_c = jnp.maximum(jnp.max(s_c, axis=1, keepdims=True), 0.5 * NEG)
    e_c = jnp.exp(s_c - m_c)
    acc_c = _dot(e_c.astype(BF16), vc_ref[...])
    inv_c = 1.0 / jnp.maximum(acc_c[:, LANES:VW], 1e-30)
    p_c = e_c * jnp.concatenate([inv_c] * (ncmp // LANES), axis=1)

    p_sum = p_c[0:TQ] + p_c[TQ:2 * TQ] + p_c[2 * TQ:3 * TQ] + p_c[3 * TQ:4 * TQ]
    p_hi = p_sum.astype(BF16)
    p_lo = (p_sum - p_hi.astype(F32)).astype(BF16)
    imp = _dot(p_hi, ov_ref[...]) + _dot(p_lo, ov_ref[...])
    imp_t = imp.T
    jb = lax.broadcasted_iota(jnp.int32, (n_sel, TQ), 0)
    tq = t0 + lax.broadcasted_iota(jnp.int32, (n_sel, TQ), 1)
    imp_t = jnp.where(SEL_BLOCK * jb > tq, -jnp.inf, imp_t)
    forced = (jb == 0) | (jb == (tq >> 6))
    imp_t = jnp.where(forced, jnp.inf, imp_t)

    thr = _kth_largest(imp_t, SEL_TOP, (win_scores, win_max, win_exp, win_pv))
    acc_w = win["acc"]
    above = imp_t > thr
    tied = imp_t == thr
    n_above = jnp.sum(jnp.where(above, 1.0, 0.0), axis=0, keepdims=True)
    lower = (lax.broadcasted_iota(jnp.int32, (n_sel, n_sel), 1)
             < lax.broadcasted_iota(jnp.int32, (n_sel, n_sel), 0))
    tied_before = _dot(jnp.where(lower, 1.0, 0.0).astype(BF16), jnp.where(tied, 1.0, 0.0).astype(BF16))
    take_tie = jnp.where(tied_before + n_above < float(SEL_TOP), 1.0, 0.0)
    sel_t = jnp.where(above, 1.0, jnp.where(tied, take_tie, 0.0))
    sel_q = sel_t.T
    bias = ((1.0 - sel_q) * MASK_BIAS).astype(BF16)
    for g in range(GQA):
        qa_ref[g * TQ:(g + 1) * TQ, LANES:2 * LANES] = bias

    n_past = t0 // TK
    first_mask = (t0 - n_past * TK) // TQ
    tiles_ref[0] = n_past

    def scores(j, s_ref, mask=None):
        k0 = pl.multiple_of(tiles_ref[j] * TK, TK)
        s = _dot_nt(qa_ref[...], ks_ref[pl.ds(k0, TK), :])
        s_ref[...] = s if mask is None else s + mask

    scores(0, sa_ref, mb_ref[first_mask])

    bpt = TK // SEL_BLOCK
    n_tiles = n_sel // bpt
    blk_cnt = _dot(jnp.ones((SUBLANES, TQ), BF16), sel_q.astype(BF16))
    in_tile = (lax.broadcasted_iota(jnp.int32, (n_sel, LANES), 0) // bpt
               == lax.broadcasted_iota(jnp.int32, (n_sel, LANES), 1))
    tile_cnt = _dot(jnp.where(blk_cnt > 0.5, 1.0, 0.0).astype(BF16),
                    jnp.where(in_tile, 1.0, 0.0).astype(BF16))
    lane = lax.broadcasted_iota(jnp.int32, (1, LANES), 1)
    bit = jnp.where(tile_cnt[0:1, :] > 0.5, jnp.left_shift(1, lane & 15).astype(F32), 0.0)
    words = [jnp.sum(jnp.where((lane >= 16 * w) & (lane < 16 * (w + 1)), bit, 0.0)).astype(jnp.int32)
             for w in range((n_tiles + 15) // 16)]
    n_tasks = jnp.int32(1)
    for kt in range(n_tiles - 1):
        used = (words[kt // 16] >> (kt % 16)) & 1
        tiles_ref[n_tasks] = kt
        n_tasks = n_tasks + jnp.where(kt < n_past, used, 0)
    tiles_ref[n_tasks] = 0
    tiles_ref[n_tasks + 1] = 0

    m_ref[...] = jnp.full(m_ref.shape, NEG, F32)
    acc_ref[...] = jnp.zeros(acc_ref.shape, F32)

    gates = _dot(_hi_lo(g_ref[...]), ge_ref[...])
    gw = GQA * HEAD_DIM
    merged = gates[:, 0:gw] * _head_layout(acc_c) + gates[:, 2 * gw:3 * gw] * _head_layout(acc_w)
    gates_sel = gates[:, gw:2 * gw]

    def accumulate(j, s_ref):
        k0 = pl.multiple_of(tiles_ref[j] * TK, TK)
        slot = jnp.where(j < n_tasks, 0, 1)
        s = s_ref[...]
        m = m_ref[slot]
        m_new = jnp.maximum(m, jnp.max(s, axis=1, keepdims=True))
        p = jnp.concatenate([jnp.exp(s[:, c:c + LANES] - m_new) for c in range(0, TK, LANES)], axis=1)
        acc_ref[slot] = (jnp.exp(m - m_new) * acc_ref[slot]
                         + _dot(p.astype(BF16), vs_ref[pl.ds(k0, TK), :]))
        m_ref[slot] = m_new

    def pair(jj, c):
        a = 2 * jj
        scores(a + 1, sb_ref)
        accumulate(a, sa_ref)
        scores(a + 2, sa_ref)
        accumulate(a + 1, sb_ref)
        return c

    lax.fori_loop(0, (n_tasks + 1) // 2, pair, 0)

    o_ref[...] = (merged + gates_sel * _head_layout(acc_ref[0])).astype(o_ref.dtype)


def _nsa(q, qfeat, kc_aug, vc, ks_aug, vs, kw_aug, vw, ov, cmask, tile_masks, gate_expand, gates, B, S):
    nq = S // TQ
    gw = GQA * HEAD_DIM
    ncmp = kc_aug.shape[2]
    n_sel = ov.shape[1]
    sp = kw_aug.shape[2]
    bk = lambda b, k, i: (b, k, 0, 0)
    kb = lambda b, k, i: (k, b, 0, 0)
    return pl.pallas_call(
        _nsa_kernel,
        grid=(B, N_KV_HEADS, nq),
        in_specs=[
            pl.BlockSpec((TQ, gw), lambda b, k, i: (b * nq + i, k)),
            pl.BlockSpec((None, GQA * TQ, HEAD_DIM), lambda b, k, i: (k, 0, 0)),
            pl.BlockSpec((None, None, ncmp, LANES), bk),
            pl.BlockSpec((None, None, ncmp, VW), bk),
            pl.BlockSpec((None, None, S, 2 * LANES), kb),
            pl.BlockSpec((None, None, S, LANES), kb),
            pl.BlockSpec((None, None, sp, LANES), kb),
            pl.BlockSpec((None, None, sp, VW), kb),
            pl.BlockSpec((ncmp, n_sel), lambda b, k, i: (0, 0)),
            pl.BlockSpec((GQA * TQ, ncmp), lambda b, k, i: (0, 0)),
            pl.BlockSpec(tile_masks.shape, lambda b, k, i: (0, 0, 0)),
            pl.BlockSpec((None,) + gate_expand.shape[1:], lambda b, k, i: (k, 0, 0)),
            pl.BlockSpec((TQ, _G_W), lambda b, k, i: (b * nq + i, 0)),
        ],
        out_specs=pl.BlockSpec((TQ, gw), lambda b, k, i: (b * nq + i, k)),
        out_shape=jax.ShapeDtypeStruct((B * S, N_KV_HEADS * gw), BF16),
        scratch_shapes=[
            pltpu.VMEM((GQA * TQ, 2 * LANES), BF16),
            pltpu.SMEM((S // TK + 2,), jnp.int32),
            pltpu.VMEM((2, GQA * TQ, LANES), F32),
            pltpu.VMEM((2, GQA * TQ, LANES), F32),
            pltpu.VMEM((GQA * TQ, TK), F32),
            pltpu.VMEM((GQA * TQ, TK), F32),
        ],
        compiler_params=pltpu.CompilerParams(
            dimension_semantics=("arbitrary",) * 3, vmem_limit_bytes=VMEM_LIMIT),
        name="nsa",
    )(q, qfeat, kc_aug, vc, ks_aug, vs, kw_aug, vw, ov, cmask, tile_masks, gate_expand, gates)


_FF_CHUNK = 768


def _outproj_kernel(mc_ref, nsa_ref, x_ref, gn_ref, wo_ref, h_ref):
    mixed_n = _rms(nsa_ref[...].astype(F32), gn_ref[...]).astype(BF16)
    h_ref[...] = (x_ref[...] + _dot(mc_ref[...], wo_ref[0:CONV_CH, :])
                  + _dot(mixed_n, wo_ref[CONV_CH:2 * CONV_CH, :]))


def _outproj(mixed_c, nsa_out, x2, gn, wo, tm=512):
    rows = x2.shape[0]
    row = lambda i: (i, 0)
    const = lambda i: (0, 0)
    return pl.pallas_call(
        _outproj_kernel,
        grid=(rows // tm,),
        in_specs=[
            pl.BlockSpec((tm, CONV_CH), row),
            pl.BlockSpec((tm, N_HEADS * HEAD_DIM), row),
            pl.BlockSpec((tm, D_MODEL), row),
            pl.BlockSpec((1, N_HEADS * HEAD_DIM), const),
            pl.BlockSpec((2 * CONV_CH, D_MODEL), const, pipeline_mode=pl.Buffered(1)),
        ],
        out_specs=pl.BlockSpec((tm, D_MODEL), row),
        out_shape=jax.ShapeDtypeStruct((rows, D_MODEL), F32),
        compiler_params=pltpu.CompilerParams(
            dimension_semantics=("arbitrary",), vmem_limit_bytes=VMEM_LIMIT),
        name="outproj",
    )(mixed_c, nsa_out, x2, gn, wo)


def _outffn_kernel(h_ref, g2_ref, wg_ref, wu_ref, wd_ref, gf_ref, o_ref):
    h = h_ref[...]
    u = _rms(h, g2_ref[...]).astype(BF16)
    acc = h
    for c in range(0, D_FF, _FF_CHUNK):
        sl = slice(c, min(c + _FF_CHUNK, D_FF))
        a = jax.nn.silu(_dot(u, wg_ref[:, sl])) * _dot(u, wu_ref[:, sl])
        acc = acc + _dot(a.astype(BF16), wd_ref[sl, :])
    o_ref[...] = _rms(acc, gf_ref[...])


def _outffn(h, g2, wg, wu, wd, gf, tm=512):
    rows = h.shape[0]
    row = lambda i: (i, 0)
    const = lambda i: (0, 0)
    resident = lambda shape: pl.BlockSpec(shape, const, pipeline_mode=pl.Buffered(1))
    return pl.pallas_call(
        _outffn_kernel,
        grid=(rows // tm,),
        in_specs=[
            pl.BlockSpec((tm, D_MODEL), row),
            pl.BlockSpec((1, D_MODEL), const),
            resident((D_MODEL, D_FF)),
            resident((D_MODEL, D_FF)),
            resident((D_FF, D_MODEL)),
            pl.BlockSpec((1, D_MODEL), const),
        ],
        out_specs=pl.BlockSpec((tm, D_MODEL), row),
        out_shape=jax.ShapeDtypeStruct((rows, D_MODEL), F32),
        compiler_params=pltpu.CompilerParams(
            dimension_semantics=("arbitrary",), vmem_limit_bytes=VMEM_LIMIT),
        name="outffn",
    )(h, g2, wg, wu, wd, gf)


def _key_features(pos, n_sel):
    f = np.zeros((pos.shape[0], LANES + n_sel), np.float32)
    f[:, HEAD_DIM] = pos // SEL_BLOCK
    f[:, HEAD_DIM + 1] = pos % SEL_BLOCK
    if n_sel:
        f[np.arange(pos.shape[0]), LANES + pos // SEL_BLOCK] = 1.0
    return f


def _query_features():
    f = np.zeros((N_KV_HEADS, GQA * TQ, HEAD_DIM), np.float32)
    for k in range(N_KV_HEADS):
        for g in range(GQA):
            slope = 2.0 ** (-8.0 * (k * GQA + g + 1) / N_HEADS)
            f[k, g * TQ:(g + 1) * TQ, 0] = slope * SEL_BLOCK
            f[k, g * TQ:(g + 1) * TQ, 1] = slope
            f[k, g * TQ:(g + 1) * TQ, 2] = MASK_BIAS
    return f


def _causal_masks(ncmp):
    r = np.arange(GQA * TQ)[:, None] % TQ
    cmask = (np.arange(ncmp)[None, :] - ((r + 1) // CMP_STRIDE)).astype(np.int32)
    c = np.arange(TK)[None, :]
    tile_masks = [np.where(c - r <= d * TQ, 0.0, NEG).astype(np.float32) for d in range(max(TK // TQ, 1))]
    return cmask, np.stack(tile_masks)


def _gate_expand():
    gw = GQA * HEAD_DIM
    e = np.zeros((N_KV_HEADS, 2 * _G_W, N_BRANCH * gw), np.float32)
    for k in range(N_KV_HEADS):
        for g in range(GQA):
            for br in range(N_BRANCH):
                for half in range(2):
                    e[k, half * _G_W + (k * GQA + g) * N_BRANCH + br,
                      br * gw + g * HEAD_DIM:br * gw + (g + 1) * HEAD_DIM] = 1.0
    return e


def _overlap(ncmp_pad, n_sel):
    nc = ncmp_pad - 1
    c_start = CMP_STRIDE * np.arange(nc)
    c_end = c_start + CMP_BLOCK - 1
    s_start = SEL_BLOCK * np.arange(n_sel)
    ov = np.minimum(c_end[:, None] + 1, s_start[None, :] + SEL_BLOCK) - np.maximum(c_start[:, None], s_start[None, :])
    out = np.zeros((ncmp_pad, n_sel), np.float32)
    out[:nc] = np.clip(ov, 0, None).astype(np.float32) / CMP_BLOCK
    return out


def _layer(x2, B, S, norm1_g, w_in, conv_w, k_cmp_pos, k_cmp_w1, k_cmp_w2, v_cmp_pos, v_cmp_w1, v_cmp_w2,
           gn_conv_g):
    n_sel = S // SEL_BLOCK
    nch = S // CMP_STRIDE
    hd = HEAD_DIM

    kfeat = jnp.asarray(_key_features(np.arange(S), n_sel), BF16)
    mixed_c, q, gates, kvc, ks_aug, vs, kw_aug, vw = _inproj(
        x2, norm1_g.reshape(1, D_MODEL), w_in.astype(BF16), kfeat, conv_w, gn_conv_g.reshape(1, CONV_CH))

    w1 = jnp.stack([k_cmp_w1, v_cmp_w1]).astype(BF16)
    w2 = jnp.stack([k_cmp_w2, v_cmp_w2]).astype(BF16)
    pos = jnp.stack([k_cmp_pos, v_cmp_pos]).reshape(2, 1, CMP_BLOCK * hd)
    pos = jnp.broadcast_to(pos, (2, SUBLANES, CMP_BLOCK * hd)).astype(BF16)
    c_end = CMP_STRIDE * np.arange(nch) + CMP_BLOCK - 1
    cfeat = jnp.asarray(_key_features(c_end, 0), BF16)
    kc_aug, vc = _compress(kvc, pos, w1, w2, cfeat, B, S)

    pad_rows = np.zeros((WINDOW, LANES), np.float32)
    pad_rows[:, hd + 2] = 1.0
    pad_k = jnp.broadcast_to(jnp.asarray(pad_rows, BF16), (N_KV_HEADS, B, WINDOW, LANES))
    kw_aug = jnp.concatenate([pad_k, kw_aug.reshape(N_KV_HEADS, B, S, LANES)], axis=2)
    vw = jnp.pad(vw.reshape(N_KV_HEADS, B, S, VW), ((0, 0), (0, 0), (WINDOW, 0), (0, 0)))
    cmask, tile_masks = _causal_masks(nch)

    nsa_out = _nsa(q, jnp.asarray(_query_features(), BF16), kc_aug, vc,
                   ks_aug.reshape(N_KV_HEADS, B, S, 2 * LANES), vs.reshape(N_KV_HEADS, B, S, LANES),
                   kw_aug, vw, jnp.asarray(_overlap(nch, n_sel), BF16), jnp.asarray(cmask),
                   jnp.asarray(tile_masks), jnp.asarray(_gate_expand(), BF16), gates, B, S)
    return mixed_c, nsa_out


def kernel(x, norm1_g, w_in, conv_w, k_cmp_pos, k_cmp_w1, k_cmp_w2, v_cmp_pos, v_cmp_w1, v_cmp_w2,
           gn_conv_g, gn_nsa_g, w_out, norm2_g, w_gate, w_up, w_down, norm_f_g):
    B, S, _ = x.shape
    depth = norm1_g.shape[0]
    assert depth == 1, "the final norm is fused into the (single) layer's FFN kernel"
    x2 = x.reshape(B * S, D_MODEL)
    mixed_c, nsa_out = _layer(x2, B, S, norm1_g[0], w_in[0], conv_w[0], k_cmp_pos[0], k_cmp_w1[0],
                              k_cmp_w2[0], v_cmp_pos[0], v_cmp_w1[0], v_cmp_w2[0], gn_conv_g[0])
    h = _outproj(mixed_c, nsa_out, x2, gn_nsa_g[0].reshape(1, N_HEADS * HEAD_DIM), w_out[0].astype(BF16))
    out = _outffn(h, norm2_g[0].reshape(1, D_MODEL), w_gate[0].astype(BF16), w_up[0].astype(BF16),
                  w_down[0].astype(BF16), norm_f_g.reshape(1, D_MODEL))
    return out.reshape(B, S, D_MODEL)
```

```python
import functools

import numpy as np
import jax
import jax.numpy as jnp
from jax import lax
from jax.experimental import pallas as pl
from jax.experimental.pallas import tpu as pltpu

D_MODEL = 1024
CONV_CH = 512
CONV_K = 3
N_HEADS = 8
HEAD_DIM = 64
N_KV_HEADS = 2
GQA = 4
N_BRANCH = 3
CMP_BLOCK = 32
CMP_STRIDE = 16
CMP_HIDDEN = 256
SEL_BLOCK = 64
SEL_TOP = 16
WINDOW = 512
D_FF = 2816
NORM_EPS = 1e-6

F32 = jnp.float32
BF16 = jnp.bfloat16

LANES = 128
VW = 2 * LANES
SUBLANES = 8
VMEM_LIMIT = 56 * 1024 * 1024

TQ = 256
TK = 256
NEG = -1e30
MASK_BIAS = -float(2 ** 30)

_CONV_W = 3 * CONV_CH
_Q_OFF = _CONV_W
_KVC_OFF = _Q_OFF + N_HEADS * HEAD_DIM
_KV_OFF = _KVC_OFF + 2 * N_KV_HEADS * HEAD_DIM
_G_OFF = _KV_OFF + 4 * N_KV_HEADS * HEAD_DIM
_G_W = N_HEADS * N_BRANCH
_PROJ_W = _G_OFF + _G_W


def _rms(x, g):
    return x * lax.rsqrt(jnp.mean(x * x, axis=-1, keepdims=True) + NORM_EPS) * g


def _dot(a, b):
    return jnp.dot(a, b, preferred_element_type=F32)


def _value_rows(v2, h, wide):
    lane = lax.broadcasted_iota(jnp.int32, v2.shape, 1)
    swapped = pltpu.roll(v2, HEAD_DIM, axis=1)
    if not wide:
        return jnp.where(lane < HEAD_DIM, v2 if h == 0 else swapped, 1.0).astype(BF16)
    dup = jnp.where((lane < HEAD_DIM) == (h == 0), v2, swapped)
    return jnp.concatenate([dup.astype(BF16), jnp.ones(v2.shape, BF16)], axis=1)


def _dot_nt(a, b):
    return lax.dot_general(a, b, (((1,), (1,)), ((), ())), preferred_element_type=F32)


def _inproj_kernel(seq_tiles, x_ref, xp_ref, g_ref, w_ref, kf_ref, cw_ref, gc_ref, mixc_ref, q_ref, gate_ref,
                   kvc_ref, ks_ref, vs_ref, kw_ref, vw_ref):
    u = _rms(x_ref[...], g_ref[...]).astype(BF16)

    cbch = _dot(u, w_ref[:, 0:_CONV_W])
    cu = cbch[:, CONV_CH:2 * CONV_CH] * cbch[:, 2 * CONV_CH:3 * CONV_CH]
    up = _rms(xp_ref[...], g_ref[...]).astype(BF16)
    chp = _dot(up, w_ref[:, CONV_CH:_CONV_W])
    starts_sequence = pl.program_id(0) % seq_tiles == 0
    prev = jnp.where(starts_sequence, 0.0, chp[:, 0:CONV_CH] * chp[:, CONV_CH:2 * CONV_CH])
    r = lax.broadcasted_iota(jnp.int32, cu.shape, 0)
    cu1 = jnp.where(r == 0, prev[7:8, :], pltpu.roll(cu, 1, axis=0))
    cu2 = pltpu.roll(cu, 2, axis=0)
    cu2 = jnp.where(r == 0, prev[6:7, :], jnp.where(r == 1, prev[7:8, :], cu2))
    cw = cw_ref[...]
    y = cbch[:, 0:CONV_CH] * (cw[0:1, :] * cu2 + cw[1:2, :] * cu1 + cw[2:3, :] * cu)
    mixc_ref[...] = _rms(y, gc_ref[...]).astype(BF16)

    q_ref[...] = _dot(u, w_ref[:, _Q_OFF:_KVC_OFF]).astype(BF16)
    kvc_ref[...] = _dot(u, w_ref[:, _KVC_OFF:_KV_OFF])
    gate_ref[...] = jax.nn.sigmoid(_dot(u, w_ref[:, _G_OFF:_PROJ_W]))
    kv = _dot(u, w_ref[:, _KV_OFF:_G_OFF])
    kf = kf_ref[...]
    hd, kvw = HEAD_DIM, N_KV_HEADS * HEAD_DIM
    for h in range(N_KV_HEADS):
        ks_ref[h] = kf
        ks_ref[h, :, 0:hd] = kv[:, h * hd:(h + 1) * hd].astype(BF16)
        vs_ref[h] = _value_rows(kv[:, kvw:2 * kvw], h, False)
        kw_ref[h] = kf[:, 0:LANES]
        kw_ref[h, :, 0:hd] = kv[:, 2 * kvw + h * hd:2 * kvw + (h + 1) * hd].astype(BF16)
        vw_ref[h] = _value_rows(kv[:, 3 * kvw:4 * kvw], h, True)


def _inproj(x2, g, w, kfeat, conv_w, gc, tm=1024):
    rows = x2.shape[0]
    seq_tiles = kfeat.shape[0] // tm
    row = lambda i: (i, 0)
    head_row = lambda i: (0, i, 0)
    const = lambda i: (0, 0)
    return pl.pallas_call(
        functools.partial(_inproj_kernel, seq_tiles),
        grid=(rows // tm,),
        in_specs=[
            pl.BlockSpec((tm, D_MODEL), row),
            pl.BlockSpec((SUBLANES, D_MODEL), lambda i: (jnp.maximum(i * (tm // SUBLANES) - 1, 0), 0)),
            pl.BlockSpec((1, D_MODEL), const),
            pl.BlockSpec((D_MODEL, _PROJ_W), const, pipeline_mode=pl.Buffered(1)),
            pl.BlockSpec((tm, 2 * LANES), lambda i: (i % seq_tiles, 0)),
            pl.BlockSpec((CONV_K, CONV_CH), const),
            pl.BlockSpec((1, CONV_CH), const),
        ],
        out_specs=[
            pl.BlockSpec((tm, CONV_CH), row),
            pl.BlockSpec((tm, N_HEADS * HEAD_DIM), row),
            pl.BlockSpec((tm, _G_W), row),
            pl.BlockSpec((tm, _KV_OFF - _KVC_OFF), row),
            pl.BlockSpec((N_KV_HEADS, tm, 2 * LANES), head_row),
            pl.BlockSpec((N_KV_HEADS, tm, LANES), head_row),
            pl.BlockSpec((N_KV_HEADS, tm, LANES), head_row),
            pl.BlockSpec((N_KV_HEADS, tm, VW), head_row),
        ],
        out_shape=[
            jax.ShapeDtypeStruct((rows, CONV_CH), BF16),
            jax.ShapeDtypeStruct((rows, N_HEADS * HEAD_DIM), BF16),
            jax.ShapeDtypeStruct((rows, _G_W), F32),
            jax.ShapeDtypeStruct((rows, _KV_OFF - _KVC_OFF), F32),
            jax.ShapeDtypeStruct((N_KV_HEADS, rows, 2 * LANES), BF16),
            jax.ShapeDtypeStruct((N_KV_HEADS, rows, LANES), BF16),
            jax.ShapeDtypeStruct((N_KV_HEADS, rows, LANES), BF16),
            jax.ShapeDtypeStruct((N_KV_HEADS, rows, VW), BF16),
        ],
        compiler_params=pltpu.CompilerParams(
            dimension_semantics=("arbitrary",), vmem_limit_bytes=VMEM_LIMIT),
        name="inproj",
    )(x2, x2, g, w, kfeat, conv_w, gc)


def _compress_kernel(xk_ref, xv_ref, pos_ref, w1_ref, w2_ref, cf_ref, kc_ref, vc_ref):
    nch = xk_ref.shape[0] // CMP_STRIDE
    hd = HEAD_DIM
    first = [jnp.zeros((nch, CMP_HIDDEN), F32) for _ in range(2 * N_KV_HEADS)]
    second = [jnp.zeros((nch, CMP_HIDDEN), F32) for _ in range(2 * N_KV_HEADS)]
    for l in range(CMP_STRIDE):
        for s, x_ref in enumerate((xk_ref, xv_ref)):
            xl = x_ref[pl.ds(l, nch, stride=CMP_STRIDE), :].astype(BF16)
            for h in range(N_KV_HEADS):
                c = s * N_KV_HEADS + h
                xh = xl[:, h * hd:(h + 1) * hd]
                first[c] = first[c] + _dot(xh, w1_ref[s, l * hd:(l + 1) * hd, :])
                second[c] = second[c] + _dot(xh, w1_ref[s, (CMP_STRIDE + l) * hd:(CMP_STRIDE + l + 1) * hd, :])
    for s in range(2):
        pb = _dot(pos_ref[s], w1_ref[s])[0:1, :]
        outs = []
        for h in range(N_KV_HEADS):
            c = s * N_KV_HEADS + h
            hid = first[c] + pltpu.roll(second[c], nch - 1, axis=0) + pb
            outs.append(_dot(jax.nn.gelu(hid, approximate=True).astype(BF16), w2_ref[s]))
        for h in range(N_KV_HEADS):
            if s == 0:
                kc_ref[h] = cf_ref[...]
                kc_ref[h, :, 0:hd] = outs[h].astype(BF16)
            else:
                vc_ref[h] = _value_rows(jnp.concatenate(outs, axis=1), h, True)


def _compress(kvc, pos, w1, w2, cfeat, B, S):
    nch = S // CMP_STRIDE
    full = lambda *shape: pl.BlockSpec(shape, lambda b: (0,) * len(shape))
    return pl.pallas_call(
        _compress_kernel,
        grid=(B,),
        in_specs=[
            pl.BlockSpec((S, LANES), lambda b: (b, 0)),
            pl.BlockSpec((S, LANES), lambda b: (b, 1)),
            full(*pos.shape), full(*w1.shape), full(*w2.shape), full(*cfeat.shape),
        ],
        out_specs=[
            pl.BlockSpec((None, N_KV_HEADS, nch, LANES), lambda b: (b, 0, 0, 0)),
            pl.BlockSpec((None, N_KV_HEADS, nch, VW), lambda b: (b, 0, 0, 0)),
        ],
        out_shape=[
            jax.ShapeDtypeStruct((B, N_KV_HEADS, nch, LANES), BF16),
            jax.ShapeDtypeStruct((B, N_KV_HEADS, nch, VW), BF16),
        ],
        compiler_params=pltpu.CompilerParams(
            dimension_semantics=("arbitrary",), vmem_limit_bytes=VMEM_LIMIT),
        name="compress",
    )(kvc, kvc, pos, w1, w2, cfeat)


def _hi_lo(x):
    hi = x.astype(BF16)
    return jnp.concatenate([hi, (x - hi.astype(F32)).astype(BF16)], axis=1)


def _head_layout(acc):
    low_half = lax.broadcasted_iota(jnp.int32, (TQ, LANES), 1) < HEAD_DIM
    head = lambda x, g: x[g * TQ:(g + 1) * TQ]
    if acc.shape[1] == VW:
        x = acc[:, 0:LANES] * (1.0 / jnp.maximum(acc[:, LANES:VW], 1e-30))
        pairs = [jnp.where(low_half, head(x, 2 * j), head(x, 2 * j + 1)) for j in range(GQA // 2)]
    else:
        rot = pltpu.roll(acc, HEAD_DIM, axis=1)
        pairs = [jnp.where(low_half, head(acc, 2 * j) * (1.0 / head(rot, 2 * j)),
                           head(rot, 2 * j + 1) * (1.0 / head(acc, 2 * j + 1))) for j in range(GQA // 2)]
    return jnp.concatenate(pairs, axis=1)


def _exchange(v, i, j, descending):
    hi, lo = jnp.maximum(v[i], v[j]), jnp.minimum(v[i], v[j])
    v[i], v[j] = (hi, lo) if descending else (lo, hi)


def _bitonic_merge_desc(v):
    n = len(v)
    j = n // 2
    while j >= 1:
        for i in range(n):
            if i ^ j > i:
                _exchange(v, i, i ^ j, True)
        j //= 2


def _kth_largest(x, kth, fillers=()):
    n = x.shape[0]
    assert n == kth * SUBLANES and kth & (kth - 1) == 0
    v = [x[r * SUBLANES:(r + 1) * SUBLANES, :] for r in range(kth)]
    fillers = list(fillers)
    layers = 0
    k = 2
    while k <= kth:
        j = k // 2
        while j >= 1:
            for i in range(kth):
                if i ^ j > i:
                    _exchange(v, i, i ^ j, (i & k) == 0)
            j //= 2
            layers += 1
            if layers % 3 == 0 and fillers:
                fillers.pop(0)()
        k *= 2
    shift = SUBLANES // 2
    while shift >= 1:
        v = [jnp.maximum(v[i], pltpu.roll(v[kth - 1 - i], shift, axis=0)) for i in range(kth)]
        if shift > 1:
            _bitonic_merge_desc(v)
        if fillers:
            fillers.pop(0)()
        shift //= 2
    for filler in fillers:
        filler()
    out = v[0]
    for i in range(1, kth):
        out = jnp.minimum(out, v[i])
    return out[0:1, :]


def _nsa_kernel(q_ref, qf_ref, kc_ref, vc_ref, ks_ref, vs_ref, kw_ref, vw_ref, ov_ref, cm_ref, mb_ref,
                ge_ref, g_ref, o_ref, qa_ref, tiles_ref, m_ref, acc_ref, sa_ref, sb_ref):
    i = pl.program_id(2)
    t0 = i * TQ
    rows = GQA * TQ
    n_sel = ov_ref.shape[1]
    ncmp = kc_ref.shape[0]

    q = q_ref[...]
    for g in range(GQA):
        qg = q[:, g * HEAD_DIM:(g + 1) * HEAD_DIM].astype(F32) * (HEAD_DIM ** -0.5)
        qa_ref[g * TQ:(g + 1) * TQ, 0:HEAD_DIM] = qg.astype(BF16)
    qa_ref[:, HEAD_DIM:LANES] = qf_ref[...]
    qa1 = qa_ref[:, 0:LANES]

    diag_ok = (lax.broadcasted_iota(jnp.int32, (rows, TQ), 1)
               <= (lax.broadcasted_iota(jnp.int32, (rows, TQ), 0) & (TQ - 1)))

    wk = TQ + WINDOW
    w0 = pl.multiple_of(t0, TQ)
    win = {}

    def win_scores():
        s_w = _dot_nt(qa1, kw_ref[pl.ds(w0, wk), :])
        win["s"] = (jnp.where(diag_ok, NEG, s_w[:, 0:TQ]),
                    s_w[:, TQ:WINDOW], jnp.where(diag_ok, s_w[:, WINDOW:wk], NEG))

    def win_max():
        s_old, s_mid, s_new = win["s"]
        win["m"] = jnp.maximum(jnp.maximum(jnp.max(s_old, axis=1, keepdims=True),
                                           jnp.max(s_mid, axis=1, keepdims=True)),
                               jnp.max(s_new, axis=1, keepdims=True))

    def win_exp():
        win["e"] = jnp.concatenate([jnp.exp(s - win["m"]) for s in win["s"]], axis=1).astype(BF16)

    def win_pv():
        win["acc"] = _dot(win["e"], vw_ref[pl.ds(w0, wk), :])

    s_c = _dot_nt(qa1, kc_ref[...])
    s_c = jnp.where(cm_ref[...] <= (TQ // CMP_STRIDE) * i - 2, s_c, NEG)
    m_c = jnp.maximum(jnp.max(s_c, axis=1, keepdims=True), 0.5 * NEG)
    e_c = jnp.exp(s_c - m_c)
    acc_c = _dot(e_c.astype(BF16), vc_ref[...])
    inv_c = 1.0 / jnp.maximum(acc_c[:, LANES:VW], 1e-30)
    p_c = e_c * jnp.concatenate([inv_c] * (ncmp // LANES), axis=1)

    p_sum = p_c[0:TQ] + p_c[TQ:2 * TQ] + p_c[2 * TQ:3 * TQ] + p_c[3 * TQ:4 * TQ]
    p_hi = p_sum.astype(BF16)
    p_lo = (p_sum - p_hi.astype(F32)).astype(BF16)
    imp = _dot(p_hi, ov_ref[...]) + _dot(p_lo, ov_ref[...])
    imp_t = imp.T
    jb = lax.broadcasted_iota(jnp.int32, (n_sel, TQ), 0)
    tq = t0 + lax.broadcasted_iota(jnp.int32, (n_sel, TQ), 1)
    imp_t = jnp.where(SEL_BLOCK * jb > tq, -jnp.inf, imp_t)
    forced = (jb == 0) | (jb == (tq >> 6))
    imp_t = jnp.where(forced, jnp.inf, imp_t)

    thr = _kth_largest(imp_t, SEL_TOP, (win_scores, win_max, win_exp, win_pv))
    acc_w = win["acc"]
    above = imp_t > thr
    tied = imp_t == thr
    n_above = jnp.sum(jnp.where(above, 1.0, 0.0), axis=0, keepdims=True)
    lower = (lax.broadcasted_iota(jnp.int32, (n_sel, n_sel), 1)
             < lax.broadcasted_iota(jnp.int32, (n_sel, n_sel), 0))
    tied_before = _dot(jnp.where(lower, 1.0, 0.0).astype(BF16), jnp.where(tied, 1.0, 0.0).astype(BF16))
    take_tie = jnp.where(tied_before + n_above < float(SEL_TOP), 1.0, 0.0)
    sel_t = jnp.where(above, 1.0, jnp.where(tied, take_tie, 0.0))
    sel_q = sel_t.T
    bias = ((1.0 - sel_q) * MASK_BIAS).astype(BF16)
    for g in range(GQA):
        qa_ref[g * TQ:(g + 1) * TQ, LANES:2 * LANES] = bias

    n_past = t0 // TK
    first_mask = (t0 - n_past * TK) // TQ
    tiles_ref[0] = n_past

    def scores(j, s_ref, mask=None):
        k0 = pl.multiple_of(tiles_ref[j] * TK, TK)
        s = _dot_nt(qa_ref[...], ks_ref[pl.ds(k0, TK), :])
        s_ref[...] = s if mask is None else s + mask

    scores(0, sa_ref, mb_ref[first_mask])

    bpt = TK // SEL_BLOCK
    n_tiles = n_sel // bpt
    blk_cnt = _dot(jnp.ones((SUBLANES, TQ), BF16), sel_q.astype(BF16))
    in_tile = (lax.broadcasted_iota(jnp.int32, (n_sel, LANES), 0) // bpt
               == lax.broadcasted_iota(jnp.int32, (n_sel, LANES), 1))
    tile_cnt = _dot(jnp.where(blk_cnt > 0.5, 1.0, 0.0).astype(BF16),
                    jnp.where(in_tile, 1.0, 0.0).astype(BF16))
    lane = lax.broadcasted_iota(jnp.int32, (1, LANES), 1)
    bit = jnp.where(tile_cnt[0:1, :] > 0.5, jnp.left_shift(1, lane & 15).astype(F32), 0.0)
    words = [jnp.sum(jnp.where((lane >= 16 * w) & (lane < 16 * (w + 1)), bit, 0.0)).astype(jnp.int32)
             for w in range((n_tiles + 15) // 16)]
    n_tasks = jnp.int32(1)
    for kt in range(n_tiles - 1):
        used = (words[kt // 16] >> (kt % 16)) & 1
        tiles_ref[n_tasks] = kt
        n_tasks = n_tasks + jnp.where(kt < n_past, used, 0)
    tiles_ref[n_tasks] = 0
    tiles_ref[n_tasks + 1] = 0

    m_ref[...] = jnp.full(m_ref.shape, NEG, F32)
    acc_ref[...] = jnp.zeros(acc_ref.shape, F32)

    gates = _dot(_hi_lo(g_ref[...]), ge_ref[...])
    gw = GQA * HEAD_DIM
    merged = gates[:, 0:gw] * _head_layout(acc_c) + gates[:, 2 * gw:3 * gw] * _head_layout(acc_w)
    gates_sel = gates[:, gw:2 * gw]

    def accumulate(j, s_ref):
        k0 = pl.multiple_of(tiles_ref[j] * TK, TK)
        slot = jnp.where(j < n_tasks, 0, 1)
        s = s_ref[...]
        m = m_ref[slot]
        m_new = jnp.maximum(m, jnp.max(s, axis=1, keepdims=True))
        p = jnp.concatenate([jnp.exp(s[:, c:c + LANES] - m_new) for c in range(0, TK, LANES)], axis=1)
        acc_ref[slot] = (jnp.exp(m - m_new) * acc_ref[slot]
                         + _dot(p.astype(BF16), vs_ref[pl.ds(k0, TK), :]))
        m_ref[slot] = m_new

    def pair(jj, c):
        a = 2 * jj
        scores(a + 1, sb_ref)
        accumulate(a, sa_ref)
        scores(a + 2, sa_ref)
        accumulate(a + 1, sb_ref)
        return c

    lax.fori_loop(0, (n_tasks + 1) // 2, pair, 0)

    o_ref[...] = (merged + gates_sel * _head_layout(acc_ref[0])).astype(o_ref.dtype)


def _nsa(q, qfeat, kc_aug, vc, ks_aug, vs, kw_aug, vw, ov, cmask, tile_masks, gate_expand, gates, B, S):
    nq = S // TQ
    gw = GQA * HEAD_DIM
    ncmp = kc_aug.shape[2]
    n_sel = ov.shape[1]
    sp = kw_aug.shape[2]
    bk = lambda b, k, i: (b, k, 0, 0)
    kb = lambda b, k, i: (k, b, 0, 0)
    return pl.pallas_call(
        _nsa_kernel,
        grid=(B, N_KV_HEADS, nq),
        in_specs=[
            pl.BlockSpec((TQ, gw), lambda b, k, i: (b * nq + i, k)),
            pl.BlockSpec((None, GQA * TQ, HEAD_DIM), lambda b, k, i: (k, 0, 0)),
            pl.BlockSpec((None, None, ncmp, LANES), bk),
            pl.BlockSpec((None, None, ncmp, VW), bk),
            pl.BlockSpec((None, None, S, 2 * LANES), kb),
            pl.BlockSpec((None, None, S, LANES), kb),
            pl.BlockSpec((None, None, sp, LANES), kb),
            pl.BlockSpec((None, None, sp, VW), kb),
            pl.BlockSpec((ncmp, n_sel), lambda b, k, i: (0, 0)),
            pl.BlockSpec((GQA * TQ, ncmp), lambda b, k, i: (0, 0)),
            pl.BlockSpec(tile_masks.shape, lambda b, k, i: (0, 0, 0)),
            pl.BlockSpec((None,) + gate_expand.shape[1:], lambda b, k, i: (k, 0, 0)),
            pl.BlockSpec((TQ, _G_W), lambda b, k, i: (b * nq + i, 0)),
        ],
        out_specs=pl.BlockSpec((TQ, gw), lambda b, k, i: (b * nq + i, k)),
        out_shape=jax.ShapeDtypeStruct((B * S, N_KV_HEADS * gw), BF16),
        scratch_shapes=[
            pltpu.VMEM((GQA * TQ, 2 * LANES), BF16),
            pltpu.SMEM((S // TK + 2,), jnp.int32),
            pltpu.VMEM((2, GQA * TQ, LANES), F32),
            pltpu.VMEM((2, GQA * TQ, LANES), F32),
            pltpu.VMEM((GQA * TQ, TK), F32),
            pltpu.VMEM((GQA * TQ, TK), F32),
        ],
        compiler_params=pltpu.CompilerParams(
            dimension_semantics=("arbitrary",) * 3, vmem_limit_bytes=VMEM_LIMIT),
        name="nsa",
    )(q, qfeat, kc_aug, vc, ks_aug, vs, kw_aug, vw, ov, cmask, tile_masks, gate_expand, gates)


_FF_CHUNK = 768


def _outproj_kernel(mc_ref, nsa_ref, x_ref, gn_ref, wo_ref, h_ref):
    mixed_n = _rms(nsa_ref[...].astype(F32), gn_ref[...]).astype(BF16)
    h_ref[...] = (x_ref[...] + _dot(mc_ref[...], wo_ref[0:CONV_CH, :])
                  + _dot(mixed_n, wo_ref[CONV_CH:2 * CONV_CH, :]))


def _outproj(mixed_c, nsa_out, x2, gn, wo, tm=1024):
    rows = x2.shape[0]
    row = lambda i: (i, 0)
    const = lambda i: (0, 0)
    return pl.pallas_call(
        _outproj_kernel,
        grid=(rows // tm,),
        in_specs=[
            pl.BlockSpec((tm, CONV_CH), row),
            pl.BlockSpec((tm, N_HEADS * HEAD_DIM), row),
            pl.BlockSpec((tm, D_MODEL), row),
            pl.BlockSpec((1, N_HEADS * HEAD_DIM), const),
            pl.BlockSpec((2 * CONV_CH, D_MODEL), const, pipeline_mode=pl.Buffered(1)),
        ],
        out_specs=pl.BlockSpec((tm, D_MODEL), row),
        out_shape=jax.ShapeDtypeStruct((rows, D_MODEL), F32),
        compiler_params=pltpu.CompilerParams(
            dimension_semantics=("arbitrary",), vmem_limit_bytes=VMEM_LIMIT),
        name="outproj",
    )(mixed_c, nsa_out, x2, gn, wo)


def _outffn_kernel(h_ref, g2_ref, wg_ref, wu_ref, wd_ref, gf_ref, o_ref):
    h = h_ref[...]
    u = _rms(h, g2_ref[...]).astype(BF16)
    acc = h
    for c in range(0, D_FF, _FF_CHUNK):
        sl = slice(c, min(c + _FF_CHUNK, D_FF))
        a = jax.nn.silu(_dot(u, wg_ref[:, sl])) * _dot(u, wu_ref[:, sl])
        acc = acc + _dot(a.astype(BF16), wd_ref[sl, :])
    o_ref[...] = _rms(acc, gf_ref[...])


def _outffn(h, g2, wg, wu, wd, gf, tm=512):
    rows = h.shape[0]
    row = lambda i: (i, 0)
    const = lambda i: (0, 0)
    resident = lambda shape: pl.BlockSpec(shape, const, pipeline_mode=pl.Buffered(1))
    return pl.pallas_call(
        _outffn_kernel,
        grid=(rows // tm,),
        in_specs=[
            pl.BlockSpec((tm, D_MODEL), row),
            pl.BlockSpec((1, D_MODEL), const),
            resident((D_MODEL, D_FF)),
            resident((D_MODEL, D_FF)),
            resident((D_FF, D_MODEL)),
            pl.BlockSpec((1, D_MODEL), const),
        ],
        out_specs=pl.BlockSpec((tm, D_MODEL), row),
        out_shape=jax.ShapeDtypeStruct((rows, D_MODEL), F32),
        compiler_params=pltpu.CompilerParams(
            dimension_semantics=("arbitrary",), vmem_limit_bytes=VMEM_LIMIT),
        name="outffn",
    )(h, g2, wg, wu, wd, gf)


def _key_features(pos, n_sel):
    f = np.zeros((pos.shape[0], LANES + n_sel), np.float32)
    f[:, HEAD_DIM] = pos // SEL_BLOCK
    f[:, HEAD_DIM + 1] = pos % SEL_BLOCK
    if n_sel:
        f[np.arange(pos.shape[0]), LANES + pos // SEL_BLOCK] = 1.0
    return f


def _query_features():
    f = np.zeros((N_KV_HEADS, GQA * TQ, HEAD_DIM), np.float32)
    for k in range(N_KV_HEADS):
        for g in range(GQA):
            slope = 2.0 ** (-8.0 * (k * GQA + g + 1) / N_HEADS)
            f[k, g * TQ:(g + 1) * TQ, 0] = slope * SEL_BLOCK
            f[k, g * TQ:(g + 1) * TQ, 1] = slope
            f[k, g * TQ:(g + 1) * TQ, 2] = MASK_BIAS
    return f


def _causal_masks(ncmp):
    r = np.arange(GQA * TQ)[:, None] % TQ
    cmask = (np.arange(ncmp)[None, :] - ((r + 1) // CMP_STRIDE)).astype(np.int32)
    c = np.arange(TK)[None, :]
    tile_masks = [np.where(c - r <= d * TQ, 0.0, NEG).astype(np.float32) for d in range(max(TK // TQ, 1))]
    return cmask, np.stack(tile_masks)


def _gate_expand():
    gw = GQA * HEAD_DIM
    e = np.zeros((N_KV_HEADS, 2 * _G_W, N_BRANCH * gw), np.float32)
    for k in range(N_KV_HEADS):
        for g in range(GQA):
            for br in range(N_BRANCH):
                for half in range(2):
                    e[k, half * _G_W + (k * GQA + g) * N_BRANCH + br,
                      br * gw + g * HEAD_DIM:br * gw + (g + 1) * HEAD_DIM] = 1.0
    return e


def _overlap(ncmp_pad, n_sel):
    nc = ncmp_pad - 1
    c_start = CMP_STRIDE * np.arange(nc)
    c_end = c_start + CMP_BLOCK - 1
    s_start = SEL_BLOCK * np.arange(n_sel)
    ov = np.minimum(c_end[:, None] + 1, s_start[None, :] + SEL_BLOCK) - np.maximum(c_start[:, None], s_start[None, :])
    out = np.zeros((ncmp_pad, n_sel), np.float32)
    out[:nc] = np.clip(ov, 0, None).astype(np.float32) / CMP_BLOCK
    return out


def _layer(x2, B, S, norm1_g, w_in, conv_w, k_cmp_pos, k_cmp_w1, k_cmp_w2, v_cmp_pos, v_cmp_w1, v_cmp_w2,
           gn_conv_g):
    n_sel = S // SEL_BLOCK
    nch = S // CMP_STRIDE
    hd = HEAD_DIM

    kfeat = jnp.asarray(_key_features(np.arange(S), n_sel), BF16)
    mixed_c, q, gates, kvc, ks_aug, vs, kw_aug, vw = _inproj(
        x2, norm1_g.reshape(1, D_MODEL), w_in.astype(BF16), kfeat, conv_w, gn_conv_g.reshape(1, CONV_CH))

    w1 = jnp.stack([k_cmp_w1, v_cmp_w1]).astype(BF16)
    w2 = jnp.stack([k_cmp_w2, v_cmp_w2]).astype(BF16)
    pos = jnp.stack([k_cmp_pos, v_cmp_pos]).reshape(2, 1, CMP_BLOCK * hd)
    pos = jnp.broadcast_to(pos, (2, SUBLANES, CMP_BLOCK * hd)).astype(BF16)
    c_end = CMP_STRIDE * np.arange(nch) + CMP_BLOCK - 1
    cfeat = jnp.asarray(_key_features(c_end, 0), BF16)
    kc_aug, vc = _compress(kvc, pos, w1, w2, cfeat, B, S)

    pad_rows = np.zeros((WINDOW, LANES), np.float32)
    pad_rows[:, hd + 2] = 1.0
    pad_k = jnp.broadcast_to(jnp.asarray(pad_rows, BF16), (N_KV_HEADS, B, WINDOW, LANES))
    kw_aug = jnp.concatenate([pad_k, kw_aug.reshape(N_KV_HEADS, B, S, LANES)], axis=2)
    vw = jnp.pad(vw.reshape(N_KV_HEADS, B, S, VW), ((0, 0), (0, 0), (WINDOW, 0), (0, 0)))
    cmask, tile_masks = _causal_masks(nch)

    nsa_out = _nsa(q, jnp.asarray(_query_features(), BF16), kc_aug, vc,
                   ks_aug.reshape(N_KV_HEADS, B, S, 2 * LANES), vs.reshape(N_KV_HEADS, B, S, LANES),
                   kw_aug, vw, jnp.asarray(_overlap(nch, n_sel), BF16), jnp.asarray(cmask),
                   jnp.asarray(tile_masks), jnp.asarray(_gate_expand(), BF16), gates, B, S)
    return mixed_c, nsa_out


def kernel(x, norm1_g, w_in, conv_w, k_cmp_pos, k_cmp_w1, k_cmp_w2, v_cmp_pos, v_cmp_w1, v_cmp_w2,
           gn_conv_g, gn_nsa_g, w_out, norm2_g, w_gate, w_up, w_down, norm_f_g):
    B, S, _ = x.shape
    depth = norm1_g.shape[0]
    assert depth == 1, "the final norm is fused into the (single) layer's FFN kernel"
    x2 = x.reshape(B * S, D_MODEL)
    mixed_c, nsa_out = _layer(x2, B, S, norm1_g[0], w_in[0], conv_w[0], k_cmp_pos[0], k_cmp_w1[0],
                              k_cmp_w2[0], v_cmp_pos[0], v_cmp_w1[0], v_cmp_w2[0], gn_conv_g[0])
    h = _outproj(mixed_c, nsa_out, x2, gn_nsa_g[0].reshape(1, N_HEADS * HEAD_DIM), w_out[0].astype(BF16))
    out = _outffn(h, norm2_g[0].reshape(1, D_MODEL), w_gate[0].astype(BF16), w_up[0].astype(BF16),
                  w_down[0].astype(BF16), norm_f_g.reshape(1, D_MODEL))
    return out.reshape(B, S, D_MODEL)
```

```python
import functools

import numpy as np
import jax
import jax.numpy as jnp
from jax import lax
from jax.experimental import pallas as pl
from jax.experimental.pallas import tpu as pltpu

D_MODEL = 1024
CONV_CH = 512
CONV_K = 3
N_HEADS = 8
HEAD_DIM = 64
N_KV_HEADS = 2
GQA = 4
N_BRANCH = 3
CMP_BLOCK = 32
CMP_STRIDE = 16
CMP_HIDDEN = 256
SEL_BLOCK = 64
SEL_TOP = 16
WINDOW = 512
D_FF = 2816
NORM_EPS = 1e-6

F32 = jnp.float32
BF16 = jnp.bfloat16

LANES = 128
VW = 2 * LANES
SUBLANES = 8
VMEM_LIMIT = 56 * 1024 * 1024

TQ = 256
TK = 256
NEG = -1e30
MASK_BIAS = -float(2 ** 30)

_CONV_W = 3 * CONV_CH
_Q_OFF = _CONV_W
_KVC_OFF = _Q_OFF + N_HEADS * HEAD_DIM
_KV_OFF = _KVC_OFF + 2 * N_KV_HEADS * HEAD_DIM
_G_OFF = _KV_OFF + 4 * N_KV_HEADS * HEAD_DIM
_G_W = N_HEADS * N_BRANCH
_PROJ_W = _G_OFF + _G_W


def _rms(x, g):
    return x * lax.rsqrt(jnp.mean(x * x, axis=-1, keepdims=True) + NORM_EPS) * g


def _dot(a, b):
    return jnp.dot(a, b, preferred_element_type=F32)


def _value_rows(v2, h, wide):
    lane = lax.broadcasted_iota(jnp.int32, v2.shape, 1)
    swapped = pltpu.roll(v2, HEAD_DIM, axis=1)
    if not wide:
        return jnp.where(lane < HEAD_DIM, v2 if h == 0 else swapped, 1.0).astype(BF16)
    dup = jnp.where((lane < HEAD_DIM) == (h == 0), v2, swapped)
    return jnp.concatenate([dup.astype(BF16), jnp.ones(v2.shape, BF16)], axis=1)


def _dot_nt(a, b):
    return lax.dot_general(a, b, (((1,), (1,)), ((), ())), preferred_element_type=F32)


def _inproj_kernel(seq_tiles, x_ref, xp_ref, g_ref, w_ref, kf_ref, cw_ref, gc_ref, mixc_ref, q_ref, gate_ref,
                   kvc_ref, ks_ref, vs_ref, kw_ref, vw_ref):
    u = _rms(x_ref[...], g_ref[...]).astype(BF16)

    cbch = _dot(u, w_ref[:, 0:_CONV_W])
    cu = cbch[:, CONV_CH:2 * CONV_CH] * cbch[:, 2 * CONV_CH:3 * CONV_CH]
    up = _rms(xp_ref[...], g_ref[...]).astype(BF16)
    chp = _dot(up, w_ref[:, CONV_CH:_CONV_W])
    starts_sequence = pl.program_id(0) % seq_tiles == 0
    prev = jnp.where(starts_sequence, 0.0, chp[:, 0:CONV_CH] * chp[:, CONV_CH:2 * CONV_CH])
    r = lax.broadcasted_iota(jnp.int32, cu.shape, 0)
    cu1 = jnp.where(r == 0, prev[7:8, :], pltpu.roll(cu, 1, axis=0))
    cu2 = pltpu.roll(cu, 2, axis=0)
    cu2 = jnp.where(r == 0, prev[6:7, :], jnp.where(r == 1, prev[7:8, :], cu2))
    cw = cw_ref[...]
    y = cbch[:, 0:CONV_CH] * (cw[0:1, :] * cu2 + cw[1:2, :] * cu1 + cw[2:3, :] * cu)
    mixc_ref[...] = _rms(y, gc_ref[...]).astype(BF16)

    q_ref[...] = _dot(u, w_ref[:, _Q_OFF:_KVC_OFF]).astype(BF16)
    kvc_ref[...] = _dot(u, w_ref[:, _KVC_OFF:_KV_OFF])
    gate_ref[...] = jax.nn.sigmoid(_dot(u, w_ref[:, _G_OFF:_PROJ_W]))
    kv = _dot(u, w_ref[:, _KV_OFF:_G_OFF])
    kf = kf_ref[...]
    hd, kvw = HEAD_DIM, N_KV_HEADS * HEAD_DIM
    for h in range(N_KV_HEADS):
        ks_ref[h] = kf
        ks_ref[h, :, 0:hd] = kv[:, h * hd:(h + 1) * hd].astype(BF16)
        vs_ref[h] = _value_rows(kv[:, kvw:2 * kvw], h, False)
        kw_ref[h] = kf[:, 0:LANES]
        kw_ref[h, :, 0:hd] = kv[:, 2 * kvw + h * hd:2 * kvw + (h + 1) * hd].astype(BF16)
        vw_ref[h] = _value_rows(kv[:, 3 * kvw:4 * kvw], h, True)


def _inproj(x2, g, w, kfeat, conv_w, gc, tm=1024):
    rows = x2.shape[0]
    seq_tiles = kfeat.shape[0] // tm
    row = lambda i: (i, 0)
    head_row = lambda i: (0, i, 0)
    const = lambda i: (0, 0)
    return pl.pallas_call(
        functools.partial(_inproj_kernel, seq_tiles),
        grid=(rows // tm,),
        in_specs=[
            pl.BlockSpec((tm, D_MODEL), row),
            pl.BlockSpec((SUBLANES, D_MODEL), lambda i: (jnp.maximum(i * (tm // SUBLANES) - 1, 0), 0)),
            pl.BlockSpec((1, D_MODEL), const),
            pl.BlockSpec((D_MODEL, _PROJ_W), const, pipeline_mode=pl.Buffered(1)),
            pl.BlockSpec((tm, 2 * LANES), lambda i: (i % seq_tiles, 0)),
            pl.BlockSpec((CONV_K, CONV_CH), const),
            pl.BlockSpec((1, CONV_CH), const),
        ],
        out_specs=[
            pl.BlockSpec((tm, CONV_CH), row),
            pl.BlockSpec((tm, N_HEADS * HEAD_DIM), row),
            pl.BlockSpec((tm, _G_W), row),
            pl.BlockSpec((tm, _KV_OFF - _KVC_OFF), row),
            pl.BlockSpec((N_KV_HEADS, tm, 2 * LANES), head_row),
            pl.BlockSpec((N_KV_HEADS, tm, LANES), head_row),
            pl.BlockSpec((N_KV_HEADS, tm, LANES), head_row),
            pl.BlockSpec((N_KV_HEADS, tm, VW), head_row),
        ],
        out_shape=[
            jax.ShapeDtypeStruct((rows, CONV_CH), BF16),
            jax.ShapeDtypeStruct((rows, N_HEADS * HEAD_DIM), BF16),
            jax.ShapeDtypeStruct((rows, _G_W), F32),
            jax.ShapeDtypeStruct((rows, _KV_OFF - _KVC_OFF), F32),
            jax.ShapeDtypeStruct((N_KV_HEADS, rows, 2 * LANES), BF16),
            jax.ShapeDtypeStruct((N_KV_HEADS, rows, LANES), BF16),
            jax.ShapeDtypeStruct((N_KV_HEADS, rows, LANES), BF16),
            jax.ShapeDtypeStruct((N_KV_HEADS, rows, VW), BF16),
        ],
        compiler_params=pltpu.CompilerParams(
            dimension_semantics=("arbitrary",), vmem_limit_bytes=VMEM_LIMIT),
        name="inproj",
    )(x2, x2, g, w, kfeat, conv_w, gc)


def _compress_kernel(xk_ref, xv_ref, pos_ref, w1_ref, w2_ref, cf_ref, kc_ref, vc_ref):
    nch = xk_ref.shape[0] // CMP_STRIDE
    hd = HEAD_DIM
    first = [jnp.zeros((nch, CMP_HIDDEN), F32) for _ in range(2 * N_KV_HEADS)]
    second = [jnp.zeros((nch, CMP_HIDDEN), F32) for _ in range(2 * N_KV_HEADS)]
    for l in range(CMP_STRIDE):
        for s, x_ref in enumerate((xk_ref, xv_ref)):
            xl = x_ref[pl.ds(l, nch, stride=CMP_STRIDE), :].astype(BF16)
            for h in range(N_KV_HEADS):
                c = s * N_KV_HEADS + h
                xh = xl[:, h * hd:(h + 1) * hd]
                first[c] = first[c] + _dot(xh, w1_ref[s, l * hd:(l + 1) * hd, :])
                second[c] = second[c] + _dot(xh, w1_ref[s, (CMP_STRIDE + l) * hd:(CMP_STRIDE + l + 1) * hd, :])
    for s in range(2):
        pb = _dot(pos_ref[s], w1_ref[s])[0:1, :]
        outs = []
        for h in range(N_KV_HEADS):
            c = s * N_KV_HEADS + h
            hid = first[c] + pltpu.roll(second[c], nch - 1, axis=0) + pb
            outs.append(_dot(jax.nn.gelu(hid, approximate=True).astype(BF16), w2_ref[s]))
        for h in range(N_KV_HEADS):
            if s == 0:
                kc_ref[h] = cf_ref[...]
                kc_ref[h, :, 0:hd] = outs[h].astype(BF16)
            else:
                vc_ref[h] = _value_rows(jnp.concatenate(outs, axis=1), h, True)


def _compress(kvc, pos, w1, w2, cfeat, B, S):
    nch = S // CMP_STRIDE
    full = lambda *shape: pl.BlockSpec(shape, lambda b: (0,) * len(shape))
    return pl.pallas_call(
        _compress_kernel,
        grid=(B,),
        in_specs=[
            pl.BlockSpec((S, LANES), lambda b: (b, 0)),
            pl.BlockSpec((S, LANES), lambda b: (b, 1)),
            full(*pos.shape), full(*w1.shape), full(*w2.shape), full(*cfeat.shape),
        ],
        out_specs=[
            pl.BlockSpec((None, N_KV_HEADS, nch, LANES), lambda b: (b, 0, 0, 0)),
            pl.BlockSpec((None, N_KV_HEADS, nch, VW), lambda b: (b, 0, 0, 0)),
        ],
        out_shape=[
            jax.ShapeDtypeStruct((B, N_KV_HEADS, nch, LANES), BF16),
            jax.ShapeDtypeStruct((B, N_KV_HEADS, nch, VW), BF16),
        ],
        compiler_params=pltpu.CompilerParams(
            dimension_semantics=("arbitrary",), vmem_limit_bytes=VMEM_LIMIT),
        name="compress",
    )(kvc, kvc, pos, w1, w2, cfeat)


def _hi_lo(x):
    hi = x.astype(BF16)
    return jnp.concatenate([hi, (x - hi.astype(F32)).astype(BF16)], axis=1)


def _head_layout(acc):
    low_half = lax.broadcasted_iota(jnp.int32, (TQ, LANES), 1) < HEAD_DIM
    head = lambda x, g: x[g * TQ:(g + 1) * TQ]
    if acc.shape[1] == VW:
        x = acc[:, 0:LANES] * (1.0 / jnp.maximum(acc[:, LANES:VW], 1e-30))
        pairs = [jnp.where(low_half, head(x, 2 * j), head(x, 2 * j + 1)) for j in range(GQA // 2)]
    else:
        rot = pltpu.roll(acc, HEAD_DIM, axis=1)
        pairs = [jnp.where(low_half, head(acc, 2 * j) * (1.0 / head(rot, 2 * j)),
                           head(rot, 2 * j + 1) * (1.0 / head(acc, 2 * j + 1))) for j in range(GQA // 2)]
    return jnp.concatenate(pairs, axis=1)


def _exchange(v, i, j, descending):
    hi, lo = jnp.maximum(v[i], v[j]), jnp.minimum(v[i], v[j])
    v[i], v[j] = (hi, lo) if descending else (lo, hi)


def _bitonic_merge_desc(v):
    n = len(v)
    j = n // 2
    while j >= 1:
        for i in range(n):
            if i ^ j > i:
                _exchange(v, i, i ^ j, True)
        j //= 2


def _kth_largest(x, kth, fillers=()):
    n = x.shape[0]
    assert n == kth * SUBLANES and kth & (kth - 1) == 0
    v = [x[r * SUBLANES:(r + 1) * SUBLANES, :] for r in range(kth)]
    fillers = list(fillers)
    layers = 0
    k = 2
    while k <= kth:
        j = k // 2
        while j >= 1:
            for i in range(kth):
                if i ^ j > i:
                    _exchange(v, i, i ^ j, (i & k) == 0)
            j //= 2
            layers += 1
            if layers % 3 == 0 and fillers:
                fillers.pop(0)()
        k *= 2
    shift = SUBLANES // 2
    while shift >= 1:
        v = [jnp.maximum(v[i], pltpu.roll(v[kth - 1 - i], shift, axis=0)) for i in range(kth)]
        if shift > 1:
            _bitonic_merge_desc(v)
        if fillers:
            fillers.pop(0)()
        shift //= 2
    for filler in fillers:
        filler()
    out = v[0]
    for i in range(1, kth):
        out = jnp.minimum(out, v[i])
    return out[0:1, :]


def _nsa_kernel(q_ref, qf_ref, kc_ref, vc_ref, ks_ref, vs_ref, kw_ref, vw_ref, ov_ref, cm_ref, mb_ref,
                ge_ref, g_ref, o_ref, qa_ref, tiles_ref, m_ref, acc_ref, sa_ref, sb_ref):
    i = pl.program_id(2)
    t0 = i * TQ
    rows = GQA * TQ
    n_sel = ov_ref.shape[1]
    ncmp = kc_ref.shape[0]

    q = q_ref[...]
    for g in range(GQA):
        qg = q[:, g * HEAD_DIM:(g + 1) * HEAD_DIM].astype(F32) * (HEAD_DIM ** -0.5)
        qa_ref[g * TQ:(g + 1) * TQ, 0:HEAD_DIM] = qg.astype(BF16)
    qa_ref[:, HEAD_DIM:LANES] = qf_ref[...]
    qa1 = qa_ref[:, 0:LANES]

    diag_ok = (lax.broadcasted_iota(jnp.int32, (rows, TQ), 1)
               <= (lax.broadcasted_iota(jnp.int32, (rows, TQ), 0) & (TQ - 1)))

    wk = TQ + WINDOW
    w0 = pl.multiple_of(t0, TQ)
    win = {}

    def win_scores():
        s_w = _dot_nt(qa1, kw_ref[pl.ds(w0, wk), :])
        win["s"] = (jnp.where(diag_ok, NEG, s_w[:, 0:TQ]),
                    s_w[:, TQ:WINDOW], jnp.where(diag_ok, s_w[:, WINDOW:wk], NEG))

    def win_max():
        s_old, s_mid, s_new = win["s"]
        win["m"] = jnp.maximum(jnp.maximum(jnp.max(s_old, axis=1, keepdims=True),
                                           jnp.max(s_mid, axis=1, keepdims=True)),
                               jnp.max(s_new, axis=1, keepdims=True))

    def win_exp():
        win["e"] = jnp.concatenate([jnp.exp(s - win["m"]) for s in win["s"]], axis=1).astype(BF16)

    def win_pv():
        win["acc"] = _dot(win["e"], vw_ref[pl.ds(w0, wk), :])

    s_c = _dot_nt(qa1, kc_ref[...])
    s_c = jnp.where(cm_ref[...] <= (TQ // CMP_STRIDE) * i - 2, s_c, NEG)
    m_c = jnp.maximum(jnp.max(s_c, axis=1, keepdims=True), 0.5 * NEG)
    e_c = jnp.exp(s_c - m_c)
    acc_c = _dot(e_c.astype(BF16), vc_ref[...])
    inv_c = 1.0 / jnp.maximum(acc_c[:, LANES:VW], 1e-30)
    p_c = e_c * jnp.concatenate([inv_c] * (ncmp // LANES), axis=1)

    p_sum = p_c[0:TQ] + p_c[TQ:2 * TQ] + p_c[2 * TQ:3 * TQ] + p_c[3 * TQ:4 * TQ]
    p_hi = p_sum.astype(BF16)
    p_lo = (p_sum - p_hi.astype(F32)).astype(BF16)
    imp = _dot(p_hi, ov_ref[...]) + _dot(p_lo, ov_ref[...])
    imp_t = imp.T
    jb = lax.broadcasted_iota(jnp.int32, (n_sel, TQ), 0)
    tq = t0 + lax.broadcasted_iota(jnp.int32, (n_sel, TQ), 1)
    imp_t = jnp.where(SEL_BLOCK * jb > tq, -jnp.inf, imp_t)
    forced = (jb == 0) | (jb == (tq >> 6))
    imp_t = jnp.where(forced, jnp.inf, imp_t)

    thr = _kth_largest(imp_t, SEL_TOP, (win_scores, win_max, win_exp, win_pv))
    acc_w = win["acc"]
    above = imp_t > thr
    tied = imp_t == thr
    n_above = jnp.sum(jnp.where(above, 1.0, 0.0), axis=0, keepdims=True)
    lower = (lax.broadcasted_iota(jnp.int32, (n_sel, n_sel), 1)
             < lax.broadcasted_iota(jnp.int32, (n_sel, n_sel), 0))
    tied_before = _dot(jnp.where(lower, 1.0, 0.0).astype(BF16), jnp.where(tied, 1.0, 0.0).astype(BF16))
    take_tie = jnp.where(tied_before + n_above < float(SEL_TOP), 1.0, 0.0)
    sel_t = jnp.where(above, 1.0, jnp.where(tied, take_tie, 0.0))
    sel_q = sel_t.T
    bias = ((1.0 - sel_q) * MASK_BIAS).astype(BF16)
    for g in range(GQA):
        qa_ref[g * TQ:(g + 1) * TQ, LANES:2 * LANES] = bias

    n_past = t0 // TK
    first_mask = (t0 - n_past * TK) // TQ
    tiles_ref[0] = n_past

    def scores(j, s_ref, mask=None):
        k0 = pl.multiple_of(tiles_ref[j] * TK, TK)
        s = _dot_nt(qa_ref[...], ks_ref[pl.ds(k0, TK), :])
        s_ref[...] = s if mask is None else s + mask

    scores(0, sa_ref, mb_ref[first_mask])

    bpt = TK // SEL_BLOCK
    n_tiles = n_sel // bpt
    blk_cnt = _dot(jnp.ones((SUBLANES, TQ), BF16), sel_q.astype(BF16))
    in_tile = (lax.broadcasted_iota(jnp.int32, (n_sel, LANES), 0) // bpt
               == lax.broadcasted_iota(jnp.int32, (n_sel, LANES), 1))
    tile_cnt = _dot(jnp.where(blk_cnt > 0.5, 1.0, 0.0).astype(BF16),
                    jnp.where(in_tile, 1.0, 0.0).astype(BF16))
    lane = lax.broadcasted_iota(jnp.int32, (1, LANES), 1)
    bit = jnp.where(tile_cnt[0:1, :] > 0.5, jnp.left_shift(1, lane & 15).astype(F32), 0.0)
    words = [jnp.sum(jnp.where((lane >= 16 * w) & (lane < 16 * (w + 1)), bit, 0.0)).astype(jnp.int32)
             for w in range((n_tiles + 15) // 16)]
    n_tasks = jnp.int32(1)
    for kt in range(n_tiles - 1):
        used = (words[kt // 16] >> (kt % 16)) & 1
        tiles_ref[n_tasks] = kt
        n_tasks = n_tasks + jnp.where(kt < n_past, used, 0)
    tiles_ref[n_tasks] = 0
    tiles_ref[n_tasks + 1] = 0

    m_ref[...] = jnp.full(m_ref.shape, NEG, F32)
    acc_ref[...] = jnp.zeros(acc_ref.shape, F32)

    gates = _dot(_hi_lo(g_ref[...]), ge_ref[...])
    gw = GQA * HEAD_DIM
    merged = gates[:, 0:gw] * _head_layout(acc_c) + gates[:, 2 * gw:3 * gw] * _head_layout(acc_w)
    gates_sel = gates[:, gw:2 * gw]

    def accumulate(j, s_ref):
        k0 = pl.multiple_of(tiles_ref[j] * TK, TK)
        slot = jnp.where(j < n_tasks, 0, 1)
        s = s_ref[...]
        m = m_ref[slot]
        m_new = jnp.maximum(m, jnp.max(s, axis=1, keepdims=True))
        p = jnp.concatenate([jnp.exp((s[:, c:c + LANES] - m_new).astype(BF16)) for c in range(0, TK, LANES)],
                            axis=1)
        acc_ref[slot] = (jnp.exp(m - m_new) * acc_ref[slot]
                         + _dot(p, vs_ref[pl.ds(k0, TK), :]))
        m_ref[slot] = m_new

    def pair(jj, c):
        a = 2 * jj
        scores(a + 1, sb_ref)
        accumulate(a, sa_ref)
        scores(a + 2, sa_ref)
        accumulate(a + 1, sb_ref)
        return c

    lax.fori_loop(0, (n_tasks + 1) // 2, pair, 0)

    o_ref[...] = (merged + gates_sel * _head_layout(acc_ref[0])).astype(o_ref.dtype)


def _nsa(q, qfeat, kc_aug, vc, ks_aug, vs, kw_aug, vw, ov, cmask, tile_masks, gate_expand, gates, B, S):
    nq = S // TQ
    gw = GQA * HEAD_DIM
    ncmp = kc_aug.shape[2]
    n_sel = ov.shape[1]
    sp = kw_aug.shape[2]
    bk = lambda b, k, i: (b, k, 0, 0)
    kb = lambda b, k, i: (k, b, 0, 0)
    return pl.pallas_call(
        _nsa_kernel,
        grid=(B, N_KV_HEADS, nq),
        in_specs=[
            pl.BlockSpec((TQ, gw), lambda b, k, i: (b * nq + i, k)),
            pl.BlockSpec((None, GQA * TQ, HEAD_DIM), lambda b, k, i: (k, 0, 0)),
            pl.BlockSpec((None, None, ncmp, LANES), bk),
            pl.BlockSpec((None, None, ncmp, VW), bk),
            pl.BlockSpec((None, None, S, 2 * LANES), kb),
            pl.BlockSpec((None, None, S, LANES), kb),
            pl.BlockSpec((None, None, sp, LANES), kb),
            pl.BlockSpec((None, None, sp, VW), kb),
            pl.BlockSpec((ncmp, n_sel), lambda b, k, i: (0, 0)),
            pl.BlockSpec((GQA * TQ, ncmp), lambda b, k, i: (0, 0)),
            pl.BlockSpec(tile_masks.shape, lambda b, k, i: (0, 0, 0)),
            pl.BlockSpec((None,) + gate_expand.shape[1:], lambda b, k, i: (k, 0, 0)),
            pl.BlockSpec((TQ, _G_W), lambda b, k, i: (b * nq + i, 0)),
        ],
        out_specs=pl.BlockSpec((TQ, gw), lambda b, k, i: (b * nq + i, k)),
        out_shape=jax.ShapeDtypeStruct((B * S, N_KV_HEADS * gw), BF16),
        scratch_shapes=[
            pltpu.VMEM((GQA * TQ, 2 * LANES), BF16),
            pltpu.SMEM((S // TK + 2,), jnp.int32),
            pltpu.VMEM((2, GQA * TQ, LANES), F32),
            pltpu.VMEM((2, GQA * TQ, LANES), F32),
            pltpu.VMEM((GQA * TQ, TK), F32),
            pltpu.VMEM((GQA * TQ, TK), F32),
        ],
        compiler_params=pltpu.CompilerParams(
            dimension_semantics=("arbitrary",) * 3, vmem_limit_bytes=VMEM_LIMIT),
        name="nsa",
    )(q, qfeat, kc_aug, vc, ks_aug, vs, kw_aug, vw, ov, cmask, tile_masks, gate_expand, gates)


_FF_CHUNK = 768


def _outproj_kernel(mc_ref, nsa_ref, x_ref, gn_ref, wo_ref, h_ref):
    mixed_n = _rms(nsa_ref[...].astype(F32), gn_ref[...]).astype(BF16)
    h_ref[...] = (x_ref[...] + _dot(mc_ref[...], wo_ref[0:CONV_CH, :])
                  + _dot(mixed_n, wo_ref[CONV_CH:2 * CONV_CH, :]))


def _outproj(mixed_c, nsa_out, x2, gn, wo, tm=1024):
    rows = x2.shape[0]
    row = lambda i: (i, 0)
    const = lambda i: (0, 0)
    return pl.pallas_call(
        _outproj_kernel,
        grid=(rows // tm,),
        in_specs=[
            pl.BlockSpec((tm, CONV_CH), row),
            pl.BlockSpec((tm, N_HEADS * HEAD_DIM), row),
            pl.BlockSpec((tm, D_MODEL), row),
            pl.BlockSpec((1, N_HEADS * HEAD_DIM), const),
            pl.BlockSpec((2 * CONV_CH, D_MODEL), const, pipeline_mode=pl.Buffered(1)),
        ],
        out_specs=pl.BlockSpec((tm, D_MODEL), row),
        out_shape=jax.ShapeDtypeStruct((rows, D_MODEL), F32),
        compiler_params=pltpu.CompilerParams(
            dimension_semantics=("arbitrary",), vmem_limit_bytes=VMEM_LIMIT),
        name="outproj",
    )(mixed_c, nsa_out, x2, gn, wo)


def _outffn_kernel(h_ref, g2_ref, wg_ref, wu_ref, wd_ref, gf_ref, o_ref):
    h = h_ref[...]
    u = _rms(h, g2_ref[...]).astype(BF16)
    acc = h
    for c in range(0, D_FF, _FF_CHUNK):
        sl = slice(c, min(c + _FF_CHUNK, D_FF))
        a = jax.nn.silu(_dot(u, wg_ref[:, sl])) * _dot(u, wu_ref[:, sl])
        acc = acc + _dot(a.astype(BF16), wd_ref[sl, :])
    o_ref[...] = _rms(acc, gf_ref[...])


def _outffn(h, g2, wg, wu, wd, gf, tm=512):
    rows = h.shape[0]
    row = lambda i: (i, 0)
    const = lambda i: (0, 0)
    resident = lambda shape: pl.BlockSpec(shape, const, pipeline_mode=pl.Buffered(1))
    return pl.pallas_call(
        _outffn_kernel,
        grid=(rows // tm,),
        in_specs=[
            pl.BlockSpec((tm, D_MODEL), row),
            pl.BlockSpec((1, D_MODEL), const),
            resident((D_MODEL, D_FF)),
            resident((D_MODEL, D_FF)),
            resident((D_FF, D_MODEL)),
            pl.BlockSpec((1, D_MODEL), const),
        ],
        out_specs=pl.BlockSpec((tm, D_MODEL), row),
        out_shape=jax.ShapeDtypeStruct((rows, D_MODEL), F32),
        compiler_params=pltpu.CompilerParams(
            dimension_semantics=("arbitrary",), vmem_limit_bytes=VMEM_LIMIT),
        name="outffn",
    )(h, g2, wg, wu, wd, gf)


def _key_features(pos, n_sel):
    f = np.zeros((pos.shape[0], LANES + n_sel), np.float32)
    f[:, HEAD_DIM] = pos // SEL_BLOCK
    f[:, HEAD_DIM + 1] = pos % SEL_BLOCK
    if n_sel:
        f[np.arange(pos.shape[0]), LANES + pos // SEL_BLOCK] = 1.0
    return f


def _query_features():
    f = np.zeros((N_KV_HEADS, GQA * TQ, HEAD_DIM), np.float32)
    for k in range(N_KV_HEADS):
        for g in range(GQA):
            slope = 2.0 ** (-8.0 * (k * GQA + g + 1) / N_HEADS)
            f[k, g * TQ:(g + 1) * TQ, 0] = slope * SEL_BLOCK
            f[k, g * TQ:(g + 1) * TQ, 1] = slope
            f[k, g * TQ:(g + 1) * TQ, 2] = MASK_BIAS
    return f


def _causal_masks(ncmp):
    r = np.arange(GQA * TQ)[:, None] % TQ
    cmask = (np.arange(ncmp)[None, :] - ((r + 1) // CMP_STRIDE)).astype(np.int32)
    c = np.arange(TK)[None, :]
    tile_masks = [np.where(c - r <= d * TQ, 0.0, NEG).astype(np.float32) for d in range(max(TK // TQ, 1))]
    return cmask, np.stack(tile_masks)


def _gate_expand():
    gw = GQA * HEAD_DIM
    e = np.zeros((N_KV_HEADS, 2 * _G_W, N_BRANCH * gw), np.float32)
    for k in range(N_KV_HEADS):
        for g in range(GQA):
            for br in range(N_BRANCH):
                for half in range(2):
                    e[k, half * _G_W + (k * GQA + g) * N_BRANCH + br,
                      br * gw + g * HEAD_DIM:br * gw + (g + 1) * HEAD_DIM] = 1.0
    return e


def _overlap(ncmp_pad, n_sel):
    nc = ncmp_pad - 1
    c_start = CMP_STRIDE * np.arange(nc)
    c_end = c_start + CMP_BLOCK - 1
    s_start = SEL_BLOCK * np.arange(n_sel)
    ov = np.minimum(c_end[:, None] + 1, s_start[None, :] + SEL_BLOCK) - np.maximum(c_start[:, None], s_start[None, :])
    out = np.zeros((ncmp_pad, n_sel), np.float32)
    out[:nc] = np.clip(ov, 0, None).astype(np.float32) / CMP_BLOCK
    return out


def _layer(x2, B, S, norm1_g, w_in, conv_w, k_cmp_pos, k_cmp_w1, k_cmp_w2, v_cmp_pos, v_cmp_w1, v_cmp_w2,
           gn_conv_g):
    n_sel = S // SEL_BLOCK
    nch = S // CMP_STRIDE
    hd = HEAD_DIM

    kfeat = jnp.asarray(_key_features(np.arange(S), n_sel), BF16)
    mixed_c, q, gates, kvc, ks_aug, vs, kw_aug, vw = _inproj(
        x2, norm1_g.reshape(1, D_MODEL), w_in.astype(BF16), kfeat, conv_w, gn_conv_g.reshape(1, CONV_CH))

    w1 = jnp.stack([k_cmp_w1, v_cmp_w1]).astype(BF16)
    w2 = jnp.stack([k_cmp_w2, v_cmp_w2]).astype(BF16)
    pos = jnp.stack([k_cmp_pos, v_cmp_pos]).reshape(2, 1, CMP_BLOCK * hd)
    pos = jnp.broadcast_to(pos, (2, SUBLANES, CMP_BLOCK * hd)).astype(BF16)
    c_end = CMP_STRIDE * np.arange(nch) + CMP_BLOCK - 1
    cfeat = jnp.asarray(_key_features(c_end, 0), BF16)
    kc_aug, vc = _compress(kvc, pos, w1, w2, cfeat, B, S)

    pad_rows = np.zeros((WINDOW, LANES), np.float32)
    pad_rows[:, hd + 2] = 1.0
    pad_k = jnp.broadcast_to(jnp.asarray(pad_rows, BF16), (N_KV_HEADS, B, WINDOW, LANES))
    kw_aug = jnp.concatenate([pad_k, kw_aug.reshape(N_KV_HEADS, B, S, LANES)], axis=2)
    vw = jnp.pad(vw.reshape(N_KV_HEADS, B, S, VW), ((0, 0), (0, 0), (WINDOW, 0), (0, 0)))
    cmask, tile_masks = _causal_masks(nch)

    nsa_out = _nsa(q, jnp.asarray(_query_features(), BF16), kc_aug, vc,
                   ks_aug.reshape(N_KV_HEADS, B, S, 2 * LANES), vs.reshape(N_KV_HEADS, B, S, LANES),
                   kw_aug, vw, jnp.asarray(_overlap(nch, n_sel), BF16), jnp.asarray(cmask),
                   jnp.asarray(tile_masks), jnp.asarray(_gate_expand(), BF16), gates, B, S)
    return mixed_c, nsa_out


def kernel(x, norm1_g, w_in, conv_w, k_cmp_pos, k_cmp_w1, k_cmp_w2, v_cmp_pos, v_cmp_w1, v_cmp_w2,
           gn_conv_g, gn_nsa_g, w_out, norm2_g, w_gate, w_up, w_down, norm_f_g):
    B, S, _ = x.shape
    depth = norm1_g.shape[0]
    assert depth == 1, "the final norm is fused into the (single) layer's FFN kernel"
    x2 = x.reshape(B * S, D_MODEL)
    mixed_c, nsa_out = _layer(x2, B, S, norm1_g[0], w_in[0], conv_w[0], k_cmp_pos[0], k_cmp_w1[0],
                              k_cmp_w2[0], v_cmp_pos[0], v_cmp_w1[0], v_cmp_w2[0], gn_conv_g[0])
    h = _outproj(mixed_c, nsa_out, x2, gn_nsa_g[0].reshape(1, N_HEADS * HEAD_DIM), w_out[0].astype(BF16))
    out = _outffn(h, norm2_g[0].reshape(1, D_MODEL), w_gate[0].astype(BF16), w_up[0].astype(BF16),
                  w_down[0].astype(BF16), norm_f_g.reshape(1, D_MODEL))
    return out.reshape(B, S, D_MODEL)
```

```python
import functools

import numpy as np
import jax
import jax.numpy as jnp
from jax import lax
from jax.experimental import pallas as pl
from jax.experimental.pallas import tpu as pltpu

D_MODEL = 1024
CONV_CH = 512
CONV_K = 3
N_HEADS = 8
HEAD_DIM = 64
N_KV_HEADS = 2
GQA = 4
N_BRANCH = 3
CMP_BLOCK = 32
CMP_STRIDE = 16
CMP_HIDDEN = 256
SEL_BLOCK = 64
SEL_TOP = 16
WINDOW = 512
D_FF = 2816
NORM_EPS = 1e-6

F32 = jnp.float32
BF16 = jnp.bfloat16

LANES = 128
VW = 2 * LANES
SUBLANES = 8
VMEM_LIMIT = 56 * 1024 * 1024

TQ = 256
TK = 256
NEG = -1e30
MASK_BIAS = -float(2 ** 30)

_CONV_W = 3 * CONV_CH
_Q_OFF = _CONV_W
_KVC_OFF = _Q_OFF + N_HEADS * HEAD_DIM
_KV_OFF = _KVC_OFF + 2 * N_KV_HEADS * HEAD_DIM
_G_OFF = _KV_OFF + 4 * N_KV_HEADS * HEAD_DIM
_G_W = N_HEADS * N_BRANCH
_PROJ_W = _G_OFF + _G_W


def _rms(x, g):
    return x * lax.rsqrt(jnp.mean(x * x, axis=-1, keepdims=True) + NORM_EPS) * g


def _dot(a, b):
    return jnp.dot(a, b, preferred_element_type=F32)


def _value_rows(v2, h, wide):
    lane = lax.broadcasted_iota(jnp.int32, v2.shape, 1)
    swapped = pltpu.roll(v2, HEAD_DIM, axis=1)
    if not wide:
        return jnp.where(lane < HEAD_DIM, v2 if h == 0 else swapped, 1.0).astype(BF16)
    dup = jnp.where((lane < HEAD_DIM) == (h == 0), v2, swapped)
    return jnp.concatenate([dup.astype(BF16), jnp.ones(v2.shape, BF16)], axis=1)


def _dot_nt(a, b):
    return lax.dot_general(a, b, (((1,), (1,)), ((), ())), preferred_element_type=F32)


def _inproj_kernel(seq_tiles, n_cast, x_ref, xp_ref, g_ref, w_ref, kf_ref, cw_ref, gc_ref, *refs):
    cast_in, refs = refs[:n_cast], refs[n_cast:]
    (mixc_ref, q_ref, gate_ref, kvc_ref, ks_ref, vs_ref, kw_ref, vw_ref), cast_out = refs[:8], refs[8:]
    for src_ref, dst_ref in zip(cast_in, cast_out):
        dst_ref[...] = src_ref[...].astype(BF16)

    u = _rms(x_ref[...], g_ref[...]).astype(BF16)

    cbch = _dot(u, w_ref[:, 0:_CONV_W])
    cu = cbch[:, CONV_CH:2 * CONV_CH] * cbch[:, 2 * CONV_CH:3 * CONV_CH]
    up = _rms(xp_ref[...], g_ref[...]).astype(BF16)
    chp = _dot(up, w_ref[:, CONV_CH:_CONV_W])
    starts_sequence = pl.program_id(0) % seq_tiles == 0
    prev = jnp.where(starts_sequence, 0.0, chp[:, 0:CONV_CH] * chp[:, CONV_CH:2 * CONV_CH])
    r = lax.broadcasted_iota(jnp.int32, cu.shape, 0)
    cu1 = jnp.where(r == 0, prev[7:8, :], pltpu.roll(cu, 1, axis=0))
    cu2 = pltpu.roll(cu, 2, axis=0)
    cu2 = jnp.where(r == 0, prev[6:7, :], jnp.where(r == 1, prev[7:8, :], cu2))
    cw = cw_ref[...]
    y = cbch[:, 0:CONV_CH] * (cw[0:1, :] * cu2 + cw[1:2, :] * cu1 + cw[2:3, :] * cu)
    mixc_ref[...] = _rms(y, gc_ref[...]).astype(BF16)

    q_ref[...] = _dot(u, w_ref[:, _Q_OFF:_KVC_OFF]).astype(BF16)
    kvc_ref[...] = _dot(u, w_ref[:, _KVC_OFF:_KV_OFF])
    gate_ref[...] = jax.nn.sigmoid(_dot(u, w_ref[:, _G_OFF:_PROJ_W]))
    kv = _dot(u, w_ref[:, _KV_OFF:_G_OFF])
    kf = kf_ref[...]
    hd, kvw = HEAD_DIM, N_KV_HEADS * HEAD_DIM
    for h in range(N_KV_HEADS):
        ks_ref[h] = kf
        ks_ref[h, :, 0:hd] = kv[:, h * hd:(h + 1) * hd].astype(BF16)
        vs_ref[h] = _value_rows(kv[:, kvw:2 * kvw], h, False)
        kw_ref[h] = kf[:, 0:LANES]
        kw_ref[h, :, 0:hd] = kv[:, 2 * kvw + h * hd:2 * kvw + (h + 1) * hd].astype(BF16)
        vw_ref[h] = _value_rows(kv[:, 3 * kvw:4 * kvw], h, True)


def _inproj(x2, g, w, kfeat, conv_w, gc, to_cast, tm=1024):
    rows = x2.shape[0]
    steps = rows // tm
    seq_tiles = kfeat.shape[0] // tm
    row = lambda i: (i, 0)
    head_row = lambda i: (0, i, 0)
    const = lambda i: (0, 0)
    cast_specs = [pl.BlockSpec((a.shape[0] // steps, a.shape[1]), row) for a in to_cast]
    return pl.pallas_call(
        functools.partial(_inproj_kernel, seq_tiles, len(to_cast)),
        grid=(steps,),
        in_specs=[
            pl.BlockSpec((tm, D_MODEL), row),
            pl.BlockSpec((SUBLANES, D_MODEL), lambda i: (jnp.maximum(i * (tm // SUBLANES) - 1, 0), 0)),
            pl.BlockSpec((1, D_MODEL), const),
            pl.BlockSpec((D_MODEL, _PROJ_W), const, pipeline_mode=pl.Buffered(1)),
            pl.BlockSpec((tm, 2 * LANES), lambda i: (i % seq_tiles, 0)),
            pl.BlockSpec((CONV_K, CONV_CH), const),
            pl.BlockSpec((1, CONV_CH), const),
        ] + cast_specs,
        out_specs=[
            pl.BlockSpec((tm, CONV_CH), row),
            pl.BlockSpec((tm, N_HEADS * HEAD_DIM), row),
            pl.BlockSpec((tm, _G_W), row),
            pl.BlockSpec((tm, _KV_OFF - _KVC_OFF), row),
            pl.BlockSpec((N_KV_HEADS, tm, 2 * LANES), head_row),
            pl.BlockSpec((N_KV_HEADS, tm, LANES), head_row),
            pl.BlockSpec((N_KV_HEADS, tm, LANES), head_row),
            pl.BlockSpec((N_KV_HEADS, tm, VW), head_row),
        ] + cast_specs,
        out_shape=[
            jax.ShapeDtypeStruct((rows, CONV_CH), BF16),
            jax.ShapeDtypeStruct((rows, N_HEADS * HEAD_DIM), BF16),
            jax.ShapeDtypeStruct((rows, _G_W), F32),
            jax.ShapeDtypeStruct((rows, _KV_OFF - _KVC_OFF), F32),
            jax.ShapeDtypeStruct((N_KV_HEADS, rows, 2 * LANES), BF16),
            jax.ShapeDtypeStruct((N_KV_HEADS, rows, LANES), BF16),
            jax.ShapeDtypeStruct((N_KV_HEADS, rows, LANES), BF16),
            jax.ShapeDtypeStruct((N_KV_HEADS, rows, VW), BF16),
        ] + [jax.ShapeDtypeStruct(a.shape, BF16) for a in to_cast],
        compiler_params=pltpu.CompilerParams(
            dimension_semantics=("arbitrary",), vmem_limit_bytes=VMEM_LIMIT),
        name="inproj",
    )(x2, x2, g, w, kfeat, conv_w, gc, *to_cast)


def _compress_kernel(xk_ref, xv_ref, pos_ref, w1_ref, w2_ref, cf_ref, kc_ref, vc_ref):
    nch = xk_ref.shape[0] // CMP_STRIDE
    hd = HEAD_DIM
    first = [jnp.zeros((nch, CMP_HIDDEN), F32) for _ in range(2 * N_KV_HEADS)]
    second = [jnp.zeros((nch, CMP_HIDDEN), F32) for _ in range(2 * N_KV_HEADS)]
    for l in range(CMP_STRIDE):
        for s, x_ref in enumerate((xk_ref, xv_ref)):
            xl = x_ref[pl.ds(l, nch, stride=CMP_STRIDE), :].astype(BF16)
            for h in range(N_KV_HEADS):
                c = s * N_KV_HEADS + h
                xh = xl[:, h * hd:(h + 1) * hd]
                first[c] = first[c] + _dot(xh, w1_ref[s, l * hd:(l + 1) * hd, :])
                second[c] = second[c] + _dot(xh, w1_ref[s, (CMP_STRIDE + l) * hd:(CMP_STRIDE + l + 1) * hd, :])
    for s in range(2):
        pb = _dot(pos_ref[s], w1_ref[s])[0:1, :]
        outs = []
        for h in range(N_KV_HEADS):
            c = s * N_KV_HEADS + h
            hid = first[c] + pltpu.roll(second[c], nch - 1, axis=0) + pb
            outs.append(_dot(jax.nn.gelu(hid, approximate=True).astype(BF16), w2_ref[s]))
        for h in range(N_KV_HEADS):
            if s == 0:
                kc_ref[h] = cf_ref[...]
                kc_ref[h, :, 0:hd] = outs[h].astype(BF16)
            else:
                vc_ref[h] = _value_rows(jnp.concatenate(outs, axis=1), h, True)


def _compress(kvc, pos, w1, w2, cfeat, B, S):
    nch = S // CMP_STRIDE
    full = lambda *shape: pl.BlockSpec(shape, lambda b: (0,) * len(shape))
    return pl.pallas_call(
        _compress_kernel,
        grid=(B,),
        in_specs=[
            pl.BlockSpec((S, LANES), lambda b: (b, 0)),
            pl.BlockSpec((S, LANES), lambda b: (b, 1)),
            full(*pos.shape), full(*w1.shape), full(*w2.shape), full(*cfeat.shape),
        ],
        out_specs=[
            pl.BlockSpec((None, N_KV_HEADS, nch, LANES), lambda b: (b, 0, 0, 0)),
            pl.BlockSpec((None, N_KV_HEADS, nch, VW), lambda b: (b, 0, 0, 0)),
        ],
        out_shape=[
            jax.ShapeDtypeStruct((B, N_KV_HEADS, nch, LANES), BF16),
            jax.ShapeDtypeStruct((B, N_KV_HEADS, nch, VW), BF16),
        ],
        compiler_params=pltpu.CompilerParams(
            dimension_semantics=("arbitrary",), vmem_limit_bytes=VMEM_LIMIT),
        name="compress",
    )(kvc, kvc, pos, w1, w2, cfeat)


def _hi_lo(x):
    hi = x.astype(BF16)
    return jnp.concatenate([hi, (x - hi.astype(F32)).astype(BF16)], axis=1)


def _head_layout(acc):
    low_half = lax.broadcasted_iota(jnp.int32, (TQ, LANES), 1) < HEAD_DIM
    head = lambda x, g: x[g * TQ:(g + 1) * TQ]
    if acc.shape[1] == VW:
        x = acc[:, 0:LANES] * (1.0 / jnp.maximum(acc[:, LANES:VW], 1e-30))
        pairs = [jnp.where(low_half, head(x, 2 * j), head(x, 2 * j + 1)) for j in range(GQA // 2)]
    else:
        rot = pltpu.roll(acc, HEAD_DIM, axis=1)
        pairs = [jnp.where(low_half, head(acc, 2 * j) * (1.0 / head(rot, 2 * j)),
                           head(rot, 2 * j + 1) * (1.0 / head(acc, 2 * j + 1))) for j in range(GQA // 2)]
    return jnp.concatenate(pairs, axis=1)


def _exchange(v, i, j, descending):
    hi, lo = jnp.maximum(v[i], v[j]), jnp.minimum(v[i], v[j])
    v[i], v[j] = (hi, lo) if descending else (lo, hi)


def _bitonic_merge_desc(v):
    n = len(v)
    j = n // 2
    while j >= 1:
        for i in range(n):
            if i ^ j > i:
                _exchange(v, i, i ^ j, True)
        j //= 2


def _kth_largest(x, kth, fillers=()):
    n = x.shape[0]
    assert n == kth * SUBLANES and kth & (kth - 1) == 0
    v = [x[r * SUBLANES:(r + 1) * SUBLANES, :] for r in range(kth)]
    fillers = list(fillers)
    layers = 0
    k = 2
    while k <= kth:
        j = k // 2
        while j >= 1:
            for i in range(kth):
                if i ^ j > i:
                    _exchange(v, i, i ^ j, (i & k) == 0)
            j //= 2
            layers += 1
            if layers % 3 == 0 and fillers:
                fillers.pop(0)()
        k *= 2
    shift = SUBLANES // 2
    while shift >= 1:
        v = [jnp.maximum(v[i], pltpu.roll(v[kth - 1 - i], shift, axis=0)) for i in range(kth)]
        if shift > 1:
            _bitonic_merge_desc(v)
        if fillers:
            fillers.pop(0)()
        shift //= 2
    for filler in fillers:
        filler()
    out = v[0]
    for i in range(1, kth):
        out = jnp.minimum(out, v[i])
    return out[0:1, :]


def _nsa_kernel(q_ref, qf_ref, kc_ref, vc_ref, ks_ref, vs_ref, kw_ref, vw_ref, ov_ref, cm_ref, mb_ref,
                ge_ref, g_ref, o_ref, qa_ref, tiles_ref, m_ref, acc_ref, sa_ref, sb_ref):
    i = pl.program_id(2)
    t0 = i * TQ
    rows = GQA * TQ
    n_sel = ov_ref.shape[1]
    ncmp = kc_ref.shape[0]

    q = q_ref[...]
    for g in range(GQA):
        qg = q[:, g * HEAD_DIM:(g + 1) * HEAD_DIM].astype(F32) * (HEAD_DIM ** -0.5)
        qa_ref[g * TQ:(g + 1) * TQ, 0:HEAD_DIM] = qg.astype(BF16)
    qa_ref[:, HEAD_DIM:LANES] = qf_ref[...]
    qa1 = qa_ref[:, 0:LANES]

    diag_ok = (lax.broadcasted_iota(jnp.int32, (rows, TQ), 1)
               <= (lax.broadcasted_iota(jnp.int32, (rows, TQ), 0) & (TQ - 1)))

    wk = TQ + WINDOW
    w0 = pl.multiple_of(t0, TQ)
    win = {}

    def win_scores():
        s_w = _dot_nt(qa1, kw_ref[pl.ds(w0, wk), :])
        win["s"] = (jnp.where(diag_ok, NEG, s_w[:, 0:TQ]),
                    s_w[:, TQ:WINDOW], jnp.where(diag_ok, s_w[:, WINDOW:wk], NEG))

    def win_max():
        s_old, s_mid, s_new = win["s"]
        win["m"] = jnp.maximum(jnp.maximum(jnp.max(s_old, axis=1, keepdims=True),
                                           jnp.max(s_mid, axis=1, keepdims=True)),
                               jnp.max(s_new, axis=1, keepdims=True))

    def win_exp():
        win["e"] = jnp.concatenate([jnp.exp(s - win["m"]) for s in win["s"]], axis=1).astype(BF16)

    def win_pv():
        win["acc"] = _dot(win["e"], vw_ref[pl.ds(w0, wk), :])

    s_c = _dot_nt(qa1, kc_ref[...])
    s_c = jnp.where(cm_ref[...] <= (TQ // CMP_STRIDE) * i - 2, s_c, NEG)
    m_c = jnp.maximum(jnp.max(s_c, axis=1, keepdims=True), 0.5 * NEG)
    e_c = jnp.exp(s_c - m_c)
    acc_c = _dot(e_c.astype(BF16), vc_ref[...])
    inv_c = 1.0 / jnp.maximum(acc_c[:, LANES:VW], 1e-30)
    p_c = e_c * jnp.concatenate([inv_c] * (ncmp // LANES), axis=1)

    p_sum = p_c[0:TQ] + p_c[TQ:2 * TQ] + p_c[2 * TQ:3 * TQ] + p_c[3 * TQ:4 * TQ]
    p_hi = p_sum.astype(BF16)
    p_lo = (p_sum - p_hi.astype(F32)).astype(BF16)
    imp = _dot(p_hi, ov_ref[...]) + _dot(p_lo, ov_ref[...])
    imp_t = imp.T
    jb = lax.broadcasted_iota(jnp.int32, (n_sel, TQ), 0)
    tq = t0 + lax.broadcasted_iota(jnp.int32, (n_sel, TQ), 1)
    imp_t = jnp.where(SEL_BLOCK * jb > tq, -jnp.inf, imp_t)
    forced = (jb == 0) | (jb == (tq >> 6))
    imp_t = jnp.where(forced, jnp.inf, imp_t)

    thr = _kth_largest(imp_t, SEL_TOP, (win_scores, win_max, win_exp, win_pv))
    acc_w = win["acc"]
    above = imp_t > thr
    tied = imp_t == thr
    n_above = jnp.sum(jnp.where(above, 1.0, 0.0), axis=0, keepdims=True)
    lower = (lax.broadcasted_iota(jnp.int32, (n_sel, n_sel), 1)
             < lax.broadcasted_iota(jnp.int32, (n_sel, n_sel), 0))
    tied_before = _dot(jnp.where(lower, 1.0, 0.0).astype(BF16), jnp.where(tied, 1.0, 0.0).astype(BF16))
    take_tie = jnp.where(tied_before + n_above < float(SEL_TOP), 1.0, 0.0)
    sel_t = jnp.where(above, 1.0, jnp.where(tied, take_tie, 0.0))
    sel_q = sel_t.T
    bias = ((1.0 - sel_q) * MASK_BIAS).astype(BF16)
    for g in range(GQA):
        qa_ref[g * TQ:(g + 1) * TQ, LANES:2 * LANES] = bias

    n_past = t0 // TK
    first_mask = (t0 - n_past * TK) // TQ
    tiles_ref[0] = n_past

    def scores(j, s_ref, mask=None):
        k0 = pl.multiple_of(tiles_ref[j] * TK, TK)
        s = _dot_nt(qa_ref[...], ks_ref[pl.ds(k0, TK), :])
        s_ref[...] = s if mask is None else s + mask

    scores(0, sa_ref, mb_ref[first_mask])

    bpt = TK // SEL_BLOCK
    n_tiles = n_sel // bpt
    blk_cnt = _dot(jnp.ones((SUBLANES, TQ), BF16), sel_q.astype(BF16))
    in_tile = (lax.broadcasted_iota(jnp.int32, (n_sel, LANES), 0) // bpt
               == lax.broadcasted_iota(jnp.int32, (n_sel, LANES), 1))
    tile_cnt = _dot(jnp.where(blk_cnt > 0.5, 1.0, 0.0).astype(BF16),
                    jnp.where(in_tile, 1.0, 0.0).astype(BF16))
    lane = lax.broadcasted_iota(jnp.int32, (1, LANES), 1)
    bit = jnp.where(tile_cnt[0:1, :] > 0.5, jnp.left_shift(1, lane & 15).astype(F32), 0.0)
    words = [jnp.sum(jnp.where((lane >= 16 * w) & (lane < 16 * (w + 1)), bit, 0.0)).astype(jnp.int32)
             for w in range((n_tiles + 15) // 16)]
    n_tasks = jnp.int32(1)
    for kt in range(n_tiles - 1):
        used = (words[kt // 16] >> (kt % 16)) & 1
        tiles_ref[n_tasks] = kt
        n_tasks = n_tasks + jnp.where(kt < n_past, used, 0)
    tiles_ref[n_tasks] = 0
    tiles_ref[n_tasks + 1] = 0

    m_ref[...] = jnp.full(m_ref.shape, NEG, F32)
    acc_ref[...] = jnp.zeros(acc_ref.shape, F32)

    gates = _dot(_hi_lo(g_ref[...]), ge_ref[...])
    gw = GQA * HEAD_DIM
    merged = gates[:, 0:gw] * _head_layout(acc_c) + gates[:, 2 * gw:3 * gw] * _head_layout(acc_w)
    gates_sel = gates[:, gw:2 * gw]

    def accumulate(j, s_ref):
        k0 = pl.multiple_of(tiles_ref[j] * TK, TK)
        slot = jnp.where(j < n_tasks, 0, 1)
        s = s_ref[...]
        m = m_ref[slot]
        m_new = jnp.maximum(m, jnp.max(s, axis=1, keepdims=True))
        p = jnp.concatenate([jnp.exp((s[:, c:c + LANES] - m_new).astype(BF16)) for c in range(0, TK, LANES)],
                            axis=1)
        acc_ref[slot] = (jnp.exp(m - m_new) * acc_ref[slot]
                         + _dot(p, vs_ref[pl.ds(k0, TK), :]))
        m_ref[slot] = m_new

    def pair(jj, c):
        a = 2 * jj
        scores(a + 1, sb_ref)
        accumulate(a, sa_ref)
        scores(a + 2, sa_ref)
        accumulate(a + 1, sb_ref)
        return c

    lax.fori_loop(0, (n_tasks + 1) // 2, pair, 0)

    o_ref[...] = (merged + gates_sel * _head_layout(acc_ref[0])).astype(o_ref.dtype)


def _nsa(q, qfeat, kc_aug, vc, ks_aug, vs, kw_aug, vw, ov, cmask, tile_masks, gate_expand, gates, B, S):
    nq = S // TQ
    gw = GQA * HEAD_DIM
    ncmp = kc_aug.shape[2]
    n_sel = ov.shape[1]
    sp = kw_aug.shape[2]
    bk = lambda b, k, i: (b, k, 0, 0)
    kb = lambda b, k, i: (k, b, 0, 0)
    return pl.pallas_call(
        _nsa_kernel,
        grid=(B, N_KV_HEADS, nq),
        in_specs=[
            pl.BlockSpec((TQ, gw), lambda b, k, i: (b * nq + i, k)),
            pl.BlockSpec((None, GQA * TQ, HEAD_DIM), lambda b, k, i: (k, 0, 0)),
            pl.BlockSpec((None, None, ncmp, LANES), bk),
            pl.BlockSpec((None, None, ncmp, VW), bk),
            pl.BlockSpec((None, None, S, 2 * LANES), kb),
            pl.BlockSpec((None, None, S, LANES), kb),
            pl.BlockSpec((None, None, sp, LANES), kb),
            pl.BlockSpec((None, None, sp, VW), kb),
            pl.BlockSpec((ncmp, n_sel), lambda b, k, i: (0, 0)),
            pl.BlockSpec((GQA * TQ, ncmp), lambda b, k, i: (0, 0)),
            pl.BlockSpec(tile_masks.shape, lambda b, k, i: (0, 0, 0)),
            pl.BlockSpec((None,) + gate_expand.shape[1:], lambda b, k, i: (k, 0, 0)),
            pl.BlockSpec((TQ, _G_W), lambda b, k, i: (b * nq + i, 0)),
        ],
        out_specs=pl.BlockSpec((TQ, gw), lambda b, k, i: (b * nq + i, k)),
        out_shape=jax.ShapeDtypeStruct((B * S, N_KV_HEADS * gw), BF16),
        scratch_shapes=[
            pltpu.VMEM((GQA * TQ, 2 * LANES), BF16),
            pltpu.SMEM((S // TK + 2,), jnp.int32),
            pltpu.VMEM((2, GQA * TQ, LANES), F32),
            pltpu.VMEM((2, GQA * TQ, LANES), F32),
            pltpu.VMEM((GQA * TQ, TK), F32),
            pltpu.VMEM((GQA * TQ, TK), F32),
        ],
        compiler_params=pltpu.CompilerParams(
            dimension_semantics=("arbitrary",) * 3, vmem_limit_bytes=VMEM_LIMIT),
        name="nsa",
    )(q, qfeat, kc_aug, vc, ks_aug, vs, kw_aug, vw, ov, cmask, tile_masks, gate_expand, gates)


_FF_CHUNK = 768


def _outproj_kernel(mc_ref, nsa_ref, x_ref, gn_ref, wo_ref, h_ref):
    mixed_n = _rms(nsa_ref[...].astype(F32), gn_ref[...]).astype(BF16)
    h_ref[...] = (x_ref[...] + _dot(mc_ref[...], wo_ref[0:CONV_CH, :])
                  + _dot(mixed_n, wo_ref[CONV_CH:2 * CONV_CH, :]))


def _outproj(mixed_c, nsa_out, x2, gn, wo, tm=1024):
    rows = x2.shape[0]
    row = lambda i: (i, 0)
    const = lambda i: (0, 0)
    return pl.pallas_call(
        _outproj_kernel,
        grid=(rows // tm,),
        in_specs=[
            pl.BlockSpec((tm, CONV_CH), row),
            pl.BlockSpec((tm, N_HEADS * HEAD_DIM), row),
            pl.BlockSpec((tm, D_MODEL), row),
            pl.BlockSpec((1, N_HEADS * HEAD_DIM), const),
            pl.BlockSpec((2 * CONV_CH, D_MODEL), const, pipeline_mode=pl.Buffered(1)),
        ],
        out_specs=pl.BlockSpec((tm, D_MODEL), row),
        out_shape=jax.ShapeDtypeStruct((rows, D_MODEL), F32),
        compiler_params=pltpu.CompilerParams(
            dimension_semantics=("arbitrary",), vmem_limit_bytes=VMEM_LIMIT),
        name="outproj",
    )(mixed_c, nsa_out, x2, gn, wo)


def _outffn_kernel(h_ref, g2_ref, wg_ref, wu_ref, wd_ref, gf_ref, o_ref):
    h = h_ref[...]
    u = _rms(h, g2_ref[...]).astype(BF16)
    acc = h
    for c in range(0, D_FF, _FF_CHUNK):
        sl = slice(c, min(c + _FF_CHUNK, D_FF))
        a = jax.nn.silu(_dot(u, wg_ref[:, sl])) * _dot(u, wu_ref[:, sl])
        acc = acc + _dot(a.astype(BF16), wd_ref[sl, :])
    o_ref[...] = _rms(acc, gf_ref[...])


def _outffn(h, g2, wg, wu, wd, gf, tm=512):
    rows = h.shape[0]
    row = lambda i: (i, 0)
    const = lambda i: (0, 0)
    resident = lambda shape: pl.BlockSpec(shape, const, pipeline_mode=pl.Buffered(1))
    return pl.pallas_call(
        _outffn_kernel,
        grid=(rows // tm,),
        in_specs=[
            pl.BlockSpec((tm, D_MODEL), row),
            pl.BlockSpec((1, D_MODEL), const),
            resident((D_MODEL, D_FF)),
            resident((D_MODEL, D_FF)),
            resident((D_FF, D_MODEL)),
            pl.BlockSpec((1, D_MODEL), const),
        ],
        out_specs=pl.BlockSpec((tm, D_MODEL), row),
        out_shape=jax.ShapeDtypeStruct((rows, D_MODEL), F32),
        compiler_params=pltpu.CompilerParams(
            dimension_semantics=("arbitrary",), vmem_limit_bytes=VMEM_LIMIT),
        name="outffn",
    )(h, g2, wg, wu, wd, gf)


def _key_features(pos, n_sel):
    f = np.zeros((pos.shape[0], LANES + n_sel), np.float32)
    f[:, HEAD_DIM] = pos // SEL_BLOCK
    f[:, HEAD_DIM + 1] = pos % SEL_BLOCK
    if n_sel:
        f[np.arange(pos.shape[0]), LANES + pos // SEL_BLOCK] = 1.0
    return f


def _query_features():
    f = np.zeros((N_KV_HEADS, GQA * TQ, HEAD_DIM), np.float32)
    for k in range(N_KV_HEADS):
        for g in range(GQA):
            slope = 2.0 ** (-8.0 * (k * GQA + g + 1) / N_HEADS)
            f[k, g * TQ:(g + 1) * TQ, 0] = slope * SEL_BLOCK
            f[k, g * TQ:(g + 1) * TQ, 1] = slope
            f[k, g * TQ:(g + 1) * TQ, 2] = MASK_BIAS
    return f


def _causal_masks(ncmp):
    r = np.arange(GQA * TQ)[:, None] % TQ
    cmask = (np.arange(ncmp)[None, :] - ((r + 1) // CMP_STRIDE)).astype(np.int32)
    c = np.arange(TK)[None, :]
    tile_masks = [np.where(c - r <= d * TQ, 0.0, NEG).astype(np.float32) for d in range(max(TK // TQ, 1))]
    return cmask, np.stack(tile_masks)


def _gate_expand():
    gw = GQA * HEAD_DIM
    e = np.zeros((N_KV_HEADS, 2 * _G_W, N_BRANCH * gw), np.float32)
    for k in range(N_KV_HEADS):
        for g in range(GQA):
            for br in range(N_BRANCH):
                for half in range(2):
                    e[k, half * _G_W + (k * GQA + g) * N_BRANCH + br,
                      br * gw + g * HEAD_DIM:br * gw + (g + 1) * HEAD_DIM] = 1.0
    return e


def _overlap(ncmp_pad, n_sel):
    nc = ncmp_pad - 1
    c_start = CMP_STRIDE * np.arange(nc)
    c_end = c_start + CMP_BLOCK - 1
    s_start = SEL_BLOCK * np.arange(n_sel)
    ov = np.minimum(c_end[:, None] + 1, s_start[None, :] + SEL_BLOCK) - np.maximum(c_start[:, None], s_start[None, :])
    out = np.zeros((ncmp_pad, n_sel), np.float32)
    out[:nc] = np.clip(ov, 0, None).astype(np.float32) / CMP_BLOCK
    return out


def _layer(x2, B, S, norm1_g, w_in, conv_w, k_cmp_pos, k_cmp_w1, k_cmp_w2, v_cmp_pos, v_cmp_w1, v_cmp_w2,
           gn_conv_g, later_weights):
    n_sel = S // SEL_BLOCK
    nch = S // CMP_STRIDE
    hd = HEAD_DIM

    kfeat = jnp.asarray(_key_features(np.arange(S), n_sel), BF16)
    mixed_c, q, gates, kvc, ks_aug, vs, kw_aug, vw, *later_bf16 = _inproj(
        x2, norm1_g.reshape(1, D_MODEL), w_in.astype(BF16), kfeat, conv_w, gn_conv_g.reshape(1, CONV_CH),
        later_weights)

    w1 = jnp.stack([k_cmp_w1, v_cmp_w1]).astype(BF16)
    w2 = jnp.stack([k_cmp_w2, v_cmp_w2]).astype(BF16)
    pos = jnp.stack([k_cmp_pos, v_cmp_pos]).reshape(2, 1, CMP_BLOCK * hd)
    pos = jnp.broadcast_to(pos, (2, SUBLANES, CMP_BLOCK * hd)).astype(BF16)
    c_end = CMP_STRIDE * np.arange(nch) + CMP_BLOCK - 1
    cfeat = jnp.asarray(_key_features(c_end, 0), BF16)
    kc_aug, vc = _compress(kvc, pos, w1, w2, cfeat, B, S)

    pad_rows = np.zeros((WINDOW, LANES), np.float32)
    pad_rows[:, hd + 2] = 1.0
    pad_k = jnp.broadcast_to(jnp.asarray(pad_rows, BF16), (N_KV_HEADS, B, WINDOW, LANES))
    kw_aug = jnp.concatenate([pad_k, kw_aug.reshape(N_KV_HEADS, B, S, LANES)], axis=2)
    vw = jnp.pad(vw.reshape(N_KV_HEADS, B, S, VW), ((0, 0), (0, 0), (WINDOW, 0), (0, 0)))
    cmask, tile_masks = _causal_masks(nch)

    nsa_out = _nsa(q, jnp.asarray(_query_features(), BF16), kc_aug, vc,
                   ks_aug.reshape(N_KV_HEADS, B, S, 2 * LANES), vs.reshape(N_KV_HEADS, B, S, LANES),
                   kw_aug, vw, jnp.asarray(_overlap(nch, n_sel), BF16), jnp.asarray(cmask),
                   jnp.asarray(tile_masks), jnp.asarray(_gate_expand(), BF16), gates, B, S)
    return mixed_c, nsa_out, later_bf16


def kernel(x, norm1_g, w_in, conv_w, k_cmp_pos, k_cmp_w1, k_cmp_w2, v_cmp_pos, v_cmp_w1, v_cmp_w2,
           gn_conv_g, gn_nsa_g, w_out, norm2_g, w_gate, w_up, w_down, norm_f_g):
    B, S, _ = x.shape
    depth = norm1_g.shape[0]
    assert depth == 1, "the final norm is fused into the (single) layer's FFN kernel"
    x2 = x.reshape(B * S, D_MODEL)
    mixed_c, nsa_out, (wo, wg, wu, wd) = _layer(
        x2, B, S, norm1_g[0], w_in[0], conv_w[0], k_cmp_pos[0], k_cmp_w1[0], k_cmp_w2[0], v_cmp_pos[0],
        v_cmp_w1[0], v_cmp_w2[0], gn_conv_g[0], [w_out[0], w_gate[0], w_up[0], w_down[0]])
    h = _outproj(mixed_c, nsa_out, x2, gn_nsa_g[0].reshape(1, N_HEADS * HEAD_DIM), wo)
    out = _outffn(h, norm2_g[0].reshape(1, D_MODEL), wg, wu, wd, norm_f_g.reshape(1, D_MODEL))
    return out.reshape(B, S, D_MODEL)
```

```python
import functools

import numpy as np
import jax
import jax.numpy as jnp
from jax import lax
from jax.experimental import pallas as pl
from jax.experimental.pallas import tpu as pltpu

D_MODEL = 1024
CONV_CH = 512
CONV_K = 3
N_HEADS = 8
HEAD_DIM = 64
N_KV_HEADS = 2
GQA = 4
N_BRANCH = 3
CMP_BLOCK = 32
CMP_STRIDE = 16
CMP_HIDDEN = 256
SEL_BLOCK = 64
SEL_TOP = 16
WINDOW = 512
D_FF = 2816
NORM_EPS = 1e-6

F32 = jnp.float32
BF16 = jnp.bfloat16

LANES = 128
VW = 2 * LANES
SUBLANES = 8
VMEM_LIMIT = 56 * 1024 * 1024

TQ = 256
TK = 256
NEG = -1e30
MASK_BIAS = -float(2 ** 30)

_CONV_W = 3 * CONV_CH
_Q_OFF = _CONV_W
_KVC_OFF = _Q_OFF + N_HEADS * HEAD_DIM
_KV_OFF = _KVC_OFF + 2 * N_KV_HEADS * HEAD_DIM
_G_OFF = _KV_OFF + 4 * N_KV_HEADS * HEAD_DIM
_G_W = N_HEADS * N_BRANCH
_PROJ_W = _G_OFF + _G_W


def _rms(x, g):
    return x * lax.rsqrt(jnp.mean(x * x, axis=-1, keepdims=True) + NORM_EPS) * g


def _dot(a, b):
    return jnp.dot(a, b, preferred_element_type=F32)


def _value_rows(v2, h, wide):
    lane = lax.broadcasted_iota(jnp.int32, v2.shape, 1)
    swapped = pltpu.roll(v2, HEAD_DIM, axis=1)
    if not wide:
        return jnp.where(lane < HEAD_DIM, v2 if h == 0 else swapped, 1.0).astype(BF16)
    dup = jnp.where((lane < HEAD_DIM) == (h == 0), v2, swapped)
    return jnp.concatenate([dup.astype(BF16), jnp.ones(v2.shape, BF16)], axis=1)


def _dot_nt(a, b):
    return lax.dot_general(a, b, (((1,), (1,)), ((), ())), preferred_element_type=F32)


def _inproj_kernel(seq_tiles, n_cast, x_ref, xp_ref, g_ref, w_ref, kf_ref, cw_ref, gc_ref, *refs):
    cast_in, refs = refs[:n_cast], refs[n_cast:]
    (mixc_ref, q_ref, gate_ref, kvc_ref, ks_ref, vs_ref, kw_ref, vw_ref), cast_out = refs[:8], refs[8:]
    for src_ref, dst_ref in zip(cast_in, cast_out):
        dst_ref[...] = src_ref[...].astype(BF16)

    u = _rms(x_ref[...], g_ref[...]).astype(BF16)

    cbch = _dot(u, w_ref[:, 0:_CONV_W])
    cu = cbch[:, CONV_CH:2 * CONV_CH] * cbch[:, 2 * CONV_CH:3 * CONV_CH]
    up = _rms(xp_ref[...], g_ref[...]).astype(BF16)
    chp = _dot(up, w_ref[:, CONV_CH:_CONV_W])
    starts_sequence = pl.program_id(0) % seq_tiles == 0
    prev = jnp.where(starts_sequence, 0.0, chp[:, 0:CONV_CH] * chp[:, CONV_CH:2 * CONV_CH])
    r = lax.broadcasted_iota(jnp.int32, cu.shape, 0)
    cu1 = jnp.where(r == 0, prev[7:8, :], pltpu.roll(cu, 1, axis=0))
    cu2 = pltpu.roll(cu, 2, axis=0)
    cu2 = jnp.where(r == 0, prev[6:7, :], jnp.where(r == 1, prev[7:8, :], cu2))
    cw = cw_ref[...]
    y = cbch[:, 0:CONV_CH] * (cw[0:1, :] * cu2 + cw[1:2, :] * cu1 + cw[2:3, :] * cu)
    mixc_ref[...] = _rms(y, gc_ref[...]).astype(BF16)

    q_ref[...] = _dot(u, w_ref[:, _Q_OFF:_KVC_OFF]).astype(BF16)
    kvc_ref[...] = _dot(u, w_ref[:, _KVC_OFF:_KV_OFF])
    gate_ref[...] = jax.nn.sigmoid(_dot(u, w_ref[:, _G_OFF:_PROJ_W]))
    kv = _dot(u, w_ref[:, _KV_OFF:_G_OFF])
    kf = kf_ref[...]
    hd, kvw = HEAD_DIM, N_KV_HEADS * HEAD_DIM
    for h in range(N_KV_HEADS):
        ks_ref[h] = kf
        ks_ref[h, :, 0:hd] = kv[:, h * hd:(h + 1) * hd].astype(BF16)
        vs_ref[h] = _value_rows(kv[:, kvw:2 * kvw], h, False)
        kw_ref[h] = kf[:, 0:LANES]
        kw_ref[h, :, 0:hd] = kv[:, 2 * kvw + h * hd:2 * kvw + (h + 1) * hd].astype(BF16)
        vw_ref[h] = _value_rows(kv[:, 3 * kvw:4 * kvw], h, True)


def _inproj(x2, g, w, kfeat, conv_w, gc, to_cast, tm=1024):
    rows = x2.shape[0]
    steps = rows // tm
    seq_tiles = kfeat.shape[0] // tm
    row = lambda i: (i, 0)
    head_row = lambda i: (0, i, 0)
    const = lambda i: (0, 0)
    cast_specs = [pl.BlockSpec((a.shape[0] // steps, a.shape[1]), row) for a in to_cast]
    return pl.pallas_call(
        functools.partial(_inproj_kernel, seq_tiles, len(to_cast)),
        grid=(steps,),
        in_specs=[
            pl.BlockSpec((tm, D_MODEL), row),
            pl.BlockSpec((SUBLANES, D_MODEL), lambda i: (jnp.maximum(i * (tm // SUBLANES) - 1, 0), 0)),
            pl.BlockSpec((1, D_MODEL), const),
            pl.BlockSpec((D_MODEL, _PROJ_W), const, pipeline_mode=pl.Buffered(1)),
            pl.BlockSpec((tm, 2 * LANES), lambda i: (i % seq_tiles, 0)),
            pl.BlockSpec((CONV_K, CONV_CH), const),
            pl.BlockSpec((1, CONV_CH), const),
        ] + cast_specs,
        out_specs=[
            pl.BlockSpec((tm, CONV_CH), row),
            pl.BlockSpec((tm, N_HEADS * HEAD_DIM), row),
            pl.BlockSpec((tm, _G_W), row),
            pl.BlockSpec((tm, _KV_OFF - _KVC_OFF), row),
            pl.BlockSpec((N_KV_HEADS, tm, 2 * LANES), head_row),
            pl.BlockSpec((N_KV_HEADS, tm, LANES), head_row),
            pl.BlockSpec((N_KV_HEADS, tm, LANES), head_row),
            pl.BlockSpec((N_KV_HEADS, tm, VW), head_row),
        ] + cast_specs,
        out_shape=[
            jax.ShapeDtypeStruct((rows, CONV_CH), BF16),
            jax.ShapeDtypeStruct((rows, N_HEADS * HEAD_DIM), BF16),
            jax.ShapeDtypeStruct((rows, _G_W), F32),
            jax.ShapeDtypeStruct((rows, _KV_OFF - _KVC_OFF), F32),
            jax.ShapeDtypeStruct((N_KV_HEADS, rows, 2 * LANES), BF16),
            jax.ShapeDtypeStruct((N_KV_HEADS, rows, LANES), BF16),
            jax.ShapeDtypeStruct((N_KV_HEADS, rows, LANES), BF16),
            jax.ShapeDtypeStruct((N_KV_HEADS, rows, VW), BF16),
        ] + [jax.ShapeDtypeStruct(a.shape, BF16) for a in to_cast],
        compiler_params=pltpu.CompilerParams(
            dimension_semantics=("arbitrary",), vmem_limit_bytes=VMEM_LIMIT),
        name="inproj",
    )(x2, x2, g, w, kfeat, conv_w, gc, *to_cast)


def _compress_kernel(xk_ref, xv_ref, pos_ref, w1_ref, w2_ref, cf_ref, kc_ref, vc_ref):
    nch = xk_ref.shape[0] // CMP_STRIDE
    hd = HEAD_DIM
    first = [jnp.zeros((nch, CMP_HIDDEN), F32) for _ in range(2 * N_KV_HEADS)]
    second = [jnp.zeros((nch, CMP_HIDDEN), F32) for _ in range(2 * N_KV_HEADS)]
    group = 2 * LANES // hd
    for l0 in range(0, CMP_STRIDE, group):
        for s, x_ref in enumerate((xk_ref, xv_ref)):
            xs = [x_ref[pl.ds(l0 + j, nch, stride=CMP_STRIDE), :].astype(BF16) for j in range(group)]
            for h in range(N_KV_HEADS):
                c = s * N_KV_HEADS + h
                xh = jnp.concatenate([x[:, h * hd:(h + 1) * hd] for x in xs], axis=1)
                first[c] = first[c] + _dot(xh, w1_ref[s, l0 * hd:(l0 + group) * hd, :])
                second[c] = second[c] + _dot(xh, w1_ref[s, (CMP_STRIDE + l0) * hd:(CMP_STRIDE + l0 + group) * hd, :])
    for s in range(2):
        pb = _dot(pos_ref[s], w1_ref[s])[0:1, :]
        outs = []
        for h in range(N_KV_HEADS):
            c = s * N_KV_HEADS + h
            hid = first[c] + pltpu.roll(second[c], nch - 1, axis=0) + pb
            outs.append(_dot(jax.nn.gelu(hid, approximate=True).astype(BF16), w2_ref[s]))
        for h in range(N_KV_HEADS):
            if s == 0:
                kc_ref[h] = cf_ref[...]
                kc_ref[h, :, 0:hd] = outs[h].astype(BF16)
            else:
                vc_ref[h] = _value_rows(jnp.concatenate(outs, axis=1), h, True)


def _compress(kvc, pos, w1, w2, cfeat, B, S):
    nch = S // CMP_STRIDE
    full = lambda *shape: pl.BlockSpec(shape, lambda b: (0,) * len(shape))
    return pl.pallas_call(
        _compress_kernel,
        grid=(B,),
        in_specs=[
            pl.BlockSpec((S, LANES), lambda b: (b, 0)),
            pl.BlockSpec((S, LANES), lambda b: (b, 1)),
            full(*pos.shape), full(*w1.shape), full(*w2.shape), full(*cfeat.shape),
        ],
        out_specs=[
            pl.BlockSpec((None, N_KV_HEADS, nch, LANES), lambda b: (b, 0, 0, 0)),
            pl.BlockSpec((None, N_KV_HEADS, nch, VW), lambda b: (b, 0, 0, 0)),
        ],
        out_shape=[
            jax.ShapeDtypeStruct((B, N_KV_HEADS, nch, LANES), BF16),
            jax.ShapeDtypeStruct((B, N_KV_HEADS, nch, VW), BF16),
        ],
        compiler_params=pltpu.CompilerParams(
            dimension_semantics=("arbitrary",), vmem_limit_bytes=VMEM_LIMIT),
        name="compress",
    )(kvc, kvc, pos, w1, w2, cfeat)


def _hi_lo(x):
    hi = x.astype(BF16)
    return jnp.concatenate([hi, (x - hi.astype(F32)).astype(BF16)], axis=1)


def _head_layout(acc):
    low_half = lax.broadcasted_iota(jnp.int32, (TQ, LANES), 1) < HEAD_DIM
    head = lambda x, g: x[g * TQ:(g + 1) * TQ]
    if acc.shape[1] == VW:
        x = acc[:, 0:LANES] * (1.0 / jnp.maximum(acc[:, LANES:VW], 1e-30))
        pairs = [jnp.where(low_half, head(x, 2 * j), head(x, 2 * j + 1)) for j in range(GQA // 2)]
    else:
        rot = pltpu.roll(acc, HEAD_DIM, axis=1)
        pairs = [jnp.where(low_half, head(acc, 2 * j) * (1.0 / head(rot, 2 * j)),
                           head(rot, 2 * j + 1) * (1.0 / head(acc, 2 * j + 1))) for j in range(GQA // 2)]
    return jnp.concatenate(pairs, axis=1)


def _exchange(v, i, j, descending):
    hi, lo = jnp.maximum(v[i], v[j]), jnp.minimum(v[i], v[j])
    v[i], v[j] = (hi, lo) if descending else (lo, hi)


def _bitonic_merge_desc(v):
    n = len(v)
    j = n // 2
    while j >= 1:
        for i in range(n):
            if i ^ j > i:
                _exchange(v, i, i ^ j, True)
        j //= 2


def _kth_largest(x, kth, fillers=()):
    n = x.shape[0]
    assert n == kth * SUBLANES and kth & (kth - 1) == 0
    v = [x[r * SUBLANES:(r + 1) * SUBLANES, :] for r in range(kth)]
    fillers = list(fillers)
    layers = 0
    k = 2
    while k <= kth:
        j = k // 2
        while j >= 1:
            for i in range(kth):
                if i ^ j > i:
                    _exchange(v, i, i ^ j, (i & k) == 0)
            j //= 2
            layers += 1
            if layers % 3 == 0 and fillers:
                fillers.pop(0)()
        k *= 2
    shift = SUBLANES // 2
    while shift >= 1:
        v = [jnp.maximum(v[i], pltpu.roll(v[kth - 1 - i], shift, axis=0)) for i in range(kth)]
        if shift > 1:
            _bitonic_merge_desc(v)
        if fillers:
            fillers.pop(0)()
        shift //= 2
    for filler in fillers:
        filler()
    out = v[0]
    for i in range(1, kth):
        out = jnp.minimum(out, v[i])
    return out[0:1, :]


def _nsa_kernel(q_ref, qf_ref, kc_ref, vc_ref, ks_ref, vs_ref, kw_ref, vw_ref, ov_ref, cm_ref, mb_ref,
                ge_ref, g_ref, o_ref, qa_ref, tiles_ref, m_ref, acc_ref, sa_ref, sb_ref):
    i = pl.program_id(2)
    t0 = i * TQ
    rows = GQA * TQ
    n_sel = ov_ref.shape[1]
    ncmp = kc_ref.shape[0]

    q = q_ref[...]
    for g in range(GQA):
        qg = q[:, g * HEAD_DIM:(g + 1) * HEAD_DIM].astype(F32) * (HEAD_DIM ** -0.5)
        qa_ref[g * TQ:(g + 1) * TQ, 0:HEAD_DIM] = qg.astype(BF16)
    qa_ref[:, HEAD_DIM:LANES] = qf_ref[...]
    qa1 = qa_ref[:, 0:LANES]

    diag_ok = (lax.broadcasted_iota(jnp.int32, (rows, TQ), 1)
               <= (lax.broadcasted_iota(jnp.int32, (rows, TQ), 0) & (TQ - 1)))

    wk = TQ + WINDOW
    w0 = pl.multiple_of(t0, TQ)
    win = {}

    def win_scores():
        s_w = _dot_nt(qa1, kw_ref[pl.ds(w0, wk), :])
        win["s"] = (jnp.where(diag_ok, NEG, s_w[:, 0:TQ]),
                    s_w[:, TQ:WINDOW], jnp.where(diag_ok, s_w[:, WINDOW:wk], NEG))

    def win_max():
        s_old, s_mid, s_new = win["s"]
        win["m"] = jnp.maximum(jnp.maximum(jnp.max(s_old, axis=1, keepdims=True),
                                           jnp.max(s_mid, axis=1, keepdims=True)),
                               jnp.max(s_new, axis=1, keepdims=True))

    def win_exp():
        win["e"] = jnp.concatenate([jnp.exp(s - win["m"]) for s in win["s"]], axis=1).astype(BF16)

    def win_pv():
        win["acc"] = _dot(win["e"], vw_ref[pl.ds(w0, wk), :])

    s_c = _dot_nt(qa1, kc_ref[...])
    s_c = jnp.where(cm_ref[...] <= (TQ // CMP_STRIDE) * i - 2, s_c, NEG)
    m_c = jnp.maximum(jnp.max(s_c, axis=1, keepdims=True), 0.5 * NEG)
    e_c = jnp.exp(s_c - m_c)
    acc_c = _dot(e_c.astype(BF16), vc_ref[...])
    inv_c = 1.0 / jnp.maximum(acc_c[:, LANES:VW], 1e-30)
    p_c = e_c * jnp.concatenate([inv_c] * (ncmp // LANES), axis=1)

    p_sum = p_c[0:TQ] + p_c[TQ:2 * TQ] + p_c[2 * TQ:3 * TQ] + p_c[3 * TQ:4 * TQ]
    p_hi = p_sum.astype(BF16)
    p_lo = (p_sum - p_hi.astype(F32)).astype(BF16)
    imp = _dot(p_hi, ov_ref[...]) + _dot(p_lo, ov_ref[...])
    imp_t = imp.T
    jb = lax.broadcasted_iota(jnp.int32, (n_sel, TQ), 0)
    tq = t0 + lax.broadcasted_iota(jnp.int32, (n_sel, TQ), 1)
    imp_t = jnp.where(SEL_BLOCK * jb > tq, -jnp.inf, imp_t)
    forced = (jb == 0) | (jb == (tq >> 6))
    imp_t = jnp.where(forced, jnp.inf, imp_t)

    thr = _kth_largest(imp_t, SEL_TOP, (win_scores, win_max, win_exp, win_pv))
    acc_w = win["acc"]
    above = imp_t > thr
    tied = imp_t == thr
    n_above = jnp.sum(jnp.where(above, 1.0, 0.0), axis=0, keepdims=True)
    lower = (lax.broadcasted_iota(jnp.int32, (n_sel, n_sel), 1)
             < lax.broadcasted_iota(jnp.int32, (n_sel, n_sel), 0))
    tied_before = _dot(jnp.where(lower, 1.0, 0.0).astype(BF16), jnp.where(tied, 1.0, 0.0).astype(BF16))
    take_tie = jnp.where(tied_before + n_above < float(SEL_TOP), 1.0, 0.0)
    sel_t = jnp.where(above, 1.0, jnp.where(tied, take_tie, 0.0))
    sel_q = sel_t.T
    bias = ((1.0 - sel_q) * MASK_BIAS).astype(BF16)
    for g in range(GQA):
        qa_ref[g * TQ:(g + 1) * TQ, LANES:2 * LANES] = bias

    n_past = t0 // TK
    first_mask = (t0 - n_past * TK) // TQ
    tiles_ref[0] = n_past

    def scores(j, s_ref, mask=None):
        k0 = pl.multiple_of(tiles_ref[j] * TK, TK)
        s = _dot_nt(qa_ref[...], ks_ref[pl.ds(k0, TK), :])
        s_ref[...] = s if mask is None else s + mask

    scores(0, sa_ref, mb_ref[first_mask])

    bpt = TK // SEL_BLOCK
    n_tiles = n_sel // bpt
    blk_cnt = _dot(jnp.ones((SUBLANES, TQ), BF16), sel_q.astype(BF16))
    in_tile = (lax.broadcasted_iota(jnp.int32, (n_sel, LANES), 0) // bpt
               == lax.broadcasted_iota(jnp.int32, (n_sel, LANES), 1))
    tile_cnt = _dot(jnp.where(blk_cnt > 0.5, 1.0, 0.0).astype(BF16),
                    jnp.where(in_tile, 1.0, 0.0).astype(BF16))
    lane = lax.broadcasted_iota(jnp.int32, (1, LANES), 1)
    bit = jnp.where(tile_cnt[0:1, :] > 0.5, jnp.left_shift(1, lane & 15).astype(F32), 0.0)
    words = [jnp.sum(jnp.where((lane >= 16 * w) & (lane < 16 * (w + 1)), bit, 0.0)).astype(jnp.int32)
             for w in range((n_tiles + 15) // 16)]
    n_tasks = jnp.int32(1)
    for kt in range(n_tiles - 1):
        used = (words[kt // 16] >> (kt % 16)) & 1
        tiles_ref[n_tasks] = kt
        n_tasks = n_tasks + jnp.where(kt < n_past, used, 0)
    tiles_ref[n_tasks] = 0
    tiles_ref[n_tasks + 1] = 0

    m_ref[...] = jnp.full(m_ref.shape, NEG, F32)
    acc_ref[...] = jnp.zeros(acc_ref.shape, F32)

    gates = _dot(_hi_lo(g_ref[...]), ge_ref[...])
    gw = GQA * HEAD_DIM
    merged = gates[:, 0:gw] * _head_layout(acc_c) + gates[:, 2 * gw:3 * gw] * _head_layout(acc_w)
    gates_sel = gates[:, gw:2 * gw]

    def accumulate(j, s_ref):
        k0 = pl.multiple_of(tiles_ref[j] * TK, TK)
        slot = jnp.where(j < n_tasks, 0, 1)
        s = s_ref[...]
        m = m_ref[slot]
        m_new = jnp.maximum(m, jnp.max(s, axis=1, keepdims=True))
        p = jnp.concatenate([jnp.exp((s[:, c:c + LANES] - m_new).astype(BF16)) for c in range(0, TK, LANES)],
                            axis=1)
        acc_ref[slot] = (jnp.exp(m - m_new) * acc_ref[slot]
                         + _dot(p, vs_ref[pl.ds(k0, TK), :]))
        m_ref[slot] = m_new

    def pair(jj, c):
        a = 2 * jj
        scores(a + 1, sb_ref)
        accumulate(a, sa_ref)
        scores(a + 2, sa_ref)
        accumulate(a + 1, sb_ref)
        return c

    lax.fori_loop(0, (n_tasks + 1) // 2, pair, 0)

    o_ref[...] = (merged + gates_sel * _head_layout(acc_ref[0])).astype(o_ref.dtype)


def _nsa(q, qfeat, kc_aug, vc, ks_aug, vs, kw_aug, vw, ov, cmask, tile_masks, gate_expand, gates, B, S):
    nq = S // TQ
    gw = GQA * HEAD_DIM
    ncmp = kc_aug.shape[2]
    n_sel = ov.shape[1]
    sp = kw_aug.shape[2]
    bk = lambda b, k, i: (b, k, 0, 0)
    kb = lambda b, k, i: (k, b, 0, 0)
    return pl.pallas_call(
        _nsa_kernel,
        grid=(B, N_KV_HEADS, nq),
        in_specs=[
            pl.BlockSpec((TQ, gw), lambda b, k, i: (b * nq + i, k)),
            pl.BlockSpec((None, GQA * TQ, HEAD_DIM), lambda b, k, i: (k, 0, 0)),
            pl.BlockSpec((None, None, ncmp, LANES), bk),
            pl.BlockSpec((None, None, ncmp, VW), bk),
            pl.BlockSpec((None, None, S, 2 * LANES), kb),
            pl.BlockSpec((None, None, S, LANES), kb),
            pl.BlockSpec((None, None, sp, LANES), kb),
            pl.BlockSpec((None, None, sp, VW), kb),
            pl.BlockSpec((ncmp, n_sel), lambda b, k, i: (0, 0)),
            pl.BlockSpec((GQA * TQ, ncmp), lambda b, k, i: (0, 0)),
            pl.BlockSpec(tile_masks.shape, lambda b, k, i: (0, 0, 0)),
            pl.BlockSpec((None,) + gate_expand.shape[1:], lambda b, k, i: (k, 0, 0)),
            pl.BlockSpec((TQ, _G_W), lambda b, k, i: (b * nq + i, 0)),
        ],
        out_specs=pl.BlockSpec((TQ, gw), lambda b, k, i: (b * nq + i, k)),
        out_shape=jax.ShapeDtypeStruct((B * S, N_KV_HEADS * gw), BF16),
        scratch_shapes=[
            pltpu.VMEM((GQA * TQ, 2 * LANES), BF16),
            pltpu.SMEM((S // TK + 2,), jnp.int32),
            pltpu.VMEM((2, GQA * TQ, LANES), F32),
            pltpu.VMEM((2, GQA * TQ, LANES), F32),
            pltpu.VMEM((GQA * TQ, TK), F32),
            pltpu.VMEM((GQA * TQ, TK), F32),
        ],
        compiler_params=pltpu.CompilerParams(
            dimension_semantics=("arbitrary",) * 3, vmem_limit_bytes=VMEM_LIMIT),
        name="nsa",
    )(q, qfeat, kc_aug, vc, ks_aug, vs, kw_aug, vw, ov, cmask, tile_masks, gate_expand, gates)


_FF_CHUNK = 768


def _outproj_kernel(mc_ref, nsa_ref, x_ref, gn_ref, wo_ref, h_ref):
    mixed_n = _rms(nsa_ref[...].astype(F32), gn_ref[...]).astype(BF16)
    h_ref[...] = (x_ref[...] + _dot(mc_ref[...], wo_ref[0:CONV_CH, :])
                  + _dot(mixed_n, wo_ref[CONV_CH:2 * CONV_CH, :]))


def _outproj(mixed_c, nsa_out, x2, gn, wo, tm=1024):
    rows = x2.shape[0]
    row = lambda i: (i, 0)
    const = lambda i: (0, 0)
    return pl.pallas_call(
        _outproj_kernel,
        grid=(rows // tm,),
        in_specs=[
            pl.BlockSpec((tm, CONV_CH), row),
            pl.BlockSpec((tm, N_HEADS * HEAD_DIM), row),
            pl.BlockSpec((tm, D_MODEL), row),
            pl.BlockSpec((1, N_HEADS * HEAD_DIM), const),
            pl.BlockSpec((2 * CONV_CH, D_MODEL), const, pipeline_mode=pl.Buffered(1)),
        ],
        out_specs=pl.BlockSpec((tm, D_MODEL), row),
        out_shape=jax.ShapeDtypeStruct((rows, D_MODEL), F32),
        compiler_params=pltpu.CompilerParams(
            dimension_semantics=("arbitrary",), vmem_limit_bytes=VMEM_LIMIT),
        name="outproj",
    )(mixed_c, nsa_out, x2, gn, wo)


def _outffn_kernel(h_ref, g2_ref, wg_ref, wu_ref, wd_ref, gf_ref, o_ref):
    h = h_ref[...]
    u = _rms(h, g2_ref[...]).astype(BF16)
    acc = h
    for c in range(0, D_FF, _FF_CHUNK):
        sl = slice(c, min(c + _FF_CHUNK, D_FF))
        a = jax.nn.silu(_dot(u, wg_ref[:, sl])) * _dot(u, wu_ref[:, sl])
        acc = acc + _dot(a.astype(BF16), wd_ref[sl, :])
    o_ref[...] = _rms(acc, gf_ref[...])


def _outffn(h, g2, wg, wu, wd, gf, tm=512):
    rows = h.shape[0]
    row = lambda i: (i, 0)
    const = lambda i: (0, 0)
    resident = lambda shape: pl.BlockSpec(shape, const, pipeline_mode=pl.Buffered(1))
    return pl.pallas_call(
        _outffn_kernel,
        grid=(rows // tm,),
        in_specs=[
            pl.BlockSpec((tm, D_MODEL), row),
            pl.BlockSpec((1, D_MODEL), const),
            resident((D_MODEL, D_FF)),
            resident((D_MODEL, D_FF)),
            resident((D_FF, D_MODEL)),
            pl.BlockSpec((1, D_MODEL), const),
        ],
        out_specs=pl.BlockSpec((tm, D_MODEL), row),
        out_shape=jax.ShapeDtypeStruct((rows, D_MODEL), F32),
        compiler_params=pltpu.CompilerParams(
            dimension_semantics=("arbitrary",), vmem_limit_bytes=VMEM_LIMIT),
        name="outffn",
    )(h, g2, wg, wu, wd, gf)


def _key_features(pos, n_sel):
    f = np.zeros((pos.shape[0], LANES + n_sel), np.float32)
    f[:, HEAD_DIM] = pos // SEL_BLOCK
    f[:, HEAD_DIM + 1] = pos % SEL_BLOCK
    if n_sel:
        f[np.arange(pos.shape[0]), LANES + pos // SEL_BLOCK] = 1.0
    return f


def _query_features():
    f = np.zeros((N_KV_HEADS, GQA * TQ, HEAD_DIM), np.float32)
    for k in range(N_KV_HEADS):
        for g in range(GQA):
            slope = 2.0 ** (-8.0 * (k * GQA + g + 1) / N_HEADS)
            f[k, g * TQ:(g + 1) * TQ, 0] = slope * SEL_BLOCK
            f[k, g * TQ:(g + 1) * TQ, 1] = slope
            f[k, g * TQ:(g + 1) * TQ, 2] = MASK_BIAS
    return f


def _causal_masks(ncmp):
    r = np.arange(GQA * TQ)[:, None] % TQ
    cmask = (np.arange(ncmp)[None, :] - ((r + 1) // CMP_STRIDE)).astype(np.int32)
    c = np.arange(TK)[None, :]
    tile_masks = [np.where(c - r <= d * TQ, 0.0, NEG).astype(np.float32) for d in range(max(TK // TQ, 1))]
    return cmask, np.stack(tile_masks)


def _gate_expand():
    gw = GQA * HEAD_DIM
    e = np.zeros((N_KV_HEADS, 2 * _G_W, N_BRANCH * gw), np.float32)
    for k in range(N_KV_HEADS):
        for g in range(GQA):
            for br in range(N_BRANCH):
                for half in range(2):
                    e[k, half * _G_W + (k * GQA + g) * N_BRANCH + br,
                      br * gw + g * HEAD_DIM:br * gw + (g + 1) * HEAD_DIM] = 1.0
    return e


def _overlap(ncmp_pad, n_sel):
    nc = ncmp_pad - 1
    c_start = CMP_STRIDE * np.arange(nc)
    c_end = c_start + CMP_BLOCK - 1
    s_start = SEL_BLOCK * np.arange(n_sel)
    ov = np.minimum(c_end[:, None] + 1, s_start[None, :] + SEL_BLOCK) - np.maximum(c_start[:, None], s_start[None, :])
    out = np.zeros((ncmp_pad, n_sel), np.float32)
    out[:nc] = np.clip(ov, 0, None).astype(np.float32) / CMP_BLOCK
    return out


def _layer(x2, B, S, norm1_g, w_in, conv_w, k_cmp_pos, k_cmp_w1, k_cmp_w2, v_cmp_pos, v_cmp_w1, v_cmp_w2,
           gn_conv_g, later_weights):
    n_sel = S // SEL_BLOCK
    nch = S // CMP_STRIDE
    hd = HEAD_DIM

    kfeat = jnp.asarray(_key_features(np.arange(S), n_sel), BF16)
    mixed_c, q, gates, kvc, ks_aug, vs, kw_aug, vw, *later_bf16 = _inproj(
        x2, norm1_g.reshape(1, D_MODEL), w_in.astype(BF16), kfeat, conv_w, gn_conv_g.reshape(1, CONV_CH),
        later_weights)

    w1 = jnp.stack([k_cmp_w1, v_cmp_w1]).astype(BF16)
    w2 = jnp.stack([k_cmp_w2, v_cmp_w2]).astype(BF16)
    pos = jnp.stack([k_cmp_pos, v_cmp_pos]).reshape(2, 1, CMP_BLOCK * hd)
    pos = jnp.broadcast_to(pos, (2, SUBLANES, CMP_BLOCK * hd)).astype(BF16)
    c_end = CMP_STRIDE * np.arange(nch) + CMP_BLOCK - 1
    cfeat = jnp.asarray(_key_features(c_end, 0), BF16)
    kc_aug, vc = _compress(kvc, pos, w1, w2, cfeat, B, S)

    pad_rows = np.zeros((WINDOW, LANES), np.float32)
    pad_rows[:, hd + 2] = 1.0
    pad_k = jnp.broadcast_to(jnp.asarray(pad_rows, BF16), (N_KV_HEADS, B, WINDOW, LANES))
    kw_aug = jnp.concatenate([pad_k, kw_aug.reshape(N_KV_HEADS, B, S, LANES)], axis=2)
    vw = jnp.pad(vw.reshape(N_KV_HEADS, B, S, VW), ((0, 0), (0, 0), (WINDOW, 0), (0, 0)))
    cmask, tile_masks = _causal_masks(nch)

    nsa_out = _nsa(q, jnp.asarray(_query_features(), BF16), kc_aug, vc,
                   ks_aug.reshape(N_KV_HEADS, B, S, 2 * LANES), vs.reshape(N_KV_HEADS, B, S, LANES),
                   kw_aug, vw, jnp.asarray(_overlap(nch, n_sel), BF16), jnp.asarray(cmask),
                   jnp.asarray(tile_masks), jnp.asarray(_gate_expand(), BF16), gates, B, S)
    return mixed_c, nsa_out, later_bf16


def kernel(x, norm1_g, w_in, conv_w, k_cmp_pos, k_cmp_w1, k_cmp_w2, v_cmp_pos, v_cmp_w1, v_cmp_w2,
           gn_conv_g, gn_nsa_g, w_out, norm2_g, w_gate, w_up, w_down, norm_f_g):
    B, S, _ = x.shape
    depth = norm1_g.shape[0]
    assert depth == 1, "the final norm is fused into the (single) layer's FFN kernel"
    x2 = x.reshape(B * S, D_MODEL)
    mixed_c, nsa_out, (wo, wg, wu, wd) = _layer(
        x2, B, S, norm1_g[0], w_in[0], conv_w[0], k_cmp_pos[0], k_cmp_w1[0], k_cmp_w2[0], v_cmp_pos[0],
        v_cmp_w1[0], v_cmp_w2[0], gn_conv_g[0], [w_out[0], w_gate[0], w_up[0], w_down[0]])
    h = _outproj(mixed_c, nsa_out, x2, gn_nsa_g[0].reshape(1, N_HEADS * HEAD_DIM), wo)
    out = _outffn(h, norm2_g[0].reshape(1, D_MODEL), wg, wu, wd, norm_f_g.reshape(1, D_MODEL))
    return out.reshape(B, S, D_MODEL)
```

```python
import functools

import numpy as np
import jax
import jax.numpy as jnp
from jax import lax
from jax.experimental import pallas as pl
from jax.experimental.pallas import tpu as pltpu

D_MODEL = 1024
CONV_CH = 512
CONV_K = 3
N_HEADS = 8
HEAD_DIM = 64
N_KV_HEADS = 2
GQA = 4
N_BRANCH = 3
CMP_BLOCK = 32
CMP_STRIDE = 16
CMP_HIDDEN = 256
SEL_BLOCK = 64
SEL_TOP = 16
WINDOW = 512
D_FF = 2816
NORM_EPS = 1e-6

F32 = jnp.float32
BF16 = jnp.bfloat16

LANES = 128
VW = 2 * LANES
SUBLANES = 8
VMEM_LIMIT = 56 * 1024 * 1024

TQ = 256
TK = 256
NEG = -1e30
MASK_BIAS = -float(2 ** 30)

_CONV_W = 3 * CONV_CH
_Q_OFF = _CONV_W
_KVC_OFF = _Q_OFF + N_HEADS * HEAD_DIM
_KV_OFF = _KVC_OFF + 2 * N_KV_HEADS * HEAD_DIM
_G_OFF = _KV_OFF + 4 * N_KV_HEADS * HEAD_DIM
_G_W = N_HEADS * N_BRANCH
_PROJ_W = _G_OFF + _G_W


def _rms(x, g):
    return x * lax.rsqrt(jnp.mean(x * x, axis=-1, keepdims=True) + NORM_EPS) * g


def _dot(a, b):
    return jnp.dot(a, b, preferred_element_type=F32)


def _value_rows(v2, h, wide):
    lane = lax.broadcasted_iota(jnp.int32, v2.shape, 1)
    swapped = pltpu.roll(v2, HEAD_DIM, axis=1)
    if not wide:
        return jnp.where(lane < HEAD_DIM, v2 if h == 0 else swapped, 1.0).astype(BF16)
    dup = jnp.where((lane < HEAD_DIM) == (h == 0), v2, swapped)
    return jnp.concatenate([dup.astype(BF16), jnp.ones(v2.shape, BF16)], axis=1)


def _dot_nt(a, b):
    return lax.dot_general(a, b, (((1,), (1,)), ((), ())), preferred_element_type=F32)


def _inproj_kernel(seq_tiles, n_cast, x_ref, xp_ref, g_ref, w_ref, kf_ref, cw_ref, gc_ref, *refs):
    cast_in, refs = refs[:n_cast], refs[n_cast:]
    (mixc_ref, q_ref, gate_ref, kvc_ref, ks_ref, vs_ref, kw_ref, vw_ref), cast_out = refs[:8], refs[8:]
    for src_ref, dst_ref in zip(cast_in, cast_out):
        dst_ref[...] = src_ref[...].astype(BF16)

    u = _rms(x_ref[...], g_ref[...]).astype(BF16)

    cbch = _dot(u, w_ref[:, 0:_CONV_W])
    cu = cbch[:, CONV_CH:2 * CONV_CH] * cbch[:, 2 * CONV_CH:3 * CONV_CH]
    up = _rms(xp_ref[...], g_ref[...]).astype(BF16)
    chp = _dot(up, w_ref[:, CONV_CH:_CONV_W])
    starts_sequence = pl.program_id(0) % seq_tiles == 0
    prev = jnp.where(starts_sequence, 0.0, chp[:, 0:CONV_CH] * chp[:, CONV_CH:2 * CONV_CH])
    r = lax.broadcasted_iota(jnp.int32, cu.shape, 0)
    cu1 = jnp.where(r == 0, prev[7:8, :], pltpu.roll(cu, 1, axis=0))
    cu2 = pltpu.roll(cu, 2, axis=0)
    cu2 = jnp.where(r == 0, prev[6:7, :], jnp.where(r == 1, prev[7:8, :], cu2))
    cw = cw_ref[...]
    y = cbch[:, 0:CONV_CH] * (cw[0:1, :] * cu2 + cw[1:2, :] * cu1 + cw[2:3, :] * cu)
    mixc_ref[...] = _rms(y, gc_ref[...]).astype(BF16)

    q_ref[...] = _dot(u, w_ref[:, _Q_OFF:_KVC_OFF]).astype(BF16)
    kvc_ref[...] = _dot(u, w_ref[:, _KVC_OFF:_KV_OFF])
    gate_ref[...] = jax.nn.sigmoid(_dot(u, w_ref[:, _G_OFF:_PROJ_W]))
    kv = _dot(u, w_ref[:, _KV_OFF:_G_OFF])
    kf = kf_ref[...]
    hd, kvw = HEAD_DIM, N_KV_HEADS * HEAD_DIM
    for h in range(N_KV_HEADS):
        ks_ref[h] = kf
        ks_ref[h, :, 0:hd] = kv[:, h * hd:(h + 1) * hd].astype(BF16)
        vs_ref[h] = _value_rows(kv[:, kvw:2 * kvw], h, False)
        kw_ref[h] = kf[:, 0:LANES]
        kw_ref[h, :, 0:hd] = kv[:, 2 * kvw + h * hd:2 * kvw + (h + 1) * hd].astype(BF16)
        vw_ref[h] = _value_rows(kv[:, 3 * kvw:4 * kvw], h, True)


def _inproj(x2, g, w, kfeat, conv_w, gc, to_cast, tm=1024):
    rows = x2.shape[0]
    steps = rows // tm
    seq_tiles = kfeat.shape[0] // tm
    row = lambda i: (i, 0)
    head_row = lambda i: (0, i, 0)
    const = lambda i: (0, 0)
    cast_specs = [pl.BlockSpec((a.shape[0] // steps, a.shape[1]), row) for a in to_cast]
    return pl.pallas_call(
        functools.partial(_inproj_kernel, seq_tiles, len(to_cast)),
        grid=(steps,),
        in_specs=[
            pl.BlockSpec((tm, D_MODEL), row),
            pl.BlockSpec((SUBLANES, D_MODEL), lambda i: (jnp.maximum(i * (tm // SUBLANES) - 1, 0), 0)),
            pl.BlockSpec((1, D_MODEL), const),
            pl.BlockSpec((D_MODEL, _PROJ_W), const, pipeline_mode=pl.Buffered(1)),
            pl.BlockSpec((tm, 2 * LANES), lambda i: (i % seq_tiles, 0)),
            pl.BlockSpec((CONV_K, CONV_CH), const),
            pl.BlockSpec((1, CONV_CH), const),
        ] + cast_specs,
        out_specs=[
            pl.BlockSpec((tm, CONV_CH), row),
            pl.BlockSpec((tm, N_HEADS * HEAD_DIM), row),
            pl.BlockSpec((tm, _G_W), row),
            pl.BlockSpec((tm, _KV_OFF - _KVC_OFF), row),
            pl.BlockSpec((N_KV_HEADS, tm, 2 * LANES), head_row),
            pl.BlockSpec((N_KV_HEADS, tm, LANES), head_row),
            pl.BlockSpec((N_KV_HEADS, tm, LANES), head_row),
            pl.BlockSpec((N_KV_HEADS, tm, VW), head_row),
        ] + cast_specs,
        out_shape=[
            jax.ShapeDtypeStruct((rows, CONV_CH), BF16),
            jax.ShapeDtypeStruct((rows, N_HEADS * HEAD_DIM), BF16),
            jax.ShapeDtypeStruct((rows, _G_W), F32),
            jax.ShapeDtypeStruct((rows, _KV_OFF - _KVC_OFF), F32),
            jax.ShapeDtypeStruct((N_KV_HEADS, rows, 2 * LANES), BF16),
            jax.ShapeDtypeStruct((N_KV_HEADS, rows, LANES), BF16),
            jax.ShapeDtypeStruct((N_KV_HEADS, rows, LANES), BF16),
            jax.ShapeDtypeStruct((N_KV_HEADS, rows, VW), BF16),
        ] + [jax.ShapeDtypeStruct(a.shape, BF16) for a in to_cast],
        compiler_params=pltpu.CompilerParams(
            dimension_semantics=("arbitrary",), vmem_limit_bytes=VMEM_LIMIT),
        name="inproj",
    )(x2, x2, g, w, kfeat, conv_w, gc, *to_cast)


def _compress_kernel(xk_ref, xv_ref, pos_ref, w1_ref, w2_ref, cf_ref, kc_ref, vc_ref):
    nch = xk_ref.shape[0] // CMP_STRIDE
    hd = HEAD_DIM
    first = [jnp.zeros((nch, CMP_HIDDEN), F32) for _ in range(2 * N_KV_HEADS)]
    second = [jnp.zeros((nch, CMP_HIDDEN), F32) for _ in range(2 * N_KV_HEADS)]
    group = 2 * LANES // hd
    for l0 in range(0, CMP_STRIDE, group):
        for s, x_ref in enumerate((xk_ref, xv_ref)):
            xs = [x_ref[pl.ds(l0 + j, nch, stride=CMP_STRIDE), :].astype(BF16) for j in range(group)]
            for h in range(N_KV_HEADS):
                c = s * N_KV_HEADS + h
                xh = jnp.concatenate([x[:, h * hd:(h + 1) * hd] for x in xs], axis=1)
                first[c] = first[c] + _dot(xh, w1_ref[s, l0 * hd:(l0 + group) * hd, :])
                second[c] = second[c] + _dot(xh, w1_ref[s, (CMP_STRIDE + l0) * hd:(CMP_STRIDE + l0 + group) * hd, :])
    for s in range(2):
        pb = _dot(pos_ref[s], w1_ref[s])[0:1, :]
        outs = []
        for h in range(N_KV_HEADS):
            c = s * N_KV_HEADS + h
            hid = first[c] + pltpu.roll(second[c], nch - 1, axis=0) + pb
            outs.append(_dot(jax.nn.gelu(hid, approximate=True).astype(BF16), w2_ref[s]))
        for h in range(N_KV_HEADS):
            if s == 0:
                kc_ref[h] = cf_ref[...]
                kc_ref[h, :, 0:hd] = outs[h].astype(BF16)
            else:
                vc_ref[h] = _value_rows(jnp.concatenate(outs, axis=1), h, True)


def _compress(kvc, pos, w1, w2, cfeat, B, S):
    nch = S // CMP_STRIDE
    full = lambda *shape: pl.BlockSpec(shape, lambda b: (0,) * len(shape))
    return pl.pallas_call(
        _compress_kernel,
        grid=(B,),
        in_specs=[
            pl.BlockSpec((S, LANES), lambda b: (b, 0)),
            pl.BlockSpec((S, LANES), lambda b: (b, 1)),
            full(*pos.shape), full(*w1.shape), full(*w2.shape), full(*cfeat.shape),
        ],
        out_specs=[
            pl.BlockSpec((None, N_KV_HEADS, nch, LANES), lambda b: (b, 0, 0, 0)),
            pl.BlockSpec((None, N_KV_HEADS, nch, VW), lambda b: (b, 0, 0, 0)),
        ],
        out_shape=[
            jax.ShapeDtypeStruct((B, N_KV_HEADS, nch, LANES), BF16),
            jax.ShapeDtypeStruct((B, N_KV_HEADS, nch, VW), BF16),
        ],
        compiler_params=pltpu.CompilerParams(
            dimension_semantics=("arbitrary",), vmem_limit_bytes=VMEM_LIMIT),
        name="compress",
    )(kvc, kvc, pos, w1, w2, cfeat)


def _hi_lo(x):
    hi = x.astype(BF16)
    return jnp.concatenate([hi, (x - hi.astype(F32)).astype(BF16)], axis=1)


def _head_layout(acc):
    low_half = lax.broadcasted_iota(jnp.int32, (TQ, LANES), 1) < HEAD_DIM
    head = lambda x, g: x[g * TQ:(g + 1) * TQ]
    if acc.shape[1] == VW:
        x = acc[:, 0:LANES] * (1.0 / jnp.maximum(acc[:, LANES:VW], 1e-30))
        pairs = [jnp.where(low_half, head(x, 2 * j), head(x, 2 * j + 1)) for j in range(GQA // 2)]
    else:
        rot = pltpu.roll(acc, HEAD_DIM, axis=1)
        pairs = [jnp.where(low_half, head(acc, 2 * j) * (1.0 / head(rot, 2 * j)),
                           head(rot, 2 * j + 1) * (1.0 / head(acc, 2 * j + 1))) for j in range(GQA // 2)]
    return jnp.concatenate(pairs, axis=1)


def _exchange(v, i, j, descending):
    hi, lo = jnp.maximum(v[i], v[j]), jnp.minimum(v[i], v[j])
    v[i], v[j] = (hi, lo) if descending else (lo, hi)


def _bitonic_merge_desc(v):
    n = len(v)
    j = n // 2
    while j >= 1:
        for i in range(n):
            if i ^ j > i:
                _exchange(v, i, i ^ j, True)
        j //= 2


def _kth_largest(x, kth, fillers=()):
    n = x.shape[0]
    assert n == kth * SUBLANES and kth & (kth - 1) == 0
    v = [x[r * SUBLANES:(r + 1) * SUBLANES, :] for r in range(kth)]
    fillers = list(fillers)
    layers = 0
    k = 2
    while k <= kth:
        j = k // 2
        while j >= 1:
            for i in range(kth):
                if i ^ j > i:
                    _exchange(v, i, i ^ j, (i & k) == 0)
            j //= 2
            layers += 1
            if layers % 3 == 0 and fillers:
                fillers.pop(0)()
        k *= 2
    shift = SUBLANES // 2
    while shift >= 1:
        v = [jnp.maximum(v[i], pltpu.roll(v[kth - 1 - i], shift, axis=0)) for i in range(kth)]
        if shift > 1:
            _bitonic_merge_desc(v)
        if fillers:
            fillers.pop(0)()
        shift //= 2
    for filler in fillers:
        filler()
    out = v[0]
    for i in range(1, kth):
        out = jnp.minimum(out, v[i])
    return out[0:1, :]


def _nsa_kernel(q_ref, qf_ref, kc_ref, vc_ref, ks_ref, vs_ref, kw_ref, vw_ref, ov_ref, cm_ref, mb_ref,
                ge_ref, g_ref, o_ref, qa_ref, tiles_ref, m_ref, acc_ref, sa_ref, sb_ref, kwp_ref, vwp_ref):
    i = pl.program_id(2)
    t0 = i * TQ
    rows = GQA * TQ
    n_sel = ov_ref.shape[1]
    ncmp = kc_ref.shape[0]

    @pl.when(i == 0)
    def _():
        lane = lax.broadcasted_iota(jnp.int32, (WINDOW, LANES), 1)
        kwp_ref[0:WINDOW, :] = jnp.where(lane == HEAD_DIM + 2, 1.0, 0.0).astype(BF16)
        kwp_ref[WINDOW:, :] = kw_ref[...]
        vwp_ref[0:WINDOW, :] = jnp.zeros((WINDOW, VW), BF16)
        vwp_ref[WINDOW:, :] = vw_ref[...]

    q = q_ref[...]
    for g in range(GQA):
        qg = q[:, g * HEAD_DIM:(g + 1) * HEAD_DIM].astype(F32) * (HEAD_DIM ** -0.5)
        qa_ref[g * TQ:(g + 1) * TQ, 0:HEAD_DIM] = qg.astype(BF16)
    qa_ref[:, HEAD_DIM:LANES] = qf_ref[...]
    qa1 = qa_ref[:, 0:LANES]

    diag_ok = (lax.broadcasted_iota(jnp.int32, (rows, TQ), 1)
               <= (lax.broadcasted_iota(jnp.int32, (rows, TQ), 0) & (TQ - 1)))


    wk = TQ + WINDOW
    w0 = pl.multiple_of(t0, TQ)
    win = {}

    def win_scores():
        s_w = _dot_nt(qa1, kwp_ref[pl.ds(w0, wk), :])
        win["s"] = (jnp.where(diag_ok, NEG, s_w[:, 0:TQ]),
                    s_w[:, TQ:WINDOW], jnp.where(diag_ok, s_w[:, WINDOW:wk], NEG))

    def win_max():
        s_old, s_mid, s_new = win["s"]
        win["m"] = jnp.maximum(jnp.maximum(jnp.max(s_old, axis=1, keepdims=True),
                                           jnp.max(s_mid, axis=1, keepdims=True)),
                               jnp.max(s_new, axis=1, keepdims=True))

    def win_exp():
        win["e"] = jnp.concatenate([jnp.exp(s - win["m"]) for s in win["s"]], axis=1).astype(BF16)

    def win_pv():
        win["acc"] = _dot(win["e"], vwp_ref[pl.ds(w0, wk), :])

    s_c = _dot_nt(qa1, kc_ref[...])
    s_c = jnp.where(cm_ref[...] <= (TQ // CMP_STRIDE) * i - 2, s_c, NEG)
    m_c = jnp.maximum(jnp.max(s_c, axis=1, keepdims=True), 0.5 * NEG)
    e_c = jnp.exp(s_c - m_c)
    acc_c = _dot(e_c.astype(BF16), vc_ref[...])
    inv_c = 1.0 / jnp.maximum(acc_c[:, LANES:VW], 1e-30)
    p_c = e_c * jnp.concatenate([inv_c] * (ncmp // LANES), axis=1)

    p_sum = p_c[0:TQ] + p_c[TQ:2 * TQ] + p_c[2 * TQ:3 * TQ] + p_c[3 * TQ:4 * TQ]
    p_hi = p_sum.astype(BF16)
    p_lo = (p_sum - p_hi.astype(F32)).astype(BF16)
    imp = _dot(p_hi, ov_ref[...]) + _dot(p_lo, ov_ref[...])
    imp_t = imp.T
    jb = lax.broadcasted_iota(jnp.int32, (n_sel, TQ), 0)
    tq = t0 + lax.broadcasted_iota(jnp.int32, (n_sel, TQ), 1)
    imp_t = jnp.where(SEL_BLOCK * jb > tq, -jnp.inf, imp_t)
    forced = (jb == 0) | (jb == (tq >> 6))
    imp_t = jnp.where(forced, jnp.inf, imp_t)

    thr = _kth_largest(imp_t, SEL_TOP, (win_scores, win_max, win_exp, win_pv))
    acc_w = win["acc"]
    above = imp_t > thr
    tied = imp_t == thr
    n_above = jnp.sum(jnp.where(above, 1.0, 0.0), axis=0, keepdims=True)
    lower = (lax.broadcasted_iota(jnp.int32, (n_sel, n_sel), 1)
             < lax.broadcasted_iota(jnp.int32, (n_sel, n_sel), 0))
    tied_before = _dot(jnp.where(lower, 1.0, 0.0).astype(BF16), jnp.where(tied, 1.0, 0.0).astype(BF16))
    take_tie = jnp.where(tied_before + n_above < float(SEL_TOP), 1.0, 0.0)
    sel_t = jnp.where(above, 1.0, jnp.where(tied, take_tie, 0.0))
    sel_q = sel_t.T
    bias = ((1.0 - sel_q) * MASK_BIAS).astype(BF16)
    for g in range(GQA):
        qa_ref[g * TQ:(g + 1) * TQ, LANES:2 * LANES] = bias

    n_past = t0 // TK
    first_mask = (t0 - n_past * TK) // TQ
    tiles_ref[0] = n_past

    def scores(j, s_ref, mask=None):
        k0 = pl.multiple_of(tiles_ref[j] * TK, TK)
        s = _dot_nt(qa_ref[...], ks_ref[pl.ds(k0, TK), :])
        s_ref[...] = s if mask is None else s + mask

    scores(0, sa_ref, mb_ref[first_mask])

    bpt = TK // SEL_BLOCK
    n_tiles = n_sel // bpt
    blk_cnt = _dot(jnp.ones((SUBLANES, TQ), BF16), sel_q.astype(BF16))
    in_tile = (lax.broadcasted_iota(jnp.int32, (n_sel, LANES), 0) // bpt
               == lax.broadcasted_iota(jnp.int32, (n_sel, LANES), 1))
    tile_cnt = _dot(jnp.where(blk_cnt > 0.5, 1.0, 0.0).astype(BF16),
                    jnp.where(in_tile, 1.0, 0.0).astype(BF16))
    lane = lax.broadcasted_iota(jnp.int32, (1, LANES), 1)
    bit = jnp.where(tile_cnt[0:1, :] > 0.5, jnp.left_shift(1, lane & 15).astype(F32), 0.0)
    words = [jnp.sum(jnp.where((lane >= 16 * w) & (lane < 16 * (w + 1)), bit, 0.0)).astype(jnp.int32)
             for w in range((n_tiles + 15) // 16)]
    n_tasks = jnp.int32(1)
    for kt in range(n_tiles - 1):
        used = (words[kt // 16] >> (kt % 16)) & 1
        tiles_ref[n_tasks] = kt
        n_tasks = n_tasks + jnp.where(kt < n_past, used, 0)
    tiles_ref[n_tasks] = 0
    tiles_ref[n_tasks + 1] = 0

    m_ref[...] = jnp.full(m_ref.shape, NEG, F32)
    acc_ref[...] = jnp.zeros(acc_ref.shape, F32)

    gates = _dot(_hi_lo(g_ref[...]), ge_ref[...])
    gw = GQA * HEAD_DIM
    merged = gates[:, 0:gw] * _head_layout(acc_c) + gates[:, 2 * gw:3 * gw] * _head_layout(acc_w)
    gates_sel = gates[:, gw:2 * gw]

    def accumulate(j, s_ref):
        k0 = pl.multiple_of(tiles_ref[j] * TK, TK)
        slot = jnp.where(j < n_tasks, 0, 1)
        s = s_ref[...]
        m = m_ref[slot]
        m_new = jnp.maximum(m, jnp.max(s, axis=1, keepdims=True))
        p = jnp.concatenate([jnp.exp((s[:, c:c + LANES] - m_new).astype(BF16)) for c in range(0, TK, LANES)],
                            axis=1)
        acc_ref[slot] = (jnp.exp(m - m_new) * acc_ref[slot]
                         + _dot(p, vs_ref[pl.ds(k0, TK), :]))
        m_ref[slot] = m_new

    def pair(jj, c):
        a = 2 * jj
        scores(a + 1, sb_ref)
        accumulate(a, sa_ref)
        scores(a + 2, sa_ref)
        accumulate(a + 1, sb_ref)
        return c

    lax.fori_loop(0, (n_tasks + 1) // 2, pair, 0)

    o_ref[...] = (merged + gates_sel * _head_layout(acc_ref[0])).astype(o_ref.dtype)


def _nsa(q, qfeat, kc_aug, vc, ks_aug, vs, kw_aug, vw, ov, cmask, tile_masks, gate_expand, gates, B, S):
    nq = S // TQ
    gw = GQA * HEAD_DIM
    ncmp = kc_aug.shape[2]
    n_sel = ov.shape[1]
    bk = lambda b, k, i: (b, k, 0, 0)
    kb = lambda b, k, i: (k, b, 0, 0)
    return pl.pallas_call(
        _nsa_kernel,
        grid=(B, N_KV_HEADS, nq),
        in_specs=[
            pl.BlockSpec((TQ, gw), lambda b, k, i: (b * nq + i, k)),
            pl.BlockSpec((None, GQA * TQ, HEAD_DIM), lambda b, k, i: (k, 0, 0)),
            pl.BlockSpec((None, None, ncmp, LANES), bk),
            pl.BlockSpec((None, None, ncmp, VW), bk),
            pl.BlockSpec((None, None, S, 2 * LANES), kb),
            pl.BlockSpec((None, None, S, LANES), kb),
            pl.BlockSpec((None, None, S, LANES), kb),
            pl.BlockSpec((None, None, S, VW), kb),
            pl.BlockSpec((ncmp, n_sel), lambda b, k, i: (0, 0)),
            pl.BlockSpec((GQA * TQ, ncmp), lambda b, k, i: (0, 0)),
            pl.BlockSpec(tile_masks.shape, lambda b, k, i: (0, 0, 0)),
            pl.BlockSpec((None,) + gate_expand.shape[1:], lambda b, k, i: (k, 0, 0)),
            pl.BlockSpec((TQ, _G_W), lambda b, k, i: (b * nq + i, 0)),
        ],
        out_specs=pl.BlockSpec((TQ, gw), lambda b, k, i: (b * nq + i, k)),
        out_shape=jax.ShapeDtypeStruct((B * S, N_KV_HEADS * gw), BF16),
        scratch_shapes=[
            pltpu.VMEM((GQA * TQ, 2 * LANES), BF16),
            pltpu.SMEM((S // TK + 2,), jnp.int32),
            pltpu.VMEM((2, GQA * TQ, LANES), F32),
            pltpu.VMEM((2, GQA * TQ, LANES), F32),
            pltpu.VMEM((GQA * TQ, TK), F32),
            pltpu.VMEM((GQA * TQ, TK), F32),
            pltpu.VMEM((S + WINDOW, LANES), BF16),
            pltpu.VMEM((S + WINDOW, VW), BF16),
        ],
        compiler_params=pltpu.CompilerParams(
            dimension_semantics=("arbitrary",) * 3, vmem_limit_bytes=VMEM_LIMIT),
        name="nsa",
    )(q, qfeat, kc_aug, vc, ks_aug, vs, kw_aug, vw, ov, cmask, tile_masks, gate_expand, gates)


_FF_CHUNK = 768


def _outproj_kernel(mc_ref, nsa_ref, x_ref, gn_ref, wo_ref, h_ref):
    mixed_n = _rms(nsa_ref[...].astype(F32), gn_ref[...]).astype(BF16)
    h_ref[...] = (x_ref[...] + _dot(mc_ref[...], wo_ref[0:CONV_CH, :])
                  + _dot(mixed_n, wo_ref[CONV_CH:2 * CONV_CH, :]))


def _outproj(mixed_c, nsa_out, x2, gn, wo, tm=1024):
    rows = x2.shape[0]
    row = lambda i: (i, 0)
    const = lambda i: (0, 0)
    return pl.pallas_call(
        _outproj_kernel,
        grid=(rows // tm,),
        in_specs=[
            pl.BlockSpec((tm, CONV_CH), row),
            pl.BlockSpec((tm, N_HEADS * HEAD_DIM), row),
            pl.BlockSpec((tm, D_MODEL), row),
            pl.BlockSpec((1, N_HEADS * HEAD_DIM), const),
            pl.BlockSpec((2 * CONV_CH, D_MODEL), const, pipeline_mode=pl.Buffered(1)),
        ],
        out_specs=pl.BlockSpec((tm, D_MODEL), row),
        out_shape=jax.ShapeDtypeStruct((rows, D_MODEL), F32),
        compiler_params=pltpu.CompilerParams(
            dimension_semantics=("arbitrary",), vmem_limit_bytes=VMEM_LIMIT),
        name="outproj",
    )(mixed_c, nsa_out, x2, gn, wo)


def _outffn_kernel(h_ref, g2_ref, wg_ref, wu_ref, wd_ref, gf_ref, o_ref):
    h = h_ref[...]
    u = _rms(h, g2_ref[...]).astype(BF16)
    acc = h
    for c in range(0, D_FF, _FF_CHUNK):
        sl = slice(c, min(c + _FF_CHUNK, D_FF))
        a = jax.nn.silu(_dot(u, wg_ref[:, sl])) * _dot(u, wu_ref[:, sl])
        acc = acc + _dot(a.astype(BF16), wd_ref[sl, :])
    o_ref[...] = _rms(acc, gf_ref[...])


def _outffn(h, g2, wg, wu, wd, gf, tm=512):
    rows = h.shape[0]
    row = lambda i: (i, 0)
    const = lambda i: (0, 0)
    resident = lambda shape: pl.BlockSpec(shape, const, pipeline_mode=pl.Buffered(1))
    return pl.pallas_call(
        _outffn_kernel,
        grid=(rows // tm,),
        in_specs=[
            pl.BlockSpec((tm, D_MODEL), row),
            pl.BlockSpec((1, D_MODEL), const),
            resident((D_MODEL, D_FF)),
            resident((D_MODEL, D_FF)),
            resident((D_FF, D_MODEL)),
            pl.BlockSpec((1, D_MODEL), const),
        ],
        out_specs=pl.BlockSpec((tm, D_MODEL), row),
        out_shape=jax.ShapeDtypeStruct((rows, D_MODEL), F32),
        compiler_params=pltpu.CompilerParams(
            dimension_semantics=("arbitrary",), vmem_limit_bytes=VMEM_LIMIT),
        name="outffn",
    )(h, g2, wg, wu, wd, gf)


def _key_features(pos, n_sel):
    f = np.zeros((pos.shape[0], LANES + n_sel), np.float32)
    f[:, HEAD_DIM] = pos // SEL_BLOCK
    f[:, HEAD_DIM + 1] = pos % SEL_BLOCK
    if n_sel:
        f[np.arange(pos.shape[0]), LANES + pos // SEL_BLOCK] = 1.0
    return f


def _query_features():
    f = np.zeros((N_KV_HEADS, GQA * TQ, HEAD_DIM), np.float32)
    for k in range(N_KV_HEADS):
        for g in range(GQA):
            slope = 2.0 ** (-8.0 * (k * GQA + g + 1) / N_HEADS)
            f[k, g * TQ:(g + 1) * TQ, 0] = slope * SEL_BLOCK
            f[k, g * TQ:(g + 1) * TQ, 1] = slope
            f[k, g * TQ:(g + 1) * TQ, 2] = MASK_BIAS
    return f


def _causal_masks(ncmp):
    r = np.arange(GQA * TQ)[:, None] % TQ
    cmask = (np.arange(ncmp)[None, :] - ((r + 1) // CMP_STRIDE)).astype(np.int32)
    c = np.arange(TK)[None, :]
    tile_masks = [np.where(c - r <= d * TQ, 0.0, NEG).astype(np.float32) for d in range(max(TK // TQ, 1))]
    return cmask, np.stack(tile_masks)


def _gate_expand():
    gw = GQA * HEAD_DIM
    e = np.zeros((N_KV_HEADS, 2 * _G_W, N_BRANCH * gw), np.float32)
    for k in range(N_KV_HEADS):
        for g in range(GQA):
            for br in range(N_BRANCH):
                for half in range(2):
                    e[k, half * _G_W + (k * GQA + g) * N_BRANCH + br,
                      br * gw + g * HEAD_DIM:br * gw + (g + 1) * HEAD_DIM] = 1.0
    return e


def _overlap(ncmp_pad, n_sel):
    nc = ncmp_pad - 1
    c_start = CMP_STRIDE * np.arange(nc)
    c_end = c_start + CMP_BLOCK - 1
    s_start = SEL_BLOCK * np.arange(n_sel)
    ov = np.minimum(c_end[:, None] + 1, s_start[None, :] + SEL_BLOCK) - np.maximum(c_start[:, None], s_start[None, :])
    out = np.zeros((ncmp_pad, n_sel), np.float32)
    out[:nc] = np.clip(ov, 0, None).astype(np.float32) / CMP_BLOCK
    return out


def _layer(x2, B, S, norm1_g, w_in, conv_w, k_cmp_pos, k_cmp_w1, k_cmp_w2, v_cmp_pos, v_cmp_w1, v_cmp_w2,
           gn_conv_g, later_weights):
    n_sel = S // SEL_BLOCK
    nch = S // CMP_STRIDE
    hd = HEAD_DIM

    kfeat = jnp.asarray(_key_features(np.arange(S), n_sel), BF16)
    mixed_c, q, gates, kvc, ks_aug, vs, kw_aug, vw, *later_bf16 = _inproj(
        x2, norm1_g.reshape(1, D_MODEL), w_in.astype(BF16), kfeat, conv_w, gn_conv_g.reshape(1, CONV_CH),
        later_weights)

    w1 = jnp.stack([k_cmp_w1, v_cmp_w1]).astype(BF16)
    w2 = jnp.stack([k_cmp_w2, v_cmp_w2]).astype(BF16)
    pos = jnp.stack([k_cmp_pos, v_cmp_pos]).reshape(2, 1, CMP_BLOCK * hd)
    pos = jnp.broadcast_to(pos, (2, SUBLANES, CMP_BLOCK * hd)).astype(BF16)
    c_end = CMP_STRIDE * np.arange(nch) + CMP_BLOCK - 1
    cfeat = jnp.asarray(_key_features(c_end, 0), BF16)
    kc_aug, vc = _compress(kvc, pos, w1, w2, cfeat, B, S)

    cmask, tile_masks = _causal_masks(nch)

    nsa_out = _nsa(q, jnp.asarray(_query_features(), BF16), kc_aug, vc,
                   ks_aug.reshape(N_KV_HEADS, B, S, 2 * LANES), vs.reshape(N_KV_HEADS, B, S, LANES),
                   kw_aug.reshape(N_KV_HEADS, B, S, LANES), vw.reshape(N_KV_HEADS, B, S, VW),
                   jnp.asarray(_overlap(nch, n_sel), BF16), jnp.asarray(cmask),
                   jnp.asarray(tile_masks), jnp.asarray(_gate_expand(), BF16), gates, B, S)
    return mixed_c, nsa_out, later_bf16


def kernel(x, norm1_g, w_in, conv_w, k_cmp_pos, k_cmp_w1, k_cmp_w2, v_cmp_pos, v_cmp_w1, v_cmp_w2,
           gn_conv_g, gn_nsa_g, w_out, norm2_g, w_gate, w_up, w_down, norm_f_g):
    B, S, _ = x.shape
    depth = norm1_g.shape[0]
    assert depth == 1, "the final norm is fused into the (single) layer's FFN kernel"
    x2 = x.reshape(B * S, D_MODEL)
    mixed_c, nsa_out, (wo, wg, wu, wd) = _layer(
        x2, B, S, norm1_g[0], w_in[0], conv_w[0], k_cmp_pos[0], k_cmp_w1[0], k_cmp_w2[0], v_cmp_pos[0],
        v_cmp_w1[0], v_cmp_w2[0], gn_conv_g[0], [w_out[0], w_gate[0], w_up[0], w_down[0]])
    h = _outproj(mixed_c, nsa_out, x2, gn_nsa_g[0].reshape(1, N_HEADS * HEAD_DIM), wo)
    out = _outffn(h, norm2_g[0].reshape(1, D_MODEL), wg, wu, wd, norm_f_g.reshape(1, D_MODEL))
    return out.reshape(B, S, D_MODEL)
```

```python
import functools

import numpy as np
import jax
import jax.numpy as jnp
from jax import lax
from jax.experimental import pallas as pl
from jax.experimental.pallas import tpu as pltpu

D_MODEL = 1024
CONV_CH = 512
CONV_K = 3
N_HEADS = 8
HEAD_DIM = 64
N_KV_HEADS = 2
GQA = 4
N_BRANCH = 3
CMP_BLOCK = 32
CMP_STRIDE = 16
CMP_HIDDEN = 256
SEL_BLOCK = 64
SEL_TOP = 16
WINDOW = 512
D_FF = 2816
NORM_EPS = 1e-6

F32 = jnp.float32
BF16 = jnp.bfloat16

LANES = 128
VW = 2 * LANES
SUBLANES = 8
VMEM_LIMIT = 56 * 1024 * 1024

TQ = 256
TK = 256
NEG = -1e30
MASK_BIAS = -float(2 ** 30)
_POS_HI, _POS_LO, _PAD_MARK = HEAD_DIM, HEAD_DIM + 1, HEAD_DIM + 2
_FLAG_BITS = 16

_CONV_W = 3 * CONV_CH
_Q_OFF = _CONV_W
_KVC_OFF = _Q_OFF + N_HEADS * HEAD_DIM
_KV_OFF = _KVC_OFF + 2 * N_KV_HEADS * HEAD_DIM
_G_OFF = _KV_OFF + 4 * N_KV_HEADS * HEAD_DIM
_G_W = N_HEADS * N_BRANCH
_PROJ_W = _G_OFF + _G_W


def _rms(x, g):
    return x * lax.rsqrt(jnp.mean(x * x, axis=-1, keepdims=True) + NORM_EPS) * g


def _dot(a, b):
    return jnp.dot(a, b, preferred_element_type=F32)


def _value_rows(v2, h, wide):
    lane = lax.broadcasted_iota(jnp.int32, v2.shape, 1)
    swapped = pltpu.roll(v2, HEAD_DIM, axis=1)
    if not wide:
        return jnp.where(lane < HEAD_DIM, v2 if h == 0 else swapped, 1.0).astype(BF16)
    dup = jnp.where((lane < HEAD_DIM) == (h == 0), v2, swapped)
    return jnp.concatenate([dup.astype(BF16), jnp.ones(v2.shape, BF16)], axis=1)


def _dot_nt(a, b):
    return lax.dot_general(a, b, (((1,), (1,)), ((), ())), preferred_element_type=F32)


def _inproj_kernel(seq_tiles, n_cast, x_ref, xp_ref, g_ref, w_ref, kf_ref, cw_ref, gc_ref, *refs):
    cast_in, refs = refs[:n_cast], refs[n_cast:]
    (mixc_ref, q_ref, gate_ref, kvc_ref, ks_ref, vs_ref, kw_ref, vw_ref), cast_out = refs[:8], refs[8:]
    for src_ref, dst_ref in zip(cast_in, cast_out):
        dst_ref[...] = src_ref[...].astype(BF16)

    u = _rms(x_ref[...], g_ref[...]).astype(BF16)

    cbch = _dot(u, w_ref[:, 0:_CONV_W])
    cu = cbch[:, CONV_CH:2 * CONV_CH] * cbch[:, 2 * CONV_CH:3 * CONV_CH]
    up = _rms(xp_ref[...], g_ref[...]).astype(BF16)
    chp = _dot(up, w_ref[:, CONV_CH:_CONV_W])
    starts_sequence = pl.program_id(0) % seq_tiles == 0
    prev = jnp.where(starts_sequence, 0.0, chp[:, 0:CONV_CH] * chp[:, CONV_CH:2 * CONV_CH])
    r = lax.broadcasted_iota(jnp.int32, cu.shape, 0)
    cu1 = jnp.where(r == 0, prev[7:8, :], pltpu.roll(cu, 1, axis=0))
    cu2 = pltpu.roll(cu, 2, axis=0)
    cu2 = jnp.where(r == 0, prev[6:7, :], jnp.where(r == 1, prev[7:8, :], cu2))
    cw = cw_ref[...]
    y = cbch[:, 0:CONV_CH] * (cw[0:1, :] * cu2 + cw[1:2, :] * cu1 + cw[2:3, :] * cu)
    mixc_ref[...] = _rms(y, gc_ref[...]).astype(BF16)

    q_ref[...] = _dot(u, w_ref[:, _Q_OFF:_KVC_OFF]).astype(BF16)
    kvc_ref[...] = _dot(u, w_ref[:, _KVC_OFF:_KV_OFF])
    gate_ref[...] = jax.nn.sigmoid(_dot(u, w_ref[:, _G_OFF:_PROJ_W]))
    kv = _dot(u, w_ref[:, _KV_OFF:_G_OFF])
    kf = kf_ref[...]
    hd, kvw = HEAD_DIM, N_KV_HEADS * HEAD_DIM
    for h in range(N_KV_HEADS):
        ks_ref[h] = kf
        ks_ref[h, :, 0:hd] = kv[:, h * hd:(h + 1) * hd].astype(BF16)
        vs_ref[h] = _value_rows(kv[:, kvw:2 * kvw], h, False)
        kw_ref[h] = kf[:, 0:LANES]
        kw_ref[h, :, 0:hd] = kv[:, 2 * kvw + h * hd:2 * kvw + (h + 1) * hd].astype(BF16)
        vw_ref[h] = _value_rows(kv[:, 3 * kvw:4 * kvw], h, True)


def _inproj(x2, g, w, kfeat, conv_w, gc, to_cast, tm=1024):
    rows = x2.shape[0]
    steps = rows // tm
    seq_tiles = kfeat.shape[0] // tm
    row = lambda i: (i, 0)
    head_row = lambda i: (0, i, 0)
    const = lambda i: (0, 0)
    cast_specs = [pl.BlockSpec((a.shape[0] // steps, a.shape[1]), row) for a in to_cast]
    return pl.pallas_call(
        functools.partial(_inproj_kernel, seq_tiles, len(to_cast)),
        grid=(steps,),
        in_specs=[
            pl.BlockSpec((tm, D_MODEL), row),
            pl.BlockSpec((SUBLANES, D_MODEL), lambda i: (jnp.maximum(i * (tm // SUBLANES) - 1, 0), 0)),
            pl.BlockSpec((1, D_MODEL), const),
            pl.BlockSpec((D_MODEL, _PROJ_W), const, pipeline_mode=pl.Buffered(1)),
            pl.BlockSpec((tm, 2 * LANES), lambda i: (i % seq_tiles, 0)),
            pl.BlockSpec((CONV_K, CONV_CH), const),
            pl.BlockSpec((1, CONV_CH), const),
        ] + cast_specs,
        out_specs=[
            pl.BlockSpec((tm, CONV_CH), row),
            pl.BlockSpec((tm, N_HEADS * HEAD_DIM), row),
            pl.BlockSpec((tm, _G_W), row),
            pl.BlockSpec((tm, _KV_OFF - _KVC_OFF), row),
            pl.BlockSpec((N_KV_HEADS, tm, 2 * LANES), head_row),
            pl.BlockSpec((N_KV_HEADS, tm, LANES), head_row),
            pl.BlockSpec((N_KV_HEADS, tm, LANES), head_row),
            pl.BlockSpec((N_KV_HEADS, tm, VW), head_row),
        ] + cast_specs,
        out_shape=[
            jax.ShapeDtypeStruct((rows, CONV_CH), BF16),
            jax.ShapeDtypeStruct((rows, N_HEADS * HEAD_DIM), BF16),
            jax.ShapeDtypeStruct((rows, _G_W), F32),
            jax.ShapeDtypeStruct((rows, _KV_OFF - _KVC_OFF), F32),
            jax.ShapeDtypeStruct((N_KV_HEADS, rows, 2 * LANES), BF16),
            jax.ShapeDtypeStruct((N_KV_HEADS, rows, LANES), BF16),
            jax.ShapeDtypeStruct((N_KV_HEADS, rows, LANES), BF16),
            jax.ShapeDtypeStruct((N_KV_HEADS, rows, VW), BF16),
        ] + [jax.ShapeDtypeStruct(a.shape, BF16) for a in to_cast],
        compiler_params=pltpu.CompilerParams(
            dimension_semantics=("arbitrary",), vmem_limit_bytes=VMEM_LIMIT),
        name="inproj",
    )(x2, x2, g, w, kfeat, conv_w, gc, *to_cast)


def _compress_kernel(xk_ref, xv_ref, pos_ref, w1_ref, w2_ref, cf_ref, kc_ref, vc_ref):
    nch = xk_ref.shape[0] // CMP_STRIDE
    hd = HEAD_DIM
    first = [jnp.zeros((nch, CMP_HIDDEN), F32) for _ in range(2 * N_KV_HEADS)]
    second = [jnp.zeros((nch, CMP_HIDDEN), F32) for _ in range(2 * N_KV_HEADS)]
    group = 2 * LANES // hd
    for l0 in range(0, CMP_STRIDE, group):
        for s, x_ref in enumerate((xk_ref, xv_ref)):
            xs = [x_ref[pl.ds(l0 + j, nch, stride=CMP_STRIDE), :].astype(BF16) for j in range(group)]
            for h in range(N_KV_HEADS):
                c = s * N_KV_HEADS + h
                xh = jnp.concatenate([x[:, h * hd:(h + 1) * hd] for x in xs], axis=1)
                first[c] = first[c] + _dot(xh, w1_ref[s, l0 * hd:(l0 + group) * hd, :])
                second[c] = second[c] + _dot(xh, w1_ref[s, (CMP_STRIDE + l0) * hd:(CMP_STRIDE + l0 + group) * hd, :])
    for s in range(2):
        pb = _dot(pos_ref[s], w1_ref[s])[0:1, :]
        outs = []
        for h in range(N_KV_HEADS):
            c = s * N_KV_HEADS + h
            hid = first[c] + pltpu.roll(second[c], nch - 1, axis=0) + pb
            outs.append(_dot(jax.nn.gelu(hid, approximate=True).astype(BF16), w2_ref[s]))
        for h in range(N_KV_HEADS):
            if s == 0:
                kc_ref[h] = cf_ref[...]
                kc_ref[h, :, 0:hd] = outs[h].astype(BF16)
            else:
                vc_ref[h] = _value_rows(jnp.concatenate(outs, axis=1), h, True)


def _compress(kvc, pos, w1, w2, cfeat, B, S):
    nch = S // CMP_STRIDE
    full = lambda *shape: pl.BlockSpec(shape, lambda b: (0,) * len(shape))
    return pl.pallas_call(
        _compress_kernel,
        grid=(B,),
        in_specs=[
            pl.BlockSpec((S, LANES), lambda b: (b, 0)),
            pl.BlockSpec((S, LANES), lambda b: (b, 1)),
            full(*pos.shape), full(*w1.shape), full(*w2.shape), full(*cfeat.shape),
        ],
        out_specs=[
            pl.BlockSpec((None, N_KV_HEADS, nch, LANES), lambda b: (b, 0, 0, 0)),
            pl.BlockSpec((None, N_KV_HEADS, nch, VW), lambda b: (b, 0, 0, 0)),
        ],
        out_shape=[
            jax.ShapeDtypeStruct((B, N_KV_HEADS, nch, LANES), BF16),
            jax.ShapeDtypeStruct((B, N_KV_HEADS, nch, VW), BF16),
        ],
        compiler_params=pltpu.CompilerParams(
            dimension_semantics=("arbitrary",), vmem_limit_bytes=VMEM_LIMIT),
        name="compress",
    )(kvc, kvc, pos, w1, w2, cfeat)


def _hi_lo(x):
    hi = x.astype(BF16)
    return jnp.concatenate([hi, (x - hi.astype(F32)).astype(BF16)], axis=1)


def _head_layout(acc):
    low_half = lax.broadcasted_iota(jnp.int32, (TQ, LANES), 1) < HEAD_DIM
    head = lambda x, g: x[g * TQ:(g + 1) * TQ]
    if acc.shape[1] == VW:
        x = acc[:, 0:LANES] * (1.0 / jnp.maximum(acc[:, LANES:VW], 1e-30))
        pairs = [jnp.where(low_half, head(x, 2 * j), head(x, 2 * j + 1)) for j in range(GQA // 2)]
    else:
        rot = pltpu.roll(acc, HEAD_DIM, axis=1)
        pairs = [jnp.where(low_half, head(acc, 2 * j) * (1.0 / head(rot, 2 * j)),
                           head(rot, 2 * j + 1) * (1.0 / head(acc, 2 * j + 1))) for j in range(GQA // 2)]
    return jnp.concatenate(pairs, axis=1)


def _exchange(v, i, j, descending):
    hi, lo = jnp.maximum(v[i], v[j]), jnp.minimum(v[i], v[j])
    v[i], v[j] = (hi, lo) if descending else (lo, hi)


def _bitonic_merge_desc(v):
    n = len(v)
    j = n // 2
    while j >= 1:
        for i in range(n):
            if i ^ j > i:
                _exchange(v, i, i ^ j, True)
        j //= 2


def _kth_largest(x, kth, fillers=()):
    n = x.shape[0]
    assert n == kth * SUBLANES and kth & (kth - 1) == 0
    v = [x[r * SUBLANES:(r + 1) * SUBLANES, :] for r in range(kth)]
    fillers = list(fillers)
    layers = 0
    k = 2
    while k <= kth:
        j = k // 2
        while j >= 1:
            for i in range(kth):
                if i ^ j > i:
                    _exchange(v, i, i ^ j, (i & k) == 0)
            j //= 2
            layers += 1
            if layers % 3 == 0 and fillers:
                fillers.pop(0)()
        k *= 2
    shift = SUBLANES // 2
    while shift >= 1:
        v = [jnp.maximum(v[i], pltpu.roll(v[kth - 1 - i], shift, axis=0)) for i in range(kth)]
        if shift > 1:
            _bitonic_merge_desc(v)
        if fillers:
            fillers.pop(0)()
        shift //= 2
    for filler in fillers:
        filler()
    out = v[0]
    for i in range(1, kth):
        out = jnp.minimum(out, v[i])
    return out[0:1, :]


def _nsa_kernel(q_ref, qf_ref, kc_ref, vc_ref, ks_ref, vs_ref, kw_ref, vw_ref, ov_ref, cm_ref, mb_ref,
                ge_ref, g_ref, o_ref, qa_ref, tiles_ref, m_ref, acc_ref, sa_ref, sb_ref, kwp_ref, vwp_ref):
    i = pl.program_id(2)
    t0 = i * TQ
    rows = GQA * TQ
    n_sel = ov_ref.shape[1]
    ncmp = kc_ref.shape[0]

    @pl.when(i == 0)
    def _():
        lane = lax.broadcasted_iota(jnp.int32, (WINDOW, LANES), 1)
        kwp_ref[0:WINDOW, :] = jnp.where(lane == _PAD_MARK, 1.0, 0.0).astype(BF16)
        kwp_ref[WINDOW:, :] = kw_ref[...]
        vwp_ref[0:WINDOW, :] = jnp.zeros((WINDOW, VW), BF16)
        vwp_ref[WINDOW:, :] = vw_ref[...]

    q = q_ref[...]
    for g in range(GQA):
        qg = q[:, g * HEAD_DIM:(g + 1) * HEAD_DIM].astype(F32) * (HEAD_DIM ** -0.5)
        qa_ref[g * TQ:(g + 1) * TQ, 0:HEAD_DIM] = qg.astype(BF16)
    qa_ref[:, HEAD_DIM:LANES] = qf_ref[...]
    qa1 = qa_ref[:, 0:LANES]

    diag_ok = (lax.broadcasted_iota(jnp.int32, (rows, TQ), 1)
               <= (lax.broadcasted_iota(jnp.int32, (rows, TQ), 0) & (TQ - 1)))


    wk = TQ + WINDOW
    w0 = pl.multiple_of(t0, TQ)
    win = {}

    def win_scores():
        s_w = _dot_nt(qa1, kwp_ref[pl.ds(w0, wk), :])
        win["s"] = (jnp.where(diag_ok, NEG, s_w[:, 0:TQ]),
                    s_w[:, TQ:WINDOW], jnp.where(diag_ok, s_w[:, WINDOW:wk], NEG))

    def win_max():
        s_old, s_mid, s_new = win["s"]
        win["m"] = jnp.maximum(jnp.maximum(jnp.max(s_old, axis=1, keepdims=True),
                                           jnp.max(s_mid, axis=1, keepdims=True)),
                               jnp.max(s_new, axis=1, keepdims=True))

    def win_exp():
        win["e"] = jnp.concatenate([jnp.exp(s - win["m"]) for s in win["s"]], axis=1).astype(BF16)

    def win_pv():
        win["acc"] = _dot(win["e"], vwp_ref[pl.ds(w0, wk), :])

    s_c = _dot_nt(qa1, kc_ref[...])
    s_c = jnp.where(cm_ref[...] <= (TQ // CMP_STRIDE) * i - 2, s_c, NEG)
    m_c = jnp.maximum(jnp.max(s_c, axis=1, keepdims=True), 0.5 * NEG)
    e_c = jnp.exp(s_c - m_c)
    acc_c = _dot(e_c.astype(BF16), vc_ref[...])
    inv_c = 1.0 / jnp.maximum(acc_c[:, LANES:VW], 1e-30)
    p_c = e_c * jnp.concatenate([inv_c] * (ncmp // LANES), axis=1)

    p_sum = p_c[0:TQ] + p_c[TQ:2 * TQ] + p_c[2 * TQ:3 * TQ] + p_c[3 * TQ:4 * TQ]
    p_hi = p_sum.astype(BF16)
    p_lo = (p_sum - p_hi.astype(F32)).astype(BF16)
    imp = _dot(p_hi, ov_ref[...]) + _dot(p_lo, ov_ref[...])
    imp_t = imp.T
    jb = lax.broadcasted_iota(jnp.int32, (n_sel, TQ), 0)
    tq = t0 + lax.broadcasted_iota(jnp.int32, (n_sel, TQ), 1)
    imp_t = jnp.where(SEL_BLOCK * jb > tq, -jnp.inf, imp_t)
    forced = (jb == 0) | (jb == (tq >> (SEL_BLOCK.bit_length() - 1)))
    imp_t = jnp.where(forced, jnp.inf, imp_t)

    thr = _kth_largest(imp_t, SEL_TOP, (win_scores, win_max, win_exp, win_pv))
    acc_w = win["acc"]
    above = imp_t > thr
    tied = imp_t == thr
    n_above = jnp.sum(jnp.where(above, 1.0, 0.0), axis=0, keepdims=True)
    lower = (lax.broadcasted_iota(jnp.int32, (n_sel, n_sel), 1)
             < lax.broadcasted_iota(jnp.int32, (n_sel, n_sel), 0))
    tied_before = _dot(jnp.where(lower, 1.0, 0.0).astype(BF16), jnp.where(tied, 1.0, 0.0).astype(BF16))
    take_tie = jnp.where(tied_before + n_above < float(SEL_TOP), 1.0, 0.0)
    sel_t = jnp.where(above, 1.0, jnp.where(tied, take_tie, 0.0))
    sel_q = sel_t.T
    bias = ((1.0 - sel_q) * MASK_BIAS).astype(BF16)
    for g in range(GQA):
        qa_ref[g * TQ:(g + 1) * TQ, LANES:2 * LANES] = bias

    n_past = t0 // TK
    first_mask = (t0 - n_past * TK) // TQ
    tiles_ref[0] = n_past

    def scores(j, s_ref, mask=None):
        k0 = pl.multiple_of(tiles_ref[j] * TK, TK)
        s = _dot_nt(qa_ref[...], ks_ref[pl.ds(k0, TK), :])
        s_ref[...] = s if mask is None else s + mask

    scores(0, sa_ref, mb_ref[first_mask])

    bpt = TK // SEL_BLOCK
    n_tiles = n_sel // bpt
    blk_cnt = _dot(jnp.ones((SUBLANES, TQ), BF16), sel_q.astype(BF16))
    in_tile = (lax.broadcasted_iota(jnp.int32, (n_sel, LANES), 0) // bpt
               == lax.broadcasted_iota(jnp.int32, (n_sel, LANES), 1))
    tile_cnt = _dot(jnp.where(blk_cnt > 0.5, 1.0, 0.0).astype(BF16),
                    jnp.where(in_tile, 1.0, 0.0).astype(BF16))
    lane = lax.broadcasted_iota(jnp.int32, (1, LANES), 1)
    bit = jnp.where(tile_cnt[0:1, :] > 0.5, jnp.left_shift(1, lane & (_FLAG_BITS - 1)).astype(F32), 0.0)
    words = [jnp.sum(jnp.where((lane >= _FLAG_BITS * w) & (lane < _FLAG_BITS * (w + 1)), bit, 0.0)
                     ).astype(jnp.int32) for w in range(-(-n_tiles // _FLAG_BITS))]
    n_tasks = jnp.int32(1)
    for kt in range(n_tiles - 1):
        used = (words[kt // _FLAG_BITS] >> (kt % _FLAG_BITS)) & 1
        tiles_ref[n_tasks] = kt
        n_tasks = n_tasks + jnp.where(kt < n_past, used, 0)
    tiles_ref[n_tasks] = 0
    tiles_ref[n_tasks + 1] = 0

    m_ref[...] = jnp.full(m_ref.shape, NEG, F32)
    acc_ref[...] = jnp.zeros(acc_ref.shape, F32)

    gates = _dot(_hi_lo(g_ref[...]), ge_ref[...])
    gw = GQA * HEAD_DIM
    merged = gates[:, 0:gw] * _head_layout(acc_c) + gates[:, 2 * gw:3 * gw] * _head_layout(acc_w)
    gates_sel = gates[:, gw:2 * gw]

    def accumulate(j, s_ref):
        k0 = pl.multiple_of(tiles_ref[j] * TK, TK)
        slot = jnp.where(j < n_tasks, 0, 1)
        s = s_ref[...]
        m = m_ref[slot]
        m_new = jnp.maximum(m, jnp.max(s, axis=1, keepdims=True))
        p = jnp.concatenate([jnp.exp((s[:, c:c + LANES] - m_new).astype(BF16)) for c in range(0, TK, LANES)],
                            axis=1)
        acc_ref[slot] = (jnp.exp(m - m_new) * acc_ref[slot]
                         + _dot(p, vs_ref[pl.ds(k0, TK), :]))
        m_ref[slot] = m_new

    def pair(jj, c):
        a = 2 * jj
        scores(a + 1, sb_ref)
        accumulate(a, sa_ref)
        scores(a + 2, sa_ref)
        accumulate(a + 1, sb_ref)
        return c

    lax.fori_loop(0, (n_tasks + 1) // 2, pair, 0)

    o_ref[...] = (merged + gates_sel * _head_layout(acc_ref[0])).astype(o_ref.dtype)


def _nsa(q, qfeat, kc_aug, vc, ks_aug, vs, kw_aug, vw, ov, cmask, tile_masks, gate_expand, gates, B, S):
    nq = S // TQ
    gw = GQA * HEAD_DIM
    ncmp = kc_aug.shape[2]
    n_sel = ov.shape[1]
    bk = lambda b, k, i: (b, k, 0, 0)
    kb = lambda b, k, i: (k, b, 0, 0)
    return pl.pallas_call(
        _nsa_kernel,
        grid=(B, N_KV_HEADS, nq),
        in_specs=[
            pl.BlockSpec((TQ, gw), lambda b, k, i: (b * nq + i, k)),
            pl.BlockSpec((None, GQA * TQ, HEAD_DIM), lambda b, k, i: (k, 0, 0)),
            pl.BlockSpec((None, None, ncmp, LANES), bk),
            pl.BlockSpec((None, None, ncmp, VW), bk),
            pl.BlockSpec((None, None, S, 2 * LANES), kb),
            pl.BlockSpec((None, None, S, LANES), kb),
            pl.BlockSpec((None, None, S, LANES), kb),
            pl.BlockSpec((None, None, S, VW), kb),
            pl.BlockSpec((ncmp, n_sel), lambda b, k, i: (0, 0)),
            pl.BlockSpec((GQA * TQ, ncmp), lambda b, k, i: (0, 0)),
            pl.BlockSpec(tile_masks.shape, lambda b, k, i: (0, 0, 0)),
            pl.BlockSpec((None,) + gate_expand.shape[1:], lambda b, k, i: (k, 0, 0)),
            pl.BlockSpec((TQ, _G_W), lambda b, k, i: (b * nq + i, 0)),
        ],
        out_specs=pl.BlockSpec((TQ, gw), lambda b, k, i: (b * nq + i, k)),
        out_shape=jax.ShapeDtypeStruct((B * S, N_KV_HEADS * gw), BF16),
        scratch_shapes=[
            pltpu.VMEM((GQA * TQ, 2 * LANES), BF16),
            pltpu.SMEM((S // TK + 2,), jnp.int32),
            pltpu.VMEM((2, GQA * TQ, LANES), F32),
            pltpu.VMEM((2, GQA * TQ, LANES), F32),
            pltpu.VMEM((GQA * TQ, TK), F32),
            pltpu.VMEM((GQA * TQ, TK), F32),
            pltpu.VMEM((S + WINDOW, LANES), BF16),
            pltpu.VMEM((S + WINDOW, VW), BF16),
        ],
        compiler_params=pltpu.CompilerParams(
            dimension_semantics=("arbitrary",) * 3, vmem_limit_bytes=VMEM_LIMIT),
        name="nsa",
    )(q, qfeat, kc_aug, vc, ks_aug, vs, kw_aug, vw, ov, cmask, tile_masks, gate_expand, gates)


_FF_CHUNK = 768


def _outproj_kernel(mc_ref, nsa_ref, x_ref, gn_ref, wo_ref, h_ref):
    mixed_n = _rms(nsa_ref[...].astype(F32), gn_ref[...]).astype(BF16)
    h_ref[...] = (x_ref[...] + _dot(mc_ref[...], wo_ref[0:CONV_CH, :])
                  + _dot(mixed_n, wo_ref[CONV_CH:2 * CONV_CH, :]))


def _outproj(mixed_c, nsa_out, x2, gn, wo, tm=1024):
    rows = x2.shape[0]
    row = lambda i: (i, 0)
    const = lambda i: (0, 0)
    return pl.pallas_call(
        _outproj_kernel,
        grid=(rows // tm,),
        in_specs=[
            pl.BlockSpec((tm, CONV_CH), row),
            pl.BlockSpec((tm, N_HEADS * HEAD_DIM), row),
            pl.BlockSpec((tm, D_MODEL), row),
            pl.BlockSpec((1, N_HEADS * HEAD_DIM), const),
            pl.BlockSpec((2 * CONV_CH, D_MODEL), const, pipeline_mode=pl.Buffered(1)),
        ],
        out_specs=pl.BlockSpec((tm, D_MODEL), row),
        out_shape=jax.ShapeDtypeStruct((rows, D_MODEL), F32),
        compiler_params=pltpu.CompilerParams(
            dimension_semantics=("arbitrary",), vmem_limit_bytes=VMEM_LIMIT),
        name="outproj",
    )(mixed_c, nsa_out, x2, gn, wo)


def _outffn_kernel(h_ref, g2_ref, wg_ref, wu_ref, wd_ref, gf_ref, o_ref):
    h = h_ref[...]
    u = _rms(h, g2_ref[...]).astype(BF16)
    acc = h
    for c in range(0, D_FF, _FF_CHUNK):
        sl = slice(c, min(c + _FF_CHUNK, D_FF))
        a = jax.nn.silu(_dot(u, wg_ref[:, sl])) * _dot(u, wu_ref[:, sl])
        acc = acc + _dot(a.astype(BF16), wd_ref[sl, :])
    o_ref[...] = _rms(acc, gf_ref[...])


def _outffn(h, g2, wg, wu, wd, gf, tm=512):
    rows = h.shape[0]
    row = lambda i: (i, 0)
    const = lambda i: (0, 0)
    resident = lambda shape: pl.BlockSpec(shape, const, pipeline_mode=pl.Buffered(1))
    return pl.pallas_call(
        _outffn_kernel,
        grid=(rows // tm,),
        in_specs=[
            pl.BlockSpec((tm, D_MODEL), row),
            pl.BlockSpec((1, D_MODEL), const),
            resident((D_MODEL, D_FF)),
            resident((D_MODEL, D_FF)),
            resident((D_FF, D_MODEL)),
            pl.BlockSpec((1, D_MODEL), const),
        ],
        out_specs=pl.BlockSpec((tm, D_MODEL), row),
        out_shape=jax.ShapeDtypeStruct((rows, D_MODEL), F32),
        compiler_params=pltpu.CompilerParams(
            dimension_semantics=("arbitrary",), vmem_limit_bytes=VMEM_LIMIT),
        name="outffn",
    )(h, g2, wg, wu, wd, gf)


def _key_features(pos, n_sel):
    f = np.zeros((pos.shape[0], LANES + n_sel), np.float32)
    f[:, _POS_HI] = pos // SEL_BLOCK
    f[:, _POS_LO] = pos % SEL_BLOCK
    if n_sel:
        f[np.arange(pos.shape[0]), LANES + pos // SEL_BLOCK] = 1.0
    return f


def _query_features():
    f = np.zeros((N_KV_HEADS, GQA * TQ, HEAD_DIM), np.float32)
    for k in range(N_KV_HEADS):
        for g in range(GQA):
            slope = 2.0 ** (-8.0 * (k * GQA + g + 1) / N_HEADS)
            f[k, g * TQ:(g + 1) * TQ, _POS_HI - HEAD_DIM] = slope * SEL_BLOCK
            f[k, g * TQ:(g + 1) * TQ, _POS_LO - HEAD_DIM] = slope
            f[k, g * TQ:(g + 1) * TQ, _PAD_MARK - HEAD_DIM] = MASK_BIAS
    return f


def _causal_masks(ncmp):
    r = np.arange(GQA * TQ)[:, None] % TQ
    cmask = (np.arange(ncmp)[None, :] - ((r + 1) // CMP_STRIDE)).astype(np.int32)
    c = np.arange(TK)[None, :]
    tile_masks = [np.where(c - r <= d * TQ, 0.0, NEG).astype(np.float32) for d in range(max(TK // TQ, 1))]
    return cmask, np.stack(tile_masks)


def _gate_expand():
    gw = GQA * HEAD_DIM
    e = np.zeros((N_KV_HEADS, 2 * _G_W, N_BRANCH * gw), np.float32)
    for k in range(N_KV_HEADS):
        for g in range(GQA):
            for br in range(N_BRANCH):
                for half in range(2):
                    e[k, half * _G_W + (k * GQA + g) * N_BRANCH + br,
                      br * gw + g * HEAD_DIM:br * gw + (g + 1) * HEAD_DIM] = 1.0
    return e


def _overlap(ncmp_pad, n_sel):
    nc = ncmp_pad - 1
    c_start = CMP_STRIDE * np.arange(nc)
    c_end = c_start + CMP_BLOCK - 1
    s_start = SEL_BLOCK * np.arange(n_sel)
    ov = np.minimum(c_end[:, None] + 1, s_start[None, :] + SEL_BLOCK) - np.maximum(c_start[:, None], s_start[None, :])
    out = np.zeros((ncmp_pad, n_sel), np.float32)
    out[:nc] = np.clip(ov, 0, None).astype(np.float32) / CMP_BLOCK
    return out


def _layer(x2, B, S, norm1_g, w_in, conv_w, k_cmp_pos, k_cmp_w1, k_cmp_w2, v_cmp_pos, v_cmp_w1, v_cmp_w2,
           gn_conv_g, later_weights):
    n_sel = S // SEL_BLOCK
    nch = S // CMP_STRIDE
    hd = HEAD_DIM

    kfeat = jnp.asarray(_key_features(np.arange(S), n_sel), BF16)
    mixed_c, q, gates, kvc, ks_aug, vs, kw_aug, vw, *later_bf16 = _inproj(
        x2, norm1_g.reshape(1, D_MODEL), w_in.astype(BF16), kfeat, conv_w, gn_conv_g.reshape(1, CONV_CH),
        later_weights)

    w1 = jnp.stack([k_cmp_w1, v_cmp_w1]).astype(BF16)
    w2 = jnp.stack([k_cmp_w2, v_cmp_w2]).astype(BF16)
    pos = jnp.stack([k_cmp_pos, v_cmp_pos]).reshape(2, 1, CMP_BLOCK * hd)
    pos = jnp.broadcast_to(pos, (2, SUBLANES, CMP_BLOCK * hd)).astype(BF16)
    c_end = CMP_STRIDE * np.arange(nch) + CMP_BLOCK - 1
    cfeat = jnp.asarray(_key_features(c_end, 0), BF16)
    kc_aug, vc = _compress(kvc, pos, w1, w2, cfeat, B, S)

    cmask, tile_masks = _causal_masks(nch)

    nsa_out = _nsa(q, jnp.asarray(_query_features(), BF16), kc_aug, vc,
                   ks_aug.reshape(N_KV_HEADS, B, S, 2 * LANES), vs.reshape(N_KV_HEADS, B, S, LANES),
                   kw_aug.reshape(N_KV_HEADS, B, S, LANES), vw.reshape(N_KV_HEADS, B, S, VW),
                   jnp.asarray(_overlap(nch, n_sel), BF16), jnp.asarray(cmask),
                   jnp.asarray(tile_masks), jnp.asarray(_gate_expand(), BF16), gates, B, S)
    return mixed_c, nsa_out, later_bf16


def kernel(x, norm1_g, w_in, conv_w, k_cmp_pos, k_cmp_w1, k_cmp_w2, v_cmp_pos, v_cmp_w1, v_cmp_w2,
           gn_conv_g, gn_nsa_g, w_out, norm2_g, w_gate, w_up, w_down, norm_f_g):
    B, S, _ = x.shape
    depth = norm1_g.shape[0]
    assert depth == 1, "the final norm is fused into the (single) layer's FFN kernel"
    x2 = x.reshape(B * S, D_MODEL)
    mixed_c, nsa_out, (wo, wg, wu, wd) = _layer(
        x2, B, S, norm1_g[0], w_in[0], conv_w[0], k_cmp_pos[0], k_cmp_w1[0], k_cmp_w2[0], v_cmp_pos[0],
        v_cmp_w1[0], v_cmp_w2[0], gn_conv_g[0], [w_out[0], w_gate[0], w_up[0], w_down[0]])
    h = _outproj(mixed_c, nsa_out, x2, gn_nsa_g[0].reshape(1, N_HEADS * HEAD_DIM), wo)
    out = _outffn(h, norm2_g[0].reshape(1, D_MODEL), wg, wu, wd, norm_f_g.reshape(1, D_MODEL))
    return out.reshape(B, S, D_MODEL)
```

```python
import functools

import numpy as np
import jax
import jax.numpy as jnp
from jax import lax
from jax.experimental import pallas as pl
from jax.experimental.pallas import tpu as pltpu

D_MODEL = 1024
CONV_CH = 512
CONV_K = 3
N_HEADS = 8
HEAD_DIM = 64
N_KV_HEADS = 2
GQA = 4
N_BRANCH = 3
CMP_BLOCK = 32
CMP_STRIDE = 16
CMP_HIDDEN = 256
SEL_BLOCK = 64
SEL_TOP = 16
WINDOW = 512
D_FF = 2816
NORM_EPS = 1e-6

F32 = jnp.float32
BF16 = jnp.bfloat16

LANES = 128
VW = 2 * LANES
SUBLANES = 8
VMEM_LIMIT = 56 * 1024 * 1024

TQ = 256
TK = 256
NEG = -1e30
MASK_BIAS = -float(2 ** 30)
_POS_HI, _POS_LO, _PAD_MARK = HEAD_DIM, HEAD_DIM + 1, HEAD_DIM + 2
_FLAG_BITS = 16

_CONV_W = 3 * CONV_CH
_Q_OFF = _CONV_W
_KVC_OFF = _Q_OFF + N_HEADS * HEAD_DIM
_KV_OFF = _KVC_OFF + 2 * N_KV_HEADS * HEAD_DIM
_G_OFF = _KV_OFF + 4 * N_KV_HEADS * HEAD_DIM
_G_W = N_HEADS * N_BRANCH
_PROJ_W = _G_OFF + _G_W


def _rms(x, g):
    return x * lax.rsqrt(jnp.mean(x * x, axis=-1, keepdims=True) + NORM_EPS) * g


def _dot(a, b):
    return jnp.dot(a, b, preferred_element_type=F32)


def _value_rows(v2, h, wide):
    lane = lax.broadcasted_iota(jnp.int32, v2.shape, 1)
    swapped = pltpu.roll(v2, HEAD_DIM, axis=1)
    if not wide:
        return jnp.where(lane < HEAD_DIM, v2 if h == 0 else swapped, 1.0).astype(BF16)
    dup = jnp.where((lane < HEAD_DIM) == (h == 0), v2, swapped)
    return jnp.concatenate([dup.astype(BF16), jnp.ones(v2.shape, BF16)], axis=1)


def _dot_nt(a, b):
    return lax.dot_general(a, b, (((1,), (1,)), ((), ())), preferred_element_type=F32)


def _inproj_kernel(seq_tiles, n_cast, x_ref, xp_ref, g_ref, w_ref, kf_ref, cw_ref, gc_ref, *refs):
    cast_in, refs = refs[:n_cast], refs[n_cast:]
    (mixc_ref, q_ref, gate_ref, kvc_ref, ks_ref, vs_ref, kw_ref, vw_ref), cast_out = refs[:8], refs[8:]
    for src_ref, dst_ref in zip(cast_in, cast_out):
        dst_ref[...] = src_ref[...].astype(BF16)

    u = _rms(x_ref[...], g_ref[...]).astype(BF16)

    cbch = _dot(u, w_ref[:, 0:_CONV_W])
    cu = cbch[:, CONV_CH:2 * CONV_CH] * cbch[:, 2 * CONV_CH:3 * CONV_CH]
    up = _rms(xp_ref[...], g_ref[...]).astype(BF16)
    chp = _dot(up, w_ref[:, CONV_CH:_CONV_W])
    starts_sequence = pl.program_id(0) % seq_tiles == 0
    prev = jnp.where(starts_sequence, 0.0, chp[:, 0:CONV_CH] * chp[:, CONV_CH:2 * CONV_CH])
    r = lax.broadcasted_iota(jnp.int32, cu.shape, 0)
    cu1 = jnp.where(r == 0, prev[7:8, :], pltpu.roll(cu, 1, axis=0))
    cu2 = pltpu.roll(cu, 2, axis=0)
    cu2 = jnp.where(r == 0, prev[6:7, :], jnp.where(r == 1, prev[7:8, :], cu2))
    cw = cw_ref[...]
    y = cbch[:, 0:CONV_CH] * (cw[0:1, :] * cu2 + cw[1:2, :] * cu1 + cw[2:3, :] * cu)
    mixc_ref[...] = _rms(y, gc_ref[...]).astype(BF16)

    q_ref[...] = _dot(u, w_ref[:, _Q_OFF:_KVC_OFF]).astype(BF16)
    kvc_ref[...] = _dot(u, w_ref[:, _KVC_OFF:_KV_OFF])
    gate_ref[...] = jax.nn.sigmoid(_dot(u, w_ref[:, _G_OFF:_PROJ_W]))
    kv = _dot(u, w_ref[:, _KV_OFF:_G_OFF])
    kf = kf_ref[...]
    hd, kvw = HEAD_DIM, N_KV_HEADS * HEAD_DIM
    for h in range(N_KV_HEADS):
        ks_ref[h] = kf
        ks_ref[h, :, 0:hd] = kv[:, h * hd:(h + 1) * hd].astype(BF16)
        vs_ref[h] = _value_rows(kv[:, kvw:2 * kvw], h, False)
        kw_ref[h] = kf[:, 0:LANES]
        kw_ref[h, :, 0:hd] = kv[:, 2 * kvw + h * hd:2 * kvw + (h + 1) * hd].astype(BF16)
        vw_ref[h] = _value_rows(kv[:, 3 * kvw:4 * kvw], h, True)


def _inproj(x2, g, w, kfeat, conv_w, gc, to_cast, tm=1024):
    rows = x2.shape[0]
    steps = rows // tm
    seq_tiles = kfeat.shape[0] // tm
    row = lambda i: (i, 0)
    head_row = lambda i: (0, i, 0)
    const = lambda i: (0, 0)
    cast_specs = [pl.BlockSpec((a.shape[0] // steps, a.shape[1]), row) for a in to_cast]
    return pl.pallas_call(
        functools.partial(_inproj_kernel, seq_tiles, len(to_cast)),
        grid=(steps,),
        in_specs=[
            pl.BlockSpec((tm, D_MODEL), row),
            pl.BlockSpec((SUBLANES, D_MODEL), lambda i: (jnp.maximum(i * (tm // SUBLANES) - 1, 0), 0)),
            pl.BlockSpec((1, D_MODEL), const),
            pl.BlockSpec((D_MODEL, _PROJ_W), const, pipeline_mode=pl.Buffered(1)),
            pl.BlockSpec((tm, 2 * LANES), lambda i: (i % seq_tiles, 0)),
            pl.BlockSpec((CONV_K, CONV_CH), const),
            pl.BlockSpec((1, CONV_CH), const),
        ] + cast_specs,
        out_specs=[
            pl.BlockSpec((tm, CONV_CH), row),
            pl.BlockSpec((tm, N_HEADS * HEAD_DIM), row),
            pl.BlockSpec((tm, _G_W), row),
            pl.BlockSpec((tm, _KV_OFF - _KVC_OFF), row),
            pl.BlockSpec((N_KV_HEADS, tm, 2 * LANES), head_row),
            pl.BlockSpec((N_KV_HEADS, tm, LANES), head_row),
            pl.BlockSpec((N_KV_HEADS, tm, LANES), head_row),
            pl.BlockSpec((N_KV_HEADS, tm, VW), head_row),
        ] + cast_specs,
        out_shape=[
            jax.ShapeDtypeStruct((rows, CONV_CH), BF16),
            jax.ShapeDtypeStruct((rows, N_HEADS * HEAD_DIM), BF16),
            jax.ShapeDtypeStruct((rows, _G_W), F32),
            jax.ShapeDtypeStruct((rows, _KV_OFF - _KVC_OFF), F32),
            jax.ShapeDtypeStruct((N_KV_HEADS, rows, 2 * LANES), BF16),
            jax.ShapeDtypeStruct((N_KV_HEADS, rows, LANES), BF16),
            jax.ShapeDtypeStruct((N_KV_HEADS, rows, LANES), BF16),
            jax.ShapeDtypeStruct((N_KV_HEADS, rows, VW), BF16),
        ] + [jax.ShapeDtypeStruct(a.shape, BF16) for a in to_cast],
        compiler_params=pltpu.CompilerParams(
            dimension_semantics=("arbitrary",), vmem_limit_bytes=VMEM_LIMIT),
        name="inproj",
    )(x2, x2, g, w, kfeat, conv_w, gc, *to_cast)


def _compress_kernel(xk_ref, xv_ref, pos_ref, w1_ref, w2_ref, cf_ref, kc_ref, vc_ref):
    nch = xk_ref.shape[0] // CMP_STRIDE
    hd = HEAD_DIM
    first = [jnp.zeros((nch, CMP_HIDDEN), F32) for _ in range(2 * N_KV_HEADS)]
    second = [jnp.zeros((nch, CMP_HIDDEN), F32) for _ in range(2 * N_KV_HEADS)]
    group = 2 * LANES // hd
    for l0 in range(0, CMP_STRIDE, group):
        for s, x_ref in enumerate((xk_ref, xv_ref)):
            xs = [x_ref[pl.ds(l0 + j, nch, stride=CMP_STRIDE), :].astype(BF16) for j in range(group)]
            for h in range(N_KV_HEADS):
                c = s * N_KV_HEADS + h
                xh = jnp.concatenate([x[:, h * hd:(h + 1) * hd] for x in xs], axis=1)
                first[c] = first[c] + _dot(xh, w1_ref[s, l0 * hd:(l0 + group) * hd, :])
                second[c] = second[c] + _dot(xh, w1_ref[s, (CMP_STRIDE + l0) * hd:(CMP_STRIDE + l0 + group) * hd, :])
    for s in range(2):
        pb = _dot(pos_ref[s], w1_ref[s])[0:1, :]
        outs = []
        for h in range(N_KV_HEADS):
            c = s * N_KV_HEADS + h
            hid = first[c] + pltpu.roll(second[c], nch - 1, axis=0) + pb
            outs.append(_dot(jax.nn.gelu(hid, approximate=True).astype(BF16), w2_ref[s]))
        for h in range(N_KV_HEADS):
            if s == 0:
                kc_ref[h] = cf_ref[...]
                kc_ref[h, :, 0:hd] = outs[h].astype(BF16)
            else:
                vc_ref[h] = _value_rows(jnp.concatenate(outs, axis=1), h, True)


def _compress(kvc, pos, w1, w2, cfeat, B, S):
    nch = S // CMP_STRIDE
    full = lambda *shape: pl.BlockSpec(shape, lambda b: (0,) * len(shape))
    return pl.pallas_call(
        _compress_kernel,
        grid=(B,),
        in_specs=[
            pl.BlockSpec((S, LANES), lambda b: (b, 0)),
            pl.BlockSpec((S, LANES), lambda b: (b, 1)),
            full(*pos.shape), full(*w1.shape), full(*w2.shape), full(*cfeat.shape),
        ],
        out_specs=[
            pl.BlockSpec((None, N_KV_HEADS, nch, LANES), lambda b: (b, 0, 0, 0)),
            pl.BlockSpec((None, N_KV_HEADS, nch, VW), lambda b: (b, 0, 0, 0)),
        ],
        out_shape=[
            jax.ShapeDtypeStruct((B, N_KV_HEADS, nch, LANES), BF16),
            jax.ShapeDtypeStruct((B, N_KV_HEADS, nch, VW), BF16),
        ],
        compiler_params=pltpu.CompilerParams(
            dimension_semantics=("arbitrary",), vmem_limit_bytes=VMEM_LIMIT),
        name="compress",
    )(kvc, kvc, pos, w1, w2, cfeat)


def _hi_lo(x):
    hi = x.astype(BF16)
    return jnp.concatenate([hi, (x - hi.astype(F32)).astype(BF16)], axis=1)


def _head_layout(acc):
    low_half = lax.broadcasted_iota(jnp.int32, (TQ, LANES), 1) < HEAD_DIM
    head = lambda x, g: x[g * TQ:(g + 1) * TQ]
    if acc.shape[1] == VW:
        x = acc[:, 0:LANES] * (1.0 / jnp.maximum(acc[:, LANES:VW], 1e-30))
        pairs = [jnp.where(low_half, head(x, 2 * j), head(x, 2 * j + 1)) for j in range(GQA // 2)]
    else:
        rot = pltpu.roll(acc, HEAD_DIM, axis=1)
        pairs = [jnp.where(low_half, head(acc, 2 * j) * (1.0 / head(rot, 2 * j)),
                           head(rot, 2 * j + 1) * (1.0 / head(acc, 2 * j + 1))) for j in range(GQA // 2)]
    return jnp.concatenate(pairs, axis=1)


def _exchange(v, i, j, descending):
    hi, lo = jnp.maximum(v[i], v[j]), jnp.minimum(v[i], v[j])
    v[i], v[j] = (hi, lo) if descending else (lo, hi)


def _bitonic_merge_desc(v):
    n = len(v)
    j = n // 2
    while j >= 1:
        for i in range(n):
            if i ^ j > i:
                _exchange(v, i, i ^ j, True)
        j //= 2


def _kth_largest(x, kth, fillers=()):
    n = x.shape[0]
    assert n == kth * SUBLANES and kth & (kth - 1) == 0
    v = [x[r * SUBLANES:(r + 1) * SUBLANES, :] for r in range(kth)]
    fillers = list(fillers)
    layers = 0
    k = 2
    while k <= kth:
        j = k // 2
        while j >= 1:
            for i in range(kth):
                if i ^ j > i:
                    _exchange(v, i, i ^ j, (i & k) == 0)
            j //= 2
            layers += 1
            if layers % 3 == 0 and fillers:
                fillers.pop(0)()
        k *= 2
    shift = SUBLANES // 2
    while shift >= 1:
        v = [jnp.maximum(v[i], pltpu.roll(v[kth - 1 - i], shift, axis=0)) for i in range(kth)]
        if shift > 1:
            _bitonic_merge_desc(v)
        if fillers:
            fillers.pop(0)()
        shift //= 2
    for filler in fillers:
        filler()
    out = v[0]
    for i in range(1, kth):
        out = jnp.minimum(out, v[i])
    return out[0:1, :]


def _nsa_kernel(q_ref, qf_ref, kc_ref, vc_ref, ks_ref, vs_ref, kw_ref, vw_ref, ov_ref, cm_ref, mb_ref,
                ge_ref, g_ref, o_ref, qa_ref, tiles_ref, m_ref, acc_ref, sa_ref, sb_ref, kwp_ref, vwp_ref):
    i = pl.program_id(2)
    t0 = i * TQ
    rows = GQA * TQ
    n_sel = ov_ref.shape[1]
    ncmp = kc_ref.shape[0]

    @pl.when(i == 0)
    def _():
        lane = lax.broadcasted_iota(jnp.int32, (WINDOW, LANES), 1)
        kwp_ref[0:WINDOW, :] = jnp.where(lane == _PAD_MARK, 1.0, 0.0).astype(BF16)
        kwp_ref[WINDOW:, :] = kw_ref[...]
        vwp_ref[0:WINDOW, :] = jnp.zeros((WINDOW, VW), BF16)
        vwp_ref[WINDOW:, :] = vw_ref[...]
        qa_ref[:, HEAD_DIM:LANES] = qf_ref[...]

    q = q_ref[...]
    for g in range(GQA):
        qg = q[:, g * HEAD_DIM:(g + 1) * HEAD_DIM].astype(F32) * (HEAD_DIM ** -0.5)
        qa_ref[g * TQ:(g + 1) * TQ, 0:HEAD_DIM] = qg.astype(BF16)
    qa1 = qa_ref[:, 0:LANES]

    diag_ok = (lax.broadcasted_iota(jnp.int32, (rows, TQ), 1)
               <= (lax.broadcasted_iota(jnp.int32, (rows, TQ), 0) & (TQ - 1)))


    wk = TQ + WINDOW
    w0 = pl.multiple_of(t0, TQ)
    win = {}

    def win_scores():
        s_w = _dot_nt(qa1, kwp_ref[pl.ds(w0, wk), :])
        win["s"] = (jnp.where(diag_ok, NEG, s_w[:, 0:TQ]),
                    s_w[:, TQ:WINDOW], jnp.where(diag_ok, s_w[:, WINDOW:wk], NEG))

    def win_max():
        s_old, s_mid, s_new = win["s"]
        win["m"] = jnp.maximum(jnp.maximum(jnp.max(s_old, axis=1, keepdims=True),
                                           jnp.max(s_mid, axis=1, keepdims=True)),
                               jnp.max(s_new, axis=1, keepdims=True))

    def win_exp():
        win["e"] = jnp.concatenate([jnp.exp(s - win["m"]) for s in win["s"]], axis=1).astype(BF16)

    def win_pv():
        win["acc"] = _dot(win["e"], vwp_ref[pl.ds(w0, wk), :])

    s_c = _dot_nt(qa1, kc_ref[...])
    s_c = jnp.where(cm_ref[...] <= (TQ // CMP_STRIDE) * i - 2, s_c, NEG)
    m_c = jnp.maximum(jnp.max(s_c, axis=1, keepdims=True), 0.5 * NEG)
    e_c = jnp.exp(s_c - m_c)
    acc_c = _dot(e_c.astype(BF16), vc_ref[...])
    inv_c = 1.0 / jnp.maximum(acc_c[:, LANES:VW], 1e-30)
    p_c = e_c * jnp.concatenate([inv_c] * (ncmp // LANES), axis=1)

    p_sum = p_c[0:TQ] + p_c[TQ:2 * TQ] + p_c[2 * TQ:3 * TQ] + p_c[3 * TQ:4 * TQ]
    p_hi = p_sum.astype(BF16)
    p_lo = (p_sum - p_hi.astype(F32)).astype(BF16)
    imp = _dot(p_hi, ov_ref[...]) + _dot(p_lo, ov_ref[...])
    imp_t = imp.T
    jb = lax.broadcasted_iota(jnp.int32, (n_sel, TQ), 0)
    tq = t0 + lax.broadcasted_iota(jnp.int32, (n_sel, TQ), 1)
    imp_t = jnp.where(SEL_BLOCK * jb > tq, -jnp.inf, imp_t)
    forced = (jb == 0) | (jb == (tq >> (SEL_BLOCK.bit_length() - 1)))
    imp_t = jnp.where(forced, jnp.inf, imp_t)

    thr = _kth_largest(imp_t, SEL_TOP, (win_scores, win_max, win_exp, win_pv))
    acc_w = win["acc"]
    above = imp_t > thr
    tied = imp_t == thr
    n_above = jnp.sum(jnp.where(above, 1.0, 0.0), axis=0, keepdims=True)
    lower = (lax.broadcasted_iota(jnp.int32, (n_sel, n_sel), 1)
             < lax.broadcasted_iota(jnp.int32, (n_sel, n_sel), 0))
    tied_before = _dot(jnp.where(lower, 1.0, 0.0).astype(BF16), jnp.where(tied, 1.0, 0.0).astype(BF16))
    take_tie = jnp.where(tied_before + n_above < float(SEL_TOP), 1.0, 0.0)
    sel_t = jnp.where(above, 1.0, jnp.where(tied, take_tie, 0.0))
    sel_q = sel_t.T
    bias = ((1.0 - sel_q) * MASK_BIAS).astype(BF16)
    for g in range(GQA):
        qa_ref[g * TQ:(g + 1) * TQ, LANES:2 * LANES] = bias

    n_past = t0 // TK
    first_mask = (t0 - n_past * TK) // TQ
    tiles_ref[0] = n_past

    def scores(j, s_ref, mask=None):
        k0 = pl.multiple_of(tiles_ref[j] * TK, TK)
        s = _dot_nt(qa_ref[...], ks_ref[pl.ds(k0, TK), :])
        s_ref[...] = s if mask is None else s + mask

    scores(0, sa_ref, mb_ref[first_mask])

    bpt = TK // SEL_BLOCK
    n_tiles = n_sel // bpt
    blk_cnt = _dot(jnp.ones((SUBLANES, TQ), BF16), sel_q.astype(BF16))
    in_tile = (lax.broadcasted_iota(jnp.int32, (n_sel, LANES), 0) // bpt
               == lax.broadcasted_iota(jnp.int32, (n_sel, LANES), 1))
    tile_cnt = _dot(jnp.where(blk_cnt > 0.5, 1.0, 0.0).astype(BF16),
                    jnp.where(in_tile, 1.0, 0.0).astype(BF16))
    lane = lax.broadcasted_iota(jnp.int32, (1, LANES), 1)
    bit = jnp.where(tile_cnt[0:1, :] > 0.5, jnp.left_shift(1, lane & (_FLAG_BITS - 1)).astype(F32), 0.0)
    words = [jnp.sum(jnp.where((lane >= _FLAG_BITS * w) & (lane < _FLAG_BITS * (w + 1)), bit, 0.0)
                     ).astype(jnp.int32) for w in range(-(-n_tiles // _FLAG_BITS))]
    n_tasks = jnp.int32(1)
    for kt in range(n_tiles - 1):
        used = (words[kt // _FLAG_BITS] >> (kt % _FLAG_BITS)) & 1
        tiles_ref[n_tasks] = kt
        n_tasks = n_tasks + jnp.where(kt < n_past, used, 0)
    tiles_ref[n_tasks] = 0
    tiles_ref[n_tasks + 1] = 0

    m_ref[...] = jnp.full(m_ref.shape, NEG, F32)
    acc_ref[...] = jnp.zeros(acc_ref.shape, F32)

    gates = _dot(_hi_lo(g_ref[...]), ge_ref[...])
    gw = GQA * HEAD_DIM
    merged = gates[:, 0:gw] * _head_layout(acc_c) + gates[:, 2 * gw:3 * gw] * _head_layout(acc_w)
    gates_sel = gates[:, gw:2 * gw]

    def accumulate(j, s_ref):
        k0 = pl.multiple_of(tiles_ref[j] * TK, TK)
        slot = jnp.where(j < n_tasks, 0, 1)
        s = s_ref[...]
        m = m_ref[slot]
        m_new = jnp.maximum(m, jnp.max(s, axis=1, keepdims=True))
        p = jnp.concatenate([jnp.exp((s[:, c:c + LANES] - m_new).astype(BF16)) for c in range(0, TK, LANES)],
                            axis=1)
        acc_ref[slot] = (jnp.exp(m - m_new) * acc_ref[slot]
                         + _dot(p, vs_ref[pl.ds(k0, TK), :]))
        m_ref[slot] = m_new

    def pair(jj, c):
        a = 2 * jj
        scores(a + 1, sb_ref)
        accumulate(a, sa_ref)
        scores(a + 2, sa_ref)
        accumulate(a + 1, sb_ref)
        return c

    lax.fori_loop(0, (n_tasks + 1) // 2, pair, 0)

    o_ref[...] = (merged + gates_sel * _head_layout(acc_ref[0])).astype(o_ref.dtype)


def _nsa(q, qfeat, kc_aug, vc, ks_aug, vs, kw_aug, vw, ov, cmask, tile_masks, gate_expand, gates, B, S):
    nq = S // TQ
    gw = GQA * HEAD_DIM
    ncmp = kc_aug.shape[2]
    n_sel = ov.shape[1]
    bk = lambda b, k, i: (b, k, 0, 0)
    kb = lambda b, k, i: (k, b, 0, 0)
    return pl.pallas_call(
        _nsa_kernel,
        grid=(B, N_KV_HEADS, nq),
        in_specs=[
            pl.BlockSpec((TQ, gw), lambda b, k, i: (b * nq + i, k)),
            pl.BlockSpec((None, GQA * TQ, HEAD_DIM), lambda b, k, i: (k, 0, 0)),
            pl.BlockSpec((None, None, ncmp, LANES), bk),
            pl.BlockSpec((None, None, ncmp, VW), bk),
            pl.BlockSpec((None, None, S, 2 * LANES), kb),
            pl.BlockSpec((None, None, S, LANES), kb),
            pl.BlockSpec((None, None, S, LANES), kb),
            pl.BlockSpec((None, None, S, VW), kb),
            pl.BlockSpec((ncmp, n_sel), lambda b, k, i: (0, 0)),
            pl.BlockSpec((GQA * TQ, ncmp), lambda b, k, i: (0, 0)),
            pl.BlockSpec(tile_masks.shape, lambda b, k, i: (0, 0, 0)),
            pl.BlockSpec((None,) + gate_expand.shape[1:], lambda b, k, i: (k, 0, 0)),
            pl.BlockSpec((TQ, _G_W), lambda b, k, i: (b * nq + i, 0)),
        ],
        out_specs=pl.BlockSpec((TQ, gw), lambda b, k, i: (b * nq + i, k)),
        out_shape=jax.ShapeDtypeStruct((B * S, N_KV_HEADS * gw), BF16),
        scratch_shapes=[
            pltpu.VMEM((GQA * TQ, 2 * LANES), BF16),
            pltpu.SMEM((S // TK + 2,), jnp.int32),
            pltpu.VMEM((2, GQA * TQ, LANES), F32),
            pltpu.VMEM((2, GQA * TQ, LANES), F32),
            pltpu.VMEM((GQA * TQ, TK), F32),
            pltpu.VMEM((GQA * TQ, TK), F32),
            pltpu.VMEM((S + WINDOW, LANES), BF16),
            pltpu.VMEM((S + WINDOW, VW), BF16),
        ],
        compiler_params=pltpu.CompilerParams(
            dimension_semantics=("arbitrary",) * 3, vmem_limit_bytes=VMEM_LIMIT),
        name="nsa",
    )(q, qfeat, kc_aug, vc, ks_aug, vs, kw_aug, vw, ov, cmask, tile_masks, gate_expand, gates)


_FF_CHUNK = 768


def _outproj_kernel(mc_ref, nsa_ref, x_ref, gn_ref, wo_ref, h_ref):
    mixed_n = _rms(nsa_ref[...].astype(F32), gn_ref[...]).astype(BF16)
    h_ref[...] = (x_ref[...] + _dot(mc_ref[...], wo_ref[0:CONV_CH, :])
                  + _dot(mixed_n, wo_ref[CONV_CH:2 * CONV_CH, :]))


def _outproj(mixed_c, nsa_out, x2, gn, wo, tm=1024):
    rows = x2.shape[0]
    row = lambda i: (i, 0)
    const = lambda i: (0, 0)
    return pl.pallas_call(
        _outproj_kernel,
        grid=(rows // tm,),
        in_specs=[
            pl.BlockSpec((tm, CONV_CH), row),
            pl.BlockSpec((tm, N_HEADS * HEAD_DIM), row),
            pl.BlockSpec((tm, D_MODEL), row),
            pl.BlockSpec((1, N_HEADS * HEAD_DIM), const),
            pl.BlockSpec((2 * CONV_CH, D_MODEL), const, pipeline_mode=pl.Buffered(1)),
        ],
        out_specs=pl.BlockSpec((tm, D_MODEL), row),
        out_shape=jax.ShapeDtypeStruct((rows, D_MODEL), F32),
        compiler_params=pltpu.CompilerParams(
            dimension_semantics=("arbitrary",), vmem_limit_bytes=VMEM_LIMIT),
        name="outproj",
    )(mixed_c, nsa_out, x2, gn, wo)


def _outffn_kernel(h_ref, g2_ref, wg_ref, wu_ref, wd_ref, gf_ref, o_ref):
    h = h_ref[...]
    u = _rms(h, g2_ref[...]).astype(BF16)
    acc = h
    for c in range(0, D_FF, _FF_CHUNK):
        sl = slice(c, min(c + _FF_CHUNK, D_FF))
        a = jax.nn.silu(_dot(u, wg_ref[:, sl])) * _dot(u, wu_ref[:, sl])
        acc = acc + _dot(a.astype(BF16), wd_ref[sl, :])
    o_ref[...] = _rms(acc, gf_ref[...])


def _outffn(h, g2, wg, wu, wd, gf, tm=512):
    rows = h.shape[0]
    row = lambda i: (i, 0)
    const = lambda i: (0, 0)
    resident = lambda shape: pl.BlockSpec(shape, const, pipeline_mode=pl.Buffered(1))
    return pl.pallas_call(
        _outffn_kernel,
        grid=(rows // tm,),
        in_specs=[
            pl.BlockSpec((tm, D_MODEL), row),
            pl.BlockSpec((1, D_MODEL), const),
            resident((D_MODEL, D_FF)),
            resident((D_MODEL, D_FF)),
            resident((D_FF, D_MODEL)),
            pl.BlockSpec((1, D_MODEL), const),
        ],
        out_specs=pl.BlockSpec((tm, D_MODEL), row),
        out_shape=jax.ShapeDtypeStruct((rows, D_MODEL), F32),
        compiler_params=pltpu.CompilerParams(
            dimension_semantics=("arbitrary",), vmem_limit_bytes=VMEM_LIMIT),
        name="outffn",
    )(h, g2, wg, wu, wd, gf)


def _key_features(pos, n_sel):
    f = np.zeros((pos.shape[0], LANES + n_sel), np.float32)
    f[:, _POS_HI] = pos // SEL_BLOCK
    f[:, _POS_LO] = pos % SEL_BLOCK
    if n_sel:
        f[np.arange(pos.shape[0]), LANES + pos // SEL_BLOCK] = 1.0
    return f


def _query_features():
    f = np.zeros((N_KV_HEADS, GQA * TQ, HEAD_DIM), np.float32)
    for k in range(N_KV_HEADS):
        for g in range(GQA):
            slope = 2.0 ** (-8.0 * (k * GQA + g + 1) / N_HEADS)
            f[k, g * TQ:(g + 1) * TQ, _POS_HI - HEAD_DIM] = slope * SEL_BLOCK
            f[k, g * TQ:(g + 1) * TQ, _POS_LO - HEAD_DIM] = slope
            f[k, g * TQ:(g + 1) * TQ, _PAD_MARK - HEAD_DIM] = MASK_BIAS
    return f


def _causal_masks(ncmp):
    r = np.arange(GQA * TQ)[:, None] % TQ
    cmask = (np.arange(ncmp)[None, :] - ((r + 1) // CMP_STRIDE)).astype(np.int32)
    c = np.arange(TK)[None, :]
    tile_masks = [np.where(c - r <= d * TQ, 0.0, NEG).astype(np.float32) for d in range(max(TK // TQ, 1))]
    return cmask, np.stack(tile_masks)


def _gate_expand():
    gw = GQA * HEAD_DIM
    e = np.zeros((N_KV_HEADS, 2 * _G_W, N_BRANCH * gw), np.float32)
    for k in range(N_KV_HEADS):
        for g in range(GQA):
            for br in range(N_BRANCH):
                for half in range(2):
                    e[k, half * _G_W + (k * GQA + g) * N_BRANCH + br,
                      br * gw + g * HEAD_DIM:br * gw + (g + 1) * HEAD_DIM] = 1.0
    return e


def _overlap(ncmp_pad, n_sel):
    nc = ncmp_pad - 1
    c_start = CMP_STRIDE * np.arange(nc)
    c_end = c_start + CMP_BLOCK - 1
    s_start = SEL_BLOCK * np.arange(n_sel)
    ov = np.minimum(c_end[:, None] + 1, s_start[None, :] + SEL_BLOCK) - np.maximum(c_start[:, None], s_start[None, :])
    out = np.zeros((ncmp_pad, n_sel), np.float32)
    out[:nc] = np.clip(ov, 0, None).astype(np.float32) / CMP_BLOCK
    return out


def _layer(x2, B, S, norm1_g, w_in, conv_w, k_cmp_pos, k_cmp_w1, k_cmp_w2, v_cmp_pos, v_cmp_w1, v_cmp_w2,
           gn_conv_g, later_weights):
    n_sel = S // SEL_BLOCK
    nch = S // CMP_STRIDE
    hd = HEAD_DIM

    kfeat = jnp.asarray(_key_features(np.arange(S), n_sel), BF16)
    mixed_c, q, gates, kvc, ks_aug, vs, kw_aug, vw, *later_bf16 = _inproj(
        x2, norm1_g.reshape(1, D_MODEL), w_in.astype(BF16), kfeat, conv_w, gn_conv_g.reshape(1, CONV_CH),
        later_weights)

    w1 = jnp.stack([k_cmp_w1, v_cmp_w1]).astype(BF16)
    w2 = jnp.stack([k_cmp_w2, v_cmp_w2]).astype(BF16)
    pos = jnp.stack([k_cmp_pos, v_cmp_pos]).reshape(2, 1, CMP_BLOCK * hd)
    pos = jnp.broadcast_to(pos, (2, SUBLANES, CMP_BLOCK * hd)).astype(BF16)
    c_end = CMP_STRIDE * np.arange(nch) + CMP_BLOCK - 1
    cfeat = jnp.asarray(_key_features(c_end, 0), BF16)
    kc_aug, vc = _compress(kvc, pos, w1, w2, cfeat, B, S)

    cmask, tile_masks = _causal_masks(nch)

    nsa_out = _nsa(q, jnp.asarray(_query_features(), BF16), kc_aug, vc,
                   ks_aug.reshape(N_KV_HEADS, B, S, 2 * LANES), vs.reshape(N_KV_HEADS, B, S, LANES),
                   kw_aug.reshape(N_KV_HEADS, B, S, LANES), vw.reshape(N_KV_HEADS, B, S, VW),
                   jnp.asarray(_overlap(nch, n_sel), BF16), jnp.asarray(cmask),
                   jnp.asarray(tile_masks), jnp.asarray(_gate_expand(), BF16), gates, B, S)
    return mixed_c, nsa_out, later_bf16


def kernel(x, norm1_g, w_in, conv_w, k_cmp_pos, k_cmp_w1, k_cmp_w2, v_cmp_pos, v_cmp_w1, v_cmp_w2,
           gn_conv_g, gn_nsa_g, w_out, norm2_g, w_gate, w_up, w_down, norm_f_g):
    B, S, _ = x.shape
    depth = norm1_g.shape[0]
    assert depth == 1, "the final norm is fused into the (single) layer's FFN kernel"
    x2 = x.reshape(B * S, D_MODEL)
    mixed_c, nsa_out, (wo, wg, wu, wd) = _layer(
        x2, B, S, norm1_g[0], w_in[0], conv_w[0], k_cmp_pos[0], k_cmp_w1[0], k_cmp_w2[0], v_cmp_pos[0],
        v_cmp_w1[0], v_cmp_w2[0], gn_conv_g[0], [w_out[0], w_gate[0], w_up[0], w_down[0]])
    h = _outproj(mixed_c, nsa_out, x2, gn_nsa_g[0].reshape(1, N_HEADS * HEAD_DIM), wo)
    out = _outffn(h, norm2_g[0].reshape(1, D_MODEL), wg, wu, wd, norm_f_g.reshape(1, D_MODEL))
    return out.reshape(B, S, D_MODEL)
```

```python
import functools

import numpy as np
import jax
import jax.numpy as jnp
from jax import lax
from jax.experimental import pallas as pl
from jax.experimental.pallas import tpu as pltpu

D_MODEL = 1024
CONV_CH = 512
CONV_K = 3
N_HEADS = 8
HEAD_DIM = 64
N_KV_HEADS = 2
GQA = 4
N_BRANCH = 3
CMP_BLOCK = 32
CMP_STRIDE = 16
CMP_HIDDEN = 256
SEL_BLOCK = 64
SEL_TOP = 16
WINDOW = 512
D_FF = 2816
NORM_EPS = 1e-6

F32 = jnp.float32
BF16 = jnp.bfloat16

LANES = 128
VW = 2 * LANES
SUBLANES = 8
VMEM_LIMIT = 56 * 1024 * 1024

TQ = 256
TK = 256
NEG = -1e30
MASK_BIAS = -float(2 ** 30)
_POS_HI, _POS_LO, _PAD_MARK = HEAD_DIM, HEAD_DIM + 1, HEAD_DIM + 2
_FLAG_BITS = 16

_CONV_W = 3 * CONV_CH
_Q_OFF = _CONV_W
_KVC_OFF = _Q_OFF + N_HEADS * HEAD_DIM
_KV_OFF = _KVC_OFF + 2 * N_KV_HEADS * HEAD_DIM
_G_OFF = _KV_OFF + 4 * N_KV_HEADS * HEAD_DIM
_G_W = N_HEADS * N_BRANCH
_PROJ_W = _G_OFF + _G_W


def _rms(x, g):
    return x * lax.rsqrt(jnp.mean(x * x, axis=-1, keepdims=True) + NORM_EPS) * g


def _dot(a, b):
    return jnp.dot(a, b, preferred_element_type=F32)


def _value_rows(v2, h, wide):
    lane = lax.broadcasted_iota(jnp.int32, v2.shape, 1)
    swapped = pltpu.roll(v2, HEAD_DIM, axis=1)
    if not wide:
        return jnp.where(lane < HEAD_DIM, v2 if h == 0 else swapped, 1.0).astype(BF16)
    dup = jnp.where((lane < HEAD_DIM) == (h == 0), v2, swapped)
    return jnp.concatenate([dup.astype(BF16), jnp.ones(v2.shape, BF16)], axis=1)


def _dot_nt(a, b):
    return lax.dot_general(a, b, (((1,), (1,)), ((), ())), preferred_element_type=F32)


def _inproj_kernel(seq_tiles, n_cast, x_ref, xp_ref, g_ref, w_ref, kf_ref, cw_ref, gc_ref, *refs):
    cast_in, refs = refs[:n_cast], refs[n_cast:]
    (mixc_ref, q_ref, gate_ref, kvc_ref, ks_ref, vs_ref, kw_ref, vw_ref), cast_out = refs[:8], refs[8:]
    for src_ref, dst_ref in zip(cast_in, cast_out):
        dst_ref[...] = src_ref[...].astype(BF16)

    u = _rms(x_ref[...], g_ref[...]).astype(BF16)

    cbch = _dot(u, w_ref[:, 0:_CONV_W])
    cu = cbch[:, CONV_CH:2 * CONV_CH] * cbch[:, 2 * CONV_CH:3 * CONV_CH]
    up = _rms(xp_ref[...], g_ref[...]).astype(BF16)
    chp = _dot(up, w_ref[:, CONV_CH:_CONV_W])
    starts_sequence = pl.program_id(0) % seq_tiles == 0
    prev = jnp.where(starts_sequence, 0.0, chp[:, 0:CONV_CH] * chp[:, CONV_CH:2 * CONV_CH])
    r = lax.broadcasted_iota(jnp.int32, cu.shape, 0)
    cu1 = jnp.where(r == 0, prev[7:8, :], pltpu.roll(cu, 1, axis=0))
    cu2 = pltpu.roll(cu, 2, axis=0)
    cu2 = jnp.where(r == 0, prev[6:7, :], jnp.where(r == 1, prev[7:8, :], cu2))
    cw = cw_ref[...]
    y = cbch[:, 0:CONV_CH] * (cw[0:1, :] * cu2 + cw[1:2, :] * cu1 + cw[2:3, :] * cu)
    mixc_ref[...] = _rms(y, gc_ref[...]).astype(BF16)

    q_ref[...] = _dot(u, w_ref[:, _Q_OFF:_KVC_OFF]).astype(BF16)
    kvc_ref[...] = _dot(u, w_ref[:, _KVC_OFF:_KV_OFF])
    gate_ref[...] = jax.nn.sigmoid(_dot(u, w_ref[:, _G_OFF:_PROJ_W]))
    kv = _dot(u, w_ref[:, _KV_OFF:_G_OFF])
    kf = kf_ref[...]
    hd, kvw = HEAD_DIM, N_KV_HEADS * HEAD_DIM
    for h in range(N_KV_HEADS):
        ks_ref[h] = kf
        ks_ref[h, :, 0:hd] = kv[:, h * hd:(h + 1) * hd].astype(BF16)
        vs_ref[h] = _value_rows(kv[:, kvw:2 * kvw], h, False)
        kw_ref[h] = kf[:, 0:LANES]
        kw_ref[h, :, 0:hd] = kv[:, 2 * kvw + h * hd:2 * kvw + (h + 1) * hd].astype(BF16)
        vw_ref[h] = _value_rows(kv[:, 3 * kvw:4 * kvw], h, True)


def _inproj(x2, g, w, kfeat, conv_w, gc, to_cast, tm=1024):
    rows = x2.shape[0]
    steps = rows // tm
    seq_tiles = kfeat.shape[0] // tm
    row = lambda i: (i, 0)
    head_row = lambda i: (0, i, 0)
    const = lambda i: (0, 0)
    cast_specs = [pl.BlockSpec((a.shape[0] // steps, a.shape[1]), row) for a in to_cast]
    return pl.pallas_call(
        functools.partial(_inproj_kernel, seq_tiles, len(to_cast)),
        grid=(steps,),
        in_specs=[
            pl.BlockSpec((tm, D_MODEL), row),
            pl.BlockSpec((SUBLANES, D_MODEL), lambda i: (jnp.maximum(i * (tm // SUBLANES) - 1, 0), 0)),
            pl.BlockSpec((1, D_MODEL), const),
            pl.BlockSpec((D_MODEL, _PROJ_W), const, pipeline_mode=pl.Buffered(1)),
            pl.BlockSpec((tm, 2 * LANES), lambda i: (i % seq_tiles, 0)),
            pl.BlockSpec((CONV_K, CONV_CH), const),
            pl.BlockSpec((1, CONV_CH), const),
        ] + cast_specs,
        out_specs=[
            pl.BlockSpec((tm, CONV_CH), row),
            pl.BlockSpec((tm, N_HEADS * HEAD_DIM), row),
            pl.BlockSpec((tm, _G_W), row),
            pl.BlockSpec((tm, _KV_OFF - _KVC_OFF), row),
            pl.BlockSpec((N_KV_HEADS, tm, 2 * LANES), head_row),
            pl.BlockSpec((N_KV_HEADS, tm, LANES), head_row),
            pl.BlockSpec((N_KV_HEADS, tm, LANES), head_row),
            pl.BlockSpec((N_KV_HEADS, tm, VW), head_row),
        ] + cast_specs,
        out_shape=[
            jax.ShapeDtypeStruct((rows, CONV_CH), BF16),
            jax.ShapeDtypeStruct((rows, N_HEADS * HEAD_DIM), BF16),
            jax.ShapeDtypeStruct((rows, _G_W), F32),
            jax.ShapeDtypeStruct((rows, _KV_OFF - _KVC_OFF), F32),
            jax.ShapeDtypeStruct((N_KV_HEADS, rows, 2 * LANES), BF16),
            jax.ShapeDtypeStruct((N_KV_HEADS, rows, LANES), BF16),
            jax.ShapeDtypeStruct((N_KV_HEADS, rows, LANES), BF16),
            jax.ShapeDtypeStruct((N_KV_HEADS, rows, VW), BF16),
        ] + [jax.ShapeDtypeStruct(a.shape, BF16) for a in to_cast],
        compiler_params=pltpu.CompilerParams(
            dimension_semantics=("arbitrary",), vmem_limit_bytes=VMEM_LIMIT),
        name="inproj",
    )(x2, x2, g, w, kfeat, conv_w, gc, *to_cast)


def _compress_kernel(xk_ref, xv_ref, pos_ref, w1_ref, w2_ref, cf_ref, kc_ref, vc_ref):
    nch = xk_ref.shape[0] // CMP_STRIDE
    hd = HEAD_DIM
    first = [jnp.zeros((nch, CMP_HIDDEN), F32) for _ in range(2 * N_KV_HEADS)]
    second = [jnp.zeros((nch, CMP_HIDDEN), F32) for _ in range(2 * N_KV_HEADS)]
    group = 2 * LANES // hd
    for l0 in range(0, CMP_STRIDE, group):
        for s, x_ref in enumerate((xk_ref, xv_ref)):
            xs = [x_ref[pl.ds(l0 + j, nch, stride=CMP_STRIDE), :].astype(BF16) for j in range(group)]
            for h in range(N_KV_HEADS):
                c = s * N_KV_HEADS + h
                xh = jnp.concatenate([x[:, h * hd:(h + 1) * hd] for x in xs], axis=1)
                first[c] = first[c] + _dot(xh, w1_ref[s, l0 * hd:(l0 + group) * hd, :])
                second[c] = second[c] + _dot(xh, w1_ref[s, (CMP_STRIDE + l0) * hd:(CMP_STRIDE + l0 + group) * hd, :])
    for s in range(2):
        pb = _dot(pos_ref[s], w1_ref[s])[0:1, :]
        outs = []
        for h in range(N_KV_HEADS):
            c = s * N_KV_HEADS + h
            hid = first[c] + pltpu.roll(second[c], nch - 1, axis=0) + pb
            outs.append(_dot(jax.nn.gelu(hid, approximate=True).astype(BF16), w2_ref[s]))
        for h in range(N_KV_HEADS):
            if s == 0:
                kc_ref[h] = cf_ref[...]
                kc_ref[h, :, 0:hd] = outs[h].astype(BF16)
            else:
                vc_ref[h] = _value_rows(jnp.concatenate(outs, axis=1), h, True)


def _compress(kvc, pos, w1, w2, cfeat, B, S):
    nch = S // CMP_STRIDE
    full = lambda *shape: pl.BlockSpec(shape, lambda b: (0,) * len(shape))
    return pl.pallas_call(
        _compress_kernel,
        grid=(B,),
        in_specs=[
            pl.BlockSpec((S, LANES), lambda b: (b, 0)),
            pl.BlockSpec((S, LANES), lambda b: (b, 1)),
            full(*pos.shape), full(*w1.shape), full(*w2.shape), full(*cfeat.shape),
        ],
        out_specs=[
            pl.BlockSpec((None, N_KV_HEADS, nch, LANES), lambda b: (b, 0, 0, 0)),
            pl.BlockSpec((None, N_KV_HEADS, nch, VW), lambda b: (b, 0, 0, 0)),
        ],
        out_shape=[
            jax.ShapeDtypeStruct((B, N_KV_HEADS, nch, LANES), BF16),
            jax.ShapeDtypeStruct((B, N_KV_HEADS, nch, VW), BF16),
        ],
        compiler_params=pltpu.CompilerParams(
            dimension_semantics=("arbitrary",), vmem_limit_bytes=VMEM_LIMIT),
        name="compress",
    )(kvc, kvc, pos, w1, w2, cfeat)


def _hi_lo(x):
    hi = x.astype(BF16)
    return jnp.concatenate([hi, (x - hi.astype(F32)).astype(BF16)], axis=1)


def _head_layout(acc):
    low_half = lax.broadcasted_iota(jnp.int32, (TQ, LANES), 1) < HEAD_DIM
    head = lambda x, g: x[g * TQ:(g + 1) * TQ]
    if acc.shape[1] == VW:
        x = acc[:, 0:LANES] * (1.0 / jnp.maximum(acc[:, LANES:VW], 1e-30))
        pairs = [jnp.where(low_half, head(x, 2 * j), head(x, 2 * j + 1)) for j in range(GQA // 2)]
    else:
        rot = pltpu.roll(acc, HEAD_DIM, axis=1)
        pairs = [jnp.where(low_half, head(acc, 2 * j) * (1.0 / head(rot, 2 * j)),
                           head(rot, 2 * j + 1) * (1.0 / head(acc, 2 * j + 1))) for j in range(GQA // 2)]
    return jnp.concatenate(pairs, axis=1)


def _exchange(v, i, j, descending):
    hi, lo = jnp.maximum(v[i], v[j]), jnp.minimum(v[i], v[j])
    v[i], v[j] = (hi, lo) if descending else (lo, hi)


def _bitonic_merge_desc(v):
    n = len(v)
    j = n // 2
    while j >= 1:
        for i in range(n):
            if i ^ j > i:
                _exchange(v, i, i ^ j, True)
        j //= 2


def _kth_largest(x, kth, fillers=()):
    n = x.shape[0]
    assert n == kth * SUBLANES and kth & (kth - 1) == 0
    v = [x[r * SUBLANES:(r + 1) * SUBLANES, :] for r in range(kth)]
    fillers = list(fillers)
    layers = 0
    k = 2
    while k <= kth:
        j = k // 2
        while j >= 1:
            for i in range(kth):
                if i ^ j > i:
                    _exchange(v, i, i ^ j, (i & k) == 0)
            j //= 2
            layers += 1
            if layers % 3 == 0 and fillers:
                fillers.pop(0)()
        k *= 2
    shift = SUBLANES // 2
    while shift >= 1:
        v = [jnp.maximum(v[i], pltpu.roll(v[kth - 1 - i], shift, axis=0)) for i in range(kth)]
        if shift > 1:
            _bitonic_merge_desc(v)
        if fillers:
            fillers.pop(0)()
        shift //= 2
    for filler in fillers:
        filler()
    out = v[0]
    for i in range(1, kth):
        out = jnp.minimum(out, v[i])
    return out[0:1, :]


def _nsa_kernel(q_ref, qf_ref, kc_ref, vc_ref, ks_ref, vs_ref, kw_ref, vw_ref, ov_ref, cm_ref, mb_ref,
                ge_ref, g_ref, o_ref, qa_ref, tiles_ref, m_ref, acc_ref, sa_ref, sb_ref, kwp_ref, vwp_ref):
    i = pl.program_id(2)
    t0 = i * TQ
    rows = GQA * TQ
    n_sel = ov_ref.shape[1]
    ncmp = kc_ref.shape[0]

    @pl.when(i == 0)
    def _():
        lane = lax.broadcasted_iota(jnp.int32, (WINDOW, LANES), 1)
        kwp_ref[0:WINDOW, :] = jnp.where(lane == _PAD_MARK, 1.0, 0.0).astype(BF16)
        kwp_ref[WINDOW:, :] = kw_ref[...]
        vwp_ref[0:WINDOW, :] = jnp.zeros((WINDOW, VW), BF16)
        vwp_ref[WINDOW:, :] = vw_ref[...]

    q = q_ref[...]
    for g in range(GQA):
        qg = q[:, g * HEAD_DIM:(g + 1) * HEAD_DIM].astype(F32) * (HEAD_DIM ** -0.5)
        qa_ref[g * TQ:(g + 1) * TQ, 0:HEAD_DIM] = qg.astype(BF16)
    qa_ref[:, HEAD_DIM:LANES] = qf_ref[...]
    qa1 = qa_ref[:, 0:LANES]

    diag_ok = (lax.broadcasted_iota(jnp.int32, (rows, TQ), 1)
               <= (lax.broadcasted_iota(jnp.int32, (rows, TQ), 0) & (TQ - 1)))


    wk = TQ + WINDOW
    w0 = pl.multiple_of(t0, TQ)
    win = {}

    def win_scores():
        s_w = _dot_nt(qa1, kwp_ref[pl.ds(w0, wk), :])
        win["s"] = (jnp.where(diag_ok, NEG, s_w[:, 0:TQ]),
                    s_w[:, TQ:WINDOW], jnp.where(diag_ok, s_w[:, WINDOW:wk], NEG))

    def win_max():
        s_old, s_mid, s_new = win["s"]
        win["m"] = jnp.maximum(jnp.maximum(jnp.max(s_old, axis=1, keepdims=True),
                                           jnp.max(s_mid, axis=1, keepdims=True)),
                               jnp.max(s_new, axis=1, keepdims=True))

    def win_exp():
        win["e"] = jnp.concatenate([jnp.exp(s - win["m"]) for s in win["s"]], axis=1).astype(BF16)

    def win_pv():
        win["acc"] = _dot(win["e"], vwp_ref[pl.ds(w0, wk), :])

    s_c = _dot_nt(qa1, kc_ref[...])
    s_c = jnp.where(cm_ref[...] <= (TQ // CMP_STRIDE) * i - 2, s_c, NEG)
    m_c = jnp.maximum(jnp.max(s_c, axis=1, keepdims=True), 0.5 * NEG)
    e_c = jnp.exp(s_c - m_c)
    acc_c = _dot(e_c.astype(BF16), vc_ref[...])
    inv_c = 1.0 / jnp.maximum(acc_c[:, LANES:VW], 1e-30)
    p_c = e_c * jnp.concatenate([inv_c] * (ncmp // LANES), axis=1)

    p_sum = p_c[0:TQ] + p_c[TQ:2 * TQ] + p_c[2 * TQ:3 * TQ] + p_c[3 * TQ:4 * TQ]
    p_hi = p_sum.astype(BF16)
    p_lo = (p_sum - p_hi.astype(F32)).astype(BF16)
    imp = _dot(p_hi, ov_ref[...]) + _dot(p_lo, ov_ref[...])
    imp_t = imp.T
    jb = lax.broadcasted_iota(jnp.int32, (n_sel, TQ), 0)
    tq = t0 + lax.broadcasted_iota(jnp.int32, (n_sel, TQ), 1)
    imp_t = jnp.where(SEL_BLOCK * jb > tq, -jnp.inf, imp_t)
    forced = (jb == 0) | (jb == (tq >> (SEL_BLOCK.bit_length() - 1)))
    imp_t = jnp.where(forced, jnp.inf, imp_t)

    thr = _kth_largest(imp_t, SEL_TOP, (win_scores, win_max, win_exp, win_pv))
    acc_w = win["acc"]
    above = imp_t > thr
    tied = imp_t == thr
    n_above = jnp.sum(jnp.where(above, 1.0, 0.0), axis=0, keepdims=True)
    lower = (lax.broadcasted_iota(jnp.int32, (n_sel, n_sel), 1)
             < lax.broadcasted_iota(jnp.int32, (n_sel, n_sel), 0))
    tied_before = _dot(jnp.where(lower, 1.0, 0.0).astype(BF16), jnp.where(tied, 1.0, 0.0).astype(BF16))
    take_tie = jnp.where(tied_before + n_above < float(SEL_TOP), 1.0, 0.0)
    sel_t = jnp.where(above, 1.0, jnp.where(tied, take_tie, 0.0))
    sel_q = sel_t.T
    bias = ((1.0 - sel_q) * MASK_BIAS).astype(BF16)
    for g in range(GQA):
        qa_ref[g * TQ:(g + 1) * TQ, LANES:2 * LANES] = bias

    n_past = t0 // TK
    first_mask = (t0 - n_past * TK) // TQ
    tiles_ref[0] = n_past

    def scores(j, s_ref, mask=None):
        k0 = pl.multiple_of(tiles_ref[j] * TK, TK)
        s = _dot_nt(qa_ref[...], ks_ref[pl.ds(k0, TK), :])
        s_ref[...] = s if mask is None else s + mask

    scores(0, sa_ref, mb_ref[first_mask])

    bpt = TK // SEL_BLOCK
    n_tiles = n_sel // bpt
    blk_cnt = _dot(jnp.ones((SUBLANES, TQ), BF16), sel_q.astype(BF16))
    in_tile = (lax.broadcasted_iota(jnp.int32, (n_sel, LANES), 0) // bpt
               == lax.broadcasted_iota(jnp.int32, (n_sel, LANES), 1))
    tile_cnt = _dot(jnp.where(blk_cnt > 0.5, 1.0, 0.0).astype(BF16),
                    jnp.where(in_tile, 1.0, 0.0).astype(BF16))
    lane = lax.broadcasted_iota(jnp.int32, (1, LANES), 1)
    bit = jnp.where(tile_cnt[0:1, :] > 0.5, jnp.left_shift(1, lane & (_FLAG_BITS - 1)).astype(F32), 0.0)
    words = [jnp.sum(jnp.where((lane >= _FLAG_BITS * w) & (lane < _FLAG_BITS * (w + 1)), bit, 0.0)
                     ).astype(jnp.int32) for w in range(-(-n_tiles // _FLAG_BITS))]
    n_tasks = jnp.int32(1)
    for kt in range(n_tiles - 1):
        used = (words[kt // _FLAG_BITS] >> (kt % _FLAG_BITS)) & 1
        tiles_ref[n_tasks] = kt
        n_tasks = n_tasks + jnp.where(kt < n_past, used, 0)
    tiles_ref[n_tasks] = 0
    tiles_ref[n_tasks + 1] = 0

    m_ref[...] = jnp.full(m_ref.shape, NEG, F32)
    acc_ref[...] = jnp.zeros(acc_ref.shape, F32)

    gates = _dot(_hi_lo(g_ref[...]), ge_ref[...])
    gw = GQA * HEAD_DIM
    merged = gates[:, 0:gw] * _head_layout(acc_c) + gates[:, 2 * gw:3 * gw] * _head_layout(acc_w)
    gates_sel = gates[:, gw:2 * gw]

    def accumulate(j, s_ref):
        k0 = pl.multiple_of(tiles_ref[j] * TK, TK)
        s = s_ref[...]
        m = m_ref[...]
        m_new = jnp.maximum(m, jnp.max(s, axis=1, keepdims=True))
        p = jnp.concatenate([jnp.exp((s[:, c:c + LANES] - m_new).astype(BF16)) for c in range(0, TK, LANES)],
                            axis=1)
        acc_ref[...] = jnp.exp(m - m_new) * acc_ref[...] + _dot(p, vs_ref[pl.ds(k0, TK), :])
        m_ref[...] = m_new

    def pair(jj, c):
        a = 2 * jj
        scores(a + 1, sb_ref)
        accumulate(a, sa_ref)
        scores(a + 2, sa_ref)
        accumulate(a + 1, sb_ref)
        return c

    lax.fori_loop(0, n_tasks // 2, pair, 0)

    @pl.when(n_tasks % 2 == 1)
    def _():
        accumulate(n_tasks - 1, sa_ref)

    o_ref[...] = (merged + gates_sel * _head_layout(acc_ref[...])).astype(o_ref.dtype)


def _nsa(q, qfeat, kc_aug, vc, ks_aug, vs, kw_aug, vw, ov, cmask, tile_masks, gate_expand, gates, B, S):
    nq = S // TQ
    gw = GQA * HEAD_DIM
    ncmp = kc_aug.shape[2]
    n_sel = ov.shape[1]
    bk = lambda b, k, i: (b, k, 0, 0)
    kb = lambda b, k, i: (k, b, 0, 0)
    return pl.pallas_call(
        _nsa_kernel,
        grid=(B, N_KV_HEADS, nq),
        in_specs=[
            pl.BlockSpec((TQ, gw), lambda b, k, i: (b * nq + i, k)),
            pl.BlockSpec((None, GQA * TQ, HEAD_DIM), lambda b, k, i: (k, 0, 0)),
            pl.BlockSpec((None, None, ncmp, LANES), bk),
            pl.BlockSpec((None, None, ncmp, VW), bk),
            pl.BlockSpec((None, None, S, 2 * LANES), kb),
            pl.BlockSpec((None, None, S, LANES), kb),
            pl.BlockSpec((None, None, S, LANES), kb),
            pl.BlockSpec((None, None, S, VW), kb),
            pl.BlockSpec((ncmp, n_sel), lambda b, k, i: (0, 0)),
            pl.BlockSpec((GQA * TQ, ncmp), lambda b, k, i: (0, 0)),
            pl.BlockSpec(tile_masks.shape, lambda b, k, i: (0, 0, 0)),
            pl.BlockSpec((None,) + gate_expand.shape[1:], lambda b, k, i: (k, 0, 0)),
            pl.BlockSpec((TQ, _G_W), lambda b, k, i: (b * nq + i, 0)),
        ],
        out_specs=pl.BlockSpec((TQ, gw), lambda b, k, i: (b * nq + i, k)),
        out_shape=jax.ShapeDtypeStruct((B * S, N_KV_HEADS * gw), BF16),
        scratch_shapes=[
            pltpu.VMEM((GQA * TQ, 2 * LANES), BF16),
            pltpu.SMEM((S // TK + 2,), jnp.int32),
            pltpu.VMEM((GQA * TQ, LANES), F32),
            pltpu.VMEM((GQA * TQ, LANES), F32),
            pltpu.VMEM((GQA * TQ, TK), F32),
            pltpu.VMEM((GQA * TQ, TK), F32),
            pltpu.VMEM((S + WINDOW, LANES), BF16),
            pltpu.VMEM((S + WINDOW, VW), BF16),
        ],
        compiler_params=pltpu.CompilerParams(
            dimension_semantics=("arbitrary",) * 3, vmem_limit_bytes=VMEM_LIMIT),
        name="nsa",
    )(q, qfeat, kc_aug, vc, ks_aug, vs, kw_aug, vw, ov, cmask, tile_masks, gate_expand, gates)


_FF_CHUNK = 768


def _outproj_kernel(mc_ref, nsa_ref, x_ref, gn_ref, wo_ref, h_ref):
    mixed_n = _rms(nsa_ref[...].astype(F32), gn_ref[...]).astype(BF16)
    h_ref[...] = (x_ref[...] + _dot(mc_ref[...], wo_ref[0:CONV_CH, :])
                  + _dot(mixed_n, wo_ref[CONV_CH:2 * CONV_CH, :]))


def _outproj(mixed_c, nsa_out, x2, gn, wo, tm=1024):
    rows = x2.shape[0]
    row = lambda i: (i, 0)
    const = lambda i: (0, 0)
    return pl.pallas_call(
        _outproj_kernel,
        grid=(rows // tm,),
        in_specs=[
            pl.BlockSpec((tm, CONV_CH), row),
            pl.BlockSpec((tm, N_HEADS * HEAD_DIM), row),
            pl.BlockSpec((tm, D_MODEL), row),
            pl.BlockSpec((1, N_HEADS * HEAD_DIM), const),
            pl.BlockSpec((2 * CONV_CH, D_MODEL), const, pipeline_mode=pl.Buffered(1)),
        ],
        out_specs=pl.BlockSpec((tm, D_MODEL), row),
        out_shape=jax.ShapeDtypeStruct((rows, D_MODEL), F32),
        compiler_params=pltpu.CompilerParams(
            dimension_semantics=("arbitrary",), vmem_limit_bytes=VMEM_LIMIT),
        name="outproj",
    )(mixed_c, nsa_out, x2, gn, wo)


def _outffn_kernel(h_ref, g2_ref, wg_ref, wu_ref, wd_ref, gf_ref, o_ref):
    h = h_ref[...]
    u = _rms(h, g2_ref[...]).astype(BF16)
    acc = h
    for c in range(0, D_FF, _FF_CHUNK):
        sl = slice(c, min(c + _FF_CHUNK, D_FF))
        a = jax.nn.silu(_dot(u, wg_ref[:, sl])) * _dot(u, wu_ref[:, sl])
        acc = acc + _dot(a.astype(BF16), wd_ref[sl, :])
    o_ref[...] = _rms(acc, gf_ref[...])


def _outffn(h, g2, wg, wu, wd, gf, tm=512):
    rows = h.shape[0]
    row = lambda i: (i, 0)
    const = lambda i: (0, 0)
    resident = lambda shape: pl.BlockSpec(shape, const, pipeline_mode=pl.Buffered(1))
    return pl.pallas_call(
        _outffn_kernel,
        grid=(rows // tm,),
        in_specs=[
            pl.BlockSpec((tm, D_MODEL), row),
            pl.BlockSpec((1, D_MODEL), const),
            resident((D_MODEL, D_FF)),
            resident((D_MODEL, D_FF)),
            resident((D_FF, D_MODEL)),
            pl.BlockSpec((1, D_MODEL), const),
        ],
        out_specs=pl.BlockSpec((tm, D_MODEL), row),
        out_shape=jax.ShapeDtypeStruct((rows, D_MODEL), F32),
        compiler_params=pltpu.CompilerParams(
            dimension_semantics=("arbitrary",), vmem_limit_bytes=VMEM_LIMIT),
        name="outffn",
    )(h, g2, wg, wu, wd, gf)


def _key_features(pos, n_sel):
    f = np.zeros((pos.shape[0], LANES + n_sel), np.float32)
    f[:, _POS_HI] = pos // SEL_BLOCK
    f[:, _POS_LO] = pos % SEL_BLOCK
    if n_sel:
        f[np.arange(pos.shape[0]), LANES + pos // SEL_BLOCK] = 1.0
    return f


def _query_features():
    f = np.zeros((N_KV_HEADS, GQA * TQ, HEAD_DIM), np.float32)
    for k in range(N_KV_HEADS):
        for g in range(GQA):
            slope = 2.0 ** (-8.0 * (k * GQA + g + 1) / N_HEADS)
            f[k, g * TQ:(g + 1) * TQ, _POS_HI - HEAD_DIM] = slope * SEL_BLOCK
            f[k, g * TQ:(g + 1) * TQ, _POS_LO - HEAD_DIM] = slope
            f[k, g * TQ:(g + 1) * TQ, _PAD_MARK - HEAD_DIM] = MASK_BIAS
    return f


def _causal_masks(ncmp):
    r = np.arange(GQA * TQ)[:, None] % TQ
    cmask = (np.arange(ncmp)[None, :] - ((r + 1) // CMP_STRIDE)).astype(np.int32)
    c = np.arange(TK)[None, :]
    tile_masks = [np.where(c - r <= d * TQ, 0.0, NEG).astype(np.float32) for d in range(max(TK // TQ, 1))]
    return cmask, np.stack(tile_masks)


def _gate_expand():
    gw = GQA * HEAD_DIM
    e = np.zeros((N_KV_HEADS, 2 * _G_W, N_BRANCH * gw), np.float32)
    for k in range(N_KV_HEADS):
        for g in range(GQA):
            for br in range(N_BRANCH):
                for half in range(2):
                    e[k, half * _G_W + (k * GQA + g) * N_BRANCH + br,
                      br * gw + g * HEAD_DIM:br * gw + (g + 1) * HEAD_DIM] = 1.0
    return e


def _overlap(ncmp_pad, n_sel):
    nc = ncmp_pad - 1
    c_start = CMP_STRIDE * np.arange(nc)
    c_end = c_start + CMP_BLOCK - 1
    s_start = SEL_BLOCK * np.arange(n_sel)
    ov = np.minimum(c_end[:, None] + 1, s_start[None, :] + SEL_BLOCK) - np.maximum(c_start[:, None], s_start[None, :])
    out = np.zeros((ncmp_pad, n_sel), np.float32)
    out[:nc] = np.clip(ov, 0, None).astype(np.float32) / CMP_BLOCK
    return out


def _layer(x2, B, S, norm1_g, w_in, conv_w, k_cmp_pos, k_cmp_w1, k_cmp_w2, v_cmp_pos, v_cmp_w1, v_cmp_w2,
           gn_conv_g, later_weights):
    n_sel = S // SEL_BLOCK
    nch = S // CMP_STRIDE
    hd = HEAD_DIM

    kfeat = jnp.asarray(_key_features(np.arange(S), n_sel), BF16)
    mixed_c, q, gates, kvc, ks_aug, vs, kw_aug, vw, *later_bf16 = _inproj(
        x2, norm1_g.reshape(1, D_MODEL), w_in.astype(BF16), kfeat, conv_w, gn_conv_g.reshape(1, CONV_CH),
        later_weights)

    w1 = jnp.stack([k_cmp_w1, v_cmp_w1]).astype(BF16)
    w2 = jnp.stack([k_cmp_w2, v_cmp_w2]).astype(BF16)
    pos = jnp.stack([k_cmp_pos, v_cmp_pos]).reshape(2, 1, CMP_BLOCK * hd)
    pos = jnp.broadcast_to(pos, (2, SUBLANES, CMP_BLOCK * hd)).astype(BF16)
    c_end = CMP_STRIDE * np.arange(nch) + CMP_BLOCK - 1
    cfeat = jnp.asarray(_key_features(c_end, 0), BF16)
    kc_aug, vc = _compress(kvc, pos, w1, w2, cfeat, B, S)

    cmask, tile_masks = _causal_masks(nch)

    nsa_out = _nsa(q, jnp.asarray(_query_features(), BF16), kc_aug, vc,
                   ks_aug.reshape(N_KV_HEADS, B, S, 2 * LANES), vs.reshape(N_KV_HEADS, B, S, LANES),
                   kw_aug.reshape(N_KV_HEADS, B, S, LANES), vw.reshape(N_KV_HEADS, B, S, VW),
                   jnp.asarray(_overlap(nch, n_sel), BF16), jnp.asarray(cmask),
                   jnp.asarray(tile_masks), jnp.asarray(_gate_expand(), BF16), gates, B, S)
    return mixed_c, nsa_out, later_bf16


def kernel(x, norm1_g, w_in, conv_w, k_cmp_pos, k_cmp_w1, k_cmp_w2, v_cmp_pos, v_cmp_w1, v_cmp_w2,
           gn_conv_g, gn_nsa_g, w_out, norm2_g, w_gate, w_up, w_down, norm_f_g):
    B, S, _ = x.shape
    depth = norm1_g.shape[0]
    assert depth == 1, "the final norm is fused into the (single) layer's FFN kernel"
    x2 = x.reshape(B * S, D_MODEL)
    mixed_c, nsa_out, (wo, wg, wu, wd) = _layer(
        x2, B, S, norm1_g[0], w_in[0], conv_w[0], k_cmp_pos[0], k_cmp_w1[0], k_cmp_w2[0], v_cmp_pos[0],
        v_cmp_w1[0], v_cmp_w2[0], gn_conv_g[0], [w_out[0], w_gate[0], w_up[0], w_down[0]])
    h = _outproj(mixed_c, nsa_out, x2, gn_nsa_g[0].reshape(1, N_HEADS * HEAD_DIM), wo)
    out = _outffn(h, norm2_g[0].reshape(1, D_MODEL), wg, wu, wd, norm_f_g.reshape(1, D_MODEL))
    return out.reshape(B, S, D_MODEL)
```

```python
import functools

import numpy as np
import jax
import jax.numpy as jnp
from jax import lax
from jax.experimental import pallas as pl
from jax.experimental.pallas import tpu as pltpu

D_MODEL = 1024
CONV_CH = 512
CONV_K = 3
N_HEADS = 8
HEAD_DIM = 64
N_KV_HEADS = 2
GQA = 4
N_BRANCH = 3
CMP_BLOCK = 32
CMP_STRIDE = 16
CMP_HIDDEN = 256
SEL_BLOCK = 64
SEL_TOP = 16
WINDOW = 512
D_FF = 2816
NORM_EPS = 1e-6

F32 = jnp.float32
BF16 = jnp.bfloat16

LANES = 128
VW = 2 * LANES
SUBLANES = 8
VMEM_LIMIT = 56 * 1024 * 1024

TQ = 256
TK = 256
NEG = -1e30
MASK_BIAS = -float(2 ** 30)
_POS_HI, _POS_LO, _PAD_MARK = HEAD_DIM, HEAD_DIM + 1, HEAD_DIM + 2
_FLAG_BITS = 16

_CONV_W = 3 * CONV_CH
_Q_OFF = _CONV_W
_KVC_OFF = _Q_OFF + N_HEADS * HEAD_DIM
_KV_OFF = _KVC_OFF + 2 * N_KV_HEADS * HEAD_DIM
_G_OFF = _KV_OFF + 4 * N_KV_HEADS * HEAD_DIM
_G_W = N_HEADS * N_BRANCH
_PROJ_W = _G_OFF + _G_W


def _rms(x, g):
    return x * lax.rsqrt(jnp.mean(x * x, axis=-1, keepdims=True) + NORM_EPS) * g


def _dot(a, b):
    return jnp.dot(a, b, preferred_element_type=F32)


def _value_rows(v2, h, wide):
    lane = lax.broadcasted_iota(jnp.int32, v2.shape, 1)
    swapped = pltpu.roll(v2, HEAD_DIM, axis=1)
    if not wide:
        return jnp.where(lane < HEAD_DIM, v2 if h == 0 else swapped, 1.0).astype(BF16)
    dup = jnp.where((lane < HEAD_DIM) == (h == 0), v2, swapped)
    return jnp.concatenate([dup.astype(BF16), jnp.ones(v2.shape, BF16)], axis=1)


def _dot_nt(a, b):
    return lax.dot_general(a, b, (((1,), (1,)), ((), ())), preferred_element_type=F32)


def _inproj_kernel(seq_tiles, n_cast, x_ref, xp_ref, g_ref, w_ref, kf_ref, cw_ref, gc_ref, *refs):
    cast_in, refs = refs[:n_cast], refs[n_cast:]
    (mixc_ref, q_ref, gate_ref, kvc_ref, ks_ref, vs_ref, kw_ref, vw_ref), cast_out = refs[:8], refs[8:]
    for src_ref, dst_ref in zip(cast_in, cast_out):
        dst_ref[...] = src_ref[...].astype(BF16)

    u = _rms(x_ref[...], g_ref[...]).astype(BF16)

    up = _rms(xp_ref[...], g_ref[...]).astype(BF16)
    ext = _dot(jnp.concatenate([up, u], axis=0), w_ref[:, 0:_CONV_W])
    cbch = ext[SUBLANES:, :]
    cu = cbch[:, CONV_CH:2 * CONV_CH] * cbch[:, 2 * CONV_CH:3 * CONV_CH]
    starts_sequence = pl.program_id(0) % seq_tiles == 0
    prev = jnp.where(starts_sequence, 0.0,
                     ext[0:SUBLANES, CONV_CH:2 * CONV_CH] * ext[0:SUBLANES, 2 * CONV_CH:3 * CONV_CH])
    r = lax.broadcasted_iota(jnp.int32, cu.shape, 0)
    cu1 = jnp.where(r == 0, prev[7:8, :], pltpu.roll(cu, 1, axis=0))
    cu2 = pltpu.roll(cu, 2, axis=0)
    cu2 = jnp.where(r == 0, prev[6:7, :], jnp.where(r == 1, prev[7:8, :], cu2))
    cw = cw_ref[...]
    y = cbch[:, 0:CONV_CH] * (cw[0:1, :] * cu2 + cw[1:2, :] * cu1 + cw[2:3, :] * cu)
    mixc_ref[...] = _rms(y, gc_ref[...]).astype(BF16)

    q_ref[...] = _dot(u, w_ref[:, _Q_OFF:_KVC_OFF]).astype(BF16)
    kvc_ref[...] = _dot(u, w_ref[:, _KVC_OFF:_KV_OFF])
    gate_ref[...] = jax.nn.sigmoid(_dot(u, w_ref[:, _G_OFF:_PROJ_W]))
    kv = _dot(u, w_ref[:, _KV_OFF:_G_OFF])
    kf = kf_ref[...]
    hd, kvw = HEAD_DIM, N_KV_HEADS * HEAD_DIM
    for h in range(N_KV_HEADS):
        ks_ref[h] = kf
        ks_ref[h, :, 0:hd] = kv[:, h * hd:(h + 1) * hd].astype(BF16)
        vs_ref[h] = _value_rows(kv[:, kvw:2 * kvw], h, False)
        kw_ref[h] = kf[:, 0:LANES]
        kw_ref[h, :, 0:hd] = kv[:, 2 * kvw + h * hd:2 * kvw + (h + 1) * hd].astype(BF16)
        vw_ref[h] = _value_rows(kv[:, 3 * kvw:4 * kvw], h, True)


def _inproj(x2, g, w, kfeat, conv_w, gc, to_cast, tm=1024):
    rows = x2.shape[0]
    steps = rows // tm
    seq_tiles = kfeat.shape[0] // tm
    row = lambda i: (i, 0)
    head_row = lambda i: (0, i, 0)
    const = lambda i: (0, 0)
    cast_specs = [pl.BlockSpec((a.shape[0] // steps, a.shape[1]), row) for a in to_cast]
    return pl.pallas_call(
        functools.partial(_inproj_kernel, seq_tiles, len(to_cast)),
        grid=(steps,),
        in_specs=[
            pl.BlockSpec((tm, D_MODEL), row),
            pl.BlockSpec((SUBLANES, D_MODEL), lambda i: (jnp.maximum(i * (tm // SUBLANES) - 1, 0), 0)),
            pl.BlockSpec((1, D_MODEL), const),
            pl.BlockSpec((D_MODEL, _PROJ_W), const, pipeline_mode=pl.Buffered(1)),
            pl.BlockSpec((tm, 2 * LANES), lambda i: (i % seq_tiles, 0)),
            pl.BlockSpec((CONV_K, CONV_CH), const),
            pl.BlockSpec((1, CONV_CH), const),
        ] + cast_specs,
        out_specs=[
            pl.BlockSpec((tm, CONV_CH), row),
            pl.BlockSpec((tm, N_HEADS * HEAD_DIM), row),
            pl.BlockSpec((tm, _G_W), row),
            pl.BlockSpec((tm, _KV_OFF - _KVC_OFF), row),
            pl.BlockSpec((N_KV_HEADS, tm, 2 * LANES), head_row),
            pl.BlockSpec((N_KV_HEADS, tm, LANES), head_row),
            pl.BlockSpec((N_KV_HEADS, tm, LANES), head_row),
            pl.BlockSpec((N_KV_HEADS, tm, VW), head_row),
        ] + cast_specs,
        out_shape=[
            jax.ShapeDtypeStruct((rows, CONV_CH), BF16),
            jax.ShapeDtypeStruct((rows, N_HEADS * HEAD_DIM), BF16),
            jax.ShapeDtypeStruct((rows, _G_W), F32),
            jax.ShapeDtypeStruct((rows, _KV_OFF - _KVC_OFF), F32),
            jax.ShapeDtypeStruct((N_KV_HEADS, rows, 2 * LANES), BF16),
            jax.ShapeDtypeStruct((N_KV_HEADS, rows, LANES), BF16),
            jax.ShapeDtypeStruct((N_KV_HEADS, rows, LANES), BF16),
            jax.ShapeDtypeStruct((N_KV_HEADS, rows, VW), BF16),
        ] + [jax.ShapeDtypeStruct(a.shape, BF16) for a in to_cast],
        compiler_params=pltpu.CompilerParams(
            dimension_semantics=("arbitrary",), vmem_limit_bytes=VMEM_LIMIT),
        name="inproj",
    )(x2, x2, g, w, kfeat, conv_w, gc, *to_cast)


def _compress_kernel(xk_ref, xv_ref, pos_ref, w1_ref, w2_ref, cf_ref, kc_ref, vc_ref):
    nch = xk_ref.shape[0] // CMP_STRIDE
    hd = HEAD_DIM
    first = [jnp.zeros((nch, CMP_HIDDEN), F32) for _ in range(2 * N_KV_HEADS)]
    second = [jnp.zeros((nch, CMP_HIDDEN), F32) for _ in range(2 * N_KV_HEADS)]
    group = 2 * LANES // hd
    for l0 in range(0, CMP_STRIDE, group):
        for s, x_ref in enumerate((xk_ref, xv_ref)):
            xs = [x_ref[pl.ds(l0 + j, nch, stride=CMP_STRIDE), :].astype(BF16) for j in range(group)]
            for h in range(N_KV_HEADS):
                c = s * N_KV_HEADS + h
                xh = jnp.concatenate([x[:, h * hd:(h + 1) * hd] for x in xs], axis=1)
                first[c] = first[c] + _dot(xh, w1_ref[s, l0 * hd:(l0 + group) * hd, :])
                second[c] = second[c] + _dot(xh, w1_ref[s, (CMP_STRIDE + l0) * hd:(CMP_STRIDE + l0 + group) * hd, :])
    for s in range(2):
        pb = _dot(pos_ref[s], w1_ref[s])[0:1, :]
        outs = []
        for h in range(N_KV_HEADS):
            c = s * N_KV_HEADS + h
            hid = first[c] + pltpu.roll(second[c], nch - 1, axis=0) + pb
            outs.append(_dot(jax.nn.gelu(hid, approximate=True).astype(BF16), w2_ref[s]))
        for h in range(N_KV_HEADS):
            if s == 0:
                kc_ref[h] = cf_ref[...]
                kc_ref[h, :, 0:hd] = outs[h].astype(BF16)
            else:
                vc_ref[h] = _value_rows(jnp.concatenate(outs, axis=1), h, True)


def _compress(kvc, pos, w1, w2, cfeat, B, S):
    nch = S // CMP_STRIDE
    full = lambda *shape: pl.BlockSpec(shape, lambda b: (0,) * len(shape))
    return pl.pallas_call(
        _compress_kernel,
        grid=(B,),
        in_specs=[
            pl.BlockSpec((S, LANES), lambda b: (b, 0)),
            pl.BlockSpec((S, LANES), lambda b: (b, 1)),
            full(*pos.shape), full(*w1.shape), full(*w2.shape), full(*cfeat.shape),
        ],
        out_specs=[
            pl.BlockSpec((None, N_KV_HEADS, nch, LANES), lambda b: (b, 0, 0, 0)),
            pl.BlockSpec((None, N_KV_HEADS, nch, VW), lambda b: (b, 0, 0, 0)),
        ],
        out_shape=[
            jax.ShapeDtypeStruct((B, N_KV_HEADS, nch, LANES), BF16),
            jax.ShapeDtypeStruct((B, N_KV_HEADS, nch, VW), BF16),
        ],
        compiler_params=pltpu.CompilerParams(
            dimension_semantics=("arbitrary",), vmem_limit_bytes=VMEM_LIMIT),
        name="compress",
    )(kvc, kvc, pos, w1, w2, cfeat)


def _hi_lo(x):
    hi = x.astype(BF16)
    return jnp.concatenate([hi, (x - hi.astype(F32)).astype(BF16)], axis=1)


def _head_layout(acc):
    low_half = lax.broadcasted_iota(jnp.int32, (TQ, LANES), 1) < HEAD_DIM
    head = lambda x, g: x[g * TQ:(g + 1) * TQ]
    if acc.shape[1] == VW:
        x = acc[:, 0:LANES] * (1.0 / jnp.maximum(acc[:, LANES:VW], 1e-30))
        pairs = [jnp.where(low_half, head(x, 2 * j), head(x, 2 * j + 1)) for j in range(GQA // 2)]
    else:
        rot = pltpu.roll(acc, HEAD_DIM, axis=1)
        pairs = [jnp.where(low_half, head(acc, 2 * j) * (1.0 / head(rot, 2 * j)),
                           head(rot, 2 * j + 1) * (1.0 / head(acc, 2 * j + 1))) for j in range(GQA // 2)]
    return jnp.concatenate(pairs, axis=1)


def _exchange(v, i, j, descending):
    hi, lo = jnp.maximum(v[i], v[j]), jnp.minimum(v[i], v[j])
    v[i], v[j] = (hi, lo) if descending else (lo, hi)


def _bitonic_merge_desc(v):
    n = len(v)
    j = n // 2
    while j >= 1:
        for i in range(n):
            if i ^ j > i:
                _exchange(v, i, i ^ j, True)
        j //= 2


def _kth_largest(x, kth, fillers=()):
    n = x.shape[0]
    assert n == kth * SUBLANES and kth & (kth - 1) == 0
    v = [x[r * SUBLANES:(r + 1) * SUBLANES, :] for r in range(kth)]
    fillers = list(fillers)
    layers = 0
    k = 2
    while k <= kth:
        j = k // 2
        while j >= 1:
            for i in range(kth):
                if i ^ j > i:
                    _exchange(v, i, i ^ j, (i & k) == 0)
            j //= 2
            layers += 1
            if layers % 3 == 0 and fillers:
                fillers.pop(0)()
        k *= 2
    shift = SUBLANES // 2
    while shift >= 1:
        v = [jnp.maximum(v[i], pltpu.roll(v[kth - 1 - i], shift, axis=0)) for i in range(kth)]
        if shift > 1:
            _bitonic_merge_desc(v)
        if fillers:
            fillers.pop(0)()
        shift //= 2
    for filler in fillers:
        filler()
    out = v[0]
    for i in range(1, kth):
        out = jnp.minimum(out, v[i])
    return out[0:1, :]


def _nsa_kernel(q_ref, qf_ref, kc_ref, vc_ref, ks_ref, vs_ref, kw_ref, vw_ref, ov_ref, cm_ref, mb_ref,
                ge_ref, g_ref, o_ref, qa_ref, tiles_ref, m_ref, acc_ref, sa_ref, sb_ref, kwp_ref, vwp_ref):
    i = pl.program_id(2)
    t0 = i * TQ
    rows = GQA * TQ
    n_sel = ov_ref.shape[1]
    ncmp = kc_ref.shape[0]

    @pl.when(i == 0)
    def _():
        lane = lax.broadcasted_iota(jnp.int32, (WINDOW, LANES), 1)
        kwp_ref[0:WINDOW, :] = jnp.where(lane == _PAD_MARK, 1.0, 0.0).astype(BF16)
        kwp_ref[WINDOW:, :] = kw_ref[...]
        vwp_ref[0:WINDOW, :] = jnp.zeros((WINDOW, VW), BF16)
        vwp_ref[WINDOW:, :] = vw_ref[...]

    q = q_ref[...]
    for g in range(GQA):
        qg = q[:, g * HEAD_DIM:(g + 1) * HEAD_DIM].astype(F32) * (HEAD_DIM ** -0.5)
        qa_ref[g * TQ:(g + 1) * TQ, 0:HEAD_DIM] = qg.astype(BF16)
    qa_ref[:, HEAD_DIM:LANES] = qf_ref[...]
    qa1 = qa_ref[:, 0:LANES]

    diag_ok = (lax.broadcasted_iota(jnp.int32, (rows, TQ), 1)
               <= (lax.broadcasted_iota(jnp.int32, (rows, TQ), 0) & (TQ - 1)))


    wk = TQ + WINDOW
    w0 = pl.multiple_of(t0, TQ)
    win = {}

    def win_scores():
        s_w = _dot_nt(qa1, kwp_ref[pl.ds(w0, wk), :])
        win["s"] = (jnp.where(diag_ok, NEG, s_w[:, 0:TQ]),
                    s_w[:, TQ:WINDOW], jnp.where(diag_ok, s_w[:, WINDOW:wk], NEG))

    def win_max():
        s_old, s_mid, s_new = win["s"]
        win["m"] = jnp.maximum(jnp.maximum(jnp.max(s_old, axis=1, keepdims=True),
                                           jnp.max(s_mid, axis=1, keepdims=True)),
                               jnp.max(s_new, axis=1, keepdims=True))

    def win_exp():
        win["e"] = jnp.concatenate([jnp.exp(s - win["m"]) for s in win["s"]], axis=1).astype(BF16)

    def win_pv():
        win["acc"] = _dot(win["e"], vwp_ref[pl.ds(w0, wk), :])

    s_c = _dot_nt(qa1, kc_ref[...])
    s_c = jnp.where(cm_ref[...] <= (TQ // CMP_STRIDE) * i - 2, s_c, NEG)
    m_c = jnp.maximum(jnp.max(s_c, axis=1, keepdims=True), 0.5 * NEG)
    e_c = jnp.exp(s_c - m_c)
    acc_c = _dot(e_c.astype(BF16), vc_ref[...])
    inv_c = 1.0 / jnp.maximum(acc_c[:, LANES:VW], 1e-30)
    p_c = e_c * jnp.concatenate([inv_c] * (ncmp // LANES), axis=1)

    p_sum = p_c[0:TQ] + p_c[TQ:2 * TQ] + p_c[2 * TQ:3 * TQ] + p_c[3 * TQ:4 * TQ]
    p_hi = p_sum.astype(BF16)
    p_lo = (p_sum - p_hi.astype(F32)).astype(BF16)
    imp = _dot(p_hi, ov_ref[...]) + _dot(p_lo, ov_ref[...])
    imp_t = imp.T
    jb = lax.broadcasted_iota(jnp.int32, (n_sel, TQ), 0)
    tq = t0 + lax.broadcasted_iota(jnp.int32, (n_sel, TQ), 1)
    imp_t = jnp.where(SEL_BLOCK * jb > tq, -jnp.inf, imp_t)
    forced = (jb == 0) | (jb == (tq >> (SEL_BLOCK.bit_length() - 1)))
    imp_t = jnp.where(forced, jnp.inf, imp_t)

    thr = _kth_largest(imp_t, SEL_TOP, (win_scores, win_max, win_exp, win_pv))
    acc_w = win["acc"]
    above = imp_t > thr
    tied = imp_t == thr
    n_above = jnp.sum(jnp.where(above, 1.0, 0.0), axis=0, keepdims=True)
    lower = (lax.broadcasted_iota(jnp.int32, (n_sel, n_sel), 1)
             < lax.broadcasted_iota(jnp.int32, (n_sel, n_sel), 0))
    tied_before = _dot(jnp.where(lower, 1.0, 0.0).astype(BF16), jnp.where(tied, 1.0, 0.0).astype(BF16))
    take_tie = jnp.where(tied_before + n_above < float(SEL_TOP), 1.0, 0.0)
    sel_t = jnp.where(above, 1.0, jnp.where(tied, take_tie, 0.0))
    sel_q = sel_t.T
    bias = ((1.0 - sel_q) * MASK_BIAS).astype(BF16)
    for g in range(GQA):
        qa_ref[g * TQ:(g + 1) * TQ, LANES:2 * LANES] = bias

    n_past = t0 // TK
    first_mask = (t0 - n_past * TK) // TQ
    tiles_ref[0] = n_past

    def scores(j, s_ref, mask=None):
        k0 = pl.multiple_of(tiles_ref[j] * TK, TK)
        s = _dot_nt(qa_ref[...], ks_ref[pl.ds(k0, TK), :])
        s_ref[...] = s if mask is None else s + mask

    scores(0, sa_ref, mb_ref[first_mask])

    bpt = TK // SEL_BLOCK
    n_tiles = n_sel // bpt
    blk_cnt = _dot(jnp.ones((SUBLANES, TQ), BF16), sel_q.astype(BF16))
    in_tile = (lax.broadcasted_iota(jnp.int32, (n_sel, LANES), 0) // bpt
               == lax.broadcasted_iota(jnp.int32, (n_sel, LANES), 1))
    tile_cnt = _dot(jnp.where(blk_cnt > 0.5, 1.0, 0.0).astype(BF16),
                    jnp.where(in_tile, 1.0, 0.0).astype(BF16))
    lane = lax.broadcasted_iota(jnp.int32, (1, LANES), 1)
    bit = jnp.where(tile_cnt[0:1, :] > 0.5, jnp.left_shift(1, lane & (_FLAG_BITS - 1)).astype(F32), 0.0)
    words = [jnp.sum(jnp.where((lane >= _FLAG_BITS * w) & (lane < _FLAG_BITS * (w + 1)), bit, 0.0)
                     ).astype(jnp.int32) for w in range(-(-n_tiles // _FLAG_BITS))]
    n_tasks = jnp.int32(1)
    for kt in range(n_tiles - 1):
        used = (words[kt // _FLAG_BITS] >> (kt % _FLAG_BITS)) & 1
        tiles_ref[n_tasks] = kt
        n_tasks = n_tasks + jnp.where(kt < n_past, used, 0)
    tiles_ref[n_tasks] = 0
    tiles_ref[n_tasks + 1] = 0

    m_ref[...] = jnp.full(m_ref.shape, NEG, F32)
    acc_ref[...] = jnp.zeros(acc_ref.shape, F32)

    gates = _dot(_hi_lo(g_ref[...]), ge_ref[...])
    gw = GQA * HEAD_DIM
    merged = gates[:, 0:gw] * _head_layout(acc_c) + gates[:, 2 * gw:3 * gw] * _head_layout(acc_w)
    gates_sel = gates[:, gw:2 * gw]

    def accumulate(j, s_ref):
        k0 = pl.multiple_of(tiles_ref[j] * TK, TK)
        s = s_ref[...]
        m = m_ref[...]
        m_new = jnp.maximum(m, jnp.max(s, axis=1, keepdims=True))
        p = jnp.concatenate([jnp.exp((s[:, c:c + LANES] - m_new).astype(BF16)) for c in range(0, TK, LANES)],
                            axis=1)
        acc_ref[...] = jnp.exp(m - m_new) * acc_ref[...] + _dot(p, vs_ref[pl.ds(k0, TK), :])
        m_ref[...] = m_new

    def pair(jj, c):
        a = 2 * jj
        scores(a + 1, sb_ref)
        accumulate(a, sa_ref)
        scores(a + 2, sa_ref)
        accumulate(a + 1, sb_ref)
        return c

    lax.fori_loop(0, n_tasks // 2, pair, 0)

    @pl.when(n_tasks % 2 == 1)
    def _():
        accumulate(n_tasks - 1, sa_ref)

    o_ref[...] = (merged + gates_sel * _head_layout(acc_ref[...])).astype(o_ref.dtype)


def _nsa(q, qfeat, kc_aug, vc, ks_aug, vs, kw_aug, vw, ov, cmask, tile_masks, gate_expand, gates, B, S):
    nq = S // TQ
    gw = GQA * HEAD_DIM
    ncmp = kc_aug.shape[2]
    n_sel = ov.shape[1]
    bk = lambda b, k, i: (b, k, 0, 0)
    kb = lambda b, k, i: (k, b, 0, 0)
    return pl.pallas_call(
        _nsa_kernel,
        grid=(B, N_KV_HEADS, nq),
        in_specs=[
            pl.BlockSpec((TQ, gw), lambda b, k, i: (b * nq + i, k)),
            pl.BlockSpec((None, GQA * TQ, HEAD_DIM), lambda b, k, i: (k, 0, 0)),
            pl.BlockSpec((None, None, ncmp, LANES), bk),
            pl.BlockSpec((None, None, ncmp, VW), bk),
            pl.BlockSpec((None, None, S, 2 * LANES), kb),
            pl.BlockSpec((None, None, S, LANES), kb),
            pl.BlockSpec((None, None, S, LANES), kb),
            pl.BlockSpec((None, None, S, VW), kb),
            pl.BlockSpec((ncmp, n_sel), lambda b, k, i: (0, 0)),
            pl.BlockSpec((GQA * TQ, ncmp), lambda b, k, i: (0, 0)),
            pl.BlockSpec(tile_masks.shape, lambda b, k, i: (0, 0, 0)),
            pl.BlockSpec((None,) + gate_expand.shape[1:], lambda b, k, i: (k, 0, 0)),
            pl.BlockSpec((TQ, _G_W), lambda b, k, i: (b * nq + i, 0)),
        ],
        out_specs=pl.BlockSpec((TQ, gw), lambda b, k, i: (b * nq + i, k)),
        out_shape=jax.ShapeDtypeStruct((B * S, N_KV_HEADS * gw), BF16),
        scratch_shapes=[
            pltpu.VMEM((GQA * TQ, 2 * LANES), BF16),
            pltpu.SMEM((S // TK + 2,), jnp.int32),
            pltpu.VMEM((GQA * TQ, LANES), F32),
            pltpu.VMEM((GQA * TQ, LANES), F32),
            pltpu.VMEM((GQA * TQ, TK), F32),
            pltpu.VMEM((GQA * TQ, TK), F32),
            pltpu.VMEM((S + WINDOW, LANES), BF16),
            pltpu.VMEM((S + WINDOW, VW), BF16),
        ],
        compiler_params=pltpu.CompilerParams(
            dimension_semantics=("arbitrary",) * 3, vmem_limit_bytes=VMEM_LIMIT),
        name="nsa",
    )(q, qfeat, kc_aug, vc, ks_aug, vs, kw_aug, vw, ov, cmask, tile_masks, gate_expand, gates)


_FF_CHUNK = 768


def _outproj_kernel(mc_ref, nsa_ref, x_ref, gn_ref, wo_ref, h_ref):
    mixed_n = _rms(nsa_ref[...].astype(F32), gn_ref[...]).astype(BF16)
    h_ref[...] = (x_ref[...] + _dot(mc_ref[...], wo_ref[0:CONV_CH, :])
                  + _dot(mixed_n, wo_ref[CONV_CH:2 * CONV_CH, :]))


def _outproj(mixed_c, nsa_out, x2, gn, wo, tm=1024):
    rows = x2.shape[0]
    row = lambda i: (i, 0)
    const = lambda i: (0, 0)
    return pl.pallas_call(
        _outproj_kernel,
        grid=(rows // tm,),
        in_specs=[
            pl.BlockSpec((tm, CONV_CH), row),
            pl.BlockSpec((tm, N_HEADS * HEAD_DIM), row),
            pl.BlockSpec((tm, D_MODEL), row),
            pl.BlockSpec((1, N_HEADS * HEAD_DIM), const),
            pl.BlockSpec((2 * CONV_CH, D_MODEL), const, pipeline_mode=pl.Buffered(1)),
        ],
        out_specs=pl.BlockSpec((tm, D_MODEL), row),
        out_shape=jax.ShapeDtypeStruct((rows, D_MODEL), F32),
        compiler_params=pltpu.CompilerParams(
            dimension_semantics=("arbitrary",), vmem_limit_bytes=VMEM_LIMIT),
        name="outproj",
    )(mixed_c, nsa_out, x2, gn, wo)


def _outffn_kernel(h_ref, g2_ref, wg_ref, wu_ref, wd_ref, gf_ref, o_ref):
    h = h_ref[...]
    u = _rms(h, g2_ref[...]).astype(BF16)
    acc = h
    for c in range(0, D_FF, _FF_CHUNK):
        sl = slice(c, min(c + _FF_CHUNK, D_FF))
        a = jax.nn.silu(_dot(u, wg_ref[:, sl])) * _dot(u, wu_ref[:, sl])
        acc = acc + _dot(a.astype(BF16), wd_ref[sl, :])
    o_ref[...] = _rms(acc, gf_ref[...])


def _outffn(h, g2, wg, wu, wd, gf, tm=512):
    rows = h.shape[0]
    row = lambda i: (i, 0)
    const = lambda i: (0, 0)
    resident = lambda shape: pl.BlockSpec(shape, const, pipeline_mode=pl.Buffered(1))
    return pl.pallas_call(
        _outffn_kernel,
        grid=(rows // tm,),
        in_specs=[
            pl.BlockSpec((tm, D_MODEL), row),
            pl.BlockSpec((1, D_MODEL), const),
            resident((D_MODEL, D_FF)),
            resident((D_MODEL, D_FF)),
            resident((D_FF, D_MODEL)),
            pl.BlockSpec((1, D_MODEL), const),
        ],
        out_specs=pl.BlockSpec((tm, D_MODEL), row),
        out_shape=jax.ShapeDtypeStruct((rows, D_MODEL), F32),
        compiler_params=pltpu.CompilerParams(
            dimension_semantics=("arbitrary",), vmem_limit_bytes=VMEM_LIMIT),
        name="outffn",
    )(h, g2, wg, wu, wd, gf)


def _key_features(pos, n_sel):
    f = np.zeros((pos.shape[0], LANES + n_sel), np.float32)
    f[:, _POS_HI] = pos // SEL_BLOCK
    f[:, _POS_LO] = pos % SEL_BLOCK
    if n_sel:
        f[np.arange(pos.shape[0]), LANES + pos // SEL_BLOCK] = 1.0
    return f


def _query_features():
    f = np.zeros((N_KV_HEADS, GQA * TQ, HEAD_DIM), np.float32)
    for k in range(N_KV_HEADS):
        for g in range(GQA):
            slope = 2.0 ** (-8.0 * (k * GQA + g + 1) / N_HEADS)
            f[k, g * TQ:(g + 1) * TQ, _POS_HI - HEAD_DIM] = slope * SEL_BLOCK
            f[k, g * TQ:(g + 1) * TQ, _POS_LO - HEAD_DIM] = slope
            f[k, g * TQ:(g + 1) * TQ, _PAD_MARK - HEAD_DIM] = MASK_BIAS
    return f


def _causal_masks(ncmp):
    r = np.arange(GQA * TQ)[:, None] % TQ
    cmask = (np.arange(ncmp)[None, :] - ((r + 1) // CMP_STRIDE)).astype(np.int32)
    c = np.arange(TK)[None, :]
    tile_masks = [np.where(c - r <= d * TQ, 0.0, NEG).astype(np.float32) for d in range(max(TK // TQ, 1))]
    return cmask, np.stack(tile_masks)


def _gate_expand():
    gw = GQA * HEAD_DIM
    e = np.zeros((N_KV_HEADS, 2 * _G_W, N_BRANCH * gw), np.float32)
    for k in range(N_KV_HEADS):
        for g in range(GQA):
            for br in range(N_BRANCH):
                for half in range(2):
                    e[k, half * _G_W + (k * GQA + g) * N_BRANCH + br,
                      br * gw + g * HEAD_DIM:br * gw + (g + 1) * HEAD_DIM] = 1.0
    return e


def _overlap(ncmp_pad, n_sel):
    nc = ncmp_pad - 1
    c_start = CMP_STRIDE * np.arange(nc)
    c_end = c_start + CMP_BLOCK - 1
    s_start = SEL_BLOCK * np.arange(n_sel)
    ov = np.minimum(c_end[:, None] + 1, s_start[None, :] + SEL_BLOCK) - np.maximum(c_start[:, None], s_start[None, :])
    out = np.zeros((ncmp_pad, n_sel), np.float32)
    out[:nc] = np.clip(ov, 0, None).astype(np.float32) / CMP_BLOCK
    return out


def _layer(x2, B, S, norm1_g, w_in, conv_w, k_cmp_pos, k_cmp_w1, k_cmp_w2, v_cmp_pos, v_cmp_w1, v_cmp_w2,
           gn_conv_g, later_weights):
    n_sel = S // SEL_BLOCK
    nch = S // CMP_STRIDE
    hd = HEAD_DIM

    kfeat = jnp.asarray(_key_features(np.arange(S), n_sel), BF16)
    mixed_c, q, gates, kvc, ks_aug, vs, kw_aug, vw, *later_bf16 = _inproj(
        x2, norm1_g.reshape(1, D_MODEL), w_in.astype(BF16), kfeat, conv_w, gn_conv_g.reshape(1, CONV_CH),
        later_weights)

    w1 = jnp.stack([k_cmp_w1, v_cmp_w1]).astype(BF16)
    w2 = jnp.stack([k_cmp_w2, v_cmp_w2]).astype(BF16)
    pos = jnp.stack([k_cmp_pos, v_cmp_pos]).reshape(2, 1, CMP_BLOCK * hd)
    pos = jnp.broadcast_to(pos, (2, SUBLANES, CMP_BLOCK * hd)).astype(BF16)
    c_end = CMP_STRIDE * np.arange(nch) + CMP_BLOCK - 1
    cfeat = jnp.asarray(_key_features(c_end, 0), BF16)
    kc_aug, vc = _compress(kvc, pos, w1, w2, cfeat, B, S)

    cmask, tile_masks = _causal_masks(nch)

    nsa_out = _nsa(q, jnp.asarray(_query_features(), BF16), kc_aug, vc,
                   ks_aug.reshape(N_KV_HEADS, B, S, 2 * LANES), vs.reshape(N_KV_HEADS, B, S, LANES),
                   kw_aug.reshape(N_KV_HEADS, B, S, LANES), vw.reshape(N_KV_HEADS, B, S, VW),
                   jnp.asarray(_overlap(nch, n_sel), BF16), jnp.asarray(cmask),
                   jnp.asarray(tile_masks), jnp.asarray(_gate_expand(), BF16), gates, B, S)
    return mixed_c, nsa_out, later_bf16


def kernel(x, norm1_g, w_in, conv_w, k_cmp_pos, k_cmp_w1, k_cmp_w2, v_cmp_pos, v_cmp_w1, v_cmp_w2,
           gn_conv_g, gn_nsa_g, w_out, norm2_g, w_gate, w_up, w_down, norm_f_g):
    B, S, _ = x.shape
    depth = norm1_g.shape[0]
    assert depth == 1, "the final norm is fused into the (single) layer's FFN kernel"
    x2 = x.reshape(B * S, D_MODEL)
    mixed_c, nsa_out, (wo, wg, wu, wd) = _layer(
        x2, B, S, norm1_g[0], w_in[0], conv_w[0], k_cmp_pos[0], k_cmp_w1[0], k_cmp_w2[0], v_cmp_pos[0],
        v_cmp_w1[0], v_cmp_w2[0], gn_conv_g[0], [w_out[0], w_gate[0], w_up[0], w_down[0]])
    h = _outproj(mixed_c, nsa_out, x2, gn_nsa_g[0].reshape(1, N_HEADS * HEAD_DIM), wo)
    out = _outffn(h, norm2_g[0].reshape(1, D_MODEL), wg, wu, wd, norm_f_g.reshape(1, D_MODEL))
    return out.reshape(B, S, D_MODEL)
```

```python
import functools

import numpy as np
import jax
import jax.numpy as jnp
from jax import lax
from jax.experimental import pallas as pl
from jax.experimental.pallas import tpu as pltpu

D_MODEL = 1024
CONV_CH = 512
CONV_K = 3
N_HEADS = 8
HEAD_DIM = 64
N_KV_HEADS = 2
GQA = 4
N_BRANCH = 3
CMP_BLOCK = 32
CMP_STRIDE = 16
CMP_HIDDEN = 256
SEL_BLOCK = 64
SEL_TOP = 16
WINDOW = 512
D_FF = 2816
NORM_EPS = 1e-6

F32 = jnp.float32
BF16 = jnp.bfloat16

LANES = 128
VW = 2 * LANES
SUBLANES = 8
VMEM_LIMIT = 56 * 1024 * 1024

TQ = 256
TK = 256
NEG = -1e30
MASK_BIAS = -float(2 ** 30)
_POS_HI, _POS_LO, _PAD_MARK = HEAD_DIM, HEAD_DIM + 1, HEAD_DIM + 2
_FLAG_BITS = 16

_CONV_W = 3 * CONV_CH
_Q_OFF = _CONV_W
_KVC_OFF = _Q_OFF + N_HEADS * HEAD_DIM
_KV_OFF = _KVC_OFF + 2 * N_KV_HEADS * HEAD_DIM
_G_OFF = _KV_OFF + 4 * N_KV_HEADS * HEAD_DIM
_G_W = N_HEADS * N_BRANCH
_PROJ_W = _G_OFF + _G_W


def _rms(x, g):
    return x * lax.rsqrt(jnp.mean(x * x, axis=-1, keepdims=True) + NORM_EPS) * g


def _dot(a, b):
    return jnp.dot(a, b, preferred_element_type=F32)


def _value_rows(v2, h, wide):
    lane = lax.broadcasted_iota(jnp.int32, v2.shape, 1)
    swapped = pltpu.roll(v2, HEAD_DIM, axis=1)
    if not wide:
        return jnp.where(lane < HEAD_DIM, v2 if h == 0 else swapped, 1.0).astype(BF16)
    dup = jnp.where((lane < HEAD_DIM) == (h == 0), v2, swapped)
    return jnp.concatenate([dup.astype(BF16), jnp.ones(v2.shape, BF16)], axis=1)


def _dot_nt(a, b):
    return lax.dot_general(a, b, (((1,), (1,)), ((), ())), preferred_element_type=F32)


def _inproj_kernel(seq_tiles, n_cast, x_ref, xp_ref, g_ref, w_ref, kf_ref, cw_ref, gc_ref, *refs):
    cast_in, refs = refs[:n_cast], refs[n_cast:]
    (mixc_ref, q_ref, gate_ref, kvc_ref, ks_ref, vs_ref, kw_ref, vw_ref), cast_out = refs[:8], refs[8:]
    for src_ref, dst_ref in zip(cast_in, cast_out):
        dst_ref[...] = src_ref[...].astype(BF16)

    u = _rms(x_ref[...], g_ref[...]).astype(BF16)

    up = _rms(xp_ref[...], g_ref[...]).astype(BF16)
    ext = _dot(jnp.concatenate([up, u], axis=0), w_ref[:, 0:_CONV_W])
    cbch = ext[SUBLANES:, :]
    cu = cbch[:, CONV_CH:2 * CONV_CH] * cbch[:, 2 * CONV_CH:3 * CONV_CH]
    starts_sequence = pl.program_id(0) % seq_tiles == 0
    prev = jnp.where(starts_sequence, 0.0,
                     ext[0:SUBLANES, CONV_CH:2 * CONV_CH] * ext[0:SUBLANES, 2 * CONV_CH:3 * CONV_CH])
    r = lax.broadcasted_iota(jnp.int32, cu.shape, 0)
    cu1 = jnp.where(r == 0, prev[7:8, :], pltpu.roll(cu, 1, axis=0))
    cu2 = pltpu.roll(cu, 2, axis=0)
    cu2 = jnp.where(r == 0, prev[6:7, :], jnp.where(r == 1, prev[7:8, :], cu2))
    cw = cw_ref[...]
    y = cbch[:, 0:CONV_CH] * (cw[0:1, :] * cu2 + cw[1:2, :] * cu1 + cw[2:3, :] * cu)
    mixc_ref[...] = _rms(y, gc_ref[...]).astype(BF16)

    q_ref[...] = _dot(u, w_ref[:, _Q_OFF:_KVC_OFF]).astype(BF16)
    kvc_ref[...] = _dot(u, w_ref[:, _KVC_OFF:_KV_OFF])
    gate_ref[...] = jax.nn.sigmoid(_dot(u, w_ref[:, _G_OFF:_PROJ_W]))
    kv = _dot(u, w_ref[:, _KV_OFF:_G_OFF])
    kf = kf_ref[...]
    hd, kvw = HEAD_DIM, N_KV_HEADS * HEAD_DIM
    for h in range(N_KV_HEADS):
        ks_ref[h] = kf
        ks_ref[h, :, 0:hd] = kv[:, h * hd:(h + 1) * hd].astype(BF16)
        vs_ref[h] = _value_rows(kv[:, kvw:2 * kvw], h, False)
        kw_ref[h] = kf[:, 0:LANES]
        kw_ref[h, :, 0:hd] = kv[:, 2 * kvw + h * hd:2 * kvw + (h + 1) * hd].astype(BF16)
        vw_ref[h] = _value_rows(kv[:, 3 * kvw:4 * kvw], h, True)


def _inproj(x2, g, w, kfeat, conv_w, gc, to_cast, tm=1024):
    rows = x2.shape[0]
    steps = rows // tm
    seq_tiles = kfeat.shape[0] // tm
    row = lambda i: (i, 0)
    head_row = lambda i: (0, i, 0)
    const = lambda i: (0, 0)
    cast_specs = [pl.BlockSpec((a.shape[0] // steps, a.shape[1]), row) for a in to_cast]
    return pl.pallas_call(
        functools.partial(_inproj_kernel, seq_tiles, len(to_cast)),
        grid=(steps,),
        in_specs=[
            pl.BlockSpec((tm, D_MODEL), row),
            pl.BlockSpec((SUBLANES, D_MODEL), lambda i: (jnp.maximum(i * (tm // SUBLANES) - 1, 0), 0)),
            pl.BlockSpec((1, D_MODEL), const),
            pl.BlockSpec((D_MODEL, _PROJ_W), const, pipeline_mode=pl.Buffered(1)),
            pl.BlockSpec((tm, 2 * LANES), lambda i: (i % seq_tiles, 0)),
            pl.BlockSpec((CONV_K, CONV_CH), const),
            pl.BlockSpec((1, CONV_CH), const),
        ] + cast_specs,
        out_specs=[
            pl.BlockSpec((tm, CONV_CH), row),
            pl.BlockSpec((tm, N_HEADS * HEAD_DIM), row),
            pl.BlockSpec((tm, _G_W), row),
            pl.BlockSpec((tm, _KV_OFF - _KVC_OFF), row),
            pl.BlockSpec((N_KV_HEADS, tm, 2 * LANES), head_row),
            pl.BlockSpec((N_KV_HEADS, tm, LANES), head_row),
            pl.BlockSpec((N_KV_HEADS, tm, LANES), head_row),
            pl.BlockSpec((N_KV_HEADS, tm, VW), head_row),
        ] + cast_specs,
        out_shape=[
            jax.ShapeDtypeStruct((rows, CONV_CH), BF16),
            jax.ShapeDtypeStruct((rows, N_HEADS * HEAD_DIM), BF16),
            jax.ShapeDtypeStruct((rows, _G_W), F32),
            jax.ShapeDtypeStruct((rows, _KV_OFF - _KVC_OFF), F32),
            jax.ShapeDtypeStruct((N_KV_HEADS, rows, 2 * LANES), BF16),
            jax.ShapeDtypeStruct((N_KV_HEADS, rows, LANES), BF16),
            jax.ShapeDtypeStruct((N_KV_HEADS, rows, LANES), BF16),
            jax.ShapeDtypeStruct((N_KV_HEADS, rows, VW), BF16),
        ] + [jax.ShapeDtypeStruct(a.shape, BF16) for a in to_cast],
        compiler_params=pltpu.CompilerParams(
            dimension_semantics=("arbitrary",), vmem_limit_bytes=VMEM_LIMIT),
        name="inproj",
    )(x2, x2, g, w, kfeat, conv_w, gc, *to_cast)


def _compress_kernel(xk_ref, xv_ref, pos_ref, w1_ref, w2_ref, cf_ref, kc_ref, vc_ref):
    nch = xk_ref.shape[0] // CMP_STRIDE
    hd = HEAD_DIM
    first = [jnp.zeros((nch, CMP_HIDDEN), F32) for _ in range(2 * N_KV_HEADS)]
    second = [jnp.zeros((nch, CMP_HIDDEN), F32) for _ in range(2 * N_KV_HEADS)]
    group = 2 * LANES // hd
    for l0 in range(0, CMP_STRIDE, group):
        for s, x_ref in enumerate((xk_ref, xv_ref)):
            xs = [x_ref[pl.ds(l0 + j, nch, stride=CMP_STRIDE), :].astype(BF16) for j in range(group)]
            for h in range(N_KV_HEADS):
                c = s * N_KV_HEADS + h
                xh = jnp.concatenate([x[:, h * hd:(h + 1) * hd] for x in xs], axis=1)
                first[c] = first[c] + _dot(xh, w1_ref[s, l0 * hd:(l0 + group) * hd, :])
                second[c] = second[c] + _dot(xh, w1_ref[s, (CMP_STRIDE + l0) * hd:(CMP_STRIDE + l0 + group) * hd, :])
    for s in range(2):
        pb = _dot(pos_ref[s], w1_ref[s])[0:1, :]
        outs = []
        for h in range(N_KV_HEADS):
            c = s * N_KV_HEADS + h
            hid = first[c] + pltpu.roll(second[c], nch - 1, axis=0) + pb
            outs.append(_dot(jax.nn.gelu(hid, approximate=True).astype(BF16), w2_ref[s]))
        for h in range(N_KV_HEADS):
            if s == 0:
                kc_ref[h] = cf_ref[...]
                kc_ref[h, :, 0:hd] = outs[h].astype(BF16)
            else:
                vc_ref[h] = _value_rows(jnp.concatenate(outs, axis=1), h, True)


def _compress(kvc, pos, w1, w2, cfeat, B, S):
    nch = S // CMP_STRIDE
    full = lambda *shape: pl.BlockSpec(shape, lambda b: (0,) * len(shape))
    return pl.pallas_call(
        _compress_kernel,
        grid=(B,),
        in_specs=[
            pl.BlockSpec((S, LANES), lambda b: (b, 0)),
            pl.BlockSpec((S, LANES), lambda b: (b, 1)),
            full(*pos.shape), full(*w1.shape), full(*w2.shape), full(*cfeat.shape),
        ],
        out_specs=[
            pl.BlockSpec((None, N_KV_HEADS, nch, LANES), lambda b: (b, 0, 0, 0)),
            pl.BlockSpec((None, N_KV_HEADS, nch, VW), lambda b: (b, 0, 0, 0)),
        ],
        out_shape=[
            jax.ShapeDtypeStruct((B, N_KV_HEADS, nch, LANES), BF16),
            jax.ShapeDtypeStruct((B, N_KV_HEADS, nch, VW), BF16),
        ],
        compiler_params=pltpu.CompilerParams(
            dimension_semantics=("arbitrary",), vmem_limit_bytes=VMEM_LIMIT),
        name="compress",
    )(kvc, kvc, pos, w1, w2, cfeat)


def _hi_lo(x):
    hi = x.astype(BF16)
    return jnp.concatenate([hi, (x - hi.astype(F32)).astype(BF16)], axis=1)


def _head_layout(acc):
    low_half = lax.broadcasted_iota(jnp.int32, (TQ, LANES), 1) < HEAD_DIM
    head = lambda x, g: x[g * TQ:(g + 1) * TQ]
    if acc.shape[1] == VW:
        x = acc[:, 0:LANES] * (1.0 / jnp.maximum(acc[:, LANES:VW], 1e-30))
        pairs = [jnp.where(low_half, head(x, 2 * j), head(x, 2 * j + 1)) for j in range(GQA // 2)]
    else:
        rot = pltpu.roll(acc, HEAD_DIM, axis=1)
        pairs = [jnp.where(low_half, head(acc, 2 * j) * (1.0 / head(rot, 2 * j)),
                           head(rot, 2 * j + 1) * (1.0 / head(acc, 2 * j + 1))) for j in range(GQA // 2)]
    return jnp.concatenate(pairs, axis=1)


def _exchange(v, i, j, descending):
    hi, lo = jnp.maximum(v[i], v[j]), jnp.minimum(v[i], v[j])
    v[i], v[j] = (hi, lo) if descending else (lo, hi)


def _bitonic_merge_desc(v):
    n = len(v)
    j = n // 2
    while j >= 1:
        for i in range(n):
            if i ^ j > i:
                _exchange(v, i, i ^ j, True)
        j //= 2


def _kth_largest(x, kth, fillers=()):
    n = x.shape[0]
    assert n == kth * SUBLANES and kth & (kth - 1) == 0
    v = [x[r * SUBLANES:(r + 1) * SUBLANES, :] for r in range(kth)]
    fillers = list(fillers)
    layers = 0
    k = 2
    while k <= kth:
        j = k // 2
        while j >= 1:
            for i in range(kth):
                if i ^ j > i:
                    _exchange(v, i, i ^ j, (i & k) == 0)
            j //= 2
            layers += 1
            if layers % 3 == 0 and fillers:
                fillers.pop(0)()
        k *= 2
    shift = SUBLANES // 2
    while shift >= 1:
        v = [jnp.maximum(v[i], pltpu.roll(v[kth - 1 - i], shift, axis=0)) for i in range(kth)]
        if shift > 1:
            _bitonic_merge_desc(v)
        if fillers:
            fillers.pop(0)()
        shift //= 2
    for filler in fillers:
        filler()
    out = v[0]
    for i in range(1, kth):
        out = jnp.minimum(out, v[i])
    return out[0:1, :]


def _nsa_kernel(q_ref, qf_ref, kc_ref, vc_ref, ks_ref, vs_ref, kw_ref, vw_ref, ov_ref, cm_ref, mb_ref,
                ge_ref, g_ref, o_ref, qa_ref, tiles_ref, m_ref, acc_ref, sa_ref, sb_ref, sc_ref, kwp_ref, vwp_ref):
    i = pl.program_id(2)
    t0 = i * TQ
    rows = GQA * TQ
    n_sel = ov_ref.shape[1]
    ncmp = kc_ref.shape[0]

    @pl.when(i == 0)
    def _():
        lane = lax.broadcasted_iota(jnp.int32, (WINDOW, LANES), 1)
        kwp_ref[0:WINDOW, :] = jnp.where(lane == _PAD_MARK, 1.0, 0.0).astype(BF16)
        kwp_ref[WINDOW:, :] = kw_ref[...]
        vwp_ref[0:WINDOW, :] = jnp.zeros((WINDOW, VW), BF16)
        vwp_ref[WINDOW:, :] = vw_ref[...]

    q = q_ref[...]
    for g in range(GQA):
        qg = q[:, g * HEAD_DIM:(g + 1) * HEAD_DIM].astype(F32) * (HEAD_DIM ** -0.5)
        qa_ref[g * TQ:(g + 1) * TQ, 0:HEAD_DIM] = qg.astype(BF16)
    qa_ref[:, HEAD_DIM:LANES] = qf_ref[...]
    qa1 = qa_ref[:, 0:LANES]

    diag_ok = (lax.broadcasted_iota(jnp.int32, (rows, TQ), 1)
               <= (lax.broadcasted_iota(jnp.int32, (rows, TQ), 0) & (TQ - 1)))


    wk = TQ + WINDOW
    w0 = pl.multiple_of(t0, TQ)
    win = {}

    def win_scores():
        s_w = _dot_nt(qa1, kwp_ref[pl.ds(w0, wk), :])
        win["s"] = (jnp.where(diag_ok, NEG, s_w[:, 0:TQ]),
                    s_w[:, TQ:WINDOW], jnp.where(diag_ok, s_w[:, WINDOW:wk], NEG))

    def win_max():
        s_old, s_mid, s_new = win["s"]
        win["m"] = jnp.maximum(jnp.maximum(jnp.max(s_old, axis=1, keepdims=True),
                                           jnp.max(s_mid, axis=1, keepdims=True)),
                               jnp.max(s_new, axis=1, keepdims=True))

    def win_exp():
        win["e"] = jnp.concatenate([jnp.exp(s - win["m"]) for s in win["s"]], axis=1).astype(BF16)

    def win_pv():
        win["acc"] = _dot(win["e"], vwp_ref[pl.ds(w0, wk), :])

    s_c = _dot_nt(qa1, kc_ref[...])
    s_c = jnp.where(cm_ref[...] <= (TQ // CMP_STRIDE) * i - 2, s_c, NEG)
    m_c = jnp.maximum(jnp.max(s_c, axis=1, keepdims=True), 0.5 * NEG)
    e_c = jnp.exp(s_c - m_c)
    acc_c = _dot(e_c.astype(BF16), vc_ref[...])
    inv_c = 1.0 / jnp.maximum(acc_c[:, LANES:VW], 1e-30)
    p_c = e_c * jnp.concatenate([inv_c] * (ncmp // LANES), axis=1)

    p_sum = p_c[0:TQ] + p_c[TQ:2 * TQ] + p_c[2 * TQ:3 * TQ] + p_c[3 * TQ:4 * TQ]
    p_hi = p_sum.astype(BF16)
    p_lo = (p_sum - p_hi.astype(F32)).astype(BF16)
    imp = _dot(p_hi, ov_ref[...]) + _dot(p_lo, ov_ref[...])
    imp_t = imp.T
    jb = lax.broadcasted_iota(jnp.int32, (n_sel, TQ), 0)
    tq = t0 + lax.broadcasted_iota(jnp.int32, (n_sel, TQ), 1)
    imp_t = jnp.where(SEL_BLOCK * jb > tq, -jnp.inf, imp_t)
    forced = (jb == 0) | (jb == (tq >> (SEL_BLOCK.bit_length() - 1)))
    imp_t = jnp.where(forced, jnp.inf, imp_t)

    thr = _kth_largest(imp_t, SEL_TOP, (win_scores, win_max, win_exp, win_pv))
    acc_w = win["acc"]
    above = imp_t > thr
    tied = imp_t == thr
    n_above = jnp.sum(jnp.where(above, 1.0, 0.0), axis=0, keepdims=True)
    lower = (lax.broadcasted_iota(jnp.int32, (n_sel, n_sel), 1)
             < lax.broadcasted_iota(jnp.int32, (n_sel, n_sel), 0))
    tied_before = _dot(jnp.where(lower, 1.0, 0.0).astype(BF16), jnp.where(tied, 1.0, 0.0).astype(BF16))
    take_tie = jnp.where(tied_before + n_above < float(SEL_TOP), 1.0, 0.0)
    sel_t = jnp.where(above, 1.0, jnp.where(tied, take_tie, 0.0))
    sel_q = sel_t.T
    bias = ((1.0 - sel_q) * MASK_BIAS).astype(BF16)
    for g in range(GQA):
        qa_ref[g * TQ:(g + 1) * TQ, LANES:2 * LANES] = bias

    n_past = t0 // TK
    first_mask = (t0 - n_past * TK) // TQ
    tiles_ref[0] = n_past

    def scores(j, s_ref, mask=None):
        k0 = pl.multiple_of(tiles_ref[j] * TK, TK)
        s = _dot_nt(qa_ref[...], ks_ref[pl.ds(k0, TK), :])
        s_ref[...] = s if mask is None else s + mask

    scores(0, sa_ref, mb_ref[first_mask])

    bpt = TK // SEL_BLOCK
    n_tiles = n_sel // bpt
    blk_cnt = _dot(jnp.ones((SUBLANES, TQ), BF16), sel_q.astype(BF16))
    in_tile = (lax.broadcasted_iota(jnp.int32, (n_sel, LANES), 0) // bpt
               == lax.broadcasted_iota(jnp.int32, (n_sel, LANES), 1))
    tile_cnt = _dot(jnp.where(blk_cnt > 0.5, 1.0, 0.0).astype(BF16),
                    jnp.where(in_tile, 1.0, 0.0).astype(BF16))
    lane = lax.broadcasted_iota(jnp.int32, (1, LANES), 1)
    bit = jnp.where(tile_cnt[0:1, :] > 0.5, jnp.left_shift(1, lane & (_FLAG_BITS - 1)).astype(F32), 0.0)
    words = [jnp.sum(jnp.where((lane >= _FLAG_BITS * w) & (lane < _FLAG_BITS * (w + 1)), bit, 0.0)
                     ).astype(jnp.int32) for w in range(-(-n_tiles // _FLAG_BITS))]
    n_tasks = jnp.int32(1)
    for kt in range(n_tiles - 1):
        used = (words[kt // _FLAG_BITS] >> (kt % _FLAG_BITS)) & 1
        tiles_ref[n_tasks] = kt
        n_tasks = n_tasks + jnp.where(kt < n_past, used, 0)
    tiles_ref[n_tasks] = 0
    tiles_ref[n_tasks + 1] = 0

    m_ref[...] = jnp.full(m_ref.shape, NEG, F32)
    acc_ref[...] = jnp.zeros(acc_ref.shape, F32)

    gates = _dot(_hi_lo(g_ref[...]), ge_ref[...])
    gw = GQA * HEAD_DIM
    merged = gates[:, 0:gw] * _head_layout(acc_c) + gates[:, 2 * gw:3 * gw] * _head_layout(acc_w)
    gates_sel = gates[:, gw:2 * gw]

    def accumulate(j, s_ref):
        k0 = pl.multiple_of(tiles_ref[j] * TK, TK)
        s = s_ref[...]
        m = m_ref[...]
        m_new = jnp.maximum(m, jnp.max(s, axis=1, keepdims=True))
        p = jnp.concatenate([jnp.exp((s[:, c:c + LANES] - m_new).astype(BF16)) for c in range(0, TK, LANES)],
                            axis=1)
        acc_ref[...] = jnp.exp(m - m_new) * acc_ref[...] + _dot(p, vs_ref[pl.ds(k0, TK), :])
        m_ref[...] = m_new

    def triple(jj, c):
        a = 3 * jj
        scores(a + 1, sb_ref)
        accumulate(a, sa_ref)
        scores(a + 2, sc_ref)
        accumulate(a + 1, sb_ref)
        scores(a + 3, sa_ref)
        accumulate(a + 2, sc_ref)
        return c

    trips = n_tasks // 3
    lax.fori_loop(0, trips, triple, 0)

    rest = 3 * trips

    @pl.when(n_tasks - rest == 1)
    def _():
        accumulate(rest, sa_ref)

    @pl.when(n_tasks - rest == 2)
    def _():
        scores(rest + 1, sb_ref)
        accumulate(rest, sa_ref)
        accumulate(rest + 1, sb_ref)

    o_ref[...] = (merged + gates_sel * _head_layout(acc_ref[...])).astype(o_ref.dtype)


def _nsa(q, qfeat, kc_aug, vc, ks_aug, vs, kw_aug, vw, ov, cmask, tile_masks, gate_expand, gates, B, S):
    nq = S // TQ
    gw = GQA * HEAD_DIM
    ncmp = kc_aug.shape[2]
    n_sel = ov.shape[1]
    bk = lambda b, k, i: (b, k, 0, 0)
    kb = lambda b, k, i: (k, b, 0, 0)
    return pl.pallas_call(
        _nsa_kernel,
        grid=(B, N_KV_HEADS, nq),
        in_specs=[
            pl.BlockSpec((TQ, gw), lambda b, k, i: (b * nq + i, k)),
            pl.BlockSpec((None, GQA * TQ, HEAD_DIM), lambda b, k, i: (k, 0, 0)),
            pl.BlockSpec((None, None, ncmp, LANES), bk),
            pl.BlockSpec((None, None, ncmp, VW), bk),
            pl.BlockSpec((None, None, S, 2 * LANES), kb),
            pl.BlockSpec((None, None, S, LANES), kb),
            pl.BlockSpec((None, None, S, LANES), kb),
            pl.BlockSpec((None, None, S, VW), kb),
            pl.BlockSpec((ncmp, n_sel), lambda b, k, i: (0, 0)),
            pl.BlockSpec((GQA * TQ, ncmp), lambda b, k, i: (0, 0)),
            pl.BlockSpec(tile_masks.shape, lambda b, k, i: (0, 0, 0)),
            pl.BlockSpec((None,) + gate_expand.shape[1:], lambda b, k, i: (k, 0, 0)),
            pl.BlockSpec((TQ, _G_W), lambda b, k, i: (b * nq + i, 0)),
        ],
        out_specs=pl.BlockSpec((TQ, gw), lambda b, k, i: (b * nq + i, k)),
        out_shape=jax.ShapeDtypeStruct((B * S, N_KV_HEADS * gw), BF16),
        scratch_shapes=[
            pltpu.VMEM((GQA * TQ, 2 * LANES), BF16),
            pltpu.SMEM((S // TK + 2,), jnp.int32),
            pltpu.VMEM((GQA * TQ, LANES), F32),
            pltpu.VMEM((GQA * TQ, LANES), F32),
            pltpu.VMEM((GQA * TQ, TK), F32),
            pltpu.VMEM((GQA * TQ, TK), F32),
            pltpu.VMEM((GQA * TQ, TK), F32),
            pltpu.VMEM((S + WINDOW, LANES), BF16),
            pltpu.VMEM((S + WINDOW, VW), BF16),
        ],
        compiler_params=pltpu.CompilerParams(
            dimension_semantics=("arbitrary",) * 3, vmem_limit_bytes=VMEM_LIMIT),
        name="nsa",
    )(q, qfeat, kc_aug, vc, ks_aug, vs, kw_aug, vw, ov, cmask, tile_masks, gate_expand, gates)


_FF_CHUNK = 768


def _outproj_kernel(mc_ref, nsa_ref, x_ref, gn_ref, wo_ref, h_ref):
    mixed_n = _rms(nsa_ref[...].astype(F32), gn_ref[...]).astype(BF16)
    h_ref[...] = (x_ref[...] + _dot(mc_ref[...], wo_ref[0:CONV_CH, :])
                  + _dot(mixed_n, wo_ref[CONV_CH:2 * CONV_CH, :]))


def _outproj(mixed_c, nsa_out, x2, gn, wo, tm=1024):
    rows = x2.shape[0]
    row = lambda i: (i, 0)
    const = lambda i: (0, 0)
    return pl.pallas_call(
        _outproj_kernel,
        grid=(rows // tm,),
        in_specs=[
            pl.BlockSpec((tm, CONV_CH), row),
            pl.BlockSpec((tm, N_HEADS * HEAD_DIM), row),
            pl.BlockSpec((tm, D_MODEL), row),
            pl.BlockSpec((1, N_HEADS * HEAD_DIM), const),
            pl.BlockSpec((2 * CONV_CH, D_MODEL), const, pipeline_mode=pl.Buffered(1)),
        ],
        out_specs=pl.BlockSpec((tm, D_MODEL), row),
        out_shape=jax.ShapeDtypeStruct((rows, D_MODEL), F32),
        compiler_params=pltpu.CompilerParams(
            dimension_semantics=("arbitrary",), vmem_limit_bytes=VMEM_LIMIT),
        name="outproj",
    )(mixed_c, nsa_out, x2, gn, wo)


def _outffn_kernel(h_ref, g2_ref, wg_ref, wu_ref, wd_ref, gf_ref, o_ref):
    h = h_ref[...]
    u = _rms(h, g2_ref[...]).astype(BF16)
    acc = h
    for c in range(0, D_FF, _FF_CHUNK):
        sl = slice(c, min(c + _FF_CHUNK, D_FF))
        a = jax.nn.silu(_dot(u, wg_ref[:, sl])) * _dot(u, wu_ref[:, sl])
        acc = acc + _dot(a.astype(BF16), wd_ref[sl, :])
    o_ref[...] = _rms(acc, gf_ref[...])


def _outffn(h, g2, wg, wu, wd, gf, tm=512):
    rows = h.shape[0]
    row = lambda i: (i, 0)
    const = lambda i: (0, 0)
    resident = lambda shape: pl.BlockSpec(shape, const, pipeline_mode=pl.Buffered(1))
    return pl.pallas_call(
        _outffn_kernel,
        grid=(rows // tm,),
        in_specs=[
            pl.BlockSpec((tm, D_MODEL), row),
            pl.BlockSpec((1, D_MODEL), const),
            resident((D_MODEL, D_FF)),
            resident((D_MODEL, D_FF)),
            resident((D_FF, D_MODEL)),
            pl.BlockSpec((1, D_MODEL), const),
        ],
        out_specs=pl.BlockSpec((tm, D_MODEL), row),
        out_shape=jax.ShapeDtypeStruct((rows, D_MODEL), F32),
        compiler_params=pltpu.CompilerParams(
            dimension_semantics=("arbitrary",), vmem_limit_bytes=VMEM_LIMIT),
        name="outffn",
    )(h, g2, wg, wu, wd, gf)


def _key_features(pos, n_sel):
    f = np.zeros((pos.shape[0], LANES + n_sel), np.float32)
    f[:, _POS_HI] = pos // SEL_BLOCK
    f[:, _POS_LO] = pos % SEL_BLOCK
    if n_sel:
        f[np.arange(pos.shape[0]), LANES + pos // SEL_BLOCK] = 1.0
    return f


def _query_features():
    f = np.zeros((N_KV_HEADS, GQA * TQ, HEAD_DIM), np.float32)
    for k in range(N_KV_HEADS):
        for g in range(GQA):
            slope = 2.0 ** (-8.0 * (k * GQA + g + 1) / N_HEADS)
            f[k, g * TQ:(g + 1) * TQ, _POS_HI - HEAD_DIM] = slope * SEL_BLOCK
            f[k, g * TQ:(g + 1) * TQ, _POS_LO - HEAD_DIM] = slope
            f[k, g * TQ:(g + 1) * TQ, _PAD_MARK - HEAD_DIM] = MASK_BIAS
    return f


def _causal_masks(ncmp):
    r = np.arange(GQA * TQ)[:, None] % TQ
    cmask = (np.arange(ncmp)[None, :] - ((r + 1) // CMP_STRIDE)).astype(np.int32)
    c = np.arange(TK)[None, :]
    tile_masks = [np.where(c - r <= d * TQ, 0.0, NEG).astype(np.float32) for d in range(max(TK // TQ, 1))]
    return cmask, np.stack(tile_masks)


def _gate_expand():
    gw = GQA * HEAD_DIM
    e = np.zeros((N_KV_HEADS, 2 * _G_W, N_BRANCH * gw), np.float32)
    for k in range(N_KV_HEADS):
        for g in range(GQA):
            for br in range(N_BRANCH):
                for half in range(2):
                    e[k, half * _G_W + (k * GQA + g) * N_BRANCH + br,
                      br * gw + g * HEAD_DIM:br * gw + (g + 1) * HEAD_DIM] = 1.0
    return e


def _overlap(ncmp_pad, n_sel):
    nc = ncmp_pad - 1
    c_start = CMP_STRIDE * np.arange(nc)
    c_end = c_start + CMP_BLOCK - 1
    s_start = SEL_BLOCK * np.arange(n_sel)
    ov = np.minimum(c_end[:, None] + 1, s_start[None, :] + SEL_BLOCK) - np.maximum(c_start[:, None], s_start[None, :])
    out = np.zeros((ncmp_pad, n_sel), np.float32)
    out[:nc] = np.clip(ov, 0, None).astype(np.float32) / CMP_BLOCK
    return out


def _layer(x2, B, S, norm1_g, w_in, conv_w, k_cmp_pos, k_cmp_w1, k_cmp_w2, v_cmp_pos, v_cmp_w1, v_cmp_w2,
           gn_conv_g, later_weights):
    n_sel = S // SEL_BLOCK
    nch = S // CMP_STRIDE
    hd = HEAD_DIM

    kfeat = jnp.asarray(_key_features(np.arange(S), n_sel), BF16)
    mixed_c, q, gates, kvc, ks_aug, vs, kw_aug, vw, *later_bf16 = _inproj(
        x2, norm1_g.reshape(1, D_MODEL), w_in.astype(BF16), kfeat, conv_w, gn_conv_g.reshape(1, CONV_CH),
        later_weights)

    w1 = jnp.stack([k_cmp_w1, v_cmp_w1]).astype(BF16)
    w2 = jnp.stack([k_cmp_w2, v_cmp_w2]).astype(BF16)
    pos = jnp.stack([k_cmp_pos, v_cmp_pos]).reshape(2, 1, CMP_BLOCK * hd)
    pos = jnp.broadcast_to(pos, (2, SUBLANES, CMP_BLOCK * hd)).astype(BF16)
    c_end = CMP_STRIDE * np.arange(nch) + CMP_BLOCK - 1
    cfeat = jnp.asarray(_key_features(c_end, 0), BF16)
    kc_aug, vc = _compress(kvc, pos, w1, w2, cfeat, B, S)

    cmask, tile_masks = _causal_masks(nch)

    nsa_out = _nsa(q, jnp.asarray(_query_features(), BF16), kc_aug, vc,
                   ks_aug.reshape(N_KV_HEADS, B, S, 2 * LANES), vs.reshape(N_KV_HEADS, B, S, LANES),
                   kw_aug.reshape(N_KV_HEADS, B, S, LANES), vw.reshape(N_KV_HEADS, B, S, VW),
                   jnp.asarray(_overlap(nch, n_sel), BF16), jnp.asarray(cmask),
                   jnp.asarray(tile_masks), jnp.asarray(_gate_expand(), BF16), gates, B, S)
    return mixed_c, nsa_out, later_bf16


def kernel(x, norm1_g, w_in, conv_w, k_cmp_pos, k_cmp_w1, k_cmp_w2, v_cmp_pos, v_cmp_w1, v_cmp_w2,
           gn_conv_g, gn_nsa_g, w_out, norm2_g, w_gate, w_up, w_down, norm_f_g):
    B, S, _ = x.shape
    depth = norm1_g.shape[0]
    assert depth == 1, "the final norm is fused into the (single) layer's FFN kernel"
    x2 = x.reshape(B * S, D_MODEL)
    mixed_c, nsa_out, (wo, wg, wu, wd) = _layer(
        x2, B, S, norm1_g[0], w_in[0], conv_w[0], k_cmp_pos[0], k_cmp_w1[0], k_cmp_w2[0], v_cmp_pos[0],
        v_cmp_w1[0], v_cmp_w2[0], gn_conv_g[0], [w_out[0], w_gate[0], w_up[0], w_down[0]])
    h = _outproj(mixed_c, nsa_out, x2, gn_nsa_g[0].reshape(1, N_HEADS * HEAD_DIM), wo)
    out = _outffn(h, norm2_g[0].reshape(1, D_MODEL), wg, wu, wd, norm_f_g.reshape(1, D_MODEL))
    return out.reshape(B, S, D_MODEL)
```

```python
import functools

import numpy as np
import jax
import jax.numpy as jnp
from jax import lax
from jax.experimental import pallas as pl
from jax.experimental.pallas import tpu as pltpu

D_MODEL = 1024
CONV_CH = 512
CONV_K = 3
N_HEADS = 8
HEAD_DIM = 64
N_KV_HEADS = 2
GQA = 4
N_BRANCH = 3
CMP_BLOCK = 32
CMP_STRIDE = 16
CMP_HIDDEN = 256
SEL_BLOCK = 64
SEL_TOP = 16
WINDOW = 512
D_FF = 2816
NORM_EPS = 1e-6

F32 = jnp.float32
BF16 = jnp.bfloat16

LANES = 128
VW = 2 * LANES
SUBLANES = 8
VMEM_LIMIT = 56 * 1024 * 1024

TQ = 256
TK = 256
NEG = -1e30
MASK_BIAS = -float(2 ** 30)
_POS_HI, _POS_LO, _PAD_MARK = HEAD_DIM, HEAD_DIM + 1, HEAD_DIM + 2
_FLAG_BITS = 16

_CONV_W = 3 * CONV_CH
_Q_OFF = _CONV_W
_KVC_OFF = _Q_OFF + N_HEADS * HEAD_DIM
_KV_OFF = _KVC_OFF + 2 * N_KV_HEADS * HEAD_DIM
_G_OFF = _KV_OFF + 4 * N_KV_HEADS * HEAD_DIM
_G_W = N_HEADS * N_BRANCH
_PROJ_W = _G_OFF + _G_W


def _rms(x, g):
    return x * lax.rsqrt(jnp.mean(x * x, axis=-1, keepdims=True) + NORM_EPS) * g


def _dot(a, b):
    return jnp.dot(a, b, preferred_element_type=F32)


def _value_rows(v2, h, wide):
    lane = lax.broadcasted_iota(jnp.int32, v2.shape, 1)
    swapped = pltpu.roll(v2, HEAD_DIM, axis=1)
    if not wide:
        return jnp.where(lane < HEAD_DIM, v2 if h == 0 else swapped, 1.0).astype(BF16)
    dup = jnp.where((lane < HEAD_DIM) == (h == 0), v2, swapped)
    return jnp.concatenate([dup.astype(BF16), jnp.ones(v2.shape, BF16)], axis=1)


def _dot_nt(a, b):
    return lax.dot_general(a, b, (((1,), (1,)), ((), ())), preferred_element_type=F32)


def _inproj_kernel(seq_tiles, n_cast, x_ref, xp_ref, g_ref, w_ref, kf_ref, cw_ref, gc_ref, *refs):
    cast_in, refs = refs[:n_cast], refs[n_cast:]
    (mixc_ref, q_ref, gate_ref, kvc_ref, ks_ref, vs_ref, kw_ref, vw_ref), cast_out = refs[:8], refs[8:]
    for src_ref, dst_ref in zip(cast_in, cast_out):
        dst_ref[...] = src_ref[...].astype(BF16)

    u = _rms(x_ref[...], g_ref[...]).astype(BF16)

    up = _rms(xp_ref[...], g_ref[...]).astype(BF16)
    ext = _dot(jnp.concatenate([up, u], axis=0), w_ref[:, 0:_CONV_W])
    cbch = ext[SUBLANES:, :]
    cu = cbch[:, CONV_CH:2 * CONV_CH] * cbch[:, 2 * CONV_CH:3 * CONV_CH]
    starts_sequence = pl.program_id(0) % seq_tiles == 0
    prev = jnp.where(starts_sequence, 0.0,
                     ext[0:SUBLANES, CONV_CH:2 * CONV_CH] * ext[0:SUBLANES, 2 * CONV_CH:3 * CONV_CH])
    r = lax.broadcasted_iota(jnp.int32, cu.shape, 0)
    cu1 = jnp.where(r == 0, prev[7:8, :], pltpu.roll(cu, 1, axis=0))
    cu2 = pltpu.roll(cu, 2, axis=0)
    cu2 = jnp.where(r == 0, prev[6:7, :], jnp.where(r == 1, prev[7:8, :], cu2))
    cw = cw_ref[...]
    y = cbch[:, 0:CONV_CH] * (cw[0:1, :] * cu2 + cw[1:2, :] * cu1 + cw[2:3, :] * cu)
    mixc_ref[...] = _rms(y, gc_ref[...]).astype(BF16)

    q_ref[...] = _dot(u, w_ref[:, _Q_OFF:_KVC_OFF]).astype(BF16)
    kvc_ref[...] = _dot(u, w_ref[:, _KVC_OFF:_KV_OFF])
    gate_ref[...] = jax.nn.sigmoid(_dot(u, w_ref[:, _G_OFF:_PROJ_W]))
    kv = _dot(u, w_ref[:, _KV_OFF:_G_OFF])
    kf = kf_ref[...]
    hd, kvw = HEAD_DIM, N_KV_HEADS * HEAD_DIM
    for h in range(N_KV_HEADS):
        ks_ref[h] = kf
        ks_ref[h, :, 0:hd] = kv[:, h * hd:(h + 1) * hd].astype(BF16)
        vs_ref[h] = _value_rows(kv[:, kvw:2 * kvw], h, False)
        kw_ref[h] = kf[:, 0:LANES]
        kw_ref[h, :, 0:hd] = kv[:, 2 * kvw + h * hd:2 * kvw + (h + 1) * hd].astype(BF16)
        vw_ref[h] = _value_rows(kv[:, 3 * kvw:4 * kvw], h, True)


def _inproj(x2, g, w, kfeat, conv_w, gc, to_cast, tm=1024):
    rows = x2.shape[0]
    steps = rows // tm
    seq_tiles = kfeat.shape[0] // tm
    row = lambda i: (i, 0)
    head_row = lambda i: (0, i, 0)
    const = lambda i: (0, 0)
    cast_specs = [pl.BlockSpec((a.shape[0] // steps, a.shape[1]), row) for a in to_cast]
    return pl.pallas_call(
        functools.partial(_inproj_kernel, seq_tiles, len(to_cast)),
        grid=(steps,),
        in_specs=[
            pl.BlockSpec((tm, D_MODEL), row),
            pl.BlockSpec((SUBLANES, D_MODEL), lambda i: (jnp.maximum(i * (tm // SUBLANES) - 1, 0), 0)),
            pl.BlockSpec((1, D_MODEL), const),
            pl.BlockSpec((D_MODEL, _PROJ_W), const, pipeline_mode=pl.Buffered(1)),
            pl.BlockSpec((tm, 2 * LANES), lambda i: (i % seq_tiles, 0)),
            pl.BlockSpec((CONV_K, CONV_CH), const),
            pl.BlockSpec((1, CONV_CH), const),
        ] + cast_specs,
        out_specs=[
            pl.BlockSpec((tm, CONV_CH), row),
            pl.BlockSpec((tm, N_HEADS * HEAD_DIM), row),
            pl.BlockSpec((tm, _G_W), row),
            pl.BlockSpec((tm, _KV_OFF - _KVC_OFF), row),
            pl.BlockSpec((N_KV_HEADS, tm, 2 * LANES), head_row),
            pl.BlockSpec((N_KV_HEADS, tm, LANES), head_row),
            pl.BlockSpec((N_KV_HEADS, tm, LANES), head_row),
            pl.BlockSpec((N_KV_HEADS, tm, VW), head_row),
        ] + cast_specs,
        out_shape=[
            jax.ShapeDtypeStruct((rows, CONV_CH), BF16),
            jax.ShapeDtypeStruct((rows, N_HEADS * HEAD_DIM), BF16),
            jax.ShapeDtypeStruct((rows, _G_W), F32),
            jax.ShapeDtypeStruct((rows, _KV_OFF - _KVC_OFF), F32),
            jax.ShapeDtypeStruct((N_KV_HEADS, rows, 2 * LANES), BF16),
            jax.ShapeDtypeStruct((N_KV_HEADS, rows, LANES), BF16),
            jax.ShapeDtypeStruct((N_KV_HEADS, rows, LANES), BF16),
            jax.ShapeDtypeStruct((N_KV_HEADS, rows, VW), BF16),
        ] + [jax.ShapeDtypeStruct(a.shape, BF16) for a in to_cast],
        compiler_params=pltpu.CompilerParams(
            dimension_semantics=("arbitrary",), vmem_limit_bytes=VMEM_LIMIT),
        name="inproj",
    )(x2, x2, g, w, kfeat, conv_w, gc, *to_cast)


def _compress_kernel(xk_ref, xv_ref, pos_ref, w1_ref, w2_ref, cf_ref, kc_ref, vc_ref):
    nch = xk_ref.shape[0] // CMP_STRIDE
    hd = HEAD_DIM
    first = [jnp.zeros((nch, CMP_HIDDEN), F32) for _ in range(2 * N_KV_HEADS)]
    second = [jnp.zeros((nch, CMP_HIDDEN), F32) for _ in range(2 * N_KV_HEADS)]
    group = 2 * LANES // hd
    for l0 in range(0, CMP_STRIDE, group):
        for s, x_ref in enumerate((xk_ref, xv_ref)):
            xs = [x_ref[pl.ds(l0 + j, nch, stride=CMP_STRIDE), :].astype(BF16) for j in range(group)]
            for h in range(N_KV_HEADS):
                c = s * N_KV_HEADS + h
                xh = jnp.concatenate([x[:, h * hd:(h + 1) * hd] for x in xs], axis=1)
                first[c] = first[c] + _dot(xh, w1_ref[s, l0 * hd:(l0 + group) * hd, :])
                second[c] = second[c] + _dot(xh, w1_ref[s, (CMP_STRIDE + l0) * hd:(CMP_STRIDE + l0 + group) * hd, :])
    for s in range(2):
        pb = _dot(pos_ref[s], w1_ref[s])[0:1, :]
        outs = []
        for h in range(N_KV_HEADS):
            c = s * N_KV_HEADS + h
            hid = first[c] + pltpu.roll(second[c], nch - 1, axis=0) + pb
            outs.append(_dot(jax.nn.gelu(hid, approximate=True).astype(BF16), w2_ref[s]))
        for h in range(N_KV_HEADS):
            if s == 0:
                kc_ref[h] = cf_ref[...]
                kc_ref[h, :, 0:hd] = outs[h].astype(BF16)
            else:
                vc_ref[h] = _value_rows(jnp.concatenate(outs, axis=1), h, True)


def _compress(kvc, pos, w1, w2, cfeat, B, S):
    nch = S // CMP_STRIDE
    full = lambda *shape: pl.BlockSpec(shape, lambda b: (0,) * len(shape))
    return pl.pallas_call(
        _compress_kernel,
        grid=(B,),
        in_specs=[
            pl.BlockSpec((S, LANES), lambda b: (b, 0)),
            pl.BlockSpec((S, LANES), lambda b: (b, 1)),
            full(*pos.shape), full(*w1.shape), full(*w2.shape), full(*cfeat.shape),
        ],
        out_specs=[
            pl.BlockSpec((None, N_KV_HEADS, nch, LANES), lambda b: (b, 0, 0, 0)),
            pl.BlockSpec((None, N_KV_HEADS, nch, VW), lambda b: (b, 0, 0, 0)),
        ],
        out_shape=[
            jax.ShapeDtypeStruct((B, N_KV_HEADS, nch, LANES), BF16),
            jax.ShapeDtypeStruct((B, N_KV_HEADS, nch, VW), BF16),
        ],
        compiler_params=pltpu.CompilerParams(
            dimension_semantics=("arbitrary",), vmem_limit_bytes=VMEM_LIMIT),
        name="compress",
    )(kvc, kvc, pos, w1, w2, cfeat)


def _hi_lo(x):
    hi = x.astype(BF16)
    return jnp.concatenate([hi, (x - hi.astype(F32)).astype(BF16)], axis=1)


def _head_layout(acc):
    low_half = lax.broadcasted_iota(jnp.int32, (TQ, LANES), 1) < HEAD_DIM
    head = lambda x, g: x[g * TQ:(g + 1) * TQ]
    if acc.shape[1] == VW:
        x = acc[:, 0:LANES] * (1.0 / jnp.maximum(acc[:, LANES:VW], 1e-30))
        pairs = [jnp.where(low_half, head(x, 2 * j), head(x, 2 * j + 1)) for j in range(GQA // 2)]
    else:
        rot = pltpu.roll(acc, HEAD_DIM, axis=1)
        pairs = [jnp.where(low_half, head(acc, 2 * j) * (1.0 / head(rot, 2 * j)),
                           head(rot, 2 * j + 1) * (1.0 / head(acc, 2 * j + 1))) for j in range(GQA // 2)]
    return jnp.concatenate(pairs, axis=1)


def _exchange(v, i, j, descending):
    hi, lo = jnp.maximum(v[i], v[j]), jnp.minimum(v[i], v[j])
    v[i], v[j] = (hi, lo) if descending else (lo, hi)


def _bitonic_merge_desc(v):
    n = len(v)
    j = n // 2
    while j >= 1:
        for i in range(n):
            if i ^ j > i:
                _exchange(v, i, i ^ j, True)
        j //= 2


def _kth_largest(x, kth, fillers=()):
    n = x.shape[0]
    assert n == kth * SUBLANES and kth & (kth - 1) == 0
    v = [x[r * SUBLANES:(r + 1) * SUBLANES, :] for r in range(kth)]
    fillers = list(fillers)
    layers = 0
    k = 2
    while k <= kth:
        j = k // 2
        while j >= 1:
            for i in range(kth):
                if i ^ j > i:
                    _exchange(v, i, i ^ j, (i & k) == 0)
            j //= 2
            layers += 1
            if layers % 3 == 0 and fillers:
                fillers.pop(0)()
        k *= 2
    shift = SUBLANES // 2
    while shift >= 1:
        v = [jnp.maximum(v[i], pltpu.roll(v[kth - 1 - i], shift, axis=0)) for i in range(kth)]
        if shift > 1:
            _bitonic_merge_desc(v)
        if fillers:
            fillers.pop(0)()
        shift //= 2
    for filler in fillers:
        filler()
    out = v[0]
    for i in range(1, kth):
        out = jnp.minimum(out, v[i])
    return out[0:1, :]


def _nsa_kernel(q_ref, qf_ref, kc_ref, vc_ref, ks_ref, vs_ref, kw_ref, vw_ref, ov_ref, cm_ref, mb_ref,
                ge_ref, g_ref, o_ref, qa_ref, tiles_ref, m_ref, acc_ref, sa_ref, sb_ref, sc_ref, sd_ref, kwp_ref, vwp_ref):
    i = pl.program_id(2)
    t0 = i * TQ
    rows = GQA * TQ
    n_sel = ov_ref.shape[1]
    ncmp = kc_ref.shape[0]

    @pl.when(i == 0)
    def _():
        lane = lax.broadcasted_iota(jnp.int32, (WINDOW, LANES), 1)
        kwp_ref[0:WINDOW, :] = jnp.where(lane == _PAD_MARK, 1.0, 0.0).astype(BF16)
        kwp_ref[WINDOW:, :] = kw_ref[...]
        vwp_ref[0:WINDOW, :] = jnp.zeros((WINDOW, VW), BF16)
        vwp_ref[WINDOW:, :] = vw_ref[...]

    q = q_ref[...]
    for g in range(GQA):
        qg = q[:, g * HEAD_DIM:(g + 1) * HEAD_DIM].astype(F32) * (HEAD_DIM ** -0.5)
        qa_ref[g * TQ:(g + 1) * TQ, 0:HEAD_DIM] = qg.astype(BF16)
    qa_ref[:, HEAD_DIM:LANES] = qf_ref[...]
    qa1 = qa_ref[:, 0:LANES]

    diag_ok = (lax.broadcasted_iota(jnp.int32, (rows, TQ), 1)
               <= (lax.broadcasted_iota(jnp.int32, (rows, TQ), 0) & (TQ - 1)))


    wk = TQ + WINDOW
    w0 = pl.multiple_of(t0, TQ)
    win = {}

    def win_scores():
        s_w = _dot_nt(qa1, kwp_ref[pl.ds(w0, wk), :])
        win["s"] = (jnp.where(diag_ok, NEG, s_w[:, 0:TQ]),
                    s_w[:, TQ:WINDOW], jnp.where(diag_ok, s_w[:, WINDOW:wk], NEG))

    def win_max():
        s_old, s_mid, s_new = win["s"]
        win["m"] = jnp.maximum(jnp.maximum(jnp.max(s_old, axis=1, keepdims=True),
                                           jnp.max(s_mid, axis=1, keepdims=True)),
                               jnp.max(s_new, axis=1, keepdims=True))

    def win_exp():
        win["e"] = jnp.concatenate([jnp.exp(s - win["m"]) for s in win["s"]], axis=1).astype(BF16)

    def win_pv():
        win["acc"] = _dot(win["e"], vwp_ref[pl.ds(w0, wk), :])

    s_c = _dot_nt(qa1, kc_ref[...])
    s_c = jnp.where(cm_ref[...] <= (TQ // CMP_STRIDE) * i - 2, s_c, NEG)
    m_c = jnp.maximum(jnp.max(s_c, axis=1, keepdims=True), 0.5 * NEG)
    e_c = jnp.exp(s_c - m_c)
    acc_c = _dot(e_c.astype(BF16), vc_ref[...])
    inv_c = 1.0 / jnp.maximum(acc_c[:, LANES:VW], 1e-30)
    p_c = e_c * jnp.concatenate([inv_c] * (ncmp // LANES), axis=1)

    p_sum = p_c[0:TQ] + p_c[TQ:2 * TQ] + p_c[2 * TQ:3 * TQ] + p_c[3 * TQ:4 * TQ]
    p_hi = p_sum.astype(BF16)
    p_lo = (p_sum - p_hi.astype(F32)).astype(BF16)
    imp = _dot(p_hi, ov_ref[...]) + _dot(p_lo, ov_ref[...])
    imp_t = imp.T
    jb = lax.broadcasted_iota(jnp.int32, (n_sel, TQ), 0)
    tq = t0 + lax.broadcasted_iota(jnp.int32, (n_sel, TQ), 1)
    imp_t = jnp.where(SEL_BLOCK * jb > tq, -jnp.inf, imp_t)
    forced = (jb == 0) | (jb == (tq >> (SEL_BLOCK.bit_length() - 1)))
    imp_t = jnp.where(forced, jnp.inf, imp_t)

    thr = _kth_largest(imp_t, SEL_TOP, (win_scores, win_max, win_exp, win_pv))
    acc_w = win["acc"]
    above = imp_t > thr
    tied = imp_t == thr
    n_above = jnp.sum(jnp.where(above, 1.0, 0.0), axis=0, keepdims=True)
    lower = (lax.broadcasted_iota(jnp.int32, (n_sel, n_sel), 1)
             < lax.broadcasted_iota(jnp.int32, (n_sel, n_sel), 0))
    tied_before = _dot(jnp.where(lower, 1.0, 0.0).astype(BF16), jnp.where(tied, 1.0, 0.0).astype(BF16))
    take_tie = jnp.where(tied_before + n_above < float(SEL_TOP), 1.0, 0.0)
    sel_t = jnp.where(above, 1.0, jnp.where(tied, take_tie, 0.0))
    sel_q = sel_t.T
    bias = ((1.0 - sel_q) * MASK_BIAS).astype(BF16)
    for g in range(GQA):
        qa_ref[g * TQ:(g + 1) * TQ, LANES:2 * LANES] = bias

    n_past = t0 // TK
    first_mask = (t0 - n_past * TK) // TQ
    tiles_ref[0] = n_past

    def scores(j, s_ref, mask=None):
        k0 = pl.multiple_of(tiles_ref[j] * TK, TK)
        s = _dot_nt(qa_ref[...], ks_ref[pl.ds(k0, TK), :])
        s_ref[...] = s if mask is None else s + mask

    scores(0, sa_ref, mb_ref[first_mask])

    bpt = TK // SEL_BLOCK
    n_tiles = n_sel // bpt
    blk_cnt = _dot(jnp.ones((SUBLANES, TQ), BF16), sel_q.astype(BF16))
    in_tile = (lax.broadcasted_iota(jnp.int32, (n_sel, LANES), 0) // bpt
               == lax.broadcasted_iota(jnp.int32, (n_sel, LANES), 1))
    tile_cnt = _dot(jnp.where(blk_cnt > 0.5, 1.0, 0.0).astype(BF16),
                    jnp.where(in_tile, 1.0, 0.0).astype(BF16))
    lane = lax.broadcasted_iota(jnp.int32, (1, LANES), 1)
    bit = jnp.where(tile_cnt[0:1, :] > 0.5, jnp.left_shift(1, lane & (_FLAG_BITS - 1)).astype(F32), 0.0)
    words = [jnp.sum(jnp.where((lane >= _FLAG_BITS * w) & (lane < _FLAG_BITS * (w + 1)), bit, 0.0)
                     ).astype(jnp.int32) for w in range(-(-n_tiles // _FLAG_BITS))]
    n_tasks = jnp.int32(1)
    for kt in range(n_tiles - 1):
        used = (words[kt // _FLAG_BITS] >> (kt % _FLAG_BITS)) & 1
        tiles_ref[n_tasks] = kt
        n_tasks = n_tasks + jnp.where(kt < n_past, used, 0)
    tiles_ref[n_tasks] = 0
    tiles_ref[n_tasks + 1] = 0

    m_ref[...] = jnp.full(m_ref.shape, NEG, F32)
    acc_ref[...] = jnp.zeros(acc_ref.shape, F32)

    gates = _dot(_hi_lo(g_ref[...]), ge_ref[...])
    gw = GQA * HEAD_DIM
    merged = gates[:, 0:gw] * _head_layout(acc_c) + gates[:, 2 * gw:3 * gw] * _head_layout(acc_w)
    gates_sel = gates[:, gw:2 * gw]

    def accumulate(j, s_ref):
        k0 = pl.multiple_of(tiles_ref[j] * TK, TK)
        s = s_ref[...]
        m = m_ref[...]
        m_new = jnp.maximum(m, jnp.max(s, axis=1, keepdims=True))
        p = jnp.concatenate([jnp.exp((s[:, c:c + LANES] - m_new).astype(BF16)) for c in range(0, TK, LANES)],
                            axis=1)
        acc_ref[...] = jnp.exp(m - m_new) * acc_ref[...] + _dot(p, vs_ref[pl.ds(k0, TK), :])
        m_ref[...] = m_new

    def quad(jj, c):
        a = 4 * jj
        scores(a + 1, sb_ref)
        accumulate(a, sa_ref)
        scores(a + 2, sc_ref)
        accumulate(a + 1, sb_ref)
        scores(a + 3, sd_ref)
        accumulate(a + 2, sc_ref)
        scores(a + 4, sa_ref)
        accumulate(a + 3, sd_ref)
        return c

    trips = n_tasks // 4
    lax.fori_loop(0, trips, quad, 0)

    rest = 4 * trips

    @pl.when(n_tasks - rest == 1)
    def _():
        accumulate(rest, sa_ref)

    @pl.when(n_tasks - rest == 2)
    def _():
        scores(rest + 1, sb_ref)
        accumulate(rest, sa_ref)
        accumulate(rest + 1, sb_ref)

    @pl.when(n_tasks - rest == 3)
    def _():
        scores(rest + 1, sb_ref)
        accumulate(rest, sa_ref)
        scores(rest + 2, sc_ref)
        accumulate(rest + 1, sb_ref)
        accumulate(rest + 2, sc_ref)

    o_ref[...] = (merged + gates_sel * _head_layout(acc_ref[...])).astype(o_ref.dtype)


def _nsa(q, qfeat, kc_aug, vc, ks_aug, vs, kw_aug, vw, ov, cmask, tile_masks, gate_expand, gates, B, S):
    nq = S // TQ
    gw = GQA * HEAD_DIM
    ncmp = kc_aug.shape[2]
    n_sel = ov.shape[1]
    bk = lambda b, k, i: (b, k, 0, 0)
    kb = lambda b, k, i: (k, b, 0, 0)
    return pl.pallas_call(
        _nsa_kernel,
        grid=(B, N_KV_HEADS, nq),
        in_specs=[
            pl.BlockSpec((TQ, gw), lambda b, k, i: (b * nq + i, k)),
            pl.BlockSpec((None, GQA * TQ, HEAD_DIM), lambda b, k, i: (k, 0, 0)),
            pl.BlockSpec((None, None, ncmp, LANES), bk),
            pl.BlockSpec((None, None, ncmp, VW), bk),
            pl.BlockSpec((None, None, S, 2 * LANES), kb),
            pl.BlockSpec((None, None, S, LANES), kb),
            pl.BlockSpec((None, None, S, LANES), kb),
            pl.BlockSpec((None, None, S, VW), kb),
            pl.BlockSpec((ncmp, n_sel), lambda b, k, i: (0, 0)),
            pl.BlockSpec((GQA * TQ, ncmp), lambda b, k, i: (0, 0)),
            pl.BlockSpec(tile_masks.shape, lambda b, k, i: (0, 0, 0)),
            pl.BlockSpec((None,) + gate_expand.shape[1:], lambda b, k, i: (k, 0, 0)),
            pl.BlockSpec((TQ, _G_W), lambda b, k, i: (b * nq + i, 0)),
        ],
        out_specs=pl.BlockSpec((TQ, gw), lambda b, k, i: (b * nq + i, k)),
        out_shape=jax.ShapeDtypeStruct((B * S, N_KV_HEADS * gw), BF16),
        scratch_shapes=[
            pltpu.VMEM((GQA * TQ, 2 * LANES), BF16),
            pltpu.SMEM((S // TK + 2,), jnp.int32),
            pltpu.VMEM((GQA * TQ, LANES), F32),
            pltpu.VMEM((GQA * TQ, LANES), F32),
            pltpu.VMEM((GQA * TQ, TK), F32),
            pltpu.VMEM((GQA * TQ, TK), F32),
            pltpu.VMEM((GQA * TQ, TK), F32),
            pltpu.VMEM((GQA * TQ, TK), F32),
            pltpu.VMEM((S + WINDOW, LANES), BF16),
            pltpu.VMEM((S + WINDOW, VW), BF16),
        ],
        compiler_params=pltpu.CompilerParams(
            dimension_semantics=("arbitrary",) * 3, vmem_limit_bytes=VMEM_LIMIT),
        name="nsa",
    )(q, qfeat, kc_aug, vc, ks_aug, vs, kw_aug, vw, ov, cmask, tile_masks, gate_expand, gates)


_FF_CHUNK = 768


def _outproj_kernel(mc_ref, nsa_ref, x_ref, gn_ref, wo_ref, h_ref):
    mixed_n = _rms(nsa_ref[...].astype(F32), gn_ref[...]).astype(BF16)
    h_ref[...] = (x_ref[...] + _dot(mc_ref[...], wo_ref[0:CONV_CH, :])
                  + _dot(mixed_n, wo_ref[CONV_CH:2 * CONV_CH, :]))


def _outproj(mixed_c, nsa_out, x2, gn, wo, tm=1024):
    rows = x2.shape[0]
    row = lambda i: (i, 0)
    const = lambda i: (0, 0)
    return pl.pallas_call(
        _outproj_kernel,
        grid=(rows // tm,),
        in_specs=[
            pl.BlockSpec((tm, CONV_CH), row),
            pl.BlockSpec((tm, N_HEADS * HEAD_DIM), row),
            pl.BlockSpec((tm, D_MODEL), row),
            pl.BlockSpec((1, N_HEADS * HEAD_DIM), const),
            pl.BlockSpec((2 * CONV_CH, D_MODEL), const, pipeline_mode=pl.Buffered(1)),
        ],
        out_specs=pl.BlockSpec((tm, D_MODEL), row),
        out_shape=jax.ShapeDtypeStruct((rows, D_MODEL), F32),
        compiler_params=pltpu.CompilerParams(
            dimension_semantics=("arbitrary",), vmem_limit_bytes=VMEM_LIMIT),
        name="outproj",
    )(mixed_c, nsa_out, x2, gn, wo)


def _outffn_kernel(h_ref, g2_ref, wg_ref, wu_ref, wd_ref, gf_ref, o_ref):
    h = h_ref[...]
    u = _rms(h, g2_ref[...]).astype(BF16)
    acc = h
    for c in range(0, D_FF, _FF_CHUNK):
        sl = slice(c, min(c + _FF_CHUNK, D_FF))
        a = jax.nn.silu(_dot(u, wg_ref[:, sl])) * _dot(u, wu_ref[:, sl])
        acc = acc + _dot(a.astype(BF16), wd_ref[sl, :])
    o_ref[...] = _rms(acc, gf_ref[...])


def _outffn(h, g2, wg, wu, wd, gf, tm=512):
    rows = h.shape[0]
    row = lambda i: (i, 0)
    const = lambda i: (0, 0)
    resident = lambda shape: pl.BlockSpec(shape, const, pipeline_mode=pl.Buffered(1))
    return pl.pallas_call(
        _outffn_kernel,
        grid=(rows // tm,),
        in_specs=[
            pl.BlockSpec((tm, D_MODEL), row),
            pl.BlockSpec((1, D_MODEL), const),
            resident((D_MODEL, D_FF)),
            resident((D_MODEL, D_FF)),
            resident((D_FF, D_MODEL)),
            pl.BlockSpec((1, D_MODEL), const),
        ],
        out_specs=pl.BlockSpec((tm, D_MODEL), row),
        out_shape=jax.ShapeDtypeStruct((rows, D_MODEL), F32),
        compiler_params=pltpu.CompilerParams(
            dimension_semantics=("arbitrary",), vmem_limit_bytes=VMEM_LIMIT),
        name="outffn",
    )(h, g2, wg, wu, wd, gf)


def _key_features(pos, n_sel):
    f = np.zeros((pos.shape[0], LANES + n_sel), np.float32)
    f[:, _POS_HI] = pos // SEL_BLOCK
    f[:, _POS_LO] = pos % SEL_BLOCK
    if n_sel:
        f[np.arange(pos.shape[0]), LANES + pos // SEL_BLOCK] = 1.0
    return f


def _query_features():
    f = np.zeros((N_KV_HEADS, GQA * TQ, HEAD_DIM), np.float32)
    for k in range(N_KV_HEADS):
        for g in range(GQA):
            slope = 2.0 ** (-8.0 * (k * GQA + g + 1) / N_HEADS)
            f[k, g * TQ:(g + 1) * TQ, _POS_HI - HEAD_DIM] = slope * SEL_BLOCK
            f[k, g * TQ:(g + 1) * TQ, _POS_LO - HEAD_DIM] = slope
            f[k, g * TQ:(g + 1) * TQ, _PAD_MARK - HEAD_DIM] = MASK_BIAS
    return f


def _causal_masks(ncmp):
    r = np.arange(GQA * TQ)[:, None] % TQ
    cmask = (np.arange(ncmp)[None, :] - ((r + 1) // CMP_STRIDE)).astype(np.int32)
    c = np.arange(TK)[None, :]
    tile_masks = [np.where(c - r <= d * TQ, 0.0, NEG).astype(np.float32) for d in range(max(TK // TQ, 1))]
    return cmask, np.stack(tile_masks)


def _gate_expand():
    gw = GQA * HEAD_DIM
    e = np.zeros((N_KV_HEADS, 2 * _G_W, N_BRANCH * gw), np.float32)
    for k in range(N_KV_HEADS):
        for g in range(GQA):
            for br in range(N_BRANCH):
                for half in range(2):
                    e[k, half * _G_W + (k * GQA + g) * N_BRANCH + br,
                      br * gw + g * HEAD_DIM:br * gw + (g + 1) * HEAD_DIM] = 1.0
    return e


def _overlap(ncmp_pad, n_sel):
    nc = ncmp_pad - 1
    c_start = CMP_STRIDE * np.arange(nc)
    c_end = c_start + CMP_BLOCK - 1
    s_start = SEL_BLOCK * np.arange(n_sel)
    ov = np.minimum(c_end[:, None] + 1, s_start[None, :] + SEL_BLOCK) - np.maximum(c_start[:, None], s_start[None, :])
    out = np.zeros((ncmp_pad, n_sel), np.float32)
    out[:nc] = np.clip(ov, 0, None).astype(np.float32) / CMP_BLOCK
    return out


def _layer(x2, B, S, norm1_g, w_in, conv_w, k_cmp_pos, k_cmp_w1, k_cmp_w2, v_cmp_pos, v_cmp_w1, v_cmp_w2,
           gn_conv_g, later_weights):
    n_sel = S // SEL_BLOCK
    nch = S // CMP_STRIDE
    hd = HEAD_DIM

    kfeat = jnp.asarray(_key_features(np.arange(S), n_sel), BF16)
    mixed_c, q, gates, kvc, ks_aug, vs, kw_aug, vw, *later_bf16 = _inproj(
        x2, norm1_g.reshape(1, D_MODEL), w_in.astype(BF16), kfeat, conv_w, gn_conv_g.reshape(1, CONV_CH),
        later_weights)

    w1 = jnp.stack([k_cmp_w1, v_cmp_w1]).astype(BF16)
    w2 = jnp.stack([k_cmp_w2, v_cmp_w2]).astype(BF16)
    pos = jnp.stack([k_cmp_pos, v_cmp_pos]).reshape(2, 1, CMP_BLOCK * hd)
    pos = jnp.broadcast_to(pos, (2, SUBLANES, CMP_BLOCK * hd)).astype(BF16)
    c_end = CMP_STRIDE * np.arange(nch) + CMP_BLOCK - 1
    cfeat = jnp.asarray(_key_features(c_end, 0), BF16)
    kc_aug, vc = _compress(kvc, pos, w1, w2, cfeat, B, S)

    cmask, tile_masks = _causal_masks(nch)

    nsa_out = _nsa(q, jnp.asarray(_query_features(), BF16), kc_aug, vc,
                   ks_aug.reshape(N_KV_HEADS, B, S, 2 * LANES), vs.reshape(N_KV_HEADS, B, S, LANES),
                   kw_aug.reshape(N_KV_HEADS, B, S, LANES), vw.reshape(N_KV_HEADS, B, S, VW),
                   jnp.asarray(_overlap(nch, n_sel), BF16), jnp.asarray(cmask),
                   jnp.asarray(tile_masks), jnp.asarray(_gate_expand(), BF16), gates, B, S)
    return mixed_c, nsa_out, later_bf16


def kernel(x, norm1_g, w_in, conv_w, k_cmp_pos, k_cmp_w1, k_cmp_w2, v_cmp_pos, v_cmp_w1, v_cmp_w2,
           gn_conv_g, gn_nsa_g, w_out, norm2_g, w_gate, w_up, w_down, norm_f_g):
    B, S, _ = x.shape
    depth = norm1_g.shape[0]
    assert depth == 1, "the final norm is fused into the (single) layer's FFN kernel"
    x2 = x.reshape(B * S, D_MODEL)
    mixed_c, nsa_out, (wo, wg, wu, wd) = _layer(
        x2, B, S, norm1_g[0], w_in[0], conv_w[0], k_cmp_pos[0], k_cmp_w1[0], k_cmp_w2[0], v_cmp_pos[0],
        v_cmp_w1[0], v_cmp_w2[0], gn_conv_g[0], [w_out[0], w_gate[0], w_up[0], w_down[0]])
    h = _outproj(mixed_c, nsa_out, x2, gn_nsa_g[0].reshape(1, N_HEADS * HEAD_DIM), wo)
    out = _outffn(h, norm2_g[0].reshape(1, D_MODEL), wg, wu, wd, norm_f_g.reshape(1, D_MODEL))
    return out.reshape(B, S, D_MODEL)
```
